```python
import jax
import jax.numpy as jnp
from jax import lax
import numpy as np

D_MODEL = 1024
BATCH = 32
SEQ = 256
DEPTH = 1
DEC_BATCH = 2
DEC_SEQ = 2048
PAST_LEN = 256

GRID_W = 64
D_MIX = D_MODEL
C_CONV = D_MIX // 2
CONV_K = 31
N_HEADS = 4
D_NOPE = 128
D_ROPE = 64
D_V = 128
D_ATT = N_HEADS * D_V
Q_LORA = 384
KV_LORA = 256
IN_COLS = 2 * C_CONV + Q_LORA + KV_LORA + D_ROPE
SPLITS = (2 * C_CONV, 2 * C_CONV + Q_LORA, 2 * C_CONV + Q_LORA + KV_LORA)
N_GROUPS = 4
E_PER_GROUP = 8
N_EXPERTS = N_GROUPS * E_PER_GROUP
TOP_K = 2
D_EXPERT = 256
ROPE_BASE = 10000.0
EPS = 1e-6
Q_BLOCK = 128
ATT_SCALE = (D_NOPE + D_ROPE) ** -0.5

kernel_name = "hymba_conformer_mla_hiermoe_diffusion_step"


def _rmsnorm(x, g):
    xf = x.astype(jnp.float32)
    y = xf * lax.rsqrt(jnp.mean(xf * xf, axis=-1, keepdims=True) + EPS)
    return (y * g.astype(jnp.float32)).astype(x.dtype)


def _layernorm(x, g, b):
    xf = x.astype(jnp.float32)
    mu = jnp.mean(xf, axis=-1, keepdims=True)
    var = jnp.mean(jnp.square(xf - mu), axis=-1, keepdims=True)
    y = (xf - mu) * lax.rsqrt(var + EPS)
    return (y * g.astype(jnp.float32) + b.astype(jnp.float32)).astype(x.dtype)


def _adaln(cvec, w_ada, b_ada):
    m = jax.nn.silu(cvec) @ w_ada + b_ada
    return tuple(t[:, None, :] for t in jnp.split(m, 6, axis=-1))


def _modulate(x, g, shift, scale):
    return _rmsnorm(x, g) * (1.0 + scale) + shift


def _axial_tables(rows):
    row = jnp.repeat(jnp.arange(rows, dtype=jnp.float32), GRID_W)
    col = jnp.tile(jnp.arange(GRID_W, dtype=jnp.float32), rows)
    n_freq = D_ROPE // 4
    freqs = ROPE_BASE ** (-jnp.arange(n_freq, dtype=jnp.float32) / n_freq)
    ang_r = (row[:, None] * freqs)[:, None, :]
    ang_c = (col[:, None] * freqs)[:, None, :]
    return (jnp.cos(ang_r), jnp.sin(ang_r), jnp.cos(ang_c), jnp.sin(ang_c))


def _rot_half(x, cos, sin):
    x1, x2 = jnp.split(x, 2, axis=-1)
    cos = cos.astype(x.dtype)
    sin = sin.astype(x.dtype)
    return jnp.concatenate([x1 * cos - x2 * sin, x2 * cos + x1 * sin], axis=-1)


def _rope2d(x, tabs):
    cos_r, sin_r, cos_c, sin_c = tabs
    xr, xc = jnp.split(x, 2, axis=-1)
    return jnp.concatenate([_rot_half(xr, cos_r, sin_r), _rot_half(xc, cos_c, sin_c)], axis=-1)


def _conformer_conv(u, w_dw, b_dw, ln_g, ln_b):
    a, g = jnp.split(u, 2, axis=-1)
    h = a * jax.nn.sigmoid(g)
    h = lax.conv_general_dilated(
        h, w_dw[:, None, :].astype(h.dtype), window_strides=(1,),
        padding=[(CONV_K // 2, CONV_K // 2)],
        dimension_numbers=("NWC", "WIO", "NWC"), feature_group_count=C_CONV)
    h = h + b_dw
    return jax.nn.silu(_layernorm(h, ln_g, ln_b))


def _mla_queries(u_q, q_norm_g, w_uq):
    b, t, _ = u_q.shape
    q = (_rmsnorm(u_q, q_norm_g) @ w_uq).reshape(b, t, N_HEADS, D_NOPE + D_ROPE)
    return q[..., :D_NOPE], q[..., D_NOPE:]


def _mla_decompress(ckv, w_ukv):
    b, s, _ = ckv.shape
    kv = (ckv @ w_ukv).reshape(b, s, N_HEADS, D_NOPE + D_V)
    return kv[..., :D_NOPE], kv[..., D_NOPE:]


def _mla_attend(q_nope, q_rope, k_nope, k_rope, v):
    b, t = q_nope.shape[0], q_nope.shape[1]
    nb = t // Q_BLOCK
    qn = q_nope.reshape(b, nb, Q_BLOCK, N_HEADS, D_NOPE).transpose(1, 0, 2, 3, 4)
    qr = q_rope.reshape(b, nb, Q_BLOCK, N_HEADS, D_ROPE).transpose(1, 0, 2, 3, 4)

    def block(args):
        qn_b, qr_b = args
        s = (jnp.einsum('bqhd,bkhd->bhqk', qn_b, k_nope)
             + jnp.einsum('bqhr,bkr->bhqk', qr_b, k_rope))
        p = jax.nn.softmax(s.astype(jnp.float32) * ATT_SCALE, axis=-1).astype(v.dtype)
        return jnp.einsum('bhqk,bkhd->bqhd', p, v)

    o = lax.map(block, (qn, qr))
    return o.transpose(1, 0, 2, 3, 4).reshape(b, t, D_ATT)


def _token_mix(h, w_in, w_dw, b_dw, ln_g, ln_b, q_norm_g, kv_norm_g, w_uq, w_ukv, w_out,
               tabs, ctx_ckv, ctx_krope):
    u = h @ w_in
    u_conv, u_q, u_kv, u_kr = jnp.split(u, SPLITS, axis=-1)
    conv_out = _conformer_conv(u_conv, w_dw, b_dw, ln_g, ln_b)
    q_nope, q_rope = _mla_queries(u_q, q_norm_g, w_uq)
    ckv = _rmsnorm(u_kv, kv_norm_g)
    k_rope = u_kr
    if tabs is None:
        ckv_keys, kr_keys = ckv, k_rope
    else:
        q_rope = _rope2d(q_rope, tabs)
        k_rope = _rope2d(k_rope[:, :, None, :], tabs)[:, :, 0]
        ckv_keys = jnp.concatenate([ctx_ckv.astype(ckv.dtype), ckv], axis=1)
        kr_keys = jnp.concatenate([ctx_krope.astype(k_rope.dtype), k_rope], axis=1)
    k_nope, v = _mla_decompress(ckv_keys, w_ukv)
    att = _mla_attend(q_nope, q_rope, k_nope, kr_keys, v)
    out = jnp.concatenate([conv_out, att], axis=-1) @ w_out
    return out, ckv, k_rope


def _hier_moe(h, w_rg, b_rg, w_re, b_re, w_gate, w_up, w_down):
    b, t, d = h.shape
    xt = h.reshape(-1, d)
    g_prob = jax.nn.softmax((xt @ w_rg).astype(jnp.float32) + b_rg.astype(jnp.float32), axis=-1)
    g_top, g_idx = lax.top_k(g_prob, 1)
    e_logits = ((xt @ w_re).astype(jnp.float32) + b_re.astype(jnp.float32)).reshape(-1, N_GROUPS, E_PER_GROUP)
    e_sel = jnp.take_along_axis(e_logits, g_idx[:, :, None], axis=1)[:, 0]
    e_top, e_idx = lax.top_k(jax.nn.softmax(e_sel, axis=-1), TOP_K)
    e_w = e_top / jnp.sum(e_top, axis=-1, keepdims=True) * g_top
    expert_id = g_idx * E_PER_GROUP + e_idx
    gates = jnp.sum(jax.nn.one_hot(expert_id, N_EXPERTS, dtype=jnp.float32) * e_w[..., None], axis=1)
    gates = gates.astype(xt.dtype)
    out = jnp.zeros_like(xt)
    for e in range(N_EXPERTS):
        he = jax.nn.silu(xt @ w_gate[e]) * (xt @ w_up[e])
        out = out + gates[:, e:e + 1] * (he @ w_down[e])
    return out.reshape(b, t, d)


def setup_inputs(seed: int = 0) -> dict:
    key = jax.random.key(seed)
    ks = jax.random.split(key, 32)
    f = jnp.float32
    L, D = DEPTH, D_MODEL

    def nrm(k, shape, scale):
        return jax.random.normal(k, shape, f) * scale

    return {
        "x_prompt": nrm(ks[0], (BATCH, SEQ, D), 1.0),
        "x_sample": nrm(ks[1], (DEC_BATCH, DEC_SEQ, D), 1.0),
        "cache_ckv": nrm(ks[2], (DEC_BATCH, L, PAST_LEN, KV_LORA), 1.0),
        "cache_krope": nrm(ks[3], (DEC_BATCH, L, PAST_LEN, D_ROPE), 1.0),
        "c": nrm(ks[4], (DEC_BATCH, D), 1.0),
        "c_ctx": nrm(ks[5], (D,), 1.0),
        "w_ada": nrm(ks[6], (L, D, 6 * D), 0.5 * D ** -0.5),
        "b_ada": nrm(ks[7], (L, 6 * D), 0.02),
        "g_pre1": 1.0 + nrm(ks[8], (L, D), 0.05),
        "g_post1": 1.0 + nrm(ks[9], (L, D), 0.05),
        "g_pre2": 1.0 + nrm(ks[10], (L, D), 0.05),
        "g_post2": 1.0 + nrm(ks[11], (L, D), 0.05),
        "w_in": nrm(ks[12], (L, D, IN_COLS), D ** -0.5),
        "w_dw": nrm(ks[13], (L, CONV_K, C_CONV), CONV_K ** -0.5),
        "b_dw": nrm(ks[14], (L, C_CONV), 0.02),
        "conv_ln_g": 1.0 + nrm(ks[15], (L, C_CONV), 0.05),
        "conv_ln_b": nrm(ks[16], (L, C_CONV), 0.02),
        "q_norm_g": 1.0 + nrm(ks[17], (L, Q_LORA), 0.05),
        "kv_norm_g": 1.0 + nrm(ks[18], (L, KV_LORA), 0.05),
        "w_uq": nrm(ks[19], (L, Q_LORA, N_HEADS * (D_NOPE + D_ROPE)), Q_LORA ** -0.5),
        "w_ukv": nrm(ks[20], (L, KV_LORA, N_HEADS * (D_NOPE + D_V)), KV_LORA ** -0.5),
        "w_out": nrm(ks[21], (L, D_MIX, D), D_MIX ** -0.5),
        "w_rg": nrm(ks[22], (L, D, N_GROUPS), D ** -0.5),
        "b_rg": nrm(ks[23], (L, N_GROUPS), 0.01),
        "w_re": nrm(ks[24], (L, D, N_EXPERTS), D ** -0.5),
        "b_re": nrm(ks[25], (L, N_EXPERTS), 0.01),
        "w_gate": nrm(ks[26], (L, N_EXPERTS, D, D_EXPERT), D ** -0.5),
        "w_up": nrm(ks[27], (L, N_EXPERTS, D, D_EXPERT), D ** -0.5),
        "w_down": nrm(ks[28], (L, N_EXPERTS, D_EXPERT, D), D_EXPERT ** -0.5),
    }


def reference(x_prompt, x_sample, cache_ckv, cache_krope, c, c_ctx, w_ada, b_ada,
              g_pre1, g_post1, g_pre2, g_post2, w_in, w_dw, b_dw, conv_ln_g, conv_ln_b,
              q_norm_g, kv_norm_g, w_uq, w_ukv, w_out, w_rg, b_rg, w_re, b_re,
              w_gate, w_up, w_down):
    rows = x_sample.shape[1] // GRID_W
    tabs = _axial_tables(rows)
    xp, xs = x_prompt, x_sample
    new_ckv, new_krope = [], []
    for l in range(DEPTH):
        mix_w = (w_in[l], w_dw[l], b_dw[l], conv_ln_g[l], conv_ln_b[l], q_norm_g[l],
                 kv_norm_g[l], w_uq[l], w_ukv[l], w_out[l])
        moe_w = (w_rg[l], b_rg[l], w_re[l], b_re[l], w_gate[l], w_up[l], w_down[l])

        sh1, sc1, gt1, sh2, sc2, gt2 = _adaln(c_ctx[None, :], w_ada[l], b_ada[l])
        h = _modulate(xp, g_pre1[l], sh1, sc1)
        out, ckv, krope = _token_mix(h, *mix_w, None, None, None)
        xp = xp + gt1 * _rmsnorm(out, g_post1[l])
        h = _modulate(xp, g_pre2[l], sh2, sc2)
        xp = xp + gt2 * _rmsnorm(_hier_moe(h, *moe_w), g_post2[l])
        new_ckv.append(ckv)
        new_krope.append(krope)

        sh1, sc1, gt1, sh2, sc2, gt2 = _adaln(c, w_ada[l], b_ada[l])
        h = _modulate(xs, g_pre1[l], sh1, sc1)
        out, _, _ = _token_mix(h, *mix_w, tabs, cache_ckv[:, l], cache_krope[:, l])
        xs = xs + gt1 * _rmsnorm(out, g_post1[l])
        h = _modulate(xs, g_pre2[l], sh2, sc2)
        xs = xs + gt2 * _rmsnorm(_hier_moe(h, *moe_w), g_post2[l])

    state_ckv = jnp.stack(new_ckv, axis=1)
    state_krope = jnp.stack(new_krope, axis=1)
    return (xp, xs, state_ckv, state_krope)
```

```python
import functools

import jax
import jax.numpy as jnp
import numpy as np
from jax import lax
from jax.experimental import pallas as pl
from jax.experimental.pallas import tpu as pltpu

D_MODEL = 1024
GRID_W = 64
C_CONV = 512
CONV_K = 31
N_HEADS = 4
D_NOPE = 128
D_ROPE = 64
D_V = 128
Q_LORA = 384
KV_LORA = 256
N_GROUPS = 4
E_PER_GROUP = 8
N_EXPERTS = 32
D_EXPERT = 256
ROPE_BASE = 10000.0
EPS = 1e-6
ATT_SCALE = (D_NOPE + D_ROPE) ** -0.5

LANE = 128
SUBLANE = 8
HEAD_W = 2 * LANE
CONV_HALO = 16
VMEM_LIMIT = 56 * 1024 * 1024

BF16 = jnp.bfloat16
F32 = jnp.float32


def _cparams(*sem):
    return pltpu.CompilerParams(dimension_semantics=sem, vmem_limit_bytes=VMEM_LIMIT)


def _rms(x, g):
    return x * lax.rsqrt(jnp.mean(x * x, axis=-1, keepdims=True) + EPS) * g


def _sigmoid(x):
    return 1.0 / (1.0 + jnp.exp(-x))


def _dot(a, b):
    return jnp.dot(a, b, preferred_element_type=F32)


def _ada_kernel(c_ref, w_ref, b_ref, o_ref):
    c = c_ref[...]
    s = (c * _sigmoid(c)).astype(BF16)
    o_ref[...] = _dot(s, w_ref[...].astype(BF16)) + b_ref[...]


def _ada(cvec, w_ada, b_ada):
    n = w_ada.shape[1]
    tn = 1536
    return pl.pallas_call(
        _ada_kernel,
        grid=(n // tn,),
        in_specs=[pl.BlockSpec((8, D_MODEL), lambda j: (0, 0)),
                  pl.BlockSpec((D_MODEL, tn), lambda j: (0, j)),
                  pl.BlockSpec((1, tn), lambda j: (0, j))],
        out_specs=pl.BlockSpec((8, tn), lambda j: (0, j)),
        out_shape=jax.ShapeDtypeStruct((8, n), F32),
        compiler_params=_cparams("arbitrary"),
        name="ada",
    )(cvec, w_ada, b_ada.reshape(1, n))


def _mix_in_kernel(rope, x_ref, mod_ref, gpre_ref, win_ref, qg_ref, kvg_ref, wq_ref, wqs_ref,
                   wukv_ref, cos_ref, sin_ref, hglu_ref, q_ref, k_ref, v_ref, ckv_ref, kr_ref):
    x = x_ref[0]
    sh1 = mod_ref[0, 0:1, :]
    sc1 = mod_ref[0, 1:2, :]
    h = _rms(x, gpre_ref[...]) * (1.0 + sc1) + sh1
    u = _dot(h.astype(BF16), win_ref[...])

    a = u[:, :C_CONV]
    g = u[:, C_CONV:2 * C_CONV]
    hglu_ref[0] = a * _sigmoid(g)

    o_q = 2 * C_CONV
    o_kv = o_q + Q_LORA
    o_kr = o_kv + KV_LORA
    qn = _rms(u[:, o_q:o_kv], qg_ref[...]).astype(BF16)
    qf = _dot(qn, wq_ref[...])
    ckv = _rms(u[:, o_kv:o_kr], kvg_ref[...])
    ckv_ref[0] = ckv
    kvd = _dot(ckv.astype(BF16), wukv_ref[...])
    kr = u[:, o_kr:o_kr + LANE]
    kr_ref[0] = kr[:, :D_ROPE]
    if rope:
        cos = cos_ref[...]
        sin = sin_ref[...]
        qs = _dot(qn, wqs_ref[...])
        kr = kr * cos + u[:, o_kr + LANE:o_kr + 2 * LANE] * sin
    q_parts = []
    k_parts = []
    for hd in range(N_HEADS):
        q_parts.append(qf[:, hd * HEAD_W:hd * HEAD_W + LANE])
        qr = qf[:, hd * HEAD_W + LANE:(hd + 1) * HEAD_W]
        if rope:
            qr = qr * cos + qs[:, hd * LANE:(hd + 1) * LANE] * sin
        q_parts.append(qr)
        k_parts.append(kvd[:, hd * D_NOPE:(hd + 1) * D_NOPE])
        k_parts.append(kr)
    q_ref[0] = jnp.concatenate(q_parts, axis=-1).astype(BF16)
    k_ref[0] = jnp.concatenate(k_parts, axis=-1).astype(BF16)
    v_ref[0] = kvd[:, N_HEADS * D_NOPE:].astype(BF16)


def _mix_in(x, mod, mod_row0, g_pre1, w_in_ext, q_norm_g, kv_norm_g, wq, wqs, wukv, cos, sin, rope, tm):
    b, t, _ = x.shape
    ncol = w_in_ext.shape[1]
    const = lambda bi, i: (0, 0)
    tok = lambda bi, i: (bi, i, 0)
    table = (lambda bi, i: (i, 0)) if rope else const
    outs = [(C_CONV, F32), (N_HEADS * HEAD_W, BF16), (N_HEADS * HEAD_W, BF16), (N_HEADS * D_V, BF16),
            (KV_LORA, F32), (D_ROPE, F32)]
    return pl.pallas_call(
        functools.partial(_mix_in_kernel, rope),
        grid=(b, t // tm),
        in_specs=[pl.BlockSpec((1, tm, D_MODEL), tok),
                  pl.BlockSpec((1, 6, D_MODEL), lambda bi, i: (mod_row0 + bi, 0, 0)),
                  pl.BlockSpec((1, D_MODEL), const),
                  pl.BlockSpec((D_MODEL, ncol), const),
                  pl.BlockSpec((1, Q_LORA), const),
                  pl.BlockSpec((1, KV_LORA), const),
                  pl.BlockSpec((Q_LORA, N_HEADS * HEAD_W), const),
                  pl.BlockSpec((Q_LORA, N_HEADS * LANE), const),
                  pl.BlockSpec((KV_LORA, N_HEADS * (D_NOPE + D_V)), const),
                  pl.BlockSpec((tm, LANE), table),
                  pl.BlockSpec((tm, LANE), table)],
        out_specs=[pl.BlockSpec((1, tm, w), tok) for w, _ in outs],
        out_shape=[jax.ShapeDtypeStruct((b, t, w), dt) for w, dt in outs],
        compiler_params=_cparams("arbitrary", "arbitrary"),
        name="mix_in_rope" if rope else "mix_in",
    )(x, mod, g_pre1, w_in_ext, q_norm_g, kv_norm_g, wq, wqs, wukv, cos, sin)


def _conv_kernel(t, tb, tt, h_ref, w_ref, b_ref, g_ref, bb_ref, o_ref, pad_ref, sh_ref):
    zeros = jnp.zeros((CONV_HALO, C_CONV), F32)
    pad_ref[0:CONV_HALO, :] = zeros
    pad_ref[CONV_HALO + t:, :] = zeros
    pad_ref[CONV_HALO:CONV_HALO + t, :] = h_ref[0]
    first = CONV_HALO - CONV_K // 2
    rows = tb + 2 * CONV_HALO - SUBLANE

    def block(bi, carry):
        base = pl.multiple_of(bi * tb, tb)
        win = pad_ref[pl.ds(base, tb + 2 * CONV_HALO), :]
        for s in range(SUBLANE):
            sh_ref[s, 0:rows, :] = win[s:s + rows, :]
        for c in range(tb // tt):
            acc = jnp.zeros((tt, C_CONV), F32)
            for k in range(CONV_K):
                off = first + k
                r0 = c * tt + off // SUBLANE * SUBLANE
                acc = acc + sh_ref[off % SUBLANE, r0:r0 + tt, :] * w_ref[k:k + 1, :]
            y = acc + b_ref[...]
            mu = jnp.mean(y, axis=-1, keepdims=True)
            yc = y - mu
            var = jnp.mean(yc * yc, axis=-1, keepdims=True)
            z = yc * lax.rsqrt(var + EPS) * g_ref[...] + bb_ref[...]
            o_ref[0, pl.ds(base + c * tt, tt), :] = (z * _sigmoid(z)).astype(BF16)
        return carry

    lax.fori_loop(0, t // tb, block, 0)


def _conv(hglu, w_dw, b_dw, ln_g, ln_b):
    b, t, _ = hglu.shape
    tb, tt = 256, 32
    const = lambda bi: (0, 0)
    return pl.pallas_call(
        functools.partial(_conv_kernel, t, tb, tt),
        grid=(b,),
        in_specs=[pl.BlockSpec((1, t, C_CONV), lambda bi: (bi, 0, 0)),
                  pl.BlockSpec((CONV_K, C_CONV), const),
                  pl.BlockSpec((1, C_CONV), const),
                  pl.BlockSpec((1, C_CONV), const),
                  pl.BlockSpec((1, C_CONV), const)],
        out_specs=pl.BlockSpec((1, t, C_CONV), lambda bi: (bi, 0, 0)),
        out_shape=jax.ShapeDtypeStruct((b, t, C_CONV), BF16),
        scratch_shapes=[pltpu.VMEM((t + 2 * CONV_HALO, C_CONV), F32),
                        pltpu.VMEM((SUBLANE, tb + 2 * CONV_HALO, C_CONV), F32)],
        compiler_params=_cparams("arbitrary"),
        name="conv",
    )(hglu, w_dw, b_dw, ln_g, ln_b)


def _qk(q, k):
    return lax.dot_general(q, k, (((1,), (1,)), ((), ())), preferred_element_type=F32)


def _attn_kernel(cached, q_ref, k_ref, v_ref, *rest):
    if cached:
        kc_ref, vc_ref, o_ref = rest
    else:
        (o_ref,) = rest
    outs = []
    for hd in range(N_HEADS):
        ks = slice(hd * HEAD_W, (hd + 1) * HEAD_W)
        vs = slice(hd * D_V, (hd + 1) * D_V)
        qh = q_ref[0, :, ks]
        s = _qk(qh, k_ref[0, :, ks]) * ATT_SCALE
        m = jnp.max(s, axis=-1, keepdims=True)
        if cached:
            sc = _qk(qh, kc_ref[0, :, ks]) * ATT_SCALE
            m = jnp.maximum(m, jnp.max(sc, axis=-1, keepdims=True))
        p = jnp.exp(s - m)
        l = jnp.sum(p, axis=-1, keepdims=True)
        o = _dot(p.astype(BF16), v_ref[0, :, vs])
        if cached:
            pc = jnp.exp(sc - m)
            l = l + jnp.sum(pc, axis=-1, keepdims=True)
            o = o + _dot(pc.astype(BF16), vc_ref[0, :, vs])
        outs.append(o / l)
    o_ref[0] = jnp.concatenate(outs, axis=-1).astype(BF16)


def _attn(q, k, v, kc=None, vc=None, tq=256):
    b, t, _ = q.shape
    s = k.shape[1]
    cached = kc is not None
    whole = lambda bi, i: (bi, 0, 0)
    in_specs = [pl.BlockSpec((1, tq, N_HEADS * HEAD_W), lambda bi, i: (bi, i, 0)),
                pl.BlockSpec((1, s, N_HEADS * HEAD_W), whole),
                pl.BlockSpec((1, s, N_HEADS * D_V), whole)]
    args = [q, k, v]
    if cached:
        sc = kc.shape[1]
        in_specs += [pl.BlockSpec((1, sc, N_HEADS * HEAD_W), whole),
                     pl.BlockSpec((1, sc, N_HEADS * D_V), whole)]
        args += [kc, vc]
    return pl.pallas_call(
        functools.partial(_attn_kernel, cached),
        grid=(b, t // tq),
        in_specs=in_specs,
        out_specs=pl.BlockSpec((1, tq, N_HEADS * D_V), lambda bi, i: (bi, i, 0)),
        out_shape=jax.ShapeDtypeStruct((b, t, N_HEADS * D_V), BF16),
        compiler_params=_cparams("arbitrary", "arbitrary"),
        name="attn_cached" if cached else "attn",
    )(*args)


def _cache_kv_kernel(ckv_ref, kr_ref, wukv_ref, k_ref, v_ref):
    kvd = _dot(ckv_ref[0].astype(BF16), wukv_ref[...])
    kr = kr_ref[0]
    kr = jnp.concatenate([kr, jnp.zeros_like(kr)], axis=-1)
    parts = []
    for hd in range(N_HEADS):
        parts.append(kvd[:, hd * D_NOPE:(hd + 1) * D_NOPE])
        parts.append(kr)
    k_ref[0] = jnp.concatenate(parts, axis=-1).astype(BF16)
    v_ref[0] = kvd[:, N_HEADS * D_NOPE:].astype(BF16)


def _cache_kv(ckv, krope, wukv):
    b, s, _ = ckv.shape
    tok = lambda bi: (bi, 0, 0)
    return pl.pallas_call(
        _cache_kv_kernel,
        grid=(b,),
        in_specs=[pl.BlockSpec((1, s, KV_LORA), tok),
                  pl.BlockSpec((1, s, D_ROPE), tok),
                  pl.BlockSpec((KV_LORA, N_HEADS * (D_NOPE + D_V)), lambda bi: (0, 0))],
        out_specs=[pl.BlockSpec((1, s, N_HEADS * HEAD_W), tok),
                   pl.BlockSpec((1, s, N_HEADS * D_V), tok)],
        out_shape=[jax.ShapeDtypeStruct((b, s, N_HEADS * HEAD_W), BF16),
                   jax.ShapeDtypeStruct((b, s, N_HEADS * D_V), BF16)],
        compiler_params=_cparams("arbitrary"),
        name="cache_kv",
    )(ckv, krope, wukv)


def _post_kernel(x_ref, conv_ref, att_ref, mod_ref, wo_ref, gpost_ref, gpre2_ref, wr_hi_ref, wr_lo_ref,
                 br_ref, x1_ref, h2_ref, gates_ref):
    out = _dot(conv_ref[0], wo_ref[:C_CONV, :]) + _dot(att_ref[0], wo_ref[C_CONV:, :])
    gt1 = mod_ref[0, 2:3, :]
    sh2 = mod_ref[0, 3:4, :]
    sc2 = mod_ref[0, 4:5, :]
    x1 = x_ref[0] + gt1 * _rms(out, gpost_ref[...])
    x1_ref[0] = x1
    h2 = _rms(x1, gpre2_ref[...]) * (1.0 + sc2) + sh2
    h_hi = h2.astype(BF16)
    h2_ref[0] = h_hi
    h_lo = (h2 - h_hi.astype(F32)).astype(BF16)
    w_hi = wr_hi_ref[...]
    logits = _dot(h_hi, w_hi) + _dot(h_lo, w_hi) + _dot(h_hi, wr_lo_ref[...]) + br_ref[...]

    lane = lax.broadcasted_iota(jnp.int32, logits.shape, 1)
    neg = jnp.float32(-jnp.inf)
    big = jnp.int32(LANE)
    is_g = (lane >= N_EXPERTS) & (lane < N_EXPERTS + N_GROUPS)
    lg = jnp.where(is_g, logits, neg)
    gmax = jnp.max(lg, axis=-1, keepdims=True)
    gidx = jnp.min(jnp.where(lg == gmax, lane, big), axis=-1, keepdims=True) - N_EXPERTS
    g_top = 1.0 / jnp.sum(jnp.exp(lg - gmax), axis=-1, keepdims=True)

    in_grp = (lane >= gidx * E_PER_GROUP) & (lane < (gidx + 1) * E_PER_GROUP)
    le = jnp.where(in_grp, logits, neg)
    m1 = jnp.max(le, axis=-1, keepdims=True)
    i1 = jnp.min(jnp.where(le == m1, lane, big), axis=-1, keepdims=True)
    le2 = jnp.where(lane == i1, neg, le)
    m2 = jnp.max(le2, axis=-1, keepdims=True)
    i2 = jnp.min(jnp.where(le2 == m2, lane, big), axis=-1, keepdims=True)
    r = jnp.exp(m2 - m1)
    w1 = g_top / (1.0 + r)
    w2 = g_top * r / (1.0 + r)
    gates_ref[0] = jnp.where(lane == i1, w1, 0.0) + jnp.where(lane == i2, w2, 0.0)


def _post(x, conv_out, att, mod, mod_row0, w_out, g_post1, g_pre2, wr_hi, wr_lo, br, tm):
    b, t, _ = x.shape
    const = lambda bi, i: (0, 0)
    tok = lambda bi, i: (bi, i, 0)
    return pl.pallas_call(
        _post_kernel,
        grid=(b, t // tm),
        in_specs=[pl.BlockSpec((1, tm, D_MODEL), tok),
                  pl.BlockSpec((1, tm, C_CONV), tok),
                  pl.BlockSpec((1, tm, N_HEADS * D_V), tok),
                  pl.BlockSpec((1, 6, D_MODEL), lambda bi, i: (mod_row0 + bi, 0, 0)),
                  pl.BlockSpec((D_MODEL, D_MODEL), const),
                  pl.BlockSpec((1, D_MODEL), const),
                  pl.BlockSpec((1, D_MODEL), const),
                  pl.BlockSpec((D_MODEL, LANE), const),
                  pl.BlockSpec((D_MODEL, LANE), const),
                  pl.BlockSpec((1, LANE), const)],
        out_specs=[pl.BlockSpec((1, tm, D_MODEL), tok),
                   pl.BlockSpec((1, tm, D_MODEL), tok),
                   pl.BlockSpec((1, tm, LANE), tok)],
        out_shape=[jax.ShapeDtypeStruct((b, t, D_MODEL), F32),
                   jax.ShapeDtypeStruct((b, t, D_MODEL), BF16),
                   jax.ShapeDtypeStruct((b, t, LANE), F32)],
        compiler_params=_cparams("arbitrary", "arbitrary"),
        name="post",
    )(x, conv_out, att, mod, w_out, g_post1, g_pre2, wr_hi, wr_lo, br)


def _moe_kernel(x1_ref, h2_ref, gates_ref, mod_ref, wg_ref, wu_ref, wd_ref, gpost_ref, y_ref, acc_ref):
    e = pl.program_id(2)

    @pl.when(e == 0)
    def _():
        acc_ref[...] = jnp.zeros_like(acc_ref)

    h = h2_ref[0]
    a = _dot(h, wg_ref[0].astype(BF16))
    u = _dot(h, wu_ref[0].astype(BF16))
    he = (a * _sigmoid(a)) * u
    ye = _dot(he.astype(BF16), wd_ref[0].astype(BF16))
    gates = gates_ref[0]
    lane = lax.broadcasted_iota(jnp.int32, gates.shape, 1)
    ge = jnp.sum(jnp.where(lane == e, gates, 0.0), axis=-1, keepdims=True)
    acc_ref[...] += ge * ye

    @pl.when(e == N_EXPERTS - 1)
    def _():
        gt2 = mod_ref[0, 5:6, :]
        y_ref[0] = x1_ref[0] + gt2 * _rms(acc_ref[...], gpost_ref[...])


def _moe(x1, h2, gates, mod, mod_row0, w_gate, w_up, w_down, g_post2, tm):
    b, t, _ = x1.shape
    tok = lambda bi, i, e: (bi, i, 0)
    return pl.pallas_call(
        _moe_kernel,
        grid=(b, t // tm, N_EXPERTS),
        in_specs=[pl.BlockSpec((1, tm, D_MODEL), tok),
                  pl.BlockSpec((1, tm, D_MODEL), tok),
                  pl.BlockSpec((1, tm, LANE), tok),
                  pl.BlockSpec((1, 6, D_MODEL), lambda bi, i, e: (mod_row0 + bi, 0, 0)),
                  pl.BlockSpec((1, D_MODEL, D_EXPERT), lambda bi, i, e: (e, 0, 0)),
                  pl.BlockSpec((1, D_MODEL, D_EXPERT), lambda bi, i, e: (e, 0, 0)),
                  pl.BlockSpec((1, D_EXPERT, D_MODEL), lambda bi, i, e: (e, 0, 0)),
                  pl.BlockSpec((1, D_MODEL), lambda bi, i, e: (0, 0))],
        out_specs=pl.BlockSpec((1, tm, D_MODEL), tok),
        out_shape=jax.ShapeDtypeStruct((b, t, D_MODEL), F32),
        scratch_shapes=[pltpu.VMEM((tm, D_MODEL), F32)],
        compiler_params=_cparams("arbitrary", "arbitrary", "arbitrary"),
        name="moe",
    )(x1, h2, gates, mod, w_gate, w_up, w_down, g_post2)


def _rotate_half_cols(w):
    n = w.shape[-1]
    w4 = w.reshape(w.shape[:-1] + (n // 32, 2, 16))
    return jnp.stack([-w4[..., 1, :], w4[..., 0, :]], axis=-2).reshape(w.shape)


def _pad_lanes(w):
    return jnp.concatenate([w, jnp.zeros(w.shape[:-1] + (LANE - w.shape[-1],), w.dtype)], axis=-1)


def _rope_tables(t):
    rows = t // GRID_W
    n_freq = D_ROPE // 4
    freqs = ROPE_BASE ** (-jnp.arange(n_freq, dtype=F32) / n_freq)
    row = jnp.repeat(jnp.arange(rows, dtype=F32), GRID_W)
    col = jnp.tile(jnp.arange(GRID_W, dtype=F32), rows)
    ang_r = row[:, None] * freqs
    ang_c = col[:, None] * freqs
    cos = jnp.concatenate([jnp.cos(ang_r), jnp.cos(ang_r), jnp.cos(ang_c), jnp.cos(ang_c)], axis=-1)
    sin = jnp.concatenate([jnp.sin(ang_r), jnp.sin(ang_r), jnp.sin(ang_c), jnp.sin(ang_c)], axis=-1)
    return _pad_lanes(cos), _pad_lanes(sin)


def kernel(x_prompt, x_sample, cache_ckv, cache_krope, c, c_ctx, w_ada, b_ada, g_pre1, g_post1, g_pre2, g_post2, w_in, w_dw, b_dw, conv_ln_g, conv_ln_b, q_norm_g, kv_norm_g, w_uq, w_ukv, w_out, w_rg, b_rg, w_re, b_re, w_gate, w_up, w_down):
    nb, seq, d = x_prompt.shape
    db, dseq, _ = x_sample.shape
    l = 0

    cvec = jnp.concatenate([c_ctx[None, :], c, jnp.zeros((8 - 1 - db, d), F32)], axis=0)
    mod = _ada(cvec, w_ada[l], b_ada[l]).reshape(8, 6, d)

    o_kr = 2 * C_CONV + Q_LORA + KV_LORA
    w_in_l = w_in[l]
    w_kr = w_in_l[:, o_kr:]
    w_in_ctx = jnp.concatenate([w_in_l[:, :o_kr], _pad_lanes(w_kr)], axis=-1).astype(BF16)
    w_in_lat = jnp.concatenate([w_in_l[:, :o_kr], _pad_lanes(w_kr), _pad_lanes(_rotate_half_cols(w_kr))],
                               axis=-1).astype(BF16)
    wuq = w_uq[l].reshape(Q_LORA, N_HEADS, D_NOPE + D_ROPE)
    wq_rope = wuq[:, :, D_NOPE:]
    wq = jnp.concatenate([wuq[:, :, :D_NOPE], _pad_lanes(wq_rope)], axis=-1)
    wq = wq.reshape(Q_LORA, N_HEADS * HEAD_W).astype(BF16)
    wqs = _pad_lanes(_rotate_half_cols(wq_rope)).reshape(Q_LORA, N_HEADS * LANE).astype(BF16)
    wukv4 = w_ukv[l].reshape(KV_LORA, N_HEADS, D_NOPE + D_V)
    wukv = jnp.concatenate([wukv4[:, :, :D_NOPE].reshape(KV_LORA, N_HEADS * D_NOPE),
                            wukv4[:, :, D_NOPE:].reshape(KV_LORA, N_HEADS * D_V)], axis=-1).astype(BF16)
    w_out_b = w_out[l].astype(BF16)
    w_r = jnp.concatenate([w_re[l], w_rg[l], jnp.zeros((d, LANE - N_EXPERTS - N_GROUPS), F32)], axis=-1)
    wr_hi = w_r.astype(BF16)
    wr_lo = (w_r - wr_hi.astype(F32)).astype(BF16)
    b_r = jnp.concatenate([b_re[l], b_rg[l], jnp.zeros((LANE - N_EXPERTS - N_GROUPS,), F32)])[None, :]
    cos, sin = _rope_tables(dseq)
    row = lambda v: v[l][None, :]

    tm_c = 512
    xp_flat = x_prompt.reshape(1, nb * seq, d)
    hglu, q, k, v, ckv, kr = _mix_in(xp_flat, mod, 0, row(g_pre1), w_in_ctx, row(q_norm_g), row(kv_norm_g),
                                     wq, wqs, wukv, cos[:tm_c], sin[:tm_c], False, tm_c)
    per_seq = lambda a: a.reshape(nb, seq, a.shape[-1])
    conv_out = _conv(per_seq(hglu), w_dw[l], row(b_dw), row(conv_ln_g), row(conv_ln_b))
    att = _attn(per_seq(q), per_seq(k), per_seq(v))
    flat = lambda a: a.reshape(1, nb * seq, a.shape[-1])
    x1, h2, gates = _post(xp_flat, flat(conv_out), flat(att), mod, 0, w_out_b, row(g_post1), row(g_pre2),
                          wr_hi, wr_lo, b_r, tm_c)
    yp = _moe(x1, h2, gates, mod, 0, w_gate[l], w_up[l], w_down[l], row(g_post2), 1024)
    state_ckv = ckv.reshape(nb, 1, seq, KV_LORA)
    state_krope = kr.reshape(nb, 1, seq, D_ROPE)

    tm_s = 512
    kc, vc = _cache_kv(cache_ckv[:, l], cache_krope[:, l], wukv)
    hglu, q, k, v, _, _ = _mix_in(x_sample, mod, 1, row(g_pre1), w_in_lat, row(q_norm_g), row(kv_norm_g),
                                  wq, wqs, wukv, cos, sin, True, tm_s)
    conv_out = _conv(hglu, w_dw[l], row(b_dw), row(conv_ln_g), row(conv_ln_b))
    att = _attn(q, k, v, kc, vc)
    x1, h2, gates = _post(x_sample, conv_out, att, mod, 1, w_out_b, row(g_post1), row(g_pre2),
                          wr_hi, wr_lo, b_r, tm_s)
    ys = _moe(x1, h2, gates, mod, 1, w_gate[l], w_up[l], w_down[l], row(g_post2), 1024)

    return (yp.reshape(nb, seq, d), ys, state_ckv, state_krope)
```

```python
import functools

import jax
import jax.numpy as jnp
import numpy as np
from jax import lax
from jax.experimental import pallas as pl
from jax.experimental.pallas import tpu as pltpu

D_MODEL = 1024
GRID_W = 64
C_CONV = 512
CONV_K = 31
N_HEADS = 4
D_NOPE = 128
D_ROPE = 64
D_V = 128
Q_LORA = 384
KV_LORA = 256
N_GROUPS = 4
E_PER_GROUP = 8
N_EXPERTS = 32
D_EXPERT = 256
ROPE_BASE = 10000.0
EPS = 1e-6
ATT_SCALE = (D_NOPE + D_ROPE) ** -0.5

LANE = 128
SUBLANE = 8
HEAD_W = 2 * LANE
CONV_HALO = 16
MOE_TILE = 256
VMEM_LIMIT = 56 * 1024 * 1024

BF16 = jnp.bfloat16
F32 = jnp.float32


def _cparams(*sem):
    return pltpu.CompilerParams(dimension_semantics=sem, vmem_limit_bytes=VMEM_LIMIT)


def _rms(x, g):
    return x * lax.rsqrt(jnp.mean(x * x, axis=-1, keepdims=True) + EPS) * g


def _sigmoid(x):
    return 1.0 / (1.0 + jnp.exp(-x))


def _dot(a, b):
    return jnp.dot(a, b, preferred_element_type=F32)


def _store_token_tiles(ref2d, x):
    tm = x.shape[0]
    for c in range(SUBLANE):
        ref2d[pl.ds(c, tm, stride=SUBLANE), :] = x[:, c * LANE:(c + 1) * LANE]


def _load_token_tiles(ref2d, row0, tm):
    return jnp.concatenate([ref2d[pl.ds(row0 + c, tm, stride=SUBLANE), :] for c in range(SUBLANE)], axis=-1)


def _token_copy(src2d, src_row, dst2d, dst_row, sem):
    return pltpu.make_async_copy(src2d.at[pl.ds(pl.multiple_of(src_row, SUBLANE), SUBLANE), :],
                                 dst2d.at[pl.ds(pl.multiple_of(dst_row, SUBLANE), SUBLANE), :], sem)


def _ada_kernel(c_ref, w_ref, b_ref, o_ref):
    c = c_ref[...]
    s = (c * _sigmoid(c)).astype(BF16)
    o_ref[...] = _dot(s, w_ref[...].astype(BF16)) + b_ref[...]


def _ada(cvec, w_ada, b_ada):
    n = w_ada.shape[1]
    tn = 1536
    return pl.pallas_call(
        _ada_kernel,
        grid=(n // tn,),
        in_specs=[pl.BlockSpec((8, D_MODEL), lambda j: (0, 0)),
                  pl.BlockSpec((D_MODEL, tn), lambda j: (0, j)),
                  pl.BlockSpec((1, tn), lambda j: (0, j))],
        out_specs=pl.BlockSpec((8, tn), lambda j: (0, j)),
        out_shape=jax.ShapeDtypeStruct((8, n), F32),
        compiler_params=_cparams("arbitrary"),
        name="ada",
    )(cvec, w_ada, b_ada.reshape(1, n))


def _mix_in_kernel(rope, x_ref, mod_ref, gpre_ref, win_ref, qg_ref, kvg_ref, wq_ref, wqs_ref,
                   wukv_ref, cos_ref, sin_ref, hglu_ref, q_ref, k_ref, v_ref, ckv_ref, kr_ref):
    x = x_ref[0]
    sh1 = mod_ref[0, 0:1, :]
    sc1 = mod_ref[0, 1:2, :]
    h = _rms(x, gpre_ref[...]) * (1.0 + sc1) + sh1
    u = _dot(h.astype(BF16), win_ref[...])

    a = u[:, :C_CONV]
    g = u[:, C_CONV:2 * C_CONV]
    hglu_ref[0] = a * _sigmoid(g)

    o_q = 2 * C_CONV
    o_kv = o_q + Q_LORA
    o_kr = o_kv + KV_LORA
    qn = _rms(u[:, o_q:o_kv], qg_ref[...]).astype(BF16)
    qf = _dot(qn, wq_ref[...])
    ckv = _rms(u[:, o_kv:o_kr], kvg_ref[...])
    ckv_ref[0] = ckv
    kvd = _dot(ckv.astype(BF16), wukv_ref[...])
    kr = u[:, o_kr:o_kr + LANE]
    kr_ref[0] = kr[:, :D_ROPE]
    if rope:
        cos = cos_ref[...]
        sin = sin_ref[...]
        qs = _dot(qn, wqs_ref[...])
        kr = kr * cos + u[:, o_kr + LANE:o_kr + 2 * LANE] * sin
    q_parts = []
    k_parts = []
    for hd in range(N_HEADS):
        q_parts.append(qf[:, hd * HEAD_W:hd * HEAD_W + LANE])
        qr = qf[:, hd * HEAD_W + LANE:(hd + 1) * HEAD_W]
        if rope:
            qr = qr * cos + qs[:, hd * LANE:(hd + 1) * LANE] * sin
        q_parts.append(qr)
        k_parts.append(kvd[:, hd * D_NOPE:(hd + 1) * D_NOPE])
        k_parts.append(kr)
    q_ref[0] = jnp.concatenate(q_parts, axis=-1).astype(BF16)
    k_ref[0] = jnp.concatenate(k_parts, axis=-1).astype(BF16)
    v_ref[0] = kvd[:, N_HEADS * D_NOPE:].astype(BF16)


def _mix_in(x, mod, mod_row0, g_pre1, w_in_ext, q_norm_g, kv_norm_g, wq, wqs, wukv, cos, sin, rope, tm):
    b, t, _ = x.shape
    ncol = w_in_ext.shape[1]
    const = lambda bi, i: (0, 0)
    tok = lambda bi, i: (bi, i, 0)
    table = (lambda bi, i: (i, 0)) if rope else const
    outs = [(C_CONV, F32), (N_HEADS * HEAD_W, BF16), (N_HEADS * HEAD_W, BF16), (N_HEADS * D_V, BF16),
            (KV_LORA, F32), (D_ROPE, F32)]
    return pl.pallas_call(
        functools.partial(_mix_in_kernel, rope),
        grid=(b, t // tm),
        in_specs=[pl.BlockSpec((1, tm, D_MODEL), tok),
                  pl.BlockSpec((1, 6, D_MODEL), lambda bi, i: (mod_row0 + bi, 0, 0)),
                  pl.BlockSpec((1, D_MODEL), const),
                  pl.BlockSpec((D_MODEL, ncol), const),
                  pl.BlockSpec((1, Q_LORA), const),
                  pl.BlockSpec((1, KV_LORA), const),
                  pl.BlockSpec((Q_LORA, N_HEADS * HEAD_W), const),
                  pl.BlockSpec((Q_LORA, N_HEADS * LANE), const),
                  pl.BlockSpec((KV_LORA, N_HEADS * (D_NOPE + D_V)), const),
                  pl.BlockSpec((tm, LANE), table),
                  pl.BlockSpec((tm, LANE), table)],
        out_specs=[pl.BlockSpec((1, tm, w), tok) for w, _ in outs],
        out_shape=[jax.ShapeDtypeStruct((b, t, w), dt) for w, dt in outs],
        compiler_params=_cparams("arbitrary", "arbitrary"),
        name="mix_in_rope" if rope else "mix_in",
    )(x, mod, g_pre1, w_in_ext, q_norm_g, kv_norm_g, wq, wqs, wukv, cos, sin)


def _conv_kernel(t, tb, tt, h_ref, w_ref, b_ref, g_ref, bb_ref, o_ref, pad_ref, sh_ref):
    zeros = jnp.zeros((CONV_HALO, C_CONV), F32)
    pad_ref[0:CONV_HALO, :] = zeros
    pad_ref[CONV_HALO + t:, :] = zeros
    pad_ref[CONV_HALO:CONV_HALO + t, :] = h_ref[0]
    first = CONV_HALO - CONV_K // 2
    rows = tb + 2 * CONV_HALO - SUBLANE

    def block(bi, carry):
        base = pl.multiple_of(bi * tb, tb)
        win = pad_ref[pl.ds(base, tb + 2 * CONV_HALO), :]
        for s in range(SUBLANE):
            sh_ref[s, 0:rows, :] = win[s:s + rows, :]
        for c in range(tb // tt):
            acc = jnp.zeros((tt, C_CONV), F32)
            for k in range(CONV_K):
                off = first + k
                r0 = c * tt + off // SUBLANE * SUBLANE
                acc = acc + sh_ref[off % SUBLANE, r0:r0 + tt, :] * w_ref[k:k + 1, :]
            y = acc + b_ref[...]
            mu = jnp.mean(y, axis=-1, keepdims=True)
            yc = y - mu
            var = jnp.mean(yc * yc, axis=-1, keepdims=True)
            z = yc * lax.rsqrt(var + EPS) * g_ref[...] + bb_ref[...]
            o_ref[0, pl.ds(base + c * tt, tt), :] = (z * _sigmoid(z)).astype(BF16)
        return carry

    lax.fori_loop(0, t // tb, block, 0)


def _conv(hglu, w_dw, b_dw, ln_g, ln_b):
    b, t, _ = hglu.shape
    tb, tt = 256, 32
    const = lambda bi: (0, 0)
    return pl.pallas_call(
        functools.partial(_conv_kernel, t, tb, tt),
        grid=(b,),
        in_specs=[pl.BlockSpec((1, t, C_CONV), lambda bi: (bi, 0, 0)),
                  pl.BlockSpec((CONV_K, C_CONV), const),
                  pl.BlockSpec((1, C_CONV), const),
                  pl.BlockSpec((1, C_CONV), const),
                  pl.BlockSpec((1, C_CONV), const)],
        out_specs=pl.BlockSpec((1, t, C_CONV), lambda bi: (bi, 0, 0)),
        out_shape=jax.ShapeDtypeStruct((b, t, C_CONV), BF16),
        scratch_shapes=[pltpu.VMEM((t + 2 * CONV_HALO, C_CONV), F32),
                        pltpu.VMEM((SUBLANE, tb + 2 * CONV_HALO, C_CONV), F32)],
        compiler_params=_cparams("arbitrary"),
        name="conv",
    )(hglu, w_dw, b_dw, ln_g, ln_b)


def _qk(q, k):
    return lax.dot_general(q, k, (((1,), (1,)), ((), ())), preferred_element_type=F32)


def _attn_kernel(cached, q_ref, k_ref, v_ref, *rest):
    if cached:
        kc_ref, vc_ref, o_ref = rest
    else:
        (o_ref,) = rest
    outs = []
    for hd in range(N_HEADS):
        ks = slice(hd * HEAD_W, (hd + 1) * HEAD_W)
        vs = slice(hd * D_V, (hd + 1) * D_V)
        qh = q_ref[0, :, ks]
        s = _qk(qh, k_ref[0, :, ks]) * ATT_SCALE
        m = jnp.max(s, axis=-1, keepdims=True)
        if cached:
            sc = _qk(qh, kc_ref[0, :, ks]) * ATT_SCALE
            m = jnp.maximum(m, jnp.max(sc, axis=-1, keepdims=True))
        p = jnp.exp(s - m)
        l = jnp.sum(p, axis=-1, keepdims=True)
        o = _dot(p.astype(BF16), v_ref[0, :, vs])
        if cached:
            pc = jnp.exp(sc - m)
            l = l + jnp.sum(pc, axis=-1, keepdims=True)
            o = o + _dot(pc.astype(BF16), vc_ref[0, :, vs])
        outs.append(o / l)
    o_ref[0] = jnp.concatenate(outs, axis=-1).astype(BF16)


def _attn(q, k, v, kc=None, vc=None, tq=256):
    b, t, _ = q.shape
    s = k.shape[1]
    cached = kc is not None
    whole = lambda bi, i: (bi, 0, 0)
    in_specs = [pl.BlockSpec((1, tq, N_HEADS * HEAD_W), lambda bi, i: (bi, i, 0)),
                pl.BlockSpec((1, s, N_HEADS * HEAD_W), whole),
                pl.BlockSpec((1, s, N_HEADS * D_V), whole)]
    args = [q, k, v]
    if cached:
        sc = kc.shape[1]
        in_specs += [pl.BlockSpec((1, sc, N_HEADS * HEAD_W), whole),
                     pl.BlockSpec((1, sc, N_HEADS * D_V), whole)]
        args += [kc, vc]
    return pl.pallas_call(
        functools.partial(_attn_kernel, cached),
        grid=(b, t // tq),
        in_specs=in_specs,
        out_specs=pl.BlockSpec((1, tq, N_HEADS * D_V), lambda bi, i: (bi, i, 0)),
        out_shape=jax.ShapeDtypeStruct((b, t, N_HEADS * D_V), BF16),
        compiler_params=_cparams("arbitrary", "arbitrary"),
        name="attn_cached" if cached else "attn",
    )(*args)


def _cache_kv_kernel(ckv_ref, kr_ref, wukv_ref, k_ref, v_ref):
    kvd = _dot(ckv_ref[0].astype(BF16), wukv_ref[...])
    kr = kr_ref[0]
    kr = jnp.concatenate([kr, jnp.zeros_like(kr)], axis=-1)
    parts = []
    for hd in range(N_HEADS):
        parts.append(kvd[:, hd * D_NOPE:(hd + 1) * D_NOPE])
        parts.append(kr)
    k_ref[0] = jnp.concatenate(parts, axis=-1).astype(BF16)
    v_ref[0] = kvd[:, N_HEADS * D_NOPE:].astype(BF16)


def _cache_kv(ckv, krope, wukv):
    b, s, _ = ckv.shape
    tok = lambda bi: (bi, 0, 0)
    return pl.pallas_call(
        _cache_kv_kernel,
        grid=(b,),
        in_specs=[pl.BlockSpec((1, s, KV_LORA), tok),
                  pl.BlockSpec((1, s, D_ROPE), tok),
                  pl.BlockSpec((KV_LORA, N_HEADS * (D_NOPE + D_V)), lambda bi: (0, 0))],
        out_specs=[pl.BlockSpec((1, s, N_HEADS * HEAD_W), tok),
                   pl.BlockSpec((1, s, N_HEADS * D_V), tok)],
        out_shape=[jax.ShapeDtypeStruct((b, s, N_HEADS * HEAD_W), BF16),
                   jax.ShapeDtypeStruct((b, s, N_HEADS * D_V), BF16)],
        compiler_params=_cparams("arbitrary"),
        name="cache_kv",
    )(ckv, krope, wukv)


def _post_kernel(x_ref, conv_ref, att_ref, mod_ref, wo_ref, gpost_ref, gpre2_ref, wr_hi_ref, wr_lo_ref,
                 br_ref, x1_ref, h2t_ref, ri_ref, rw_ref):
    out = _dot(conv_ref[0], wo_ref[:C_CONV, :]) + _dot(att_ref[0], wo_ref[C_CONV:, :])
    gt1 = mod_ref[0, 2:3, :]
    sh2 = mod_ref[0, 3:4, :]
    sc2 = mod_ref[0, 4:5, :]
    x1 = x_ref[0] + gt1 * _rms(out, gpost_ref[...])
    x1_ref[0] = x1
    h2 = _rms(x1, gpre2_ref[...]) * (1.0 + sc2) + sh2
    h_hi = h2.astype(BF16)
    _store_token_tiles(h2t_ref.at[0], h2)
    h_lo = (h2 - h_hi.astype(F32)).astype(BF16)
    w_hi = wr_hi_ref[...]
    logits = _dot(h_hi, w_hi) + _dot(h_lo, w_hi) + _dot(h_hi, wr_lo_ref[...]) + br_ref[...]

    lane = lax.broadcasted_iota(jnp.int32, logits.shape, 1)
    neg = jnp.float32(-jnp.inf)
    big = jnp.int32(LANE)
    is_g = (lane >= N_EXPERTS) & (lane < N_EXPERTS + N_GROUPS)
    lg = jnp.where(is_g, logits, neg)
    gmax = jnp.max(lg, axis=-1, keepdims=True)
    gidx = jnp.min(jnp.where(lg == gmax, lane, big), axis=-1, keepdims=True) - N_EXPERTS
    g_top = 1.0 / jnp.sum(jnp.exp(lg - gmax), axis=-1, keepdims=True)

    in_grp = (lane >= gidx * E_PER_GROUP) & (lane < (gidx + 1) * E_PER_GROUP)
    le = jnp.where(in_grp, logits, neg)
    m1 = jnp.max(le, axis=-1, keepdims=True)
    i1 = jnp.min(jnp.where(le == m1, lane, big), axis=-1, keepdims=True)
    le2 = jnp.where(lane == i1, neg, le)
    m2 = jnp.max(le2, axis=-1, keepdims=True)
    i2 = jnp.min(jnp.where(le2 == m2, lane, big), axis=-1, keepdims=True)
    r = jnp.exp(m2 - m1)
    w1 = g_top / (1.0 + r)
    w2 = g_top * r / (1.0 + r)
    ri_ref[0] = jnp.where(lane == 0, i1, jnp.where(lane == 1, i2, 0))
    rw_ref[0] = jnp.where(lane == 0, w1, jnp.where(lane == 1, w2, 0.0))


def _post(x, conv_out, att, mod, mod_row0, w_out, g_post1, g_pre2, wr_hi, wr_lo, br, tm):
    b, t, _ = x.shape
    const = lambda bi, i: (0, 0)
    tok = lambda bi, i: (bi, i, 0)
    return pl.pallas_call(
        _post_kernel,
        grid=(b, t // tm),
        in_specs=[pl.BlockSpec((1, tm, D_MODEL), tok),
                  pl.BlockSpec((1, tm, C_CONV), tok),
                  pl.BlockSpec((1, tm, N_HEADS * D_V), tok),
                  pl.BlockSpec((1, 6, D_MODEL), lambda bi, i: (mod_row0 + bi, 0, 0)),
                  pl.BlockSpec((D_MODEL, D_MODEL), const),
                  pl.BlockSpec((1, D_MODEL), const),
                  pl.BlockSpec((1, D_MODEL), const),
                  pl.BlockSpec((D_MODEL, LANE), const),
                  pl.BlockSpec((D_MODEL, LANE), const),
                  pl.BlockSpec((1, LANE), const)],
        out_specs=[pl.BlockSpec((1, tm, D_MODEL), tok),
                   pl.BlockSpec((1, tm * SUBLANE, LANE), tok),
                   pl.BlockSpec((1, tm, LANE), tok),
                   pl.BlockSpec((1, tm, LANE), tok)],
        out_shape=[jax.ShapeDtypeStruct((b, t, D_MODEL), F32),
                   jax.ShapeDtypeStruct((b, t * SUBLANE, LANE), F32),
                   jax.ShapeDtypeStruct((b, t, LANE), jnp.int32),
                   jax.ShapeDtypeStruct((b, t, LANE), F32)],
        compiler_params=_cparams("arbitrary", "arbitrary"),
        name="post",
    )(x, conv_out, att, mod, w_out, g_post1, g_pre2, wr_hi, wr_lo, br)


def _rank_kernel(ri_ref, rank_ref, cnt_ref, carry_ref):
    @pl.when(pl.program_id(0) == 0)
    def _():
        carry_ref[...] = jnp.zeros_like(carry_ref)

    ri = ri_ref[...]
    tr = ri.shape[0]
    lane = lax.broadcasted_iota(jnp.int32, ri.shape, 1)
    oh1 = lane == ri[:, 0:1]
    oh2 = lane == ri[:, 1:2]
    oh = jnp.where(oh1 | oh2, 1.0, 0.0)
    rr = lax.broadcasted_iota(jnp.int32, (tr, tr), 0)
    cc = lax.broadcasted_iota(jnp.int32, (tr, tr), 1)
    earlier = jnp.where(cc < rr, 1.0, 0.0).astype(BF16)
    before = _dot(earlier, oh.astype(BF16)) + carry_ref[0:1, :]
    r1 = jnp.sum(jnp.where(oh1, before, 0.0), axis=-1, keepdims=True)
    r2 = jnp.sum(jnp.where(oh2, before, 0.0), axis=-1, keepdims=True)
    rank_ref[...] = jnp.where(lane == 0, r1, jnp.where(lane == 1, r2, 0.0)).astype(jnp.int32)
    carry_ref[0:1, :] = carry_ref[0:1, :] + jnp.sum(oh, axis=0, keepdims=True)
    cnt_ref[...] = jnp.broadcast_to(carry_ref[0:1, :], cnt_ref.shape)


def _rank(ri, tr):
    n = ri.shape[0]
    return pl.pallas_call(
        _rank_kernel,
        grid=(n // tr,),
        in_specs=[pl.BlockSpec((tr, LANE), lambda i: (i, 0))],
        out_specs=[pl.BlockSpec((tr, LANE), lambda i: (i, 0)),
                   pl.BlockSpec((SUBLANE, LANE), lambda i: (0, 0))],
        out_shape=[jax.ShapeDtypeStruct((n, LANE), jnp.int32),
                   jax.ShapeDtypeStruct((SUBLANE, LANE), F32)],
        scratch_shapes=[pltpu.VMEM((SUBLANE, LANE), F32)],
        compiler_params=_cparams("arbitrary"),
        name="moe_rank",
    )(ri)


def _plan(ri, rank, counts, tile, n_tiles_max):
    n = ri.shape[0]
    cnt = counts[0, :N_EXPERTS].astype(jnp.int32)
    padded = (cnt + tile - 1) // tile * tile
    ends = jnp.cumsum(padded)
    starts = ends - padded
    pos = starts[ri[:, :2]] + rank[:, :2]
    n_tiles = ends[-1] // tile
    tile_row0 = jnp.arange(n_tiles_max, dtype=jnp.int32) * tile
    tile_expert = jnp.sum((ends[None, :] <= tile_row0[:, None]).astype(jnp.int32), axis=1)
    tile_expert = jnp.minimum(tile_expert, N_EXPERTS - 1)
    last = tile_expert[jnp.maximum(n_tiles - 1, 0)]
    tile_expert = jnp.where(jnp.arange(n_tiles_max) < n_tiles, tile_expert, last)
    tok = jnp.repeat(jnp.arange(n, dtype=jnp.int32), 2)
    src = jnp.zeros((n_tiles_max * tile,), jnp.int32).at[pos.reshape(-1)].set(tok, unique_indices=True)
    pos = pos * SUBLANE
    return pos[:, 0], pos[:, 1], src * SUBLANE, tile_expert, n_tiles.reshape(1).astype(jnp.int32)


def _moe_gemm_kernel(tile, te_ref, nt_ref, src_ref, srcn_ref, h2_hbm, wg_ref, wu_ref, wd_ref, y_ref, xbuf, sem):
    j = pl.program_id(0)
    nt = nt_ref[0]
    slot = j % 2

    rows = tile * SUBLANE

    def gather(idx_ref, s):
        for r in range(tile):
            _token_copy(h2_hbm, idx_ref[r], xbuf, s * rows + r * SUBLANE, sem.at[s]).start()

    @pl.when(j == 0)
    def _():
        gather(src_ref, 0)

    @pl.when(j + 1 < nt)
    def _():
        gather(srcn_ref, 1 - slot)

    @pl.when(j < nt)
    def _():
        base = pl.multiple_of(slot * rows, rows)
        pltpu.make_async_copy(h2_hbm.at[pl.ds(0, rows), :], xbuf.at[pl.ds(base, rows), :], sem.at[slot]).wait()
        x = _load_token_tiles(xbuf, base, tile).astype(BF16)
        a = _dot(x, wg_ref[0].astype(BF16))
        u = _dot(x, wu_ref[0].astype(BF16))
        he = (a * _sigmoid(a)) * u
        y = _dot(he.astype(BF16), wd_ref[0].astype(BF16))
        _store_token_tiles(y_ref, y)

    @pl.when(j >= nt)
    def _():
        y_ref[...] = jnp.zeros_like(y_ref)


def _moe_gemm(src, tile_expert, n_tiles, h2t, w_gate, w_up, w_down, tile):
    n_tiles_max = tile_expert.shape[0]
    clamp = lambda j, nt: jnp.minimum(j, jnp.maximum(nt[0] - 1, 0))
    wmap = lambda j, te, nt: (te[j], 0, 0)
    return pl.pallas_call(
        functools.partial(_moe_gemm_kernel, tile),
        grid_spec=pltpu.PrefetchScalarGridSpec(
            num_scalar_prefetch=2,
            grid=(n_tiles_max,),
            in_specs=[pl.BlockSpec((tile,), lambda j, te, nt: (clamp(j, nt),), memory_space=pltpu.SMEM),
                      pl.BlockSpec((tile,), lambda j, te, nt: (clamp(j + 1, nt),), memory_space=pltpu.SMEM),
                      pl.BlockSpec(memory_space=pl.ANY),
                      pl.BlockSpec((1, D_MODEL, D_EXPERT), wmap),
                      pl.BlockSpec((1, D_MODEL, D_EXPERT), wmap),
                      pl.BlockSpec((1, D_EXPERT, D_MODEL), wmap)],
            out_specs=pl.BlockSpec((tile * SUBLANE, LANE), lambda j, te, nt: (j, 0)),
            scratch_shapes=[pltpu.VMEM((2 * tile * SUBLANE, LANE), F32), pltpu.SemaphoreType.DMA((2,))]),
        out_shape=jax.ShapeDtypeStruct((n_tiles_max * tile * SUBLANE, LANE), F32),
        compiler_params=_cparams("arbitrary"),
        name="moe_gemm",
    )(tile_expert, n_tiles, src, src, h2t, w_gate, w_up, w_down)


def _moe_combine_kernel(tm, p1_ref, p2_ref, p1n_ref, p2n_ref, rw_ref, x1_ref, mod_ref, gpost_ref, y_hbm, o_ref,
                        ybuf, sem):
    i = pl.program_id(0)
    n = pl.num_programs(0)
    slot = i % 2

    rows = tm * SUBLANE

    def gather(a_ref, b_ref, s):
        for k, p_ref in enumerate((a_ref, b_ref)):
            for r in range(tm):
                _token_copy(y_hbm, p_ref[r], ybuf, (2 * s + k) * rows + r * SUBLANE, sem.at[s]).start()

    @pl.when(i == 0)
    def _():
        gather(p1_ref, p2_ref, 0)

    @pl.when(i + 1 < n)
    def _():
        gather(p1n_ref, p2n_ref, 1 - slot)

    base = pl.multiple_of(2 * slot * rows, rows)
    pltpu.make_async_copy(y_hbm.at[pl.ds(0, 2 * rows), :], ybuf.at[pl.ds(base, 2 * rows), :], sem.at[slot]).wait()
    y1 = _load_token_tiles(ybuf, base, tm)
    y2 = _load_token_tiles(ybuf, base + rows, tm)
    rw = rw_ref[...]
    moe = rw[:, 0:1] * y1 + rw[:, 1:2] * y2
    gt2 = mod_ref[0, 5:6, :]
    o_ref[...] = x1_ref[...] + gt2 * _rms(moe, gpost_ref[...])


def _moe_combine(pos1, pos2, rw, x1, mod, mod_row, y_sorted, g_post2, tm):
    n = x1.shape[0]
    nsteps = n // tm
    cur = lambda i: (i,)
    nxt = lambda i: (jnp.minimum(i + 1, nsteps - 1),)
    smem = lambda f: pl.BlockSpec((tm,), f, memory_space=pltpu.SMEM)
    return pl.pallas_call(
        functools.partial(_moe_combine_kernel, tm),
        grid=(nsteps,),
        in_specs=[smem(cur), smem(cur), smem(nxt), smem(nxt),
                  pl.BlockSpec((tm, LANE), lambda i: (i, 0)),
                  pl.BlockSpec((tm, D_MODEL), lambda i: (i, 0)),
                  pl.BlockSpec((1, 6, D_MODEL), lambda i: (mod_row(i), 0, 0)),
                  pl.BlockSpec((1, D_MODEL), lambda i: (0, 0)),
                  pl.BlockSpec(memory_space=pl.ANY)],
        out_specs=pl.BlockSpec((tm, D_MODEL), lambda i: (i, 0)),
        out_shape=jax.ShapeDtypeStruct((n, D_MODEL), F32),
        scratch_shapes=[pltpu.VMEM((4 * tm * SUBLANE, LANE), F32), pltpu.SemaphoreType.DMA((2,))],
        compiler_params=_cparams("arbitrary"),
        name="moe_combine",
    )(pos1, pos2, pos1, pos2, rw, x1, mod, g_post2, y_sorted)


def _moe(x1, h2t, ri, rw, mod, mod_row, w_gate, w_up, w_down, g_post2):
    n = x1.shape[0]
    tile = MOE_TILE
    n_tiles_max = (2 * n + N_EXPERTS * (tile - 1)) // tile + 1
    rank, counts = _rank(ri, 512)
    pos1, pos2, src, tile_expert, n_tiles = _plan(ri, rank, counts, tile, n_tiles_max)
    y_sorted = _moe_gemm(src, tile_expert, n_tiles, h2t, w_gate, w_up, w_down, tile)
    return _moe_combine(pos1, pos2, rw, x1, mod, mod_row, y_sorted, g_post2, MOE_TILE)


def _rotate_half_cols(w):
    n = w.shape[-1]
    w4 = w.reshape(w.shape[:-1] + (n // 32, 2, 16))
    return jnp.stack([-w4[..., 1, :], w4[..., 0, :]], axis=-2).reshape(w.shape)


def _pad_lanes(w):
    return jnp.concatenate([w, jnp.zeros(w.shape[:-1] + (LANE - w.shape[-1],), w.dtype)], axis=-1)


def _rope_tables(t):
    rows = t // GRID_W
    n_freq = D_ROPE // 4
    freqs = ROPE_BASE ** (-jnp.arange(n_freq, dtype=F32) / n_freq)
    row = jnp.repeat(jnp.arange(rows, dtype=F32), GRID_W)
    col = jnp.tile(jnp.arange(GRID_W, dtype=F32), rows)
    ang_r = row[:, None] * freqs
    ang_c = col[:, None] * freqs
    cos = jnp.concatenate([jnp.cos(ang_r), jnp.cos(ang_r), jnp.cos(ang_c), jnp.cos(ang_c)], axis=-1)
    sin = jnp.concatenate([jnp.sin(ang_r), jnp.sin(ang_r), jnp.sin(ang_c), jnp.sin(ang_c)], axis=-1)
    return _pad_lanes(cos), _pad_lanes(sin)


def kernel(x_prompt, x_sample, cache_ckv, cache_krope, c, c_ctx, w_ada, b_ada, g_pre1, g_post1, g_pre2, g_post2, w_in, w_dw, b_dw, conv_ln_g, conv_ln_b, q_norm_g, kv_norm_g, w_uq, w_ukv, w_out, w_rg, b_rg, w_re, b_re, w_gate, w_up, w_down):
    nb, seq, d = x_prompt.shape
    db, dseq, _ = x_sample.shape
    l = 0

    cvec = jnp.concatenate([c_ctx[None, :], c, jnp.zeros((8 - 1 - db, d), F32)], axis=0)
    mod = _ada(cvec, w_ada[l], b_ada[l]).reshape(8, 6, d)

    o_kr = 2 * C_CONV + Q_LORA + KV_LORA
    w_in_l = w_in[l]
    w_kr = w_in_l[:, o_kr:]
    w_in_ctx = jnp.concatenate([w_in_l[:, :o_kr], _pad_lanes(w_kr)], axis=-1).astype(BF16)
    w_in_lat = jnp.concatenate([w_in_l[:, :o_kr], _pad_lanes(w_kr), _pad_lanes(_rotate_half_cols(w_kr))],
                               axis=-1).astype(BF16)
    wuq = w_uq[l].reshape(Q_LORA, N_HEADS, D_NOPE + D_ROPE)
    wq_rope = wuq[:, :, D_NOPE:]
    wq = jnp.concatenate([wuq[:, :, :D_NOPE], _pad_lanes(wq_rope)], axis=-1)
    wq = wq.reshape(Q_LORA, N_HEADS * HEAD_W).astype(BF16)
    wqs = _pad_lanes(_rotate_half_cols(wq_rope)).reshape(Q_LORA, N_HEADS * LANE).astype(BF16)
    wukv4 = w_ukv[l].reshape(KV_LORA, N_HEADS, D_NOPE + D_V)
    wukv = jnp.concatenate([wukv4[:, :, :D_NOPE].reshape(KV_LORA, N_HEADS * D_NOPE),
                            wukv4[:, :, D_NOPE:].reshape(KV_LORA, N_HEADS * D_V)], axis=-1).astype(BF16)
    w_out_b = w_out[l].astype(BF16)
    w_r = jnp.concatenate([w_re[l], w_rg[l], jnp.zeros((d, LANE - N_EXPERTS - N_GROUPS), F32)], axis=-1)
    wr_hi = w_r.astype(BF16)
    wr_lo = (w_r - wr_hi.astype(F32)).astype(BF16)
    b_r = jnp.concatenate([b_re[l], b_rg[l], jnp.zeros((LANE - N_EXPERTS - N_GROUPS,), F32)])[None, :]
    cos, sin = _rope_tables(dseq)
    row = lambda v: v[l][None, :]

    tm_c = 512
    xp_flat = x_prompt.reshape(1, nb * seq, d)
    hglu, q, k, v, ckv, kr = _mix_in(xp_flat, mod, 0, row(g_pre1), w_in_ctx, row(q_norm_g), row(kv_norm_g),
                                     wq, wqs, wukv, cos[:tm_c], sin[:tm_c], False, tm_c)
    per_seq = lambda a: a.reshape(nb, seq, a.shape[-1])
    conv_out = _conv(per_seq(hglu), w_dw[l], row(b_dw), row(conv_ln_g), row(conv_ln_b))
    att = _attn(per_seq(q), per_seq(k), per_seq(v))
    flat = lambda a: a.reshape(1, nb * seq, a.shape[-1])
    x1, h2t, ri, rw = _post(xp_flat, flat(conv_out), flat(att), mod, 0, w_out_b, row(g_post1), row(g_pre2),
                            wr_hi, wr_lo, b_r, tm_c)
    n_c = nb * seq
    yp = _moe(x1.reshape(n_c, d), h2t.reshape(n_c * SUBLANE, LANE), ri.reshape(n_c, LANE), rw.reshape(n_c, LANE),
              mod, lambda i: 0, w_gate[l], w_up[l], w_down[l], row(g_post2))
    state_ckv = ckv.reshape(nb, 1, seq, KV_LORA)
    state_krope = kr.reshape(nb, 1, seq, D_ROPE)

    tm_s = 512
    kc, vc = _cache_kv(cache_ckv[:, l], cache_krope[:, l], wukv)
    hglu, q, k, v, _, _ = _mix_in(x_sample, mod, 1, row(g_pre1), w_in_lat, row(q_norm_g), row(kv_norm_g),
                                  wq, wqs, wukv, cos, sin, True, tm_s)
    conv_out = _conv(hglu, w_dw[l], row(b_dw), row(conv_ln_g), row(conv_ln_b))
    att = _attn(q, k, v, kc, vc)
    x1, h2t, ri, rw = _post(x_sample, conv_out, att, mod, 1, w_out_b, row(g_post1), row(g_pre2),
                            wr_hi, wr_lo, b_r, tm_s)
    n_s = db * dseq
    tiles_per_req = dseq // MOE_TILE
    ys = _moe(x1.reshape(n_s, d), h2t.reshape(n_s * SUBLANE, LANE), ri.reshape(n_s, LANE), rw.reshape(n_s, LANE),
              mod, lambda i: 1 + i // tiles_per_req, w_gate[l], w_up[l], w_down[l], row(g_post2))

    return (yp.reshape(nb, seq, d), ys.reshape(db, dseq, d), state_ckv, state_krope)
```

```python
import functools

import jax
import jax.numpy as jnp
import numpy as np
from jax import lax
from jax.experimental import pallas as pl
from jax.experimental.pallas import tpu as pltpu

D_MODEL = 1024
GRID_W = 64
C_CONV = 512
CONV_K = 31
N_HEADS = 4
D_NOPE = 128
D_ROPE = 64
D_V = 128
Q_LORA = 384
KV_LORA = 256
N_GROUPS = 4
E_PER_GROUP = 8
N_EXPERTS = 32
D_EXPERT = 256
ROPE_BASE = 10000.0
EPS = 1e-6
ATT_SCALE = (D_NOPE + D_ROPE) ** -0.5

LANE = 128
SUBLANE = 8
HEAD_W = 2 * LANE
CONV_HALO = 16
MOE_TILE = 256
VMEM_LIMIT = 56 * 1024 * 1024

BF16 = jnp.bfloat16
F32 = jnp.float32


def _cparams(*sem):
    return pltpu.CompilerParams(dimension_semantics=sem, vmem_limit_bytes=VMEM_LIMIT)


def _rms(x, g):
    return x * lax.rsqrt(jnp.mean(x * x, axis=-1, keepdims=True) + EPS) * g


def _sigmoid(x):
    return 1.0 / (1.0 + jnp.exp(-x))


def _dot(a, b):
    return jnp.dot(a, b, preferred_element_type=F32)


def _store_token_tiles(ref2d, x):
    tm = x.shape[0]
    for c in range(SUBLANE):
        ref2d[pl.ds(c, tm, stride=SUBLANE), :] = x[:, c * LANE:(c + 1) * LANE]


def _load_token_tiles(ref2d, row0, tm):
    return jnp.concatenate([ref2d[pl.ds(row0 + c, tm, stride=SUBLANE), :] for c in range(SUBLANE)], axis=-1)


def _token_copy(src2d, src_row, dst2d, dst_row, sem):
    return pltpu.make_async_copy(src2d.at[pl.ds(pl.multiple_of(src_row, SUBLANE), SUBLANE), :],
                                 dst2d.at[pl.ds(pl.multiple_of(dst_row, SUBLANE), SUBLANE), :], sem)


def _ada_kernel(c_ref, w_ref, b_ref, o_ref):
    c = c_ref[...]
    s = (c * _sigmoid(c)).astype(BF16)
    o_ref[...] = _dot(s, w_ref[...].astype(BF16)) + b_ref[...]


def _ada(cvec, w_ada, b_ada):
    n = w_ada.shape[1]
    tn = 1536
    return pl.pallas_call(
        _ada_kernel,
        grid=(n // tn,),
        in_specs=[pl.BlockSpec((8, D_MODEL), lambda j: (0, 0)),
                  pl.BlockSpec((D_MODEL, tn), lambda j: (0, j)),
                  pl.BlockSpec((1, tn), lambda j: (0, j))],
        out_specs=pl.BlockSpec((8, tn), lambda j: (0, j)),
        out_shape=jax.ShapeDtypeStruct((8, n), F32),
        compiler_params=_cparams("arbitrary"),
        name="ada",
    )(cvec, w_ada, b_ada.reshape(1, n))


def _mix_in_kernel(rope, x_ref, mod_ref, gpre_ref, win_ref, qg_ref, kvg_ref, wq_ref, wqs_ref,
                   wukv_ref, cos_ref, sin_ref, hglu_ref, q_ref, k_ref, v_ref, ckv_ref, kr_ref):
    x = x_ref[0]
    sh1 = mod_ref[0, 0:1, :]
    sc1 = mod_ref[0, 1:2, :]
    h = _rms(x, gpre_ref[...]) * (1.0 + sc1) + sh1
    u = _dot(h.astype(BF16), win_ref[...])

    a = u[:, :C_CONV]
    g = u[:, C_CONV:2 * C_CONV]
    hglu_ref[0] = a * _sigmoid(g)

    o_q = 2 * C_CONV
    o_kv = o_q + Q_LORA
    o_kr = o_kv + KV_LORA
    qn = _rms(u[:, o_q:o_kv], qg_ref[...]).astype(BF16)
    qf = _dot(qn, wq_ref[...])
    ckv = _rms(u[:, o_kv:o_kr], kvg_ref[...])
    ckv_ref[0] = ckv
    kvd = _dot(ckv.astype(BF16), wukv_ref[...])
    kr = u[:, o_kr:o_kr + LANE]
    kr_ref[0] = kr[:, :D_ROPE]
    if rope:
        cos = cos_ref[...]
        sin = sin_ref[...]
        qs = _dot(qn, wqs_ref[...])
        kr = kr * cos + u[:, o_kr + LANE:o_kr + 2 * LANE] * sin
    q_parts = []
    k_parts = []
    for hd in range(N_HEADS):
        q_parts.append(qf[:, hd * HEAD_W:hd * HEAD_W + LANE])
        qr = qf[:, hd * HEAD_W + LANE:(hd + 1) * HEAD_W]
        if rope:
            qr = qr * cos + qs[:, hd * LANE:(hd + 1) * LANE] * sin
        q_parts.append(qr)
        k_parts.append(kvd[:, hd * D_NOPE:(hd + 1) * D_NOPE])
        k_parts.append(kr)
    q_ref[0] = jnp.concatenate(q_parts, axis=-1).astype(BF16)
    k_ref[0] = jnp.concatenate(k_parts, axis=-1).astype(BF16)
    v_ref[0] = kvd[:, N_HEADS * D_NOPE:].astype(BF16)


def _mix_in(x, mod, mod_row0, g_pre1, w_in_ext, q_norm_g, kv_norm_g, wq, wqs, wukv, cos, sin, rope, tm):
    b, t, _ = x.shape
    ncol = w_in_ext.shape[1]
    const = lambda bi, i: (0, 0)
    tok = lambda bi, i: (bi, i, 0)
    table = (lambda bi, i: (i, 0)) if rope else const
    outs = [(C_CONV, F32), (N_HEADS * HEAD_W, BF16), (N_HEADS * HEAD_W, BF16), (N_HEADS * D_V, BF16),
            (KV_LORA, F32), (D_ROPE, F32)]
    return pl.pallas_call(
        functools.partial(_mix_in_kernel, rope),
        grid=(b, t // tm),
        in_specs=[pl.BlockSpec((1, tm, D_MODEL), tok),
                  pl.BlockSpec((1, 6, D_MODEL), lambda bi, i: (mod_row0 + bi, 0, 0)),
                  pl.BlockSpec((1, D_MODEL), const),
                  pl.BlockSpec((D_MODEL, ncol), const),
                  pl.BlockSpec((1, Q_LORA), const),
                  pl.BlockSpec((1, KV_LORA), const),
                  pl.BlockSpec((Q_LORA, N_HEADS * HEAD_W), const),
                  pl.BlockSpec((Q_LORA, N_HEADS * LANE), const),
                  pl.BlockSpec((KV_LORA, N_HEADS * (D_NOPE + D_V)), const),
                  pl.BlockSpec((tm, LANE), table),
                  pl.BlockSpec((tm, LANE), table)],
        out_specs=[pl.BlockSpec((1, tm, w), tok) for w, _ in outs],
        out_shape=[jax.ShapeDtypeStruct((b, t, w), dt) for w, dt in outs],
        compiler_params=_cparams("arbitrary", "arbitrary"),
        name="mix_in_rope" if rope else "mix_in",
    )(x, mod, g_pre1, w_in_ext, q_norm_g, kv_norm_g, wq, wqs, wukv, cos, sin)


def _conv_kernel(t, tb, tt, h_ref, w_ref, b_ref, g_ref, bb_ref, o_ref, pad_ref, sh_ref):
    zeros = jnp.zeros((CONV_HALO, C_CONV), F32)
    pad_ref[0:CONV_HALO, :] = zeros
    pad_ref[CONV_HALO + t:, :] = zeros
    pad_ref[CONV_HALO:CONV_HALO + t, :] = h_ref[0]
    first = CONV_HALO - CONV_K // 2
    rows = tb + 2 * CONV_HALO - SUBLANE

    def block(bi, carry):
        base = pl.multiple_of(bi * tb, tb)
        win = pad_ref[pl.ds(base, tb + 2 * CONV_HALO), :]
        for s in range(SUBLANE):
            sh_ref[s, 0:rows, :] = win[s:s + rows, :]
        for c in range(tb // tt):
            acc = jnp.zeros((tt, C_CONV), F32)
            for k in range(CONV_K):
                off = first + k
                r0 = c * tt + off // SUBLANE * SUBLANE
                acc = acc + sh_ref[off % SUBLANE, r0:r0 + tt, :] * w_ref[k:k + 1, :]
            y = acc + b_ref[...]
            mu = jnp.mean(y, axis=-1, keepdims=True)
            yc = y - mu
            var = jnp.mean(yc * yc, axis=-1, keepdims=True)
            z = yc * lax.rsqrt(var + EPS) * g_ref[...] + bb_ref[...]
            o_ref[0, pl.ds(base + c * tt, tt), :] = (z * _sigmoid(z)).astype(BF16)
        return carry

    lax.fori_loop(0, t // tb, block, 0)


def _conv(hglu, w_dw, b_dw, ln_g, ln_b):
    b, t, _ = hglu.shape
    tb, tt = 256, 32
    const = lambda bi: (0, 0)
    return pl.pallas_call(
        functools.partial(_conv_kernel, t, tb, tt),
        grid=(b,),
        in_specs=[pl.BlockSpec((1, t, C_CONV), lambda bi: (bi, 0, 0)),
                  pl.BlockSpec((CONV_K, C_CONV), const),
                  pl.BlockSpec((1, C_CONV), const),
                  pl.BlockSpec((1, C_CONV), const),
                  pl.BlockSpec((1, C_CONV), const)],
        out_specs=pl.BlockSpec((1, t, C_CONV), lambda bi: (bi, 0, 0)),
        out_shape=jax.ShapeDtypeStruct((b, t, C_CONV), BF16),
        scratch_shapes=[pltpu.VMEM((t + 2 * CONV_HALO, C_CONV), F32),
                        pltpu.VMEM((SUBLANE, tb + 2 * CONV_HALO, C_CONV), F32)],
        compiler_params=_cparams("arbitrary"),
        name="conv",
    )(hglu, w_dw, b_dw, ln_g, ln_b)


def _qk(q, k):
    return lax.dot_general(q, k, (((1,), (1,)), ((), ())), preferred_element_type=F32)


def _attn_kernel(cached, q_ref, k_ref, v_ref, *rest):
    if cached:
        kc_ref, vc_ref, o_ref = rest
    else:
        (o_ref,) = rest
    outs = []
    for hd in range(N_HEADS):
        ks = slice(hd * HEAD_W, (hd + 1) * HEAD_W)
        vs = slice(hd * D_V, (hd + 1) * D_V)
        qh = q_ref[0, :, ks]
        s = _qk(qh, k_ref[0, :, ks]) * ATT_SCALE
        m = jnp.max(s, axis=-1, keepdims=True)
        if cached:
            sc = _qk(qh, kc_ref[0, :, ks]) * ATT_SCALE
            m = jnp.maximum(m, jnp.max(sc, axis=-1, keepdims=True))
        p = jnp.exp(s - m)
        l = jnp.sum(p, axis=-1, keepdims=True)
        o = _dot(p.astype(BF16), v_ref[0, :, vs])
        if cached:
            pc = jnp.exp(sc - m)
            l = l + jnp.sum(pc, axis=-1, keepdims=True)
            o = o + _dot(pc.astype(BF16), vc_ref[0, :, vs])
        outs.append(o / l)
    o_ref[0] = jnp.concatenate(outs, axis=-1).astype(BF16)


def _attn(q, k, v, kc=None, vc=None, tq=256):
    b, t, _ = q.shape
    s = k.shape[1]
    cached = kc is not None
    whole = lambda bi, i: (bi, 0, 0)
    in_specs = [pl.BlockSpec((1, tq, N_HEADS * HEAD_W), lambda bi, i: (bi, i, 0)),
                pl.BlockSpec((1, s, N_HEADS * HEAD_W), whole),
                pl.BlockSpec((1, s, N_HEADS * D_V), whole)]
    args = [q, k, v]
    if cached:
        sc = kc.shape[1]
        in_specs += [pl.BlockSpec((1, sc, N_HEADS * HEAD_W), whole),
                     pl.BlockSpec((1, sc, N_HEADS * D_V), whole)]
        args += [kc, vc]
    return pl.pallas_call(
        functools.partial(_attn_kernel, cached),
        grid=(b, t // tq),
        in_specs=in_specs,
        out_specs=pl.BlockSpec((1, tq, N_HEADS * D_V), lambda bi, i: (bi, i, 0)),
        out_shape=jax.ShapeDtypeStruct((b, t, N_HEADS * D_V), BF16),
        compiler_params=_cparams("arbitrary", "arbitrary"),
        name="attn_cached" if cached else "attn",
    )(*args)


def _cache_kv_kernel(ckv_ref, kr_ref, wukv_ref, k_ref, v_ref):
    kvd = _dot(ckv_ref[0].astype(BF16), wukv_ref[...])
    kr = kr_ref[0]
    kr = jnp.concatenate([kr, jnp.zeros_like(kr)], axis=-1)
    parts = []
    for hd in range(N_HEADS):
        parts.append(kvd[:, hd * D_NOPE:(hd + 1) * D_NOPE])
        parts.append(kr)
    k_ref[0] = jnp.concatenate(parts, axis=-1).astype(BF16)
    v_ref[0] = kvd[:, N_HEADS * D_NOPE:].astype(BF16)


def _cache_kv(ckv, krope, wukv):
    b, s, _ = ckv.shape
    tok = lambda bi: (bi, 0, 0)
    return pl.pallas_call(
        _cache_kv_kernel,
        grid=(b,),
        in_specs=[pl.BlockSpec((1, s, KV_LORA), tok),
                  pl.BlockSpec((1, s, D_ROPE), tok),
                  pl.BlockSpec((KV_LORA, N_HEADS * (D_NOPE + D_V)), lambda bi: (0, 0))],
        out_specs=[pl.BlockSpec((1, s, N_HEADS * HEAD_W), tok),
                   pl.BlockSpec((1, s, N_HEADS * D_V), tok)],
        out_shape=[jax.ShapeDtypeStruct((b, s, N_HEADS * HEAD_W), BF16),
                   jax.ShapeDtypeStruct((b, s, N_HEADS * D_V), BF16)],
        compiler_params=_cparams("arbitrary"),
        name="cache_kv",
    )(ckv, krope, wukv)


def _post_kernel(x_ref, conv_ref, att_ref, mod_ref, wo_ref, gpost_ref, gpre2_ref, wr_hi_ref, wr_lo_ref,
                 br_ref, x1_ref, h2t_ref, ri_ref, rw_ref):
    out = _dot(conv_ref[0], wo_ref[:C_CONV, :]) + _dot(att_ref[0], wo_ref[C_CONV:, :])
    gt1 = mod_ref[0, 2:3, :]
    sh2 = mod_ref[0, 3:4, :]
    sc2 = mod_ref[0, 4:5, :]
    x1 = x_ref[0] + gt1 * _rms(out, gpost_ref[...])
    x1_ref[0] = x1
    h2 = _rms(x1, gpre2_ref[...]) * (1.0 + sc2) + sh2
    h_hi = h2.astype(BF16)
    _store_token_tiles(h2t_ref.at[0], h2)
    h_lo = (h2 - h_hi.astype(F32)).astype(BF16)
    w_hi = wr_hi_ref[...]
    logits = _dot(h_hi, w_hi) + _dot(h_lo, w_hi) + _dot(h_hi, wr_lo_ref[...]) + br_ref[...]

    lane = lax.broadcasted_iota(jnp.int32, logits.shape, 1)
    neg = jnp.float32(-jnp.inf)
    big = jnp.int32(LANE)
    is_g = (lane >= N_EXPERTS) & (lane < N_EXPERTS + N_GROUPS)
    lg = jnp.where(is_g, logits, neg)
    gmax = jnp.max(lg, axis=-1, keepdims=True)
    gidx = jnp.min(jnp.where(lg == gmax, lane, big), axis=-1, keepdims=True) - N_EXPERTS
    g_top = 1.0 / jnp.sum(jnp.exp(lg - gmax), axis=-1, keepdims=True)

    in_grp = (lane >= gidx * E_PER_GROUP) & (lane < (gidx + 1) * E_PER_GROUP)
    le = jnp.where(in_grp, logits, neg)
    m1 = jnp.max(le, axis=-1, keepdims=True)
    i1 = jnp.min(jnp.where(le == m1, lane, big), axis=-1, keepdims=True)
    le2 = jnp.where(lane == i1, neg, le)
    m2 = jnp.max(le2, axis=-1, keepdims=True)
    i2 = jnp.min(jnp.where(le2 == m2, lane, big), axis=-1, keepdims=True)
    r = jnp.exp(m2 - m1)
    w1 = g_top / (1.0 + r)
    w2 = g_top * r / (1.0 + r)
    ri_ref[0] = jnp.where(lane == 0, i1, jnp.where(lane == 1, i2, 0))
    rw_ref[0] = jnp.where(lane == 0, w1, jnp.where(lane == 1, w2, 0.0))


def _post(x, conv_out, att, mod, mod_row0, w_out, g_post1, g_pre2, wr_hi, wr_lo, br, tm):
    b, t, _ = x.shape
    const = lambda bi, i: (0, 0)
    tok = lambda bi, i: (bi, i, 0)
    return pl.pallas_call(
        _post_kernel,
        grid=(b, t // tm),
        in_specs=[pl.BlockSpec((1, tm, D_MODEL), tok),
                  pl.BlockSpec((1, tm, C_CONV), tok),
                  pl.BlockSpec((1, tm, N_HEADS * D_V), tok),
                  pl.BlockSpec((1, 6, D_MODEL), lambda bi, i: (mod_row0 + bi, 0, 0)),
                  pl.BlockSpec((D_MODEL, D_MODEL), const),
                  pl.BlockSpec((1, D_MODEL), const),
                  pl.BlockSpec((1, D_MODEL), const),
                  pl.BlockSpec((D_MODEL, LANE), const),
                  pl.BlockSpec((D_MODEL, LANE), const),
                  pl.BlockSpec((1, LANE), const)],
        out_specs=[pl.BlockSpec((1, tm, D_MODEL), tok),
                   pl.BlockSpec((1, tm * SUBLANE, LANE), tok),
                   pl.BlockSpec((1, tm, LANE), tok),
                   pl.BlockSpec((1, tm, LANE), tok)],
        out_shape=[jax.ShapeDtypeStruct((b, t, D_MODEL), F32),
                   jax.ShapeDtypeStruct((b, t * SUBLANE, LANE), F32),
                   jax.ShapeDtypeStruct((b, t, LANE), jnp.int32),
                   jax.ShapeDtypeStruct((b, t, LANE), F32)],
        compiler_params=_cparams("arbitrary", "arbitrary"),
        name="post",
    )(x, conv_out, att, mod, w_out, g_post1, g_pre2, wr_hi, wr_lo, br)


def _rank_kernel(ri_ref, rank_ref, cnt_ref, carry_ref):
    @pl.when(pl.program_id(0) == 0)
    def _():
        carry_ref[...] = jnp.zeros_like(carry_ref)

    ri = ri_ref[...]
    tr = ri.shape[0]
    lane = lax.broadcasted_iota(jnp.int32, ri.shape, 1)
    oh1 = lane == ri[:, 0:1]
    oh2 = lane == ri[:, 1:2]
    oh = jnp.where(oh1 | oh2, 1.0, 0.0)
    rr = lax.broadcasted_iota(jnp.int32, (tr, tr), 0)
    cc = lax.broadcasted_iota(jnp.int32, (tr, tr), 1)
    earlier = jnp.where(cc < rr, 1.0, 0.0).astype(BF16)
    before = _dot(earlier, oh.astype(BF16)) + carry_ref[0:1, :]
    r1 = jnp.sum(jnp.where(oh1, before, 0.0), axis=-1, keepdims=True)
    r2 = jnp.sum(jnp.where(oh2, before, 0.0), axis=-1, keepdims=True)
    rank_ref[...] = jnp.where(lane == 0, r1, jnp.where(lane == 1, r2, 0.0)).astype(jnp.int32)
    carry_ref[0:1, :] = carry_ref[0:1, :] + jnp.sum(oh, axis=0, keepdims=True)
    cnt_ref[...] = jnp.broadcast_to(carry_ref[0:1, :], cnt_ref.shape)


def _rank(ri, tr):
    n = ri.shape[0]
    return pl.pallas_call(
        _rank_kernel,
        grid=(n // tr,),
        in_specs=[pl.BlockSpec((tr, LANE), lambda i: (i, 0))],
        out_specs=[pl.BlockSpec((tr, LANE), lambda i: (i, 0)),
                   pl.BlockSpec((SUBLANE, LANE), lambda i: (0, 0))],
        out_shape=[jax.ShapeDtypeStruct((n, LANE), jnp.int32),
                   jax.ShapeDtypeStruct((SUBLANE, LANE), F32)],
        scratch_shapes=[pltpu.VMEM((SUBLANE, LANE), F32)],
        compiler_params=_cparams("arbitrary"),
        name="moe_rank",
    )(ri)


def _plan(ri, rank, counts, tile, n_tiles_max):
    n = ri.shape[0]
    cnt = counts[0, :N_EXPERTS].astype(jnp.int32)
    padded = (cnt + tile - 1) // tile * tile
    ends = jnp.cumsum(padded)
    starts = ends - padded
    pos = starts[ri[:, :2]] + rank[:, :2]
    n_tiles = ends[-1] // tile
    tile_row0 = jnp.arange(n_tiles_max, dtype=jnp.int32) * tile
    tile_expert = jnp.sum((ends[None, :] <= tile_row0[:, None]).astype(jnp.int32), axis=1)
    tile_expert = jnp.minimum(tile_expert, N_EXPERTS - 1)
    last = tile_expert[jnp.maximum(n_tiles - 1, 0)]
    tile_expert = jnp.where(jnp.arange(n_tiles_max) < n_tiles, tile_expert, last)
    tok = jnp.repeat(jnp.arange(n, dtype=jnp.int32), 2)
    src = (jnp.arange(n_tiles_max * tile, dtype=jnp.int32) % n).at[pos.reshape(-1)].set(tok, unique_indices=True)
    pos = pos * SUBLANE
    return pos[:, 0], pos[:, 1], src * SUBLANE, tile_expert, n_tiles.reshape(1).astype(jnp.int32)


def _moe_gemm_kernel(tile, te_ref, nt_ref, src_ref, srcn_ref, h2_hbm, wg_ref, wu_ref, wd_ref, y_ref, xbuf, sem):
    j = pl.program_id(0)
    nt = nt_ref[0]
    slot = j % 2

    rows = tile * SUBLANE

    def gather(idx_ref, s):
        for r in range(tile):
            _token_copy(h2_hbm, idx_ref[r], xbuf, s * rows + r * SUBLANE, sem.at[s]).start()

    @pl.when(j == 0)
    def _():
        gather(src_ref, 0)

    @pl.when(j + 1 < nt)
    def _():
        gather(srcn_ref, 1 - slot)

    @pl.when(j < nt)
    def _():
        base = pl.multiple_of(slot * rows, rows)
        pltpu.make_async_copy(h2_hbm.at[pl.ds(0, rows), :], xbuf.at[pl.ds(base, rows), :], sem.at[slot]).wait()
        x = _load_token_tiles(xbuf, base, tile).astype(BF16)
        a = _dot(x, wg_ref[0].astype(BF16))
        u = _dot(x, wu_ref[0].astype(BF16))
        he = (a * _sigmoid(a)) * u
        y = _dot(he.astype(BF16), wd_ref[0].astype(BF16))
        _store_token_tiles(y_ref, y)

    @pl.when(j >= nt)
    def _():
        y_ref[...] = jnp.zeros_like(y_ref)


def _moe_gemm(src, tile_expert, n_tiles, h2t, w_gate, w_up, w_down, tile):
    n_tiles_max = tile_expert.shape[0]
    clamp = lambda j, nt: jnp.minimum(j, jnp.maximum(nt[0] - 1, 0))
    wmap = lambda j, te, nt: (te[j], 0, 0)
    return pl.pallas_call(
        functools.partial(_moe_gemm_kernel, tile),
        grid_spec=pltpu.PrefetchScalarGridSpec(
            num_scalar_prefetch=2,
            grid=(n_tiles_max,),
            in_specs=[pl.BlockSpec((tile,), lambda j, te, nt: (clamp(j, nt),), memory_space=pltpu.SMEM),
                      pl.BlockSpec((tile,), lambda j, te, nt: (clamp(j + 1, nt),), memory_space=pltpu.SMEM),
                      pl.BlockSpec(memory_space=pl.ANY),
                      pl.BlockSpec((1, D_MODEL, D_EXPERT), wmap),
                      pl.BlockSpec((1, D_MODEL, D_EXPERT), wmap),
                      pl.BlockSpec((1, D_EXPERT, D_MODEL), wmap)],
            out_specs=pl.BlockSpec((tile * SUBLANE, LANE), lambda j, te, nt: (j, 0)),
            scratch_shapes=[pltpu.VMEM((2 * tile * SUBLANE, LANE), F32), pltpu.SemaphoreType.DMA((2,))]),
        out_shape=jax.ShapeDtypeStruct((n_tiles_max * tile * SUBLANE, LANE), F32),
        compiler_params=_cparams("arbitrary"),
        name="moe_gemm",
    )(tile_expert, n_tiles, src, src, h2t, w_gate, w_up, w_down)


def _moe_combine_kernel(tm, p1_ref, p2_ref, p1n_ref, p2n_ref, rw_ref, x1_ref, mod_ref, gpost_ref, y_hbm, o_ref,
                        ybuf, sem):
    i = pl.program_id(0)
    n = pl.num_programs(0)
    slot = i % 2

    rows = tm * SUBLANE

    def gather(a_ref, b_ref, s):
        for k, p_ref in enumerate((a_ref, b_ref)):
            for r in range(tm):
                _token_copy(y_hbm, p_ref[r], ybuf, (2 * s + k) * rows + r * SUBLANE, sem.at[s]).start()

    @pl.when(i == 0)
    def _():
        gather(p1_ref, p2_ref, 0)

    @pl.when(i + 1 < n)
    def _():
        gather(p1n_ref, p2n_ref, 1 - slot)

    base = pl.multiple_of(2 * slot * rows, rows)
    pltpu.make_async_copy(y_hbm.at[pl.ds(0, 2 * rows), :], ybuf.at[pl.ds(base, 2 * rows), :], sem.at[slot]).wait()
    y1 = _load_token_tiles(ybuf, base, tm)
    y2 = _load_token_tiles(ybuf, base + rows, tm)
    rw = rw_ref[...]
    moe = rw[:, 0:1] * y1 + rw[:, 1:2] * y2
    gt2 = mod_ref[0, 5:6, :]
    o_ref[...] = x1_ref[...] + gt2 * _rms(moe, gpost_ref[...])


def _moe_combine(pos1, pos2, rw, x1, mod, mod_row, y_sorted, g_post2, tm):
    n = x1.shape[0]
    nsteps = n // tm
    cur = lambda i: (i,)
    nxt = lambda i: (jnp.minimum(i + 1, nsteps - 1),)
    smem = lambda f: pl.BlockSpec((tm,), f, memory_space=pltpu.SMEM)
    return pl.pallas_call(
        functools.partial(_moe_combine_kernel, tm),
        grid=(nsteps,),
        in_specs=[smem(cur), smem(cur), smem(nxt), smem(nxt),
                  pl.BlockSpec((tm, LANE), lambda i: (i, 0)),
                  pl.BlockSpec((tm, D_MODEL), lambda i: (i, 0)),
                  pl.BlockSpec((1, 6, D_MODEL), lambda i: (mod_row(i), 0, 0)),
                  pl.BlockSpec((1, D_MODEL), lambda i: (0, 0)),
                  pl.BlockSpec(memory_space=pl.ANY)],
        out_specs=pl.BlockSpec((tm, D_MODEL), lambda i: (i, 0)),
        out_shape=jax.ShapeDtypeStruct((n, D_MODEL), F32),
        scratch_shapes=[pltpu.VMEM((4 * tm * SUBLANE, LANE), F32), pltpu.SemaphoreType.DMA((2,))],
        compiler_params=_cparams("arbitrary"),
        name="moe_combine",
    )(pos1, pos2, pos1, pos2, rw, x1, mod, g_post2, y_sorted)


def _moe(x1, h2t, ri, rw, mod, mod_row, w_gate, w_up, w_down, g_post2):
    n = x1.shape[0]
    tile = MOE_TILE
    n_tiles_max = (2 * n + N_EXPERTS * (tile - 1)) // tile + 1
    rank, counts = _rank(ri, 512)
    pos1, pos2, src, tile_expert, n_tiles = _plan(ri, rank, counts, tile, n_tiles_max)
    y_sorted = _moe_gemm(src, tile_expert, n_tiles, h2t, w_gate, w_up, w_down, tile)
    return _moe_combine(pos1, pos2, rw, x1, mod, mod_row, y_sorted, g_post2, MOE_TILE)


def _rotate_half_cols(w):
    n = w.shape[-1]
    w4 = w.reshape(w.shape[:-1] + (n // 32, 2, 16))
    return jnp.stack([-w4[..., 1, :], w4[..., 0, :]], axis=-2).reshape(w.shape)


def _pad_lanes(w):
    return jnp.concatenate([w, jnp.zeros(w.shape[:-1] + (LANE - w.shape[-1],), w.dtype)], axis=-1)


def _rope_tables(t):
    rows = t // GRID_W
    n_freq = D_ROPE // 4
    freqs = ROPE_BASE ** (-jnp.arange(n_freq, dtype=F32) / n_freq)
    row = jnp.repeat(jnp.arange(rows, dtype=F32), GRID_W)
    col = jnp.tile(jnp.arange(GRID_W, dtype=F32), rows)
    ang_r = row[:, None] * freqs
    ang_c = col[:, None] * freqs
    cos = jnp.concatenate([jnp.cos(ang_r), jnp.cos(ang_r), jnp.cos(ang_c), jnp.cos(ang_c)], axis=-1)
    sin = jnp.concatenate([jnp.sin(ang_r), jnp.sin(ang_r), jnp.sin(ang_c), jnp.sin(ang_c)], axis=-1)
    return _pad_lanes(cos), _pad_lanes(sin)


def kernel(x_prompt, x_sample, cache_ckv, cache_krope, c, c_ctx, w_ada, b_ada, g_pre1, g_post1, g_pre2, g_post2, w_in, w_dw, b_dw, conv_ln_g, conv_ln_b, q_norm_g, kv_norm_g, w_uq, w_ukv, w_out, w_rg, b_rg, w_re, b_re, w_gate, w_up, w_down):
    nb, seq, d = x_prompt.shape
    db, dseq, _ = x_sample.shape
    l = 0

    cvec = jnp.concatenate([c_ctx[None, :], c, jnp.zeros((8 - 1 - db, d), F32)], axis=0)
    mod = _ada(cvec, w_ada[l], b_ada[l]).reshape(8, 6, d)

    o_kr = 2 * C_CONV + Q_LORA + KV_LORA
    w_in_l = w_in[l]
    w_kr = w_in_l[:, o_kr:]
    w_in_ctx = jnp.concatenate([w_in_l[:, :o_kr], _pad_lanes(w_kr)], axis=-1).astype(BF16)
    w_in_lat = jnp.concatenate([w_in_l[:, :o_kr], _pad_lanes(w_kr), _pad_lanes(_rotate_half_cols(w_kr))],
                               axis=-1).astype(BF16)
    wuq = w_uq[l].reshape(Q_LORA, N_HEADS, D_NOPE + D_ROPE)
    wq_rope = wuq[:, :, D_NOPE:]
    wq = jnp.concatenate([wuq[:, :, :D_NOPE], _pad_lanes(wq_rope)], axis=-1)
    wq = wq.reshape(Q_LORA, N_HEADS * HEAD_W).astype(BF16)
    wqs = _pad_lanes(_rotate_half_cols(wq_rope)).reshape(Q_LORA, N_HEADS * LANE).astype(BF16)
    wukv4 = w_ukv[l].reshape(KV_LORA, N_HEADS, D_NOPE + D_V)
    wukv = jnp.concatenate([wukv4[:, :, :D_NOPE].reshape(KV_LORA, N_HEADS * D_NOPE),
                            wukv4[:, :, D_NOPE:].reshape(KV_LORA, N_HEADS * D_V)], axis=-1).astype(BF16)
    w_out_b = w_out[l].astype(BF16)
    w_r = jnp.concatenate([w_re[l], w_rg[l], jnp.zeros((d, LANE - N_EXPERTS - N_GROUPS), F32)], axis=-1)
    wr_hi = w_r.astype(BF16)
    wr_lo = (w_r - wr_hi.astype(F32)).astype(BF16)
    b_r = jnp.concatenate([b_re[l], b_rg[l], jnp.zeros((LANE - N_EXPERTS - N_GROUPS,), F32)])[None, :]
    cos, sin = _rope_tables(dseq)
    row = lambda v: v[l][None, :]

    tm_c = 512
    xp_flat = x_prompt.reshape(1, nb * seq, d)
    hglu, q, k, v, ckv, kr = _mix_in(xp_flat, mod, 0, row(g_pre1), w_in_ctx, row(q_norm_g), row(kv_norm_g),
                                     wq, wqs, wukv, cos[:tm_c], sin[:tm_c], False, tm_c)
    per_seq = lambda a: a.reshape(nb, seq, a.shape[-1])
    conv_out = _conv(per_seq(hglu), w_dw[l], row(b_dw), row(conv_ln_g), row(conv_ln_b))
    att = _attn(per_seq(q), per_seq(k), per_seq(v))
    flat = lambda a: a.reshape(1, nb * seq, a.shape[-1])
    x1, h2t, ri, rw = _post(xp_flat, flat(conv_out), flat(att), mod, 0, w_out_b, row(g_post1), row(g_pre2),
                            wr_hi, wr_lo, b_r, tm_c)
    n_c = nb * seq
    yp = _moe(x1.reshape(n_c, d), h2t.reshape(n_c * SUBLANE, LANE), ri.reshape(n_c, LANE), rw.reshape(n_c, LANE),
              mod, lambda i: 0, w_gate[l], w_up[l], w_down[l], row(g_post2))
    state_ckv = ckv.reshape(nb, 1, seq, KV_LORA)
    state_krope = kr.reshape(nb, 1, seq, D_ROPE)

    tm_s = 512
    kc, vc = _cache_kv(cache_ckv[:, l], cache_krope[:, l], wukv)
    hglu, q, k, v, _, _ = _mix_in(x_sample, mod, 1, row(g_pre1), w_in_lat, row(q_norm_g), row(kv_norm_g),
                                  wq, wqs, wukv, cos, sin, True, tm_s)
    conv_out = _conv(hglu, w_dw[l], row(b_dw), row(conv_ln_g), row(conv_ln_b))
    att = _attn(q, k, v, kc, vc)
    x1, h2t, ri, rw = _post(x_sample, conv_out, att, mod, 1, w_out_b, row(g_post1), row(g_pre2),
                            wr_hi, wr_lo, b_r, tm_s)
    n_s = db * dseq
    tiles_per_req = dseq // MOE_TILE
    ys = _moe(x1.reshape(n_s, d), h2t.reshape(n_s * SUBLANE, LANE), ri.reshape(n_s, LANE), rw.reshape(n_s, LANE),
              mod, lambda i: 1 + i // tiles_per_req, w_gate[l], w_up[l], w_down[l], row(g_post2))

    return (yp.reshape(nb, seq, d), ys.reshape(db, dseq, d), state_ckv, state_krope)
```

```python
import functools

import jax
import jax.numpy as jnp
import numpy as np
from jax import lax
from jax.experimental import pallas as pl
from jax.experimental.pallas import tpu as pltpu

D_MODEL = 1024
GRID_W = 64
C_CONV = 512
CONV_K = 31
N_HEADS = 4
D_NOPE = 128
D_ROPE = 64
D_V = 128
Q_LORA = 384
KV_LORA = 256
N_GROUPS = 4
E_PER_GROUP = 8
N_EXPERTS = 32
D_EXPERT = 256
ROPE_BASE = 10000.0
EPS = 1e-6
ATT_SCALE = (D_NOPE + D_ROPE) ** -0.5

LANE = 128
SUBLANE = 8
HEAD_W = 2 * LANE
CONV_HALO = 16
MOE_TD = 256
CHUNK = SUBLANE
MOE_L = 2 * MOE_TD + N_EXPERTS * (CHUNK - 1)
MOE_NCH = MOE_L // CHUNK
MOE_TILE = 256
TILE_CH = MOE_TILE // CHUNK
XS_W = D_MODEL + LANE
VMEM_LIMIT = 56 * 1024 * 1024

BF16 = jnp.bfloat16
F32 = jnp.float32


def _cparams(*sem):
    return pltpu.CompilerParams(dimension_semantics=sem, vmem_limit_bytes=VMEM_LIMIT)


def _rms(x, g):
    return x * lax.rsqrt(jnp.mean(x * x, axis=-1, keepdims=True) + EPS) * g


def _sigmoid(x):
    return 1.0 / (1.0 + jnp.exp(-x))


def _dot(a, b):
    return jnp.dot(a, b, preferred_element_type=F32)


def _chunk_copy(src, src_chunk, dst, dst_chunk, sem):
    rows = lambda c: pl.ds(c * CHUNK if isinstance(c, int) else pl.multiple_of(c * CHUNK, CHUNK), CHUNK)
    return pltpu.make_async_copy(src.at[rows(src_chunk), :], dst.at[rows(dst_chunk), :], sem)


def _ada_kernel(c_ref, w_ref, b_ref, o_ref):
    c = c_ref[...]
    s = (c * _sigmoid(c)).astype(BF16)
    o_ref[...] = _dot(s, w_ref[...].astype(BF16)) + b_ref[...]


def _ada(cvec, w_ada, b_ada):
    n = w_ada.shape[1]
    tn = 1536
    return pl.pallas_call(
        _ada_kernel,
        grid=(n // tn,),
        in_specs=[pl.BlockSpec((8, D_MODEL), lambda j: (0, 0)),
                  pl.BlockSpec((D_MODEL, tn), lambda j: (0, j)),
                  pl.BlockSpec((1, tn), lambda j: (0, j))],
        out_specs=pl.BlockSpec((8, tn), lambda j: (0, j)),
        out_shape=jax.ShapeDtypeStruct((8, n), F32),
        compiler_params=_cparams("arbitrary"),
        name="ada",
    )(cvec, w_ada, b_ada.reshape(1, n))


def _mix_in_kernel(rope, x_ref, mod_ref, gpre_ref, win_ref, qg_ref, kvg_ref, wq_ref, wqs_ref,
                   wukv_ref, cos_ref, sin_ref, hglu_ref, q_ref, k_ref, v_ref, ckv_ref, kr_ref):
    x = x_ref[0]
    sh1 = mod_ref[0, 0:1, :]
    sc1 = mod_ref[0, 1:2, :]
    h = _rms(x, gpre_ref[...]) * (1.0 + sc1) + sh1
    u = _dot(h.astype(BF16), win_ref[...])

    a = u[:, :C_CONV]
    g = u[:, C_CONV:2 * C_CONV]
    hglu_ref[0] = a * _sigmoid(g)

    o_q = 2 * C_CONV
    o_kv = o_q + Q_LORA
    o_kr = o_kv + KV_LORA
    qn = _rms(u[:, o_q:o_kv], qg_ref[...]).astype(BF16)
    qf = _dot(qn, wq_ref[...])
    ckv = _rms(u[:, o_kv:o_kr], kvg_ref[...])
    ckv_ref[0] = ckv
    kvd = _dot(ckv.astype(BF16), wukv_ref[...])
    kr = u[:, o_kr:o_kr + LANE]
    kr_ref[0] = kr[:, :D_ROPE]
    if rope:
        cos = cos_ref[...]
        sin = sin_ref[...]
        qs = _dot(qn, wqs_ref[...])
        kr = kr * cos + u[:, o_kr + LANE:o_kr + 2 * LANE] * sin
    q_parts = []
    k_parts = []
    for hd in range(N_HEADS):
        q_parts.append(qf[:, hd * HEAD_W:hd * HEAD_W + LANE])
        qr = qf[:, hd * HEAD_W + LANE:(hd + 1) * HEAD_W]
        if rope:
            qr = qr * cos + qs[:, hd * LANE:(hd + 1) * LANE] * sin
        q_parts.append(qr)
        k_parts.append(kvd[:, hd * D_NOPE:(hd + 1) * D_NOPE])
        k_parts.append(kr)
    q_ref[0] = jnp.concatenate(q_parts, axis=-1).astype(BF16)
    k_ref[0] = jnp.concatenate(k_parts, axis=-1).astype(BF16)
    v_ref[0] = kvd[:, N_HEADS * D_NOPE:].astype(BF16)


def _mix_in(x, mod, mod_row0, g_pre1, w_in_ext, q_norm_g, kv_norm_g, wq, wqs, wukv, cos, sin, rope, tm):
    b, t, _ = x.shape
    ncol = w_in_ext.shape[1]
    const = lambda bi, i: (0, 0)
    tok = lambda bi, i: (bi, i, 0)
    table = (lambda bi, i: (i, 0)) if rope else const
    outs = [(C_CONV, F32), (N_HEADS * HEAD_W, BF16), (N_HEADS * HEAD_W, BF16), (N_HEADS * D_V, BF16),
            (KV_LORA, F32), (D_ROPE, F32)]
    return pl.pallas_call(
        functools.partial(_mix_in_kernel, rope),
        grid=(b, t // tm),
        in_specs=[pl.BlockSpec((1, tm, D_MODEL), tok),
                  pl.BlockSpec((1, 6, D_MODEL), lambda bi, i: (mod_row0 + bi, 0, 0)),
                  pl.BlockSpec((1, D_MODEL), const),
                  pl.BlockSpec((D_MODEL, ncol), const),
                  pl.BlockSpec((1, Q_LORA), const),
                  pl.BlockSpec((1, KV_LORA), const),
                  pl.BlockSpec((Q_LORA, N_HEADS * HEAD_W), const),
                  pl.BlockSpec((Q_LORA, N_HEADS * LANE), const),
                  pl.BlockSpec((KV_LORA, N_HEADS * (D_NOPE + D_V)), const),
                  pl.BlockSpec((tm, LANE), table),
                  pl.BlockSpec((tm, LANE), table)],
        out_specs=[pl.BlockSpec((1, tm, w), tok) for w, _ in outs],
        out_shape=[jax.ShapeDtypeStruct((b, t, w), dt) for w, dt in outs],
        compiler_params=_cparams("arbitrary", "arbitrary"),
        name="mix_in_rope" if rope else "mix_in",
    )(x, mod, g_pre1, w_in_ext, q_norm_g, kv_norm_g, wq, wqs, wukv, cos, sin)


def _conv_kernel(t, tb, tt, h_ref, w_ref, b_ref, g_ref, bb_ref, o_ref, pad_ref, sh_ref):
    zeros = jnp.zeros((CONV_HALO, C_CONV), F32)
    pad_ref[0:CONV_HALO, :] = zeros
    pad_ref[CONV_HALO + t:, :] = zeros
    pad_ref[CONV_HALO:CONV_HALO + t, :] = h_ref[0]
    first = CONV_HALO - CONV_K // 2
    rows = tb + 2 * CONV_HALO - SUBLANE

    def block(bi, carry):
        base = pl.multiple_of(bi * tb, tb)
        win = pad_ref[pl.ds(base, tb + 2 * CONV_HALO), :]
        for s in range(SUBLANE):
            sh_ref[s, 0:rows, :] = win[s:s + rows, :]
        for c in range(tb // tt):
            acc = jnp.zeros((tt, C_CONV), F32)
            for k in range(CONV_K):
                off = first + k
                r0 = c * tt + off // SUBLANE * SUBLANE
                acc = acc + sh_ref[off % SUBLANE, r0:r0 + tt, :] * w_ref[k:k + 1, :]
            y = acc + b_ref[...]
            mu = jnp.mean(y, axis=-1, keepdims=True)
            yc = y - mu
            var = jnp.mean(yc * yc, axis=-1, keepdims=True)
            z = yc * lax.rsqrt(var + EPS) * g_ref[...] + bb_ref[...]
            o_ref[0, pl.ds(base + c * tt, tt), :] = (z * _sigmoid(z)).astype(BF16)
        return carry

    lax.fori_loop(0, t // tb, block, 0)


def _conv(hglu, w_dw, b_dw, ln_g, ln_b):
    b, t, _ = hglu.shape
    tb, tt = 256, 32
    const = lambda bi: (0, 0)
    return pl.pallas_call(
        functools.partial(_conv_kernel, t, tb, tt),
        grid=(b,),
        in_specs=[pl.BlockSpec((1, t, C_CONV), lambda bi: (bi, 0, 0)),
                  pl.BlockSpec((CONV_K, C_CONV), const),
                  pl.BlockSpec((1, C_CONV), const),
                  pl.BlockSpec((1, C_CONV), const),
                  pl.BlockSpec((1, C_CONV), const)],
        out_specs=pl.BlockSpec((1, t, C_CONV), lambda bi: (bi, 0, 0)),
        out_shape=jax.ShapeDtypeStruct((b, t, C_CONV), BF16),
        scratch_shapes=[pltpu.VMEM((t + 2 * CONV_HALO, C_CONV), F32),
                        pltpu.VMEM((SUBLANE, tb + 2 * CONV_HALO, C_CONV), F32)],
        compiler_params=_cparams("arbitrary"),
        name="conv",
    )(hglu, w_dw, b_dw, ln_g, ln_b)


def _qk(q, k):
    return lax.dot_general(q, k, (((1,), (1,)), ((), ())), preferred_element_type=F32)


def _attn_kernel(cached, q_ref, k_ref, v_ref, *rest):
    if cached:
        kc_ref, vc_ref, o_ref = rest
    else:
        (o_ref,) = rest
    outs = []
    for hd in range(N_HEADS):
        ks = slice(hd * HEAD_W, (hd + 1) * HEAD_W)
        vs = slice(hd * D_V, (hd + 1) * D_V)
        qh = q_ref[0, :, ks]
        s = _qk(qh, k_ref[0, :, ks]) * ATT_SCALE
        m = jnp.max(s, axis=-1, keepdims=True)
        if cached:
            sc = _qk(qh, kc_ref[0, :, ks]) * ATT_SCALE
            m = jnp.maximum(m, jnp.max(sc, axis=-1, keepdims=True))
        p = jnp.exp(s - m)
        l = jnp.sum(p, axis=-1, keepdims=True)
        o = _dot(p.astype(BF16), v_ref[0, :, vs])
        if cached:
            pc = jnp.exp(sc - m)
            l = l + jnp.sum(pc, axis=-1, keepdims=True)
            o = o + _dot(pc.astype(BF16), vc_ref[0, :, vs])
        outs.append(o / l)
    o_ref[0] = jnp.concatenate(outs, axis=-1).astype(BF16)


def _attn(q, k, v, kc=None, vc=None, tq=256):
    b, t, _ = q.shape
    s = k.shape[1]
    cached = kc is not None
    whole = lambda bi, i: (bi, 0, 0)
    in_specs = [pl.BlockSpec((1, tq, N_HEADS * HEAD_W), lambda bi, i: (bi, i, 0)),
                pl.BlockSpec((1, s, N_HEADS * HEAD_W), whole),
                pl.BlockSpec((1, s, N_HEADS * D_V), whole)]
    args = [q, k, v]
    if cached:
        sc = kc.shape[1]
        in_specs += [pl.BlockSpec((1, sc, N_HEADS * HEAD_W), whole),
                     pl.BlockSpec((1, sc, N_HEADS * D_V), whole)]
        args += [kc, vc]
    return pl.pallas_call(
        functools.partial(_attn_kernel, cached),
        grid=(b, t // tq),
        in_specs=in_specs,
        out_specs=pl.BlockSpec((1, tq, N_HEADS * D_V), lambda bi, i: (bi, i, 0)),
        out_shape=jax.ShapeDtypeStruct((b, t, N_HEADS * D_V), BF16),
        compiler_params=_cparams("arbitrary", "arbitrary"),
        name="attn_cached" if cached else "attn",
    )(*args)


def _cache_kv_kernel(ckv_ref, kr_ref, wukv_ref, k_ref, v_ref):
    kvd = _dot(ckv_ref[0].astype(BF16), wukv_ref[...])
    kr = kr_ref[0]
    kr = jnp.concatenate([kr, jnp.zeros_like(kr)], axis=-1)
    parts = []
    for hd in range(N_HEADS):
        parts.append(kvd[:, hd * D_NOPE:(hd + 1) * D_NOPE])
        parts.append(kr)
    k_ref[0] = jnp.concatenate(parts, axis=-1).astype(BF16)
    v_ref[0] = kvd[:, N_HEADS * D_NOPE:].astype(BF16)


def _cache_kv(ckv, krope, wukv):
    b, s, _ = ckv.shape
    tok = lambda bi: (bi, 0, 0)
    return pl.pallas_call(
        _cache_kv_kernel,
        grid=(b,),
        in_specs=[pl.BlockSpec((1, s, KV_LORA), tok),
                  pl.BlockSpec((1, s, D_ROPE), tok),
                  pl.BlockSpec((KV_LORA, N_HEADS * (D_NOPE + D_V)), lambda bi: (0, 0))],
        out_specs=[pl.BlockSpec((1, s, N_HEADS * HEAD_W), tok),
                   pl.BlockSpec((1, s, N_HEADS * D_V), tok)],
        out_shape=[jax.ShapeDtypeStruct((b, s, N_HEADS * HEAD_W), BF16),
                   jax.ShapeDtypeStruct((b, s, N_HEADS * D_V), BF16)],
        compiler_params=_cparams("arbitrary"),
        name="cache_kv",
    )(ckv, krope, wukv)


def _post_kernel(x_ref, conv_ref, att_ref, mod_ref, wo_ref, gpost_ref, gpre2_ref, wr_hi_ref, wr_lo_ref,
                 br_ref, x1_ref, h2_ref, ri_ref, rw_ref):
    out = _dot(conv_ref[0], wo_ref[:C_CONV, :]) + _dot(att_ref[0], wo_ref[C_CONV:, :])
    gt1 = mod_ref[0, 2:3, :]
    sh2 = mod_ref[0, 3:4, :]
    sc2 = mod_ref[0, 4:5, :]
    x1 = x_ref[0] + gt1 * _rms(out, gpost_ref[...])
    x1_ref[0] = x1
    h2 = _rms(x1, gpre2_ref[...]) * (1.0 + sc2) + sh2
    h_hi = h2.astype(BF16)
    h2_ref[0] = h_hi
    h_lo = (h2 - h_hi.astype(F32)).astype(BF16)
    w_hi = wr_hi_ref[...]
    logits = _dot(h_hi, w_hi) + _dot(h_lo, w_hi) + _dot(h_hi, wr_lo_ref[...]) + br_ref[...]

    lane = lax.broadcasted_iota(jnp.int32, logits.shape, 1)
    neg = jnp.float32(-jnp.inf)
    big = jnp.int32(LANE)
    is_g = (lane >= N_EXPERTS) & (lane < N_EXPERTS + N_GROUPS)
    lg = jnp.where(is_g, logits, neg)
    gmax = jnp.max(lg, axis=-1, keepdims=True)
    gidx = jnp.min(jnp.where(lg == gmax, lane, big), axis=-1, keepdims=True) - N_EXPERTS
    g_top = 1.0 / jnp.sum(jnp.exp(lg - gmax), axis=-1, keepdims=True)

    in_grp = (lane >= gidx * E_PER_GROUP) & (lane < (gidx + 1) * E_PER_GROUP)
    le = jnp.where(in_grp, logits, neg)
    m1 = jnp.max(le, axis=-1, keepdims=True)
    i1 = jnp.min(jnp.where(le == m1, lane, big), axis=-1, keepdims=True)
    le2 = jnp.where(lane == i1, neg, le)
    m2 = jnp.max(le2, axis=-1, keepdims=True)
    i2 = jnp.min(jnp.where(le2 == m2, lane, big), axis=-1, keepdims=True)
    r = jnp.exp(m2 - m1)
    w1 = g_top / (1.0 + r)
    w2 = g_top * r / (1.0 + r)
    ri_ref[0] = jnp.where(lane == 0, i1, jnp.where(lane == 1, i2, 0))
    rw_ref[0] = jnp.where(lane == 0, w1, jnp.where(lane == 1, w2, 0.0))


def _post(x, conv_out, att, mod, mod_row0, w_out, g_post1, g_pre2, wr_hi, wr_lo, br, tm):
    b, t, _ = x.shape
    const = lambda bi, i: (0, 0)
    tok = lambda bi, i: (bi, i, 0)
    return pl.pallas_call(
        _post_kernel,
        grid=(b, t // tm),
        in_specs=[pl.BlockSpec((1, tm, D_MODEL), tok),
                  pl.BlockSpec((1, tm, C_CONV), tok),
                  pl.BlockSpec((1, tm, N_HEADS * D_V), tok),
                  pl.BlockSpec((1, 6, D_MODEL), lambda bi, i: (mod_row0 + bi, 0, 0)),
                  pl.BlockSpec((D_MODEL, D_MODEL), const),
                  pl.BlockSpec((1, D_MODEL), const),
                  pl.BlockSpec((1, D_MODEL), const),
                  pl.BlockSpec((D_MODEL, LANE), const),
                  pl.BlockSpec((D_MODEL, LANE), const),
                  pl.BlockSpec((1, LANE), const)],
        out_specs=[pl.BlockSpec((1, tm, D_MODEL), tok),
                   pl.BlockSpec((1, tm, D_MODEL), tok),
                   pl.BlockSpec((1, tm, LANE), tok),
                   pl.BlockSpec((1, tm, LANE), tok)],
        out_shape=[jax.ShapeDtypeStruct((b, t, D_MODEL), F32),
                   jax.ShapeDtypeStruct((b, t, D_MODEL), BF16),
                   jax.ShapeDtypeStruct((b, t, LANE), jnp.int32),
                   jax.ShapeDtypeStruct((b, t, LANE), F32)],
        compiler_params=_cparams("arbitrary", "arbitrary"),
        name="post",
    )(x, conv_out, att, mod, w_out, g_post1, g_pre2, wr_hi, wr_lo, br)


def _one_hots(ri):
    lane = lax.broadcasted_iota(jnp.int32, ri.shape, 1)
    oh1 = lane == ri[:, 0:1]
    oh2 = lane == ri[:, 1:2]
    return oh1, oh2, jnp.where(oh1 | oh2, 1.0, 0.0)


def _local_positions(oh1, oh2, oh):
    td = oh.shape[0]
    cnt = jnp.sum(oh, axis=0, keepdims=True)
    nch = jnp.floor((cnt + (CHUNK - 1)) * (1.0 / CHUNK))
    a = lax.broadcasted_iota(jnp.int32, (LANE, LANE), 0)
    b = lax.broadcasted_iota(jnp.int32, (LANE, LANE), 1)
    lower_experts = jnp.where(a < b, 1.0, 0.0).astype(BF16)
    run_start = _dot(jnp.broadcast_to(nch, (SUBLANE, LANE)).astype(BF16), lower_experts)[0:1, :] * CHUNK
    rr = lax.broadcasted_iota(jnp.int32, (td, td), 0)
    cc = lax.broadcasted_iota(jnp.int32, (td, td), 1)
    earlier = jnp.where(cc < rr, 1.0, 0.0).astype(BF16)
    pos = _dot(earlier, oh.astype(BF16)) + run_start
    lp1 = jnp.sum(jnp.where(oh1, pos, 0.0), axis=-1, keepdims=True)
    lp2 = jnp.sum(jnp.where(oh2, pos, 0.0), axis=-1, keepdims=True)
    return lp1, lp2


def _count_kernel(ri_ref, cnt_ref):
    _, _, oh = _one_hots(ri_ref[...])
    cnt_ref[0] = jnp.broadcast_to(jnp.sum(oh, axis=0, keepdims=True), (SUBLANE, LANE))


def _count(ri):
    steps = ri.shape[0] // MOE_TD
    return pl.pallas_call(
        _count_kernel,
        grid=(steps,),
        in_specs=[pl.BlockSpec((MOE_TD, LANE), lambda i: (i, 0))],
        out_specs=pl.BlockSpec((1, SUBLANE, LANE), lambda i: (i, 0, 0)),
        out_shape=jax.ShapeDtypeStruct((steps, SUBLANE, LANE), F32),
        compiler_params=_cparams("arbitrary"),
        name="moe_count",
    )(ri)


def _plan(counts, n_tiles_max):
    cnt = counts[:, 0, :N_EXPERTS].astype(jnp.int32)
    k = (cnt + (CHUNK - 1)) // CHUNK
    run_end = jnp.cumsum(k, axis=1)
    run_start = run_end - k
    n_chunks = run_end[:, -1]
    total = jnp.sum(k, axis=0)
    padded = (total + (TILE_CH - 1)) // TILE_CH * TILE_CH
    seg_end = jnp.cumsum(padded)
    seg_start = seg_end - padded
    base = seg_start[None, :] + jnp.cumsum(k, axis=0) - k
    c = jnp.arange(MOE_NCH, dtype=jnp.int32)
    e_of_c = jnp.sum((run_end[:, None, :] <= c[None, :, None]).astype(jnp.int32), axis=2)
    e_of_c = jnp.minimum(e_of_c, N_EXPERTS - 1)
    take = lambda t: jnp.take_along_axis(t, e_of_c, axis=1)
    dst = take(base) + c[None, :] - take(run_start)
    n_tiles = seg_end[-1] // TILE_CH
    tile_chunk0 = jnp.arange(n_tiles_max, dtype=jnp.int32) * TILE_CH
    tile_expert = jnp.sum((seg_end[None, :] <= tile_chunk0[:, None]).astype(jnp.int32), axis=1)
    tile_expert = jnp.minimum(tile_expert, N_EXPERTS - 1)
    last = tile_expert[jnp.maximum(n_tiles - 1, 0)]
    tile_expert = jnp.where(jnp.arange(n_tiles_max) < n_tiles, tile_expert, last)
    tail_start = seg_start + total
    tail_len = padded - total
    i32 = lambda t: t.astype(jnp.int32)
    return (i32(dst), i32(n_chunks), i32(tail_start), i32(tail_len), i32(tile_expert), i32(n_tiles.reshape(1)))


def _dispatch_kernel(dst_ref, nch_ref, ts_ref, tl_ref, nt_ref, h2_ref, ri_ref, rw_ref, xs_hbm, lp_ref,
                     obuf, zbuf, sem, zsem):
    i = pl.program_id(0)
    n = pl.num_programs(0)
    slot = i % 2
    td = MOE_TD
    n_tiles_max = xs_hbm.shape[0] // MOE_TILE

    def zero_tile(m):
        return pltpu.make_async_copy(zbuf, xs_hbm.at[pl.ds(pl.multiple_of(m * MOE_TILE, MOE_TILE), MOE_TILE), :],
                                     zsem.at[1])

    @pl.when(i == 0)
    def _():
        zbuf[...] = jnp.zeros_like(zbuf)
        for e in range(N_EXPERTS):
            def fill(m, carry, e=e):
                _chunk_copy(zbuf, 0, xs_hbm, ts_ref[e] + m, zsem.at[0]).start()
                return carry
            lax.fori_loop(0, tl_ref[e], fill, 0)

        def fill_tile(m, carry):
            zero_tile(m).start()
            return carry
        lax.fori_loop(nt_ref[0], n_tiles_max, fill_tile, 0)

    ri = ri_ref[...]
    oh1, oh2, oh = _one_hots(ri)
    lp1, lp2 = _local_positions(oh1, oh2, oh)
    lane = lax.broadcasted_iota(jnp.int32, ri.shape, 1)
    lp_ref[...] = jnp.where(lane == 0, lp1, jnp.where(lane == 1, lp2, 0.0)).astype(jnp.int32)

    rr = lax.broadcasted_iota(jnp.int32, (td, td), 0)
    cc = lax.broadcasted_iota(jnp.int32, (td, td), 1)
    to_lanes = lambda col: jnp.sum(jnp.where(rr == cc, col, 0.0), axis=0, keepdims=True)
    rw = rw_ref[...]
    row = lax.broadcasted_iota(jnp.int32, (MOE_L, td), 0).astype(F32)
    p1 = row == to_lanes(lp1)
    p2 = row == to_lanes(lp2)
    perm = jnp.where(p1 | p2, 1.0, 0.0).astype(BF16)
    xs = _dot(perm, h2_ref[...])
    w = jnp.sum(jnp.where(p1, to_lanes(rw[:, 0:1]), 0.0) + jnp.where(p2, to_lanes(rw[:, 1:2]), 0.0),
                axis=-1, keepdims=True)
    obuf[slot, :, 0:D_MODEL] = xs
    obuf[slot, :, D_MODEL:XS_W] = jnp.broadcast_to(w, (MOE_L, LANE))

    def wait_chunks(count, s):
        def body(c, carry):
            _chunk_copy(obuf.at[s], 0, xs_hbm, 0, sem.at[s]).wait()
            return carry
        lax.fori_loop(0, count, body, 0)

    @pl.when(i > 0)
    def _():
        wait_chunks(nch_ref[i - 1], 1 - slot)

    def send(c, carry):
        _chunk_copy(obuf.at[slot], c, xs_hbm, dst_ref[i, c], sem.at[slot]).start()
        return carry
    lax.fori_loop(0, nch_ref[i], send, 0)

    @pl.when(i == n - 1)
    def _():
        wait_chunks(nch_ref[i], slot)
        for e in range(N_EXPERTS):
            def drain(m, carry):
                _chunk_copy(zbuf, 0, xs_hbm, 0, zsem.at[0]).wait()
                return carry
            lax.fori_loop(0, tl_ref[e], drain, 0)

        def drain_tile(m, carry):
            zero_tile(m).wait()
            return carry
        lax.fori_loop(nt_ref[0], n_tiles_max, drain_tile, 0)


def _dispatch(dst, n_chunks, tail_start, tail_len, n_tiles, h2, ri, rw, n_tiles_max):
    n = h2.shape[0]
    tok = lambda i, *_: (i, 0)
    return pl.pallas_call(
        _dispatch_kernel,
        grid_spec=pltpu.PrefetchScalarGridSpec(
            num_scalar_prefetch=5,
            grid=(n // MOE_TD,),
            in_specs=[pl.BlockSpec((MOE_TD, D_MODEL), tok),
                      pl.BlockSpec((MOE_TD, LANE), tok),
                      pl.BlockSpec((MOE_TD, LANE), tok)],
            out_specs=[pl.BlockSpec(memory_space=pl.ANY),
                       pl.BlockSpec((MOE_TD, LANE), tok)],
            scratch_shapes=[pltpu.VMEM((2, MOE_L, XS_W), F32), pltpu.VMEM((MOE_TILE, XS_W), F32),
                            pltpu.SemaphoreType.DMA((2,)), pltpu.SemaphoreType.DMA((2,))]),
        out_shape=[jax.ShapeDtypeStruct((n_tiles_max * MOE_TILE, XS_W), F32),
                   jax.ShapeDtypeStruct((n, LANE), jnp.int32)],
        compiler_params=_cparams("arbitrary"),
        name="moe_dispatch",
    )(dst, n_chunks, tail_start, tail_len, n_tiles, h2, ri, rw)


def _moe_gemm_kernel(te_ref, nt_ref, xs_ref, wg_ref, wu_ref, wd_ref, y_ref):
    j = pl.program_id(0)
    nt = nt_ref[0]

    @pl.when(j < nt)
    def _():
        x = xs_ref[:, 0:D_MODEL].astype(BF16)
        w = xs_ref[:, D_MODEL:D_MODEL + 1]
        a = _dot(x, wg_ref[0].astype(BF16))
        u = _dot(x, wu_ref[0].astype(BF16))
        he = (a * _sigmoid(a)) * u
        y_ref[...] = _dot(he.astype(BF16), wd_ref[0].astype(BF16)) * w

    @pl.when(j >= nt)
    def _():
        y_ref[...] = jnp.zeros_like(y_ref)


def _moe_gemm(tile_expert, n_tiles, xs, w_gate, w_up, w_down):
    n_tiles_max = tile_expert.shape[0]
    clamp = lambda j, nt: jnp.minimum(j, jnp.maximum(nt[0] - 1, 0))
    wmap = lambda j, te, nt: (te[j], 0, 0)
    return pl.pallas_call(
        _moe_gemm_kernel,
        grid_spec=pltpu.PrefetchScalarGridSpec(
            num_scalar_prefetch=2,
            grid=(n_tiles_max,),
            in_specs=[pl.BlockSpec((MOE_TILE, XS_W), lambda j, te, nt: (clamp(j, nt), 0)),
                      pl.BlockSpec((1, D_MODEL, D_EXPERT), wmap),
                      pl.BlockSpec((1, D_MODEL, D_EXPERT), wmap),
                      pl.BlockSpec((1, D_EXPERT, D_MODEL), wmap)],
            out_specs=pl.BlockSpec((MOE_TILE, D_MODEL), lambda j, te, nt: (j, 0))),
        out_shape=jax.ShapeDtypeStruct((n_tiles_max * MOE_TILE, D_MODEL), F32),
        compiler_params=_cparams("arbitrary"),
        name="moe_gemm",
    )(tile_expert, n_tiles, xs, w_gate, w_up, w_down)


def _moe_combine_kernel(dst_ref, nch_ref, lp_ref, x1_ref, mod_ref, gpost_ref, y_hbm, o_ref, ybuf, sem):
    i = pl.program_id(0)
    n = pl.num_programs(0)
    slot = i % 2

    def fetch(step, s):
        def body(c, carry):
            _chunk_copy(y_hbm, dst_ref[step, c], ybuf.at[s], c, sem.at[s]).start()
            return carry
        lax.fori_loop(0, nch_ref[step], body, 0)

    @pl.when(i == 0)
    def _():
        ybuf[...] = jnp.zeros_like(ybuf)
        fetch(0, 0)

    @pl.when(i + 1 < n)
    def _():
        fetch(i + 1, 1 - slot)

    def wait(c, carry):
        _chunk_copy(y_hbm, 0, ybuf.at[slot], 0, sem.at[slot]).wait()
        return carry
    lax.fori_loop(0, nch_ref[i], wait, 0)

    lp = lp_ref[...]
    col = lax.broadcasted_iota(jnp.int32, (MOE_TD, MOE_L), 1)
    unperm = jnp.where((col == lp[:, 0:1]) | (col == lp[:, 1:2]), 1.0, 0.0).astype(BF16)
    moe = _dot(unperm, ybuf[slot].astype(BF16))
    gt2 = mod_ref[0, 5:6, :]
    o_ref[...] = x1_ref[...] + gt2 * _rms(moe, gpost_ref[...])


def _moe_combine(dst, n_chunks, lp, x1, mod, mod_row, y_sorted, g_post2):
    n = x1.shape[0]
    tok = lambda i, *_: (i, 0)
    return pl.pallas_call(
        _moe_combine_kernel,
        grid_spec=pltpu.PrefetchScalarGridSpec(
            num_scalar_prefetch=2,
            grid=(n // MOE_TD,),
            in_specs=[pl.BlockSpec((MOE_TD, LANE), tok),
                      pl.BlockSpec((MOE_TD, D_MODEL), tok),
                      pl.BlockSpec((1, 6, D_MODEL), lambda i, *_: (mod_row(i), 0, 0)),
                      pl.BlockSpec((1, D_MODEL), lambda i, *_: (0, 0)),
                      pl.BlockSpec(memory_space=pl.ANY)],
            out_specs=pl.BlockSpec((MOE_TD, D_MODEL), tok),
            scratch_shapes=[pltpu.VMEM((2, MOE_L, D_MODEL), F32), pltpu.SemaphoreType.DMA((2,))]),
        out_shape=jax.ShapeDtypeStruct((n, D_MODEL), F32),
        compiler_params=_cparams("arbitrary"),
        name="moe_combine",
    )(dst, n_chunks, lp, x1, mod, g_post2, y_sorted)


def _moe(x1, h2, ri, rw, mod, mod_row, w_gate, w_up, w_down, g_post2):
    n = x1.shape[0]
    steps = n // MOE_TD
    n_tiles_max = (steps * MOE_NCH + N_EXPERTS * (TILE_CH - 1)) // TILE_CH + 1
    dst, n_chunks, tail_start, tail_len, tile_expert, n_tiles = _plan(_count(ri), n_tiles_max)
    xs, lp = _dispatch(dst, n_chunks, tail_start, tail_len, n_tiles, h2, ri, rw, n_tiles_max)
    y_sorted = _moe_gemm(tile_expert, n_tiles, xs, w_gate, w_up, w_down)
    return _moe_combine(dst, n_chunks, lp, x1, mod, mod_row, y_sorted, g_post2)


def _rotate_half_cols(w):
    n = w.shape[-1]
    w4 = w.reshape(w.shape[:-1] + (n // 32, 2, 16))
    return jnp.stack([-w4[..., 1, :], w4[..., 0, :]], axis=-2).reshape(w.shape)


def _pad_lanes(w):
    return jnp.concatenate([w, jnp.zeros(w.shape[:-1] + (LANE - w.shape[-1],), w.dtype)], axis=-1)


def _rope_tables(t):
    rows = t // GRID_W
    n_freq = D_ROPE // 4
    freqs = ROPE_BASE ** (-jnp.arange(n_freq, dtype=F32) / n_freq)
    row = jnp.repeat(jnp.arange(rows, dtype=F32), GRID_W)
    col = jnp.tile(jnp.arange(GRID_W, dtype=F32), rows)
    ang_r = row[:, None] * freqs
    ang_c = col[:, None] * freqs
    cos = jnp.concatenate([jnp.cos(ang_r), jnp.cos(ang_r), jnp.cos(ang_c), jnp.cos(ang_c)], axis=-1)
    sin = jnp.concatenate([jnp.sin(ang_r), jnp.sin(ang_r), jnp.sin(ang_c), jnp.sin(ang_c)], axis=-1)
    return _pad_lanes(cos), _pad_lanes(sin)


def kernel(x_prompt, x_sample, cache_ckv, cache_krope, c, c_ctx, w_ada, b_ada, g_pre1, g_post1, g_pre2, g_post2, w_in, w_dw, b_dw, conv_ln_g, conv_ln_b, q_norm_g, kv_norm_g, w_uq, w_ukv, w_out, w_rg, b_rg, w_re, b_re, w_gate, w_up, w_down):
    nb, seq, d = x_prompt.shape
    db, dseq, _ = x_sample.shape
    l = 0

    cvec = jnp.concatenate([c_ctx[None, :], c, jnp.zeros((8 - 1 - db, d), F32)], axis=0)
    mod = _ada(cvec, w_ada[l], b_ada[l]).reshape(8, 6, d)

    o_kr = 2 * C_CONV + Q_LORA + KV_LORA
    w_in_l = w_in[l]
    w_kr = w_in_l[:, o_kr:]
    w_in_ctx = jnp.concatenate([w_in_l[:, :o_kr], _pad_lanes(w_kr)], axis=-1).astype(BF16)
    w_in_lat = jnp.concatenate([w_in_l[:, :o_kr], _pad_lanes(w_kr), _pad_lanes(_rotate_half_cols(w_kr))],
                               axis=-1).astype(BF16)
    wuq = w_uq[l].reshape(Q_LORA, N_HEADS, D_NOPE + D_ROPE)
    wq_rope = wuq[:, :, D_NOPE:]
    wq = jnp.concatenate([wuq[:, :, :D_NOPE], _pad_lanes(wq_rope)], axis=-1)
    wq = wq.reshape(Q_LORA, N_HEADS * HEAD_W).astype(BF16)
    wqs = _pad_lanes(_rotate_half_cols(wq_rope)).reshape(Q_LORA, N_HEADS * LANE).astype(BF16)
    wukv4 = w_ukv[l].reshape(KV_LORA, N_HEADS, D_NOPE + D_V)
    wukv = jnp.concatenate([wukv4[:, :, :D_NOPE].reshape(KV_LORA, N_HEADS * D_NOPE),
                            wukv4[:, :, D_NOPE:].reshape(KV_LORA, N_HEADS * D_V)], axis=-1).astype(BF16)
    w_out_b = w_out[l].astype(BF16)
    w_r = jnp.concatenate([w_re[l], w_rg[l], jnp.zeros((d, LANE - N_EXPERTS - N_GROUPS), F32)], axis=-1)
    wr_hi = w_r.astype(BF16)
    wr_lo = (w_r - wr_hi.astype(F32)).astype(BF16)
    b_r = jnp.concatenate([b_re[l], b_rg[l], jnp.zeros((LANE - N_EXPERTS - N_GROUPS,), F32)])[None, :]
    cos, sin = _rope_tables(dseq)
    row = lambda v: v[l][None, :]

    tm_c = 512
    xp_flat = x_prompt.reshape(1, nb * seq, d)
    hglu, q, k, v, ckv, kr = _mix_in(xp_flat, mod, 0, row(g_pre1), w_in_ctx, row(q_norm_g), row(kv_norm_g),
                                     wq, wqs, wukv, cos[:tm_c], sin[:tm_c], False, tm_c)
    per_seq = lambda a: a.reshape(nb, seq, a.shape[-1])
    conv_out = _conv(per_seq(hglu), w_dw[l], row(b_dw), row(conv_ln_g), row(conv_ln_b))
    att = _attn(per_seq(q), per_seq(k), per_seq(v))
    flat = lambda a: a.reshape(1, nb * seq, a.shape[-1])
    x1, h2, ri, rw = _post(xp_flat, flat(conv_out), flat(att), mod, 0, w_out_b, row(g_post1), row(g_pre2),
                           wr_hi, wr_lo, b_r, tm_c)
    n_c = nb * seq
    yp = _moe(x1.reshape(n_c, d), h2.reshape(n_c, d), ri.reshape(n_c, LANE), rw.reshape(n_c, LANE),
              mod, lambda i: 0, w_gate[l], w_up[l], w_down[l], row(g_post2))
    state_ckv = ckv.reshape(nb, 1, seq, KV_LORA)
    state_krope = kr.reshape(nb, 1, seq, D_ROPE)

    tm_s = 512
    kc, vc = _cache_kv(cache_ckv[:, l], cache_krope[:, l], wukv)
    hglu, q, k, v, _, _ = _mix_in(x_sample, mod, 1, row(g_pre1), w_in_lat, row(q_norm_g), row(kv_norm_g),
                                  wq, wqs, wukv, cos, sin, True, tm_s)
    conv_out = _conv(hglu, w_dw[l], row(b_dw), row(conv_ln_g), row(conv_ln_b))
    att = _attn(q, k, v, kc, vc)
    x1, h2, ri, rw = _post(x_sample, conv_out, att, mod, 1, w_out_b, row(g_post1), row(g_pre2),
                           wr_hi, wr_lo, b_r, tm_s)
    n_s = db * dseq
    steps_per_req = dseq // MOE_TD
    ys = _moe(x1.reshape(n_s, d), h2.reshape(n_s, d), ri.reshape(n_s, LANE), rw.reshape(n_s, LANE),
              mod, lambda i: 1 + i // steps_per_req, w_gate[l], w_up[l], w_down[l], row(g_post2))

    return (yp.reshape(nb, seq, d), ys.reshape(db, dseq, d), state_ckv, state_krope)
```

```python
import functools

import jax
import jax.numpy as jnp
import numpy as np
from jax import lax
from jax.experimental import pallas as pl
from jax.experimental.pallas import tpu as pltpu

D_MODEL = 1024
GRID_W = 64
C_CONV = 512
CONV_K = 31
N_HEADS = 4
D_NOPE = 128
D_ROPE = 64
D_V = 128
Q_LORA = 384
KV_LORA = 256
N_GROUPS = 4
E_PER_GROUP = 8
N_EXPERTS = 32
D_EXPERT = 256
ROPE_BASE = 10000.0
EPS = 1e-6
ATT_SCALE = (D_NOPE + D_ROPE) ** -0.5

LANE = 128
SUBLANE = 8
HEAD_W = 2 * LANE
CONV_HALO = 16
MOE_TD = 512
CHUNK = 2 * SUBLANE
MOE_NCH = (2 * MOE_TD + N_EXPERTS * (CHUNK - 1)) // CHUNK
MOE_L = MOE_NCH * CHUNK
MOE_TILE = 256
TILE_CH = MOE_TILE // CHUNK
VMEM_LIMIT = 56 * 1024 * 1024

BF16 = jnp.bfloat16
F32 = jnp.float32


def _cparams(*sem):
    return pltpu.CompilerParams(dimension_semantics=sem, vmem_limit_bytes=VMEM_LIMIT)


def _rms(x, g):
    return x * lax.rsqrt(jnp.mean(x * x, axis=-1, keepdims=True) + EPS) * g


def _sigmoid(x):
    return 1.0 / (1.0 + jnp.exp(-x))


def _dot(a, b):
    return jnp.dot(a, b, preferred_element_type=F32)


def _chunk_copy(src, src_chunk, dst, dst_chunk, sem):
    rows = lambda c: pl.ds(c * CHUNK if isinstance(c, int) else pl.multiple_of(c * CHUNK, CHUNK), CHUNK)
    return pltpu.make_async_copy(src.at[rows(src_chunk), :], dst.at[rows(dst_chunk), :], sem)


def _ada_kernel(c_ref, w_ref, b_ref, o_ref):
    c = c_ref[...]
    s = (c * _sigmoid(c)).astype(BF16)
    o_ref[...] = _dot(s, w_ref[...].astype(BF16)) + b_ref[...]


def _ada(cvec, w_ada, b_ada):
    n = w_ada.shape[1]
    tn = 1536
    return pl.pallas_call(
        _ada_kernel,
        grid=(n // tn,),
        in_specs=[pl.BlockSpec((8, D_MODEL), lambda j: (0, 0)),
                  pl.BlockSpec((D_MODEL, tn), lambda j: (0, j)),
                  pl.BlockSpec((1, tn), lambda j: (0, j))],
        out_specs=pl.BlockSpec((8, tn), lambda j: (0, j)),
        out_shape=jax.ShapeDtypeStruct((8, n), F32),
        compiler_params=_cparams("arbitrary"),
        name="ada",
    )(cvec, w_ada, b_ada.reshape(1, n))


def _mix_in_kernel(rope, x_ref, mod_ref, gpre_ref, win_ref, qg_ref, kvg_ref, wq_ref, wqs_ref,
                   wukv_ref, cos_ref, sin_ref, hglu_ref, q_ref, k_ref, v_ref, ckv_ref, kr_ref):
    x = x_ref[0]
    sh1 = mod_ref[0, 0:1, :]
    sc1 = mod_ref[0, 1:2, :]
    h = _rms(x, gpre_ref[...]) * (1.0 + sc1) + sh1
    u = _dot(h.astype(BF16), win_ref[...])

    a = u[:, :C_CONV]
    g = u[:, C_CONV:2 * C_CONV]
    hglu_ref[0] = a * _sigmoid(g)

    o_q = 2 * C_CONV
    o_kv = o_q + Q_LORA
    o_kr = o_kv + KV_LORA
    qn = _rms(u[:, o_q:o_kv], qg_ref[...]).astype(BF16)
    qf = _dot(qn, wq_ref[...])
    ckv = _rms(u[:, o_kv:o_kr], kvg_ref[...])
    ckv_ref[0] = ckv
    kvd = _dot(ckv.astype(BF16), wukv_ref[...])
    kr = u[:, o_kr:o_kr + LANE]
    kr_ref[0] = kr[:, :D_ROPE]
    if rope:
        cos = cos_ref[...]
        sin = sin_ref[...]
        qs = _dot(qn, wqs_ref[...])
        kr = kr * cos + u[:, o_kr + LANE:o_kr + 2 * LANE] * sin
    q_parts = []
    k_parts = []
    for hd in range(N_HEADS):
        q_parts.append(qf[:, hd * HEAD_W:hd * HEAD_W + LANE])
        qr = qf[:, hd * HEAD_W + LANE:(hd + 1) * HEAD_W]
        if rope:
            qr = qr * cos + qs[:, hd * LANE:(hd + 1) * LANE] * sin
        q_parts.append(qr)
        k_parts.append(kvd[:, hd * D_NOPE:(hd + 1) * D_NOPE])
        k_parts.append(kr)
    q_ref[0] = jnp.concatenate(q_parts, axis=-1).astype(BF16)
    k_ref[0] = jnp.concatenate(k_parts, axis=-1).astype(BF16)
    v_ref[0] = kvd[:, N_HEADS * D_NOPE:].astype(BF16)


def _mix_in(x, mod, mod_row0, g_pre1, w_in_ext, q_norm_g, kv_norm_g, wq, wqs, wukv, cos, sin, rope, tm):
    b, t, _ = x.shape
    ncol = w_in_ext.shape[1]
    const = lambda bi, i: (0, 0)
    tok = lambda bi, i: (bi, i, 0)
    table = (lambda bi, i: (i, 0)) if rope else const
    outs = [(C_CONV, F32), (N_HEADS * HEAD_W, BF16), (N_HEADS * HEAD_W, BF16), (N_HEADS * D_V, BF16),
            (KV_LORA, F32), (D_ROPE, F32)]
    return pl.pallas_call(
        functools.partial(_mix_in_kernel, rope),
        grid=(b, t // tm),
        in_specs=[pl.BlockSpec((1, tm, D_MODEL), tok),
                  pl.BlockSpec((1, 6, D_MODEL), lambda bi, i: (mod_row0 + bi, 0, 0)),
                  pl.BlockSpec((1, D_MODEL), const),
                  pl.BlockSpec((D_MODEL, ncol), const),
                  pl.BlockSpec((1, Q_LORA), const),
                  pl.BlockSpec((1, KV_LORA), const),
                  pl.BlockSpec((Q_LORA, N_HEADS * HEAD_W), const),
                  pl.BlockSpec((Q_LORA, N_HEADS * LANE), const),
                  pl.BlockSpec((KV_LORA, N_HEADS * (D_NOPE + D_V)), const),
                  pl.BlockSpec((tm, LANE), table),
                  pl.BlockSpec((tm, LANE), table)],
        out_specs=[pl.BlockSpec((1, tm, w), tok) for w, _ in outs],
        out_shape=[jax.ShapeDtypeStruct((b, t, w), dt) for w, dt in outs],
        compiler_params=_cparams("arbitrary", "arbitrary"),
        name="mix_in_rope" if rope else "mix_in",
    )(x, mod, g_pre1, w_in_ext, q_norm_g, kv_norm_g, wq, wqs, wukv, cos, sin)


def _conv_kernel(t, tb, tt, h_ref, w_ref, b_ref, g_ref, bb_ref, o_ref, pad_ref, sh_ref):
    zeros = jnp.zeros((CONV_HALO, C_CONV), F32)
    pad_ref[0:CONV_HALO, :] = zeros
    pad_ref[CONV_HALO + t:, :] = zeros
    pad_ref[CONV_HALO:CONV_HALO + t, :] = h_ref[0]
    first = CONV_HALO - CONV_K // 2
    rows = tb + 2 * CONV_HALO - SUBLANE

    def block(bi, carry):
        base = pl.multiple_of(bi * tb, tb)
        win = pad_ref[pl.ds(base, tb + 2 * CONV_HALO), :]
        for s in range(SUBLANE):
            sh_ref[s, 0:rows, :] = win[s:s + rows, :]
        for c in range(tb // tt):
            acc = jnp.zeros((tt, C_CONV), F32)
            for k in range(CONV_K):
                off = first + k
                r0 = c * tt + off // SUBLANE * SUBLANE
                acc = acc + sh_ref[off % SUBLANE, r0:r0 + tt, :] * w_ref[k:k + 1, :]
            y = acc + b_ref[...]
            mu = jnp.mean(y, axis=-1, keepdims=True)
            yc = y - mu
            var = jnp.mean(yc * yc, axis=-1, keepdims=True)
            z = yc * lax.rsqrt(var + EPS) * g_ref[...] + bb_ref[...]
            o_ref[0, pl.ds(base + c * tt, tt), :] = (z * _sigmoid(z)).astype(BF16)
        return carry

    lax.fori_loop(0, t // tb, block, 0)


def _conv(hglu, w_dw, b_dw, ln_g, ln_b):
    b, t, _ = hglu.shape
    tb, tt = 256, 32
    const = lambda bi: (0, 0)
    return pl.pallas_call(
        functools.partial(_conv_kernel, t, tb, tt),
        grid=(b,),
        in_specs=[pl.BlockSpec((1, t, C_CONV), lambda bi: (bi, 0, 0)),
                  pl.BlockSpec((CONV_K, C_CONV), const),
                  pl.BlockSpec((1, C_CONV), const),
                  pl.BlockSpec((1, C_CONV), const),
                  pl.BlockSpec((1, C_CONV), const)],
        out_specs=pl.BlockSpec((1, t, C_CONV), lambda bi: (bi, 0, 0)),
        out_shape=jax.ShapeDtypeStruct((b, t, C_CONV), BF16),
        scratch_shapes=[pltpu.VMEM((t + 2 * CONV_HALO, C_CONV), F32),
                        pltpu.VMEM((SUBLANE, tb + 2 * CONV_HALO, C_CONV), F32)],
        compiler_params=_cparams("arbitrary"),
        name="conv",
    )(hglu, w_dw, b_dw, ln_g, ln_b)


def _qk(q, k):
    return lax.dot_general(q, k, (((1,), (1,)), ((), ())), preferred_element_type=F32)


def _attn_kernel(cached, q_ref, k_ref, v_ref, *rest):
    if cached:
        kc_ref, vc_ref, o_ref = rest
    else:
        (o_ref,) = rest
    outs = []
    for hd in range(N_HEADS):
        ks = slice(hd * HEAD_W, (hd + 1) * HEAD_W)
        vs = slice(hd * D_V, (hd + 1) * D_V)
        qh = q_ref[0, :, ks]
        s = _qk(qh, k_ref[0, :, ks]) * ATT_SCALE
        m = jnp.max(s, axis=-1, keepdims=True)
        if cached:
            sc = _qk(qh, kc_ref[0, :, ks]) * ATT_SCALE
            m = jnp.maximum(m, jnp.max(sc, axis=-1, keepdims=True))
        p = jnp.exp(s - m)
        l = jnp.sum(p, axis=-1, keepdims=True)
        o = _dot(p.astype(BF16), v_ref[0, :, vs])
        if cached:
            pc = jnp.exp(sc - m)
            l = l + jnp.sum(pc, axis=-1, keepdims=True)
            o = o + _dot(pc.astype(BF16), vc_ref[0, :, vs])
        outs.append(o / l)
    o_ref[0] = jnp.concatenate(outs, axis=-1).astype(BF16)


def _attn(q, k, v, kc=None, vc=None, tq=256):
    b, t, _ = q.shape
    s = k.shape[1]
    cached = kc is not None
    whole = lambda bi, i: (bi, 0, 0)
    in_specs = [pl.BlockSpec((1, tq, N_HEADS * HEAD_W), lambda bi, i: (bi, i, 0)),
                pl.BlockSpec((1, s, N_HEADS * HEAD_W), whole),
                pl.BlockSpec((1, s, N_HEADS * D_V), whole)]
    args = [q, k, v]
    if cached:
        sc = kc.shape[1]
        in_specs += [pl.BlockSpec((1, sc, N_HEADS * HEAD_W), whole),
                     pl.BlockSpec((1, sc, N_HEADS * D_V), whole)]
        args += [kc, vc]
    return pl.pallas_call(
        functools.partial(_attn_kernel, cached),
        grid=(b, t // tq),
        in_specs=in_specs,
        out_specs=pl.BlockSpec((1, tq, N_HEADS * D_V), lambda bi, i: (bi, i, 0)),
        out_shape=jax.ShapeDtypeStruct((b, t, N_HEADS * D_V), BF16),
        compiler_params=_cparams("arbitrary", "arbitrary"),
        name="attn_cached" if cached else "attn",
    )(*args)


def _cache_kv_kernel(ckv_ref, kr_ref, wukv_ref, k_ref, v_ref):
    kvd = _dot(ckv_ref[0].astype(BF16), wukv_ref[...])
    kr = kr_ref[0]
    kr = jnp.concatenate([kr, jnp.zeros_like(kr)], axis=-1)
    parts = []
    for hd in range(N_HEADS):
        parts.append(kvd[:, hd * D_NOPE:(hd + 1) * D_NOPE])
        parts.append(kr)
    k_ref[0] = jnp.concatenate(parts, axis=-1).astype(BF16)
    v_ref[0] = kvd[:, N_HEADS * D_NOPE:].astype(BF16)


def _cache_kv(ckv, krope, wukv):
    b, s, _ = ckv.shape
    tok = lambda bi: (bi, 0, 0)
    return pl.pallas_call(
        _cache_kv_kernel,
        grid=(b,),
        in_specs=[pl.BlockSpec((1, s, KV_LORA), tok),
                  pl.BlockSpec((1, s, D_ROPE), tok),
                  pl.BlockSpec((KV_LORA, N_HEADS * (D_NOPE + D_V)), lambda bi: (0, 0))],
        out_specs=[pl.BlockSpec((1, s, N_HEADS * HEAD_W), tok),
                   pl.BlockSpec((1, s, N_HEADS * D_V), tok)],
        out_shape=[jax.ShapeDtypeStruct((b, s, N_HEADS * HEAD_W), BF16),
                   jax.ShapeDtypeStruct((b, s, N_HEADS * D_V), BF16)],
        compiler_params=_cparams("arbitrary"),
        name="cache_kv",
    )(ckv, krope, wukv)


def _post_kernel(x_ref, conv_ref, att_ref, mod_ref, wo_ref, gpost_ref, gpre2_ref, wr_hi_ref, wr_lo_ref,
                 br_ref, x1_ref, h2_ref, ri_ref, rw_ref, cnt_ref):
    out = _dot(conv_ref[0], wo_ref[:C_CONV, :]) + _dot(att_ref[0], wo_ref[C_CONV:, :])
    gt1 = mod_ref[0, 2:3, :]
    sh2 = mod_ref[0, 3:4, :]
    sc2 = mod_ref[0, 4:5, :]
    x1 = x_ref[0] + gt1 * _rms(out, gpost_ref[...])
    x1_ref[0] = x1
    h2 = _rms(x1, gpre2_ref[...]) * (1.0 + sc2) + sh2
    h_hi = h2.astype(BF16)
    h2_ref[0] = h_hi
    h_lo = (h2 - h_hi.astype(F32)).astype(BF16)
    w_hi = wr_hi_ref[...]
    logits = _dot(h_hi, w_hi) + _dot(h_lo, w_hi) + _dot(h_hi, wr_lo_ref[...]) + br_ref[...]

    lane = lax.broadcasted_iota(jnp.int32, logits.shape, 1)
    neg = jnp.float32(-jnp.inf)
    big = jnp.int32(LANE)
    is_g = (lane >= N_EXPERTS) & (lane < N_EXPERTS + N_GROUPS)
    lg = jnp.where(is_g, logits, neg)
    gmax = jnp.max(lg, axis=-1, keepdims=True)
    gidx = jnp.min(jnp.where(lg == gmax, lane, big), axis=-1, keepdims=True) - N_EXPERTS
    g_top = 1.0 / jnp.sum(jnp.exp(lg - gmax), axis=-1, keepdims=True)

    in_grp = (lane >= gidx * E_PER_GROUP) & (lane < (gidx + 1) * E_PER_GROUP)
    le = jnp.where(in_grp, logits, neg)
    m1 = jnp.max(le, axis=-1, keepdims=True)
    i1 = jnp.min(jnp.where(le == m1, lane, big), axis=-1, keepdims=True)
    le2 = jnp.where(lane == i1, neg, le)
    m2 = jnp.max(le2, axis=-1, keepdims=True)
    i2 = jnp.min(jnp.where(le2 == m2, lane, big), axis=-1, keepdims=True)
    r = jnp.exp(m2 - m1)
    w1 = g_top / (1.0 + r)
    w2 = g_top * r / (1.0 + r)
    ri_ref[0] = jnp.where(lane == 0, i1, jnp.where(lane == 1, i2, 0))
    rw_ref[0] = jnp.where(lane == 0, w1, jnp.where(lane == 1, w2, 0.0))
    pairs = jnp.sum(jnp.where((lane == i1) | (lane == i2), 1.0, 0.0), axis=0, keepdims=True)
    cnt_ref[0, 0] = jnp.broadcast_to(pairs, (SUBLANE, LANE))


def _post(x, conv_out, att, mod, mod_row0, w_out, g_post1, g_pre2, wr_hi, wr_lo, br):
    b, t, _ = x.shape
    tm = MOE_TD
    const = lambda bi, i: (0, 0)
    tok = lambda bi, i: (bi, i, 0)
    return pl.pallas_call(
        _post_kernel,
        grid=(b, t // tm),
        in_specs=[pl.BlockSpec((1, tm, D_MODEL), tok),
                  pl.BlockSpec((1, tm, C_CONV), tok),
                  pl.BlockSpec((1, tm, N_HEADS * D_V), tok),
                  pl.BlockSpec((1, 6, D_MODEL), lambda bi, i: (mod_row0 + bi, 0, 0)),
                  pl.BlockSpec((D_MODEL, D_MODEL), const),
                  pl.BlockSpec((1, D_MODEL), const),
                  pl.BlockSpec((1, D_MODEL), const),
                  pl.BlockSpec((D_MODEL, LANE), const),
                  pl.BlockSpec((D_MODEL, LANE), const),
                  pl.BlockSpec((1, LANE), const)],
        out_specs=[pl.BlockSpec((1, tm, D_MODEL), tok),
                   pl.BlockSpec((1, tm, D_MODEL), tok),
                   pl.BlockSpec((1, tm, LANE), tok),
                   pl.BlockSpec((1, tm, LANE), tok),
                   pl.BlockSpec((1, 1, SUBLANE, LANE), lambda bi, i: (bi, i, 0, 0))],
        out_shape=[jax.ShapeDtypeStruct((b, t, D_MODEL), F32),
                   jax.ShapeDtypeStruct((b, t, D_MODEL), BF16),
                   jax.ShapeDtypeStruct((b, t, LANE), jnp.int32),
                   jax.ShapeDtypeStruct((b, t, LANE), F32),
                   jax.ShapeDtypeStruct((b, t // tm, SUBLANE, LANE), F32)],
        compiler_params=_cparams("arbitrary", "arbitrary"),
        name="post",
    )(x, conv_out, att, mod, w_out, g_post1, g_pre2, wr_hi, wr_lo, br)


def _one_hots(ri):
    lane = lax.broadcasted_iota(jnp.int32, ri.shape, 1)
    oh1 = lane == ri[:, 0:1]
    oh2 = lane == ri[:, 1:2]
    return oh1, oh2, jnp.where(oh1 | oh2, 1.0, 0.0)


def _local_positions(oh1, oh2, oh):
    td = oh.shape[0]
    cnt = jnp.sum(oh, axis=0, keepdims=True)
    nch = jnp.floor((cnt + (CHUNK - 1)) * (1.0 / CHUNK))
    a = lax.broadcasted_iota(jnp.int32, (LANE, LANE), 0)
    b = lax.broadcasted_iota(jnp.int32, (LANE, LANE), 1)
    lower_experts = jnp.where(a < b, 1.0, 0.0).astype(BF16)
    run_start = _dot(jnp.broadcast_to(nch, (SUBLANE, LANE)).astype(BF16), lower_experts)[0:1, :] * CHUNK
    rr = lax.broadcasted_iota(jnp.int32, (td, td), 0)
    cc = lax.broadcasted_iota(jnp.int32, (td, td), 1)
    earlier = jnp.where(cc < rr, 1.0, 0.0).astype(BF16)
    pos = _dot(earlier, oh.astype(BF16)) + run_start
    lp1 = jnp.sum(jnp.where(oh1, pos, 0.0), axis=-1, keepdims=True)
    lp2 = jnp.sum(jnp.where(oh2, pos, 0.0), axis=-1, keepdims=True)
    return lp1, lp2


def _plan(cnt, n_tiles_max):
    k = (cnt + (CHUNK - 1)) // CHUNK
    run_end = jnp.cumsum(k, axis=1)
    run_start = run_end - k
    n_chunks = run_end[:, -1]
    total = jnp.sum(k, axis=0)
    padded = (total + (TILE_CH - 1)) // TILE_CH * TILE_CH
    seg_end = jnp.cumsum(padded)
    seg_start = seg_end - padded
    base = seg_start[None, :] + jnp.cumsum(k, axis=0) - k
    c = jnp.arange(MOE_NCH, dtype=jnp.int32)[None, :, None]
    in_run = (run_start[:, None, :] <= c) & (c < run_end[:, None, :])
    dst = jnp.sum(jnp.where(in_run, base[:, None, :] + c - run_start[:, None, :], 0), axis=2)
    n_tiles = seg_end[-1] // TILE_CH
    tile_chunk0 = jnp.arange(n_tiles_max, dtype=jnp.int32) * TILE_CH
    tile_expert = jnp.sum((seg_end[None, :] <= tile_chunk0[:, None]).astype(jnp.int32), axis=1)
    tile_expert = jnp.minimum(tile_expert, N_EXPERTS - 1)
    last = tile_expert[jnp.maximum(n_tiles - 1, 0)]
    tile_expert = jnp.where(jnp.arange(n_tiles_max) < n_tiles, tile_expert, last)
    tail_start = seg_start + total
    tail_len = padded - total
    i32 = lambda t: t.astype(jnp.int32)
    return (i32(dst), i32(n_chunks), i32(tail_start), i32(tail_len), i32(tile_expert), i32(n_tiles.reshape(1)))


def _dispatch_kernel(steps_a, dst_ref, nch_ref, ts_ref, tl_ref, nt_ref, h2a_ref, ria_ref, rwa_ref, h2b_ref, rib_ref,
                     rwb_ref, xs_hbm, ws_hbm, lp_ref, xbuf, wbuf, zx, zw, sem, zsem):
    i = pl.program_id(0)
    n = pl.num_programs(0)
    slot = i % 2
    td = MOE_TD
    n_tiles_max = xs_hbm.shape[0] // MOE_TILE

    def copies(src_x, src_w, src_chunk, dst_chunk, s):
        return (_chunk_copy(src_x, src_chunk, xs_hbm, dst_chunk, s), _chunk_copy(src_w, src_chunk, ws_hbm, dst_chunk, s))

    def zero_tile(m):
        rows = pl.ds(pl.multiple_of(m * MOE_TILE, MOE_TILE), MOE_TILE)
        return (pltpu.make_async_copy(zx, xs_hbm.at[rows, :], zsem.at[1]),
                pltpu.make_async_copy(zw, ws_hbm.at[rows, :], zsem.at[1]))

    @pl.when(i == 0)
    def _():
        zx[...] = jnp.zeros_like(zx)
        zw[...] = jnp.zeros_like(zw)
        for e in range(N_EXPERTS):
            def fill(m, carry, e=e):
                for cp in copies(zx, zw, 0, ts_ref[e] + m, zsem.at[0]):
                    cp.start()
                return carry
            lax.fori_loop(0, tl_ref[e], fill, 0)

        def fill_tile(m, carry):
            for cp in zero_tile(m):
                cp.start()
            return carry
        lax.fori_loop(nt_ref[0], n_tiles_max, fill_tile, 0)

    from_a = i < steps_a
    ri = jnp.where(from_a, ria_ref[...], rib_ref[...])
    rw = jnp.where(from_a, rwa_ref[...], rwb_ref[...])
    h2 = jnp.where(from_a, h2a_ref[...], h2b_ref[...])
    oh1, oh2, oh = _one_hots(ri)
    lp1, lp2 = _local_positions(oh1, oh2, oh)
    lane = lax.broadcasted_iota(jnp.int32, ri.shape, 1)
    lp_ref[...] = jnp.where(lane == 0, lp1, jnp.where(lane == 1, lp2, 0.0)).astype(jnp.int32)

    rr = lax.broadcasted_iota(jnp.int32, (td, td), 0)
    cc = lax.broadcasted_iota(jnp.int32, (td, td), 1)
    to_lanes = lambda col: jnp.sum(jnp.where(rr == cc, col, 0.0), axis=0, keepdims=True)
    row = lax.broadcasted_iota(jnp.int32, (MOE_L, td), 0).astype(F32)
    p1 = row == to_lanes(lp1)
    p2 = row == to_lanes(lp2)
    perm = jnp.where(p1 | p2, 1.0, 0.0).astype(BF16)
    xbuf[slot] = _dot(perm, h2).astype(BF16)
    w = jnp.sum(jnp.where(p1, to_lanes(rw[:, 0:1]), 0.0) + jnp.where(p2, to_lanes(rw[:, 1:2]), 0.0),
                axis=-1, keepdims=True)
    wbuf[slot] = jnp.broadcast_to(w, (MOE_L, LANE))

    def wait_chunks(count, s):
        def body(c, carry):
            for cp in copies(xbuf.at[s], wbuf.at[s], 0, 0, sem.at[s]):
                cp.wait()
            return carry
        lax.fori_loop(0, count, body, 0)

    @pl.when(i > 0)
    def _():
        wait_chunks(nch_ref[i - 1], 1 - slot)

    def send(c, carry):
        for cp in copies(xbuf.at[slot], wbuf.at[slot], c, dst_ref[i, c], sem.at[slot]):
            cp.start()
        return carry
    lax.fori_loop(0, nch_ref[i], send, 0)

    @pl.when(i == n - 1)
    def _():
        wait_chunks(nch_ref[i], slot)
        for e in range(N_EXPERTS):
            def drain(m, carry):
                for cp in copies(zx, zw, 0, 0, zsem.at[0]):
                    cp.wait()
                return carry
            lax.fori_loop(0, tl_ref[e], drain, 0)

        def drain_tile(m, carry):
            for cp in zero_tile(m):
                cp.wait()
            return carry
        lax.fori_loop(nt_ref[0], n_tiles_max, drain_tile, 0)


def _dispatch(dst, n_chunks, tail_start, tail_len, n_tiles, part_a, part_b, n_tiles_max):
    steps_a = part_a[0].shape[0] // MOE_TD
    steps_b = part_b[0].shape[0] // MOE_TD
    in_a = lambda i, *_: (jnp.minimum(i, steps_a - 1), 0)
    in_b = lambda i, *_: (jnp.maximum(i - steps_a, 0), 0)
    specs = lambda f: [pl.BlockSpec((MOE_TD, D_MODEL), f), pl.BlockSpec((MOE_TD, LANE), f),
                       pl.BlockSpec((MOE_TD, LANE), f)]
    rows = n_tiles_max * MOE_TILE
    return pl.pallas_call(
        functools.partial(_dispatch_kernel, steps_a),
        grid_spec=pltpu.PrefetchScalarGridSpec(
            num_scalar_prefetch=5,
            grid=(steps_a + steps_b,),
            in_specs=specs(in_a) + specs(in_b),
            out_specs=[pl.BlockSpec(memory_space=pl.ANY),
                       pl.BlockSpec(memory_space=pl.ANY),
                       pl.BlockSpec((MOE_TD, LANE), lambda i, *_: (i, 0))],
            scratch_shapes=[pltpu.VMEM((2, MOE_L, D_MODEL), BF16), pltpu.VMEM((2, MOE_L, LANE), F32),
                            pltpu.VMEM((MOE_TILE, D_MODEL), BF16), pltpu.VMEM((MOE_TILE, LANE), F32),
                            pltpu.SemaphoreType.DMA((2,)), pltpu.SemaphoreType.DMA((2,))]),
        out_shape=[jax.ShapeDtypeStruct((rows, D_MODEL), BF16),
                   jax.ShapeDtypeStruct((rows, LANE), F32),
                   jax.ShapeDtypeStruct(((steps_a + steps_b) * MOE_TD, LANE), jnp.int32)],
        compiler_params=_cparams("arbitrary"),
        name="moe_dispatch",
    )(dst, n_chunks, tail_start, tail_len, n_tiles, *part_a, *part_b)


def _moe_gemm_kernel(te_ref, nt_ref, xs_ref, ws_ref, wg_ref, wu_ref, wd_ref, y_ref):
    j = pl.program_id(0)
    nt = nt_ref[0]

    @pl.when(j < nt)
    def _():
        x = xs_ref[...]
        a = _dot(x, wg_ref[0].astype(BF16))
        u = _dot(x, wu_ref[0].astype(BF16))
        he = (a * _sigmoid(a)) * u
        y = _dot(he.astype(BF16), wd_ref[0].astype(BF16)) * ws_ref[:, 0:1]
        y_ref[...] = y.astype(BF16)

    @pl.when(j >= nt)
    def _():
        y_ref[...] = jnp.zeros_like(y_ref)


def _moe_gemm(tile_expert, n_tiles, xs, ws, w_gate, w_up, w_down):
    n_tiles_max = tile_expert.shape[0]
    clamp = lambda j, te, nt: (jnp.minimum(j, jnp.maximum(nt[0] - 1, 0)), 0)
    wmap = lambda j, te, nt: (te[j], 0, 0)
    return pl.pallas_call(
        _moe_gemm_kernel,
        grid_spec=pltpu.PrefetchScalarGridSpec(
            num_scalar_prefetch=2,
            grid=(n_tiles_max,),
            in_specs=[pl.BlockSpec((MOE_TILE, D_MODEL), clamp),
                      pl.BlockSpec((MOE_TILE, LANE), clamp),
                      pl.BlockSpec((1, D_MODEL, D_EXPERT), wmap),
                      pl.BlockSpec((1, D_MODEL, D_EXPERT), wmap),
                      pl.BlockSpec((1, D_EXPERT, D_MODEL), wmap)],
            out_specs=pl.BlockSpec((MOE_TILE, D_MODEL), lambda j, te, nt: (j, 0))),
        out_shape=jax.ShapeDtypeStruct((n_tiles_max * MOE_TILE, D_MODEL), BF16),
        compiler_params=_cparams("arbitrary"),
        name="moe_gemm",
    )(tile_expert, n_tiles, xs, ws, w_gate, w_up, w_down)


def _moe_combine_kernel(step0, dst_ref, nch_ref, lp_ref, x1_ref, mod_ref, gpost_ref, y_hbm, o_ref, ybuf, sem):
    i = pl.program_id(0)
    n = pl.num_programs(0)
    slot = i % 2

    def fetch(step, s):
        def body(c, carry):
            _chunk_copy(y_hbm, dst_ref[step, c], ybuf.at[s], c, sem.at[s]).start()
            return carry
        lax.fori_loop(0, nch_ref[step], body, 0)

    @pl.when(i == 0)
    def _():
        ybuf[...] = jnp.zeros_like(ybuf)
        fetch(step0, 0)

    @pl.when(i + 1 < n)
    def _():
        fetch(step0 + i + 1, 1 - slot)

    def wait(c, carry):
        _chunk_copy(y_hbm, 0, ybuf.at[slot], 0, sem.at[slot]).wait()
        return carry
    lax.fori_loop(0, nch_ref[step0 + i], wait, 0)

    lp = lp_ref[...]
    col = lax.broadcasted_iota(jnp.int32, (MOE_TD, MOE_L), 1)
    unperm = jnp.where((col == lp[:, 0:1]) | (col == lp[:, 1:2]), 1.0, 0.0).astype(BF16)
    moe = _dot(unperm, ybuf[slot])
    gt2 = mod_ref[0, 5:6, :]
    o_ref[...] = x1_ref[...] + gt2 * _rms(moe, gpost_ref[...])


def _moe_combine(step0, dst, n_chunks, lp, x1, mod, mod_row, y_sorted, g_post2):
    n = x1.shape[0]
    tok = lambda i, *_: (i, 0)
    return pl.pallas_call(
        functools.partial(_moe_combine_kernel, step0),
        grid_spec=pltpu.PrefetchScalarGridSpec(
            num_scalar_prefetch=2,
            grid=(n // MOE_TD,),
            in_specs=[pl.BlockSpec((MOE_TD, LANE), lambda i, *_: (step0 + i, 0)),
                      pl.BlockSpec((MOE_TD, D_MODEL), tok),
                      pl.BlockSpec((1, 6, D_MODEL), lambda i, *_: (mod_row(i), 0, 0)),
                      pl.BlockSpec((1, D_MODEL), lambda i, *_: (0, 0)),
                      pl.BlockSpec(memory_space=pl.ANY)],
            out_specs=pl.BlockSpec((MOE_TD, D_MODEL), tok),
            scratch_shapes=[pltpu.VMEM((2, MOE_L, D_MODEL), BF16), pltpu.SemaphoreType.DMA((2,))]),
        out_shape=jax.ShapeDtypeStruct((n, D_MODEL), F32),
        compiler_params=_cparams("arbitrary"),
        name="moe_combine",
    )(dst, n_chunks, lp, x1, mod, g_post2, y_sorted)


def _moe(part_a, part_b, mod, mod_row_a, mod_row_b, w_gate, w_up, w_down, g_post2):
    cnt = jnp.concatenate([part_a[4], part_b[4]], axis=0)[:, 0, :N_EXPERTS].astype(jnp.int32)
    steps_a = part_a[4].shape[0]
    n_tiles_max = (cnt.shape[0] * MOE_NCH + N_EXPERTS * (TILE_CH - 1)) // TILE_CH + 1
    dst, n_chunks, tail_start, tail_len, tile_expert, n_tiles = _plan(cnt, n_tiles_max)
    xs, ws, lp = _dispatch(dst, n_chunks, tail_start, tail_len, n_tiles, part_a[1:4], part_b[1:4], n_tiles_max)
    y_sorted = _moe_gemm(tile_expert, n_tiles, xs, ws, w_gate, w_up, w_down)
    ya = _moe_combine(0, dst, n_chunks, lp, part_a[0], mod, mod_row_a, y_sorted, g_post2)
    yb = _moe_combine(steps_a, dst, n_chunks, lp, part_b[0], mod, mod_row_b, y_sorted, g_post2)
    return ya, yb


def _rotate_half_cols(w):
    n = w.shape[-1]
    w4 = w.reshape(w.shape[:-1] + (n // 32, 2, 16))
    return jnp.stack([-w4[..., 1, :], w4[..., 0, :]], axis=-2).reshape(w.shape)


def _pad_lanes(w):
    return jnp.concatenate([w, jnp.zeros(w.shape[:-1] + (LANE - w.shape[-1],), w.dtype)], axis=-1)


def _rope_tables(t):
    rows = t // GRID_W
    n_freq = D_ROPE // 4
    freqs = ROPE_BASE ** (-jnp.arange(n_freq, dtype=F32) / n_freq)
    row = jnp.repeat(jnp.arange(rows, dtype=F32), GRID_W)
    col = jnp.tile(jnp.arange(GRID_W, dtype=F32), rows)
    ang_r = row[:, None] * freqs
    ang_c = col[:, None] * freqs
    cos = jnp.concatenate([jnp.cos(ang_r), jnp.cos(ang_r), jnp.cos(ang_c), jnp.cos(ang_c)], axis=-1)
    sin = jnp.concatenate([jnp.sin(ang_r), jnp.sin(ang_r), jnp.sin(ang_c), jnp.sin(ang_c)], axis=-1)
    return _pad_lanes(cos), _pad_lanes(sin)


def kernel(x_prompt, x_sample, cache_ckv, cache_krope, c, c_ctx, w_ada, b_ada, g_pre1, g_post1, g_pre2, g_post2, w_in, w_dw, b_dw, conv_ln_g, conv_ln_b, q_norm_g, kv_norm_g, w_uq, w_ukv, w_out, w_rg, b_rg, w_re, b_re, w_gate, w_up, w_down):
    nb, seq, d = x_prompt.shape
    db, dseq, _ = x_sample.shape
    l = 0

    cvec = jnp.concatenate([c_ctx[None, :], c, jnp.zeros((8 - 1 - db, d), F32)], axis=0)
    mod = _ada(cvec, w_ada[l], b_ada[l]).reshape(8, 6, d)

    o_kr = 2 * C_CONV + Q_LORA + KV_LORA
    w_in_l = w_in[l]
    w_kr = w_in_l[:, o_kr:]
    w_in_ctx = jnp.concatenate([w_in_l[:, :o_kr], _pad_lanes(w_kr)], axis=-1).astype(BF16)
    w_in_lat = jnp.concatenate([w_in_l[:, :o_kr], _pad_lanes(w_kr), _pad_lanes(_rotate_half_cols(w_kr))],
                               axis=-1).astype(BF16)
    wuq = w_uq[l].reshape(Q_LORA, N_HEADS, D_NOPE + D_ROPE)
    wq_rope = wuq[:, :, D_NOPE:]
    wq = jnp.concatenate([wuq[:, :, :D_NOPE], _pad_lanes(wq_rope)], axis=-1)
    wq = wq.reshape(Q_LORA, N_HEADS * HEAD_W).astype(BF16)
    wqs = _pad_lanes(_rotate_half_cols(wq_rope)).reshape(Q_LORA, N_HEADS * LANE).astype(BF16)
    wukv4 = w_ukv[l].reshape(KV_LORA, N_HEADS, D_NOPE + D_V)
    wukv = jnp.concatenate([wukv4[:, :, :D_NOPE].reshape(KV_LORA, N_HEADS * D_NOPE),
                            wukv4[:, :, D_NOPE:].reshape(KV_LORA, N_HEADS * D_V)], axis=-1).astype(BF16)
    w_out_b = w_out[l].astype(BF16)
    w_r = jnp.concatenate([w_re[l], w_rg[l], jnp.zeros((d, LANE - N_EXPERTS - N_GROUPS), F32)], axis=-1)
    wr_hi = w_r.astype(BF16)
    wr_lo = (w_r - wr_hi.astype(F32)).astype(BF16)
    b_r = jnp.concatenate([b_re[l], b_rg[l], jnp.zeros((LANE - N_EXPERTS - N_GROUPS,), F32)])[None, :]
    cos, sin = _rope_tables(dseq)
    row = lambda v: v[l][None, :]

    tm_c = 512
    xp_flat = x_prompt.reshape(1, nb * seq, d)
    hglu, q, k, v, ckv, kr = _mix_in(xp_flat, mod, 0, row(g_pre1), w_in_ctx, row(q_norm_g), row(kv_norm_g),
                                     wq, wqs, wukv, cos[:tm_c], sin[:tm_c], False, tm_c)
    per_seq = lambda a: a.reshape(nb, seq, a.shape[-1])
    conv_out = _conv(per_seq(hglu), w_dw[l], row(b_dw), row(conv_ln_g), row(conv_ln_b))
    att = _attn(per_seq(q), per_seq(k), per_seq(v))
    flat = lambda a: a.reshape(1, nb * seq, a.shape[-1])
    post_c = _post(xp_flat, flat(conv_out), flat(att), mod, 0, w_out_b, row(g_post1), row(g_pre2),
                   wr_hi, wr_lo, b_r)
    state_ckv = ckv.reshape(nb, 1, seq, KV_LORA)
    state_krope = kr.reshape(nb, 1, seq, D_ROPE)

    tm_s = 512
    kc, vc = _cache_kv(cache_ckv[:, l], cache_krope[:, l], wukv)
    hglu, q, k, v, _, _ = _mix_in(x_sample, mod, 1, row(g_pre1), w_in_lat, row(q_norm_g), row(kv_norm_g),
                                  wq, wqs, wukv, cos, sin, True, tm_s)
    conv_out = _conv(hglu, w_dw[l], row(b_dw), row(conv_ln_g), row(conv_ln_b))
    att = _attn(q, k, v, kc, vc)
    post_s = _post(x_sample, conv_out, att, mod, 1, w_out_b, row(g_post1), row(g_pre2), wr_hi, wr_lo, b_r)

    tokens = lambda parts: tuple(a.reshape((-1,) + a.shape[2:]) for a in parts)
    steps_per_req = dseq // MOE_TD
    yp, ys = _moe(tokens(post_c), tokens(post_s), mod, lambda i: 0, lambda i: 1 + i // steps_per_req,
                  w_gate[l], w_up[l], w_down[l], row(g_post2))

    return (yp.reshape(nb, seq, d), ys.reshape(db, dseq, d), state_ckv, state_krope)
```

```python
import functools

import jax
import jax.numpy as jnp
import numpy as np
from jax import lax
from jax.experimental import pallas as pl
from jax.experimental.pallas import tpu as pltpu

D_MODEL = 1024
GRID_W = 64
C_CONV = 512
CONV_K = 31
N_HEADS = 4
D_NOPE = 128
D_ROPE = 64
D_V = 128
Q_LORA = 384
KV_LORA = 256
N_GROUPS = 4
E_PER_GROUP = 8
N_EXPERTS = 32
D_EXPERT = 256
ROPE_BASE = 10000.0
EPS = 1e-6
ATT_SCALE = (D_NOPE + D_ROPE) ** -0.5

LANE = 128
SUBLANE = 8
HEAD_W = 2 * LANE
CONV_HALO = 16
MOE_TD = 512
CHUNK = 2 * SUBLANE
MOE_NCH = (2 * MOE_TD + N_EXPERTS * (CHUNK - 1)) // CHUNK
MOE_L = MOE_NCH * CHUNK
MOE_TILE = 512
TILE_CH = MOE_TILE // CHUNK
VMEM_LIMIT = 56 * 1024 * 1024

BF16 = jnp.bfloat16
F32 = jnp.float32


def _cparams(*sem):
    return pltpu.CompilerParams(dimension_semantics=sem, vmem_limit_bytes=VMEM_LIMIT)


def _rms(x, g):
    return x * lax.rsqrt(jnp.mean(x * x, axis=-1, keepdims=True) + EPS) * g


def _sigmoid(x):
    return 1.0 / (1.0 + jnp.exp(-x))


def _dot(a, b):
    return jnp.dot(a, b, preferred_element_type=F32)


def _chunk_copy(src, src_chunk, dst, dst_chunk, sem):
    rows = lambda c: pl.ds(c * CHUNK if isinstance(c, int) else pl.multiple_of(c * CHUNK, CHUNK), CHUNK)
    return pltpu.make_async_copy(src.at[rows(src_chunk), :], dst.at[rows(dst_chunk), :], sem)


def _ada_kernel(c_ref, w_ref, b_ref, o_ref):
    c = c_ref[...]
    s = (c * _sigmoid(c)).astype(BF16)
    o_ref[...] = _dot(s, w_ref[...].astype(BF16)) + b_ref[...]


def _ada(cvec, w_ada, b_ada):
    n = w_ada.shape[1]
    tn = 1536
    return pl.pallas_call(
        _ada_kernel,
        grid=(n // tn,),
        in_specs=[pl.BlockSpec((8, D_MODEL), lambda j: (0, 0)),
                  pl.BlockSpec((D_MODEL, tn), lambda j: (0, j)),
                  pl.BlockSpec((1, tn), lambda j: (0, j))],
        out_specs=pl.BlockSpec((8, tn), lambda j: (0, j)),
        out_shape=jax.ShapeDtypeStruct((8, n), F32),
        compiler_params=_cparams("arbitrary"),
        name="ada",
    )(cvec, w_ada, b_ada.reshape(1, n))


def _mix_in_kernel(rope, x_ref, mod_ref, gpre_ref, win_ref, qg_ref, kvg_ref, wq_ref, wqs_ref,
                   wukv_ref, cos_ref, sin_ref, hglu_ref, q_ref, k_ref, v_ref, ckv_ref, kr_ref):
    x = x_ref[0]
    sh1 = mod_ref[0, 0:1, :]
    sc1 = mod_ref[0, 1:2, :]
    h = _rms(x, gpre_ref[...]) * (1.0 + sc1) + sh1
    u = _dot(h.astype(BF16), win_ref[...])

    a = u[:, :C_CONV]
    g = u[:, C_CONV:2 * C_CONV]
    hglu_ref[0] = a * _sigmoid(g)

    o_q = 2 * C_CONV
    o_kv = o_q + Q_LORA
    o_kr = o_kv + KV_LORA
    qn = _rms(u[:, o_q:o_kv], qg_ref[...]).astype(BF16)
    qf = _dot(qn, wq_ref[...])
    ckv = _rms(u[:, o_kv:o_kr], kvg_ref[...])
    ckv_ref[0] = ckv
    kvd = _dot(ckv.astype(BF16), wukv_ref[...])
    kr = u[:, o_kr:o_kr + LANE]
    kr_ref[0] = kr[:, :D_ROPE]
    if rope:
        cos = cos_ref[...]
        sin = sin_ref[...]
        qs = _dot(qn, wqs_ref[...])
        kr = kr * cos + u[:, o_kr + LANE:o_kr + 2 * LANE] * sin
    q_parts = []
    k_parts = []
    for hd in range(N_HEADS):
        q_parts.append(qf[:, hd * HEAD_W:hd * HEAD_W + LANE])
        qr = qf[:, hd * HEAD_W + LANE:(hd + 1) * HEAD_W]
        if rope:
            qr = qr * cos + qs[:, hd * LANE:(hd + 1) * LANE] * sin
        q_parts.append(qr)
        k_parts.append(kvd[:, hd * D_NOPE:(hd + 1) * D_NOPE])
        k_parts.append(kr)
    q_ref[0] = jnp.concatenate(q_parts, axis=-1).astype(BF16)
    k_ref[0] = jnp.concatenate(k_parts, axis=-1).astype(BF16)
    v_ref[0] = kvd[:, N_HEADS * D_NOPE:].astype(BF16)


def _mix_in(x, mod, mod_row0, g_pre1, w_in_ext, q_norm_g, kv_norm_g, wq, wqs, wukv, cos, sin, rope, tm):
    b, t, _ = x.shape
    ncol = w_in_ext.shape[1]
    const = lambda bi, i: (0, 0)
    tok = lambda bi, i: (bi, i, 0)
    table = (lambda bi, i: (i, 0)) if rope else const
    outs = [(C_CONV, F32), (N_HEADS * HEAD_W, BF16), (N_HEADS * HEAD_W, BF16), (N_HEADS * D_V, BF16),
            (KV_LORA, F32), (D_ROPE, F32)]
    return pl.pallas_call(
        functools.partial(_mix_in_kernel, rope),
        grid=(b, t // tm),
        in_specs=[pl.BlockSpec((1, tm, D_MODEL), tok),
                  pl.BlockSpec((1, 6, D_MODEL), lambda bi, i: (mod_row0 + bi, 0, 0)),
                  pl.BlockSpec((1, D_MODEL), const),
                  pl.BlockSpec((D_MODEL, ncol), const),
                  pl.BlockSpec((1, Q_LORA), const),
                  pl.BlockSpec((1, KV_LORA), const),
                  pl.BlockSpec((Q_LORA, N_HEADS * HEAD_W), const),
                  pl.BlockSpec((Q_LORA, N_HEADS * LANE), const),
                  pl.BlockSpec((KV_LORA, N_HEADS * (D_NOPE + D_V)), const),
                  pl.BlockSpec((tm, LANE), table),
                  pl.BlockSpec((tm, LANE), table)],
        out_specs=[pl.BlockSpec((1, tm, w), tok) for w, _ in outs],
        out_shape=[jax.ShapeDtypeStruct((b, t, w), dt) for w, dt in outs],
        compiler_params=_cparams("arbitrary", "arbitrary"),
        name="mix_in_rope" if rope else "mix_in",
    )(x, mod, g_pre1, w_in_ext, q_norm_g, kv_norm_g, wq, wqs, wukv, cos, sin)


def _conv_kernel(t, tb, tt, h_ref, w_ref, b_ref, g_ref, bb_ref, o_ref, pad_ref, sh_ref):
    zeros = jnp.zeros((CONV_HALO, C_CONV), F32)
    pad_ref[0:CONV_HALO, :] = zeros
    pad_ref[CONV_HALO + t:, :] = zeros
    pad_ref[CONV_HALO:CONV_HALO + t, :] = h_ref[0]
    first = CONV_HALO - CONV_K // 2
    rows = tb + 2 * CONV_HALO - SUBLANE

    def block(bi, carry):
        base = pl.multiple_of(bi * tb, tb)
        win = pad_ref[pl.ds(base, tb + 2 * CONV_HALO), :]
        for s in range(SUBLANE):
            sh_ref[s, 0:rows, :] = win[s:s + rows, :]
        for c in range(tb // tt):
            acc = jnp.zeros((tt, C_CONV), F32)
            for k in range(CONV_K):
                off = first + k
                r0 = c * tt + off // SUBLANE * SUBLANE
                acc = acc + sh_ref[off % SUBLANE, r0:r0 + tt, :] * w_ref[k:k + 1, :]
            y = acc + b_ref[...]
            mu = jnp.mean(y, axis=-1, keepdims=True)
            yc = y - mu
            var = jnp.mean(yc * yc, axis=-1, keepdims=True)
            z = yc * lax.rsqrt(var + EPS) * g_ref[...] + bb_ref[...]
            o_ref[0, pl.ds(base + c * tt, tt), :] = (z * _sigmoid(z)).astype(BF16)
        return carry

    lax.fori_loop(0, t // tb, block, 0)


def _conv(hglu, w_dw, b_dw, ln_g, ln_b):
    b, t, _ = hglu.shape
    tb, tt = 256, 32
    const = lambda bi: (0, 0)
    return pl.pallas_call(
        functools.partial(_conv_kernel, t, tb, tt),
        grid=(b,),
        in_specs=[pl.BlockSpec((1, t, C_CONV), lambda bi: (bi, 0, 0)),
                  pl.BlockSpec((CONV_K, C_CONV), const),
                  pl.BlockSpec((1, C_CONV), const),
                  pl.BlockSpec((1, C_CONV), const),
                  pl.BlockSpec((1, C_CONV), const)],
        out_specs=pl.BlockSpec((1, t, C_CONV), lambda bi: (bi, 0, 0)),
        out_shape=jax.ShapeDtypeStruct((b, t, C_CONV), BF16),
        scratch_shapes=[pltpu.VMEM((t + 2 * CONV_HALO, C_CONV), F32),
                        pltpu.VMEM((SUBLANE, tb + 2 * CONV_HALO, C_CONV), F32)],
        compiler_params=_cparams("arbitrary"),
        name="conv",
    )(hglu, w_dw, b_dw, ln_g, ln_b)


def _qk(q, k):
    return lax.dot_general(q, k, (((1,), (1,)), ((), ())), preferred_element_type=F32)


def _attn_kernel(cached, q_ref, k_ref, v_ref, *rest):
    if cached:
        kc_ref, vc_ref, o_ref = rest
    else:
        (o_ref,) = rest
    outs = []
    for hd in range(N_HEADS):
        ks = slice(hd * HEAD_W, (hd + 1) * HEAD_W)
        vs = slice(hd * D_V, (hd + 1) * D_V)
        qh = q_ref[0, :, ks]
        s = _qk(qh, k_ref[0, :, ks]) * ATT_SCALE
        m = jnp.max(s, axis=-1, keepdims=True)
        if cached:
            sc = _qk(qh, kc_ref[0, :, ks]) * ATT_SCALE
            m = jnp.maximum(m, jnp.max(sc, axis=-1, keepdims=True))
        p = jnp.exp(s - m)
        l = jnp.sum(p, axis=-1, keepdims=True)
        o = _dot(p.astype(BF16), v_ref[0, :, vs])
        if cached:
            pc = jnp.exp(sc - m)
            l = l + jnp.sum(pc, axis=-1, keepdims=True)
            o = o + _dot(pc.astype(BF16), vc_ref[0, :, vs])
        outs.append(o / l)
    o_ref[0] = jnp.concatenate(outs, axis=-1).astype(BF16)


def _attn(q, k, v, kc=None, vc=None, tq=256):
    b, t, _ = q.shape
    s = k.shape[1]
    cached = kc is not None
    whole = lambda bi, i: (bi, 0, 0)
    in_specs = [pl.BlockSpec((1, tq, N_HEADS * HEAD_W), lambda bi, i: (bi, i, 0)),
                pl.BlockSpec((1, s, N_HEADS * HEAD_W), whole),
                pl.BlockSpec((1, s, N_HEADS * D_V), whole)]
    args = [q, k, v]
    if cached:
        sc = kc.shape[1]
        in_specs += [pl.BlockSpec((1, sc, N_HEADS * HEAD_W), whole),
                     pl.BlockSpec((1, sc, N_HEADS * D_V), whole)]
        args += [kc, vc]
    return pl.pallas_call(
        functools.partial(_attn_kernel, cached),
        grid=(b, t // tq),
        in_specs=in_specs,
        out_specs=pl.BlockSpec((1, tq, N_HEADS * D_V), lambda bi, i: (bi, i, 0)),
        out_shape=jax.ShapeDtypeStruct((b, t, N_HEADS * D_V), BF16),
        compiler_params=_cparams("arbitrary", "arbitrary"),
        name="attn_cached" if cached else "attn",
    )(*args)


def _cache_kv_kernel(ckv_ref, kr_ref, wukv_ref, k_ref, v_ref):
    kvd = _dot(ckv_ref[0].astype(BF16), wukv_ref[...])
    kr = kr_ref[0]
    kr = jnp.concatenate([kr, jnp.zeros_like(kr)], axis=-1)
    parts = []
    for hd in range(N_HEADS):
        parts.append(kvd[:, hd * D_NOPE:(hd + 1) * D_NOPE])
        parts.append(kr)
    k_ref[0] = jnp.concatenate(parts, axis=-1).astype(BF16)
    v_ref[0] = kvd[:, N_HEADS * D_NOPE:].astype(BF16)


def _cache_kv(ckv, krope, wukv):
    b, s, _ = ckv.shape
    tok = lambda bi: (bi, 0, 0)
    return pl.pallas_call(
        _cache_kv_kernel,
        grid=(b,),
        in_specs=[pl.BlockSpec((1, s, KV_LORA), tok),
                  pl.BlockSpec((1, s, D_ROPE), tok),
                  pl.BlockSpec((KV_LORA, N_HEADS * (D_NOPE + D_V)), lambda bi: (0, 0))],
        out_specs=[pl.BlockSpec((1, s, N_HEADS * HEAD_W), tok),
                   pl.BlockSpec((1, s, N_HEADS * D_V), tok)],
        out_shape=[jax.ShapeDtypeStruct((b, s, N_HEADS * HEAD_W), BF16),
                   jax.ShapeDtypeStruct((b, s, N_HEADS * D_V), BF16)],
        compiler_params=_cparams("arbitrary"),
        name="cache_kv",
    )(ckv, krope, wukv)


def _post_kernel(x_ref, conv_ref, att_ref, mod_ref, wo_ref, gpost_ref, gpre2_ref, wr_hi_ref, wr_lo_ref,
                 br_ref, x1_ref, h2_ref, ri_ref, rw_ref, cnt_ref):
    out = _dot(conv_ref[0], wo_ref[:C_CONV, :]) + _dot(att_ref[0], wo_ref[C_CONV:, :])
    gt1 = mod_ref[0, 2:3, :]
    sh2 = mod_ref[0, 3:4, :]
    sc2 = mod_ref[0, 4:5, :]
    x1 = x_ref[0] + gt1 * _rms(out, gpost_ref[...])
    x1_ref[0] = x1
    h2 = _rms(x1, gpre2_ref[...]) * (1.0 + sc2) + sh2
    h_hi = h2.astype(BF16)
    h2_ref[0] = h_hi
    h_lo = (h2 - h_hi.astype(F32)).astype(BF16)
    w_hi = wr_hi_ref[...]
    logits = _dot(h_hi, w_hi) + _dot(h_lo, w_hi) + _dot(h_hi, wr_lo_ref[...]) + br_ref[...]

    lane = lax.broadcasted_iota(jnp.int32, logits.shape, 1)
    neg = jnp.float32(-jnp.inf)
    big = jnp.int32(LANE)
    is_g = (lane >= N_EXPERTS) & (lane < N_EXPERTS + N_GROUPS)
    lg = jnp.where(is_g, logits, neg)
    gmax = jnp.max(lg, axis=-1, keepdims=True)
    gidx = jnp.min(jnp.where(lg == gmax, lane, big), axis=-1, keepdims=True) - N_EXPERTS
    g_top = 1.0 / jnp.sum(jnp.exp(lg - gmax), axis=-1, keepdims=True)

    in_grp = (lane >= gidx * E_PER_GROUP) & (lane < (gidx + 1) * E_PER_GROUP)
    le = jnp.where(in_grp, logits, neg)
    m1 = jnp.max(le, axis=-1, keepdims=True)
    i1 = jnp.min(jnp.where(le == m1, lane, big), axis=-1, keepdims=True)
    le2 = jnp.where(lane == i1, neg, le)
    m2 = jnp.max(le2, axis=-1, keepdims=True)
    i2 = jnp.min(jnp.where(le2 == m2, lane, big), axis=-1, keepdims=True)
    r = jnp.exp(m2 - m1)
    w1 = g_top / (1.0 + r)
    w2 = g_top * r / (1.0 + r)
    ri_ref[0] = jnp.where(lane == 0, i1, jnp.where(lane == 1, i2, 0))
    rw_ref[0] = jnp.where(lane == 0, w1, jnp.where(lane == 1, w2, 0.0))
    pairs = jnp.sum(jnp.where((lane == i1) | (lane == i2), 1.0, 0.0), axis=0, keepdims=True)
    cnt_ref[0, 0] = jnp.broadcast_to(pairs, (SUBLANE, LANE))


def _post(x, conv_out, att, mod, mod_row0, w_out, g_post1, g_pre2, wr_hi, wr_lo, br):
    b, t, _ = x.shape
    tm = MOE_TD
    const = lambda bi, i: (0, 0)
    tok = lambda bi, i: (bi, i, 0)
    return pl.pallas_call(
        _post_kernel,
        grid=(b, t // tm),
        in_specs=[pl.BlockSpec((1, tm, D_MODEL), tok),
                  pl.BlockSpec((1, tm, C_CONV), tok),
                  pl.BlockSpec((1, tm, N_HEADS * D_V), tok),
                  pl.BlockSpec((1, 6, D_MODEL), lambda bi, i: (mod_row0 + bi, 0, 0)),
                  pl.BlockSpec((D_MODEL, D_MODEL), const),
                  pl.BlockSpec((1, D_MODEL), const),
                  pl.BlockSpec((1, D_MODEL), const),
                  pl.BlockSpec((D_MODEL, LANE), const),
                  pl.BlockSpec((D_MODEL, LANE), const),
                  pl.BlockSpec((1, LANE), const)],
        out_specs=[pl.BlockSpec((1, tm, D_MODEL), tok),
                   pl.BlockSpec((1, tm, D_MODEL), tok),
                   pl.BlockSpec((1, tm, LANE), tok),
                   pl.BlockSpec((1, tm, LANE), tok),
                   pl.BlockSpec((1, 1, SUBLANE, LANE), lambda bi, i: (bi, i, 0, 0))],
        out_shape=[jax.ShapeDtypeStruct((b, t, D_MODEL), F32),
                   jax.ShapeDtypeStruct((b, t, D_MODEL), BF16),
                   jax.ShapeDtypeStruct((b, t, LANE), jnp.int32),
                   jax.ShapeDtypeStruct((b, t, LANE), F32),
                   jax.ShapeDtypeStruct((b, t // tm, SUBLANE, LANE), F32)],
        compiler_params=_cparams("arbitrary", "arbitrary"),
        name="post",
    )(x, conv_out, att, mod, w_out, g_post1, g_pre2, wr_hi, wr_lo, br)


def _one_hots(ri):
    lane = lax.broadcasted_iota(jnp.int32, ri.shape, 1)
    oh1 = lane == ri[:, 0:1]
    oh2 = lane == ri[:, 1:2]
    return oh1, oh2, jnp.where(oh1 | oh2, 1.0, 0.0)


def _local_positions(oh1, oh2, oh):
    td = oh.shape[0]
    cnt = jnp.sum(oh, axis=0, keepdims=True)
    nch = jnp.floor((cnt + (CHUNK - 1)) * (1.0 / CHUNK))
    a = lax.broadcasted_iota(jnp.int32, (LANE, LANE), 0)
    b = lax.broadcasted_iota(jnp.int32, (LANE, LANE), 1)
    lower_experts = jnp.where(a < b, 1.0, 0.0).astype(BF16)
    run_start = _dot(jnp.broadcast_to(nch, (SUBLANE, LANE)).astype(BF16), lower_experts)[0:1, :] * CHUNK
    rr = lax.broadcasted_iota(jnp.int32, (td, td), 0)
    cc = lax.broadcasted_iota(jnp.int32, (td, td), 1)
    earlier = jnp.where(cc < rr, 1.0, 0.0).astype(BF16)
    pos = _dot(earlier, oh.astype(BF16)) + run_start
    lp1 = jnp.sum(jnp.where(oh1, pos, 0.0), axis=-1, keepdims=True)
    lp2 = jnp.sum(jnp.where(oh2, pos, 0.0), axis=-1, keepdims=True)
    return lp1, lp2


def _plan(cnt, n_tiles_max):
    k = (cnt + (CHUNK - 1)) // CHUNK
    run_end = jnp.cumsum(k, axis=1)
    run_start = run_end - k
    n_chunks = run_end[:, -1]
    total = jnp.sum(k, axis=0)
    padded = (total + (TILE_CH - 1)) // TILE_CH * TILE_CH
    seg_end = jnp.cumsum(padded)
    seg_start = seg_end - padded
    base = seg_start[None, :] + jnp.cumsum(k, axis=0) - k
    c = jnp.arange(MOE_NCH, dtype=jnp.int32)[None, :, None]
    in_run = (run_start[:, None, :] <= c) & (c < run_end[:, None, :])
    dst = jnp.sum(jnp.where(in_run, base[:, None, :] + c - run_start[:, None, :], 0), axis=2)
    n_tiles = seg_end[-1] // TILE_CH
    tile_chunk0 = jnp.arange(n_tiles_max, dtype=jnp.int32) * TILE_CH
    tile_expert = jnp.sum((seg_end[None, :] <= tile_chunk0[:, None]).astype(jnp.int32), axis=1)
    tile_expert = jnp.minimum(tile_expert, N_EXPERTS - 1)
    last = tile_expert[jnp.maximum(n_tiles - 1, 0)]
    tile_expert = jnp.where(jnp.arange(n_tiles_max) < n_tiles, tile_expert, last)
    tail_start = seg_start + total
    tail_len = padded - total
    i32 = lambda t: t.astype(jnp.int32)
    return (i32(dst), i32(n_chunks), i32(tail_start), i32(tail_len), i32(tile_expert), i32(n_tiles.reshape(1)))


def _dispatch_kernel(steps_a, dst_ref, nch_ref, ts_ref, tl_ref, nt_ref, h2a_ref, ria_ref, rwa_ref, h2b_ref, rib_ref,
                     rwb_ref, xs_hbm, ws_hbm, lp_ref, xbuf, wbuf, zx, zw, sem, zsem):
    i = pl.program_id(0)
    n = pl.num_programs(0)
    slot = i % 2
    td = MOE_TD
    n_tiles_max = xs_hbm.shape[0] // MOE_TILE

    def copies(src_x, src_w, src_chunk, dst_chunk, s):
        return (_chunk_copy(src_x, src_chunk, xs_hbm, dst_chunk, s), _chunk_copy(src_w, src_chunk, ws_hbm, dst_chunk, s))

    def zero_tile(m):
        rows = pl.ds(pl.multiple_of(m * MOE_TILE, MOE_TILE), MOE_TILE)
        return (pltpu.make_async_copy(zx, xs_hbm.at[rows, :], zsem.at[1]),
                pltpu.make_async_copy(zw, ws_hbm.at[rows, :], zsem.at[1]))

    @pl.when(i == 0)
    def _():
        zx[...] = jnp.zeros_like(zx)
        zw[...] = jnp.zeros_like(zw)
        for e in range(N_EXPERTS):
            def fill(m, carry, e=e):
                for cp in copies(zx, zw, 0, ts_ref[e] + m, zsem.at[0]):
                    cp.start()
                return carry
            lax.fori_loop(0, tl_ref[e], fill, 0)

        def fill_tile(m, carry):
            for cp in zero_tile(m):
                cp.start()
            return carry
        lax.fori_loop(nt_ref[0], n_tiles_max, fill_tile, 0)

    from_a = i < steps_a
    ri = jnp.where(from_a, ria_ref[...], rib_ref[...])
    rw = jnp.where(from_a, rwa_ref[...], rwb_ref[...])
    h2 = jnp.where(from_a, h2a_ref[...], h2b_ref[...])
    oh1, oh2, oh = _one_hots(ri)
    lp1, lp2 = _local_positions(oh1, oh2, oh)
    lane = lax.broadcasted_iota(jnp.int32, ri.shape, 1)
    lp_ref[...] = jnp.where(lane == 0, lp1, jnp.where(lane == 1, lp2, 0.0)).astype(jnp.int32)

    rr = lax.broadcasted_iota(jnp.int32, (td, td), 0)
    cc = lax.broadcasted_iota(jnp.int32, (td, td), 1)
    to_lanes = lambda col: jnp.sum(jnp.where(rr == cc, col, 0.0), axis=0, keepdims=True)
    row = lax.broadcasted_iota(jnp.int32, (MOE_L, td), 0).astype(F32)
    p1 = row == to_lanes(lp1)
    p2 = row == to_lanes(lp2)
    perm = jnp.where(p1 | p2, 1.0, 0.0).astype(BF16)
    xbuf[slot] = _dot(perm, h2).astype(BF16)
    w = jnp.sum(jnp.where(p1, to_lanes(rw[:, 0:1]), 0.0) + jnp.where(p2, to_lanes(rw[:, 1:2]), 0.0),
                axis=-1, keepdims=True)
    wbuf[slot] = jnp.broadcast_to(w, (MOE_L, LANE))

    def wait_chunks(count, s):
        def body(c, carry):
            for cp in copies(xbuf.at[s], wbuf.at[s], 0, 0, sem.at[s]):
                cp.wait()
            return carry
        lax.fori_loop(0, count, body, 0)

    @pl.when(i > 0)
    def _():
        wait_chunks(nch_ref[i - 1], 1 - slot)

    def send(c, carry):
        for cp in copies(xbuf.at[slot], wbuf.at[slot], c, dst_ref[i, c], sem.at[slot]):
            cp.start()
        return carry
    lax.fori_loop(0, nch_ref[i], send, 0)

    @pl.when(i == n - 1)
    def _():
        wait_chunks(nch_ref[i], slot)
        for e in range(N_EXPERTS):
            def drain(m, carry):
                for cp in copies(zx, zw, 0, 0, zsem.at[0]):
                    cp.wait()
                return carry
            lax.fori_loop(0, tl_ref[e], drain, 0)

        def drain_tile(m, carry):
            for cp in zero_tile(m):
                cp.wait()
            return carry
        lax.fori_loop(nt_ref[0], n_tiles_max, drain_tile, 0)


def _dispatch(dst, n_chunks, tail_start, tail_len, n_tiles, part_a, part_b, n_tiles_max):
    steps_a = part_a[0].shape[0] // MOE_TD
    steps_b = part_b[0].shape[0] // MOE_TD
    in_a = lambda i, *_: (jnp.minimum(i, steps_a - 1), 0)
    in_b = lambda i, *_: (jnp.maximum(i - steps_a, 0), 0)
    specs = lambda f: [pl.BlockSpec((MOE_TD, D_MODEL), f), pl.BlockSpec((MOE_TD, LANE), f),
                       pl.BlockSpec((MOE_TD, LANE), f)]
    rows = n_tiles_max * MOE_TILE
    return pl.pallas_call(
        functools.partial(_dispatch_kernel, steps_a),
        grid_spec=pltpu.PrefetchScalarGridSpec(
            num_scalar_prefetch=5,
            grid=(steps_a + steps_b,),
            in_specs=specs(in_a) + specs(in_b),
            out_specs=[pl.BlockSpec(memory_space=pl.ANY),
                       pl.BlockSpec(memory_space=pl.ANY),
                       pl.BlockSpec((MOE_TD, LANE), lambda i, *_: (i, 0))],
            scratch_shapes=[pltpu.VMEM((2, MOE_L, D_MODEL), BF16), pltpu.VMEM((2, MOE_L, LANE), F32),
                            pltpu.VMEM((MOE_TILE, D_MODEL), BF16), pltpu.VMEM((MOE_TILE, LANE), F32),
                            pltpu.SemaphoreType.DMA((2,)), pltpu.SemaphoreType.DMA((2,))]),
        out_shape=[jax.ShapeDtypeStruct((rows, D_MODEL), BF16),
                   jax.ShapeDtypeStruct((rows, LANE), F32),
                   jax.ShapeDtypeStruct(((steps_a + steps_b) * MOE_TD, LANE), jnp.int32)],
        compiler_params=_cparams("arbitrary"),
        name="moe_dispatch",
    )(dst, n_chunks, tail_start, tail_len, n_tiles, *part_a, *part_b)


def _moe_gemm_kernel(te_ref, nt_ref, xs_ref, ws_ref, wg_ref, wu_ref, wd_ref, y_ref):
    j = pl.program_id(0)
    nt = nt_ref[0]

    @pl.when(j < nt)
    def _():
        x = xs_ref[...]
        a = _dot(x, wg_ref[0].astype(BF16))
        u = _dot(x, wu_ref[0].astype(BF16))
        he = (a * _sigmoid(a)) * u
        y = _dot(he.astype(BF16), wd_ref[0].astype(BF16)) * ws_ref[:, 0:1]
        y_ref[...] = y.astype(BF16)

    @pl.when(j >= nt)
    def _():
        y_ref[...] = jnp.zeros_like(y_ref)


def _moe_gemm(tile_expert, n_tiles, xs, ws, w_gate, w_up, w_down):
    n_tiles_max = tile_expert.shape[0]
    clamp = lambda j, te, nt: (jnp.minimum(j, jnp.maximum(nt[0] - 1, 0)), 0)
    wmap = lambda j, te, nt: (te[j], 0, 0)
    return pl.pallas_call(
        _moe_gemm_kernel,
        grid_spec=pltpu.PrefetchScalarGridSpec(
            num_scalar_prefetch=2,
            grid=(n_tiles_max,),
            in_specs=[pl.BlockSpec((MOE_TILE, D_MODEL), clamp),
                      pl.BlockSpec((MOE_TILE, LANE), clamp),
                      pl.BlockSpec((1, D_MODEL, D_EXPERT), wmap),
                      pl.BlockSpec((1, D_MODEL, D_EXPERT), wmap),
                      pl.BlockSpec((1, D_EXPERT, D_MODEL), wmap)],
            out_specs=pl.BlockSpec((MOE_TILE, D_MODEL), lambda j, te, nt: (j, 0))),
        out_shape=jax.ShapeDtypeStruct((n_tiles_max * MOE_TILE, D_MODEL), BF16),
        compiler_params=_cparams("arbitrary"),
        name="moe_gemm",
    )(tile_expert, n_tiles, xs, ws, w_gate, w_up, w_down)


def _moe_combine_kernel(step0, dst_ref, nch_ref, lp_ref, x1_ref, mod_ref, gpost_ref, y_hbm, o_ref, ybuf, sem):
    i = pl.program_id(0)
    n = pl.num_programs(0)
    slot = i % 2

    def fetch(step, s):
        def body(c, carry):
            _chunk_copy(y_hbm, dst_ref[step, c], ybuf.at[s], c, sem.at[s]).start()
            return carry
        lax.fori_loop(0, nch_ref[step], body, 0)

    @pl.when(i == 0)
    def _():
        ybuf[...] = jnp.zeros_like(ybuf)
        fetch(step0, 0)

    @pl.when(i + 1 < n)
    def _():
        fetch(step0 + i + 1, 1 - slot)

    def wait(c, carry):
        _chunk_copy(y_hbm, 0, ybuf.at[slot], 0, sem.at[slot]).wait()
        return carry
    lax.fori_loop(0, nch_ref[step0 + i], wait, 0)

    lp = lp_ref[...]
    col = lax.broadcasted_iota(jnp.int32, (MOE_TD, MOE_L), 1)
    unperm = jnp.where((col == lp[:, 0:1]) | (col == lp[:, 1:2]), 1.0, 0.0).astype(BF16)
    moe = _dot(unperm, ybuf[slot])
    gt2 = mod_ref[0, 5:6, :]
    o_ref[...] = x1_ref[...] + gt2 * _rms(moe, gpost_ref[...])


def _moe_combine(step0, dst, n_chunks, lp, x1, mod, mod_row, y_sorted, g_post2):
    n = x1.shape[0]
    tok = lambda i, *_: (i, 0)
    return pl.pallas_call(
        functools.partial(_moe_combine_kernel, step0),
        grid_spec=pltpu.PrefetchScalarGridSpec(
            num_scalar_prefetch=2,
            grid=(n // MOE_TD,),
            in_specs=[pl.BlockSpec((MOE_TD, LANE), lambda i, *_: (step0 + i, 0)),
                      pl.BlockSpec((MOE_TD, D_MODEL), tok),
                      pl.BlockSpec((1, 6, D_MODEL), lambda i, *_: (mod_row(i), 0, 0)),
                      pl.BlockSpec((1, D_MODEL), lambda i, *_: (0, 0)),
                      pl.BlockSpec(memory_space=pl.ANY)],
            out_specs=pl.BlockSpec((MOE_TD, D_MODEL), tok),
            scratch_shapes=[pltpu.VMEM((2, MOE_L, D_MODEL), BF16), pltpu.SemaphoreType.DMA((2,))]),
        out_shape=jax.ShapeDtypeStruct((n, D_MODEL), F32),
        compiler_params=_cparams("arbitrary"),
        name="moe_combine",
    )(dst, n_chunks, lp, x1, mod, g_post2, y_sorted)


def _moe(part_a, part_b, mod, mod_row_a, mod_row_b, w_gate, w_up, w_down, g_post2):
    cnt = jnp.concatenate([part_a[4], part_b[4]], axis=0)[:, 0, :N_EXPERTS].astype(jnp.int32)
    steps_a = part_a[4].shape[0]
    n_tiles_max = (cnt.shape[0] * MOE_NCH + N_EXPERTS * (TILE_CH - 1)) // TILE_CH + 1
    dst, n_chunks, tail_start, tail_len, tile_expert, n_tiles = _plan(cnt, n_tiles_max)
    xs, ws, lp = _dispatch(dst, n_chunks, tail_start, tail_len, n_tiles, part_a[1:4], part_b[1:4], n_tiles_max)
    y_sorted = _moe_gemm(tile_expert, n_tiles, xs, ws, w_gate, w_up, w_down)
    ya = _moe_combine(0, dst, n_chunks, lp, part_a[0], mod, mod_row_a, y_sorted, g_post2)
    yb = _moe_combine(steps_a, dst, n_chunks, lp, part_b[0], mod, mod_row_b, y_sorted, g_post2)
    return ya, yb


def _rotate_half_cols(w):
    n = w.shape[-1]
    w4 = w.reshape(w.shape[:-1] + (n // 32, 2, 16))
    return jnp.stack([-w4[..., 1, :], w4[..., 0, :]], axis=-2).reshape(w.shape)


def _pad_lanes(w):
    return jnp.concatenate([w, jnp.zeros(w.shape[:-1] + (LANE - w.shape[-1],), w.dtype)], axis=-1)


def _rope_tables(t):
    rows = t // GRID_W
    n_freq = D_ROPE // 4
    freqs = ROPE_BASE ** (-jnp.arange(n_freq, dtype=F32) / n_freq)
    row = jnp.repeat(jnp.arange(rows, dtype=F32), GRID_W)
    col = jnp.tile(jnp.arange(GRID_W, dtype=F32), rows)
    ang_r = row[:, None] * freqs
    ang_c = col[:, None] * freqs
    cos = jnp.concatenate([jnp.cos(ang_r), jnp.cos(ang_r), jnp.cos(ang_c), jnp.cos(ang_c)], axis=-1)
    sin = jnp.concatenate([jnp.sin(ang_r), jnp.sin(ang_r), jnp.sin(ang_c), jnp.sin(ang_c)], axis=-1)
    return _pad_lanes(cos), _pad_lanes(sin)


def kernel(x_prompt, x_sample, cache_ckv, cache_krope, c, c_ctx, w_ada, b_ada, g_pre1, g_post1, g_pre2, g_post2, w_in, w_dw, b_dw, conv_ln_g, conv_ln_b, q_norm_g, kv_norm_g, w_uq, w_ukv, w_out, w_rg, b_rg, w_re, b_re, w_gate, w_up, w_down):
    nb, seq, d = x_prompt.shape
    db, dseq, _ = x_sample.shape
    l = 0

    cvec = jnp.concatenate([c_ctx[None, :], c, jnp.zeros((8 - 1 - db, d), F32)], axis=0)
    mod = _ada(cvec, w_ada[l], b_ada[l]).reshape(8, 6, d)

    o_kr = 2 * C_CONV + Q_LORA + KV_LORA
    w_in_l = w_in[l]
    w_kr = w_in_l[:, o_kr:]
    w_in_ctx = jnp.concatenate([w_in_l[:, :o_kr], _pad_lanes(w_kr)], axis=-1).astype(BF16)
    w_in_lat = jnp.concatenate([w_in_l[:, :o_kr], _pad_lanes(w_kr), _pad_lanes(_rotate_half_cols(w_kr))],
                               axis=-1).astype(BF16)
    wuq = w_uq[l].reshape(Q_LORA, N_HEADS, D_NOPE + D_ROPE)
    wq_rope = wuq[:, :, D_NOPE:]
    wq = jnp.concatenate([wuq[:, :, :D_NOPE], _pad_lanes(wq_rope)], axis=-1)
    wq = wq.reshape(Q_LORA, N_HEADS * HEAD_W).astype(BF16)
    wqs = _pad_lanes(_rotate_half_cols(wq_rope)).reshape(Q_LORA, N_HEADS * LANE).astype(BF16)
    wukv4 = w_ukv[l].reshape(KV_LORA, N_HEADS, D_NOPE + D_V)
    wukv = jnp.concatenate([wukv4[:, :, :D_NOPE].reshape(KV_LORA, N_HEADS * D_NOPE),
                            wukv4[:, :, D_NOPE:].reshape(KV_LORA, N_HEADS * D_V)], axis=-1).astype(BF16)
    w_out_b = w_out[l].astype(BF16)
    w_r = jnp.concatenate([w_re[l], w_rg[l], jnp.zeros((d, LANE - N_EXPERTS - N_GROUPS), F32)], axis=-1)
    wr_hi = w_r.astype(BF16)
    wr_lo = (w_r - wr_hi.astype(F32)).astype(BF16)
    b_r = jnp.concatenate([b_re[l], b_rg[l], jnp.zeros((LANE - N_EXPERTS - N_GROUPS,), F32)])[None, :]
    cos, sin = _rope_tables(dseq)
    row = lambda v: v[l][None, :]

    tm_c = 512
    xp_flat = x_prompt.reshape(1, nb * seq, d)
    hglu, q, k, v, ckv, kr = _mix_in(xp_flat, mod, 0, row(g_pre1), w_in_ctx, row(q_norm_g), row(kv_norm_g),
                                     wq, wqs, wukv, cos[:tm_c], sin[:tm_c], False, tm_c)
    per_seq = lambda a: a.reshape(nb, seq, a.shape[-1])
    conv_out = _conv(per_seq(hglu), w_dw[l], row(b_dw), row(conv_ln_g), row(conv_ln_b))
    att = _attn(per_seq(q), per_seq(k), per_seq(v))
    flat = lambda a: a.reshape(1, nb * seq, a.shape[-1])
    post_c = _post(xp_flat, flat(conv_out), flat(att), mod, 0, w_out_b, row(g_post1), row(g_pre2),
                   wr_hi, wr_lo, b_r)
    state_ckv = ckv.reshape(nb, 1, seq, KV_LORA)
    state_krope = kr.reshape(nb, 1, seq, D_ROPE)

    tm_s = 512
    kc, vc = _cache_kv(cache_ckv[:, l], cache_krope[:, l], wukv)
    hglu, q, k, v, _, _ = _mix_in(x_sample, mod, 1, row(g_pre1), w_in_lat, row(q_norm_g), row(kv_norm_g),
                                  wq, wqs, wukv, cos, sin, True, tm_s)
    conv_out = _conv(hglu, w_dw[l], row(b_dw), row(conv_ln_g), row(conv_ln_b))
    att = _attn(q, k, v, kc, vc)
    post_s = _post(x_sample, conv_out, att, mod, 1, w_out_b, row(g_post1), row(g_pre2), wr_hi, wr_lo, b_r)

    tokens = lambda parts: tuple(a.reshape((-1,) + a.shape[2:]) for a in parts)
    steps_per_req = dseq // MOE_TD
    yp, ys = _moe(tokens(post_c), tokens(post_s), mod, lambda i: 0, lambda i: 1 + i // steps_per_req,
                  w_gate[l], w_up[l], w_down[l], row(g_post2))

    return (yp.reshape(nb, seq, d), ys.reshape(db, dseq, d), state_ckv, state_krope)
```

```python
import functools

import jax
import jax.numpy as jnp
import numpy as np
from jax import lax
from jax.experimental import pallas as pl
from jax.experimental.pallas import tpu as pltpu

D_MODEL = 1024
GRID_W = 64
C_CONV = 512
CONV_K = 31
N_HEADS = 4
D_NOPE = 128
D_ROPE = 64
D_V = 128
Q_LORA = 384
KV_LORA = 256
N_GROUPS = 4
E_PER_GROUP = 8
N_EXPERTS = 32
D_EXPERT = 256
ROPE_BASE = 10000.0
EPS = 1e-6
ATT_SCALE = (D_NOPE + D_ROPE) ** -0.5

LANE = 128
SUBLANE = 8
HEAD_W = 2 * LANE
CONV_HALO = 16
MOE_TD = 512
CHUNK = 2 * SUBLANE
MOE_NCH = (2 * MOE_TD + N_EXPERTS * (CHUNK - 1)) // CHUNK
MOE_L = MOE_NCH * CHUNK
MOE_TILE = 256
TILE_CH = MOE_TILE // CHUNK
VMEM_LIMIT = 56 * 1024 * 1024

BF16 = jnp.bfloat16
F32 = jnp.float32


def _cparams(*sem):
    return pltpu.CompilerParams(dimension_semantics=sem, vmem_limit_bytes=VMEM_LIMIT)


def _rms(x, g):
    return x * lax.rsqrt(jnp.mean(x * x, axis=-1, keepdims=True) + EPS) * g


def _sigmoid(x):
    return 1.0 / (1.0 + jnp.exp(-x))


def _dot(a, b):
    return jnp.dot(a, b, preferred_element_type=F32)


def _chunk_copy(src, src_chunk, dst, dst_chunk, sem):
    rows = lambda c: pl.ds(c * CHUNK if isinstance(c, int) else pl.multiple_of(c * CHUNK, CHUNK), CHUNK)
    return pltpu.make_async_copy(src.at[rows(src_chunk), :], dst.at[rows(dst_chunk), :], sem)


def _ada_kernel(c_ref, w_ref, b_ref, o_ref):
    c = c_ref[...]
    s = (c * _sigmoid(c)).astype(BF16)
    o_ref[...] = _dot(s, w_ref[...].astype(BF16)) + b_ref[...]


def _ada(cvec, w_ada, b_ada):
    n = w_ada.shape[1]
    tn = 1536
    return pl.pallas_call(
        _ada_kernel,
        grid=(n // tn,),
        in_specs=[pl.BlockSpec((8, D_MODEL), lambda j: (0, 0)),
                  pl.BlockSpec((D_MODEL, tn), lambda j: (0, j)),
                  pl.BlockSpec((1, tn), lambda j: (0, j))],
        out_specs=pl.BlockSpec((8, tn), lambda j: (0, j)),
        out_shape=jax.ShapeDtypeStruct((8, n), F32),
        compiler_params=_cparams("arbitrary"),
        name="ada",
    )(cvec, w_ada, b_ada.reshape(1, n))


def _mix_in_kernel(rope, x_ref, mod_ref, gpre_ref, win_ref, qg_ref, kvg_ref, wq_ref, wqs_ref,
                   wukv_ref, cos_ref, sin_ref, hglu_ref, q_ref, k_ref, v_ref, ckv_ref, kr_ref):
    x = x_ref[0]
    sh1 = mod_ref[0, 0:1, :]
    sc1 = mod_ref[0, 1:2, :]
    h = _rms(x, gpre_ref[...]) * (1.0 + sc1) + sh1
    u = _dot(h.astype(BF16), win_ref[...])

    a = u[:, :C_CONV]
    g = u[:, C_CONV:2 * C_CONV]
    hglu_ref[0] = a * _sigmoid(g)

    o_q = 2 * C_CONV
    o_kv = o_q + Q_LORA
    o_kr = o_kv + KV_LORA
    qn = _rms(u[:, o_q:o_kv], qg_ref[...]).astype(BF16)
    qf = _dot(qn, wq_ref[...])
    ckv = _rms(u[:, o_kv:o_kr], kvg_ref[...])
    ckv_ref[0] = ckv
    kvd = _dot(ckv.astype(BF16), wukv_ref[...])
    kr = u[:, o_kr:o_kr + LANE]
    kr_ref[0] = kr[:, :D_ROPE]
    if rope:
        cos = cos_ref[...]
        sin = sin_ref[...]
        qs = _dot(qn, wqs_ref[...])
        kr = kr * cos + u[:, o_kr + LANE:o_kr + 2 * LANE] * sin
    q_parts = []
    k_parts = []
    for hd in range(N_HEADS):
        q_parts.append(qf[:, hd * HEAD_W:hd * HEAD_W + LANE])
        qr = qf[:, hd * HEAD_W + LANE:(hd + 1) * HEAD_W]
        if rope:
            qr = qr * cos + qs[:, hd * LANE:(hd + 1) * LANE] * sin
        q_parts.append(qr)
        k_parts.append(kvd[:, hd * D_NOPE:(hd + 1) * D_NOPE])
        k_parts.append(kr)
    q_ref[0] = jnp.concatenate(q_parts, axis=-1).astype(BF16)
    k_ref[0] = jnp.concatenate(k_parts, axis=-1).astype(BF16)
    v_ref[0] = kvd[:, N_HEADS * D_NOPE:].astype(BF16)


def _mix_in(x, mod, mod_row0, g_pre1, w_in_ext, q_norm_g, kv_norm_g, wq, wqs, wukv, cos, sin, rope, tm):
    b, t, _ = x.shape
    ncol = w_in_ext.shape[1]
    const = lambda bi, i: (0, 0)
    tok = lambda bi, i: (bi, i, 0)
    table = (lambda bi, i: (i, 0)) if rope else const
    outs = [(C_CONV, F32), (N_HEADS * HEAD_W, BF16), (N_HEADS * HEAD_W, BF16), (N_HEADS * D_V, BF16),
            (KV_LORA, F32), (D_ROPE, F32)]
    return pl.pallas_call(
        functools.partial(_mix_in_kernel, rope),
        grid=(b, t // tm),
        in_specs=[pl.BlockSpec((1, tm, D_MODEL), tok),
                  pl.BlockSpec((1, 6, D_MODEL), lambda bi, i: (mod_row0 + bi, 0, 0)),
                  pl.BlockSpec((1, D_MODEL), const),
                  pl.BlockSpec((D_MODEL, ncol), const),
                  pl.BlockSpec((1, Q_LORA), const),
                  pl.BlockSpec((1, KV_LORA), const),
                  pl.BlockSpec((Q_LORA, N_HEADS * HEAD_W), const),
                  pl.BlockSpec((Q_LORA, N_HEADS * LANE), const),
                  pl.BlockSpec((KV_LORA, N_HEADS * (D_NOPE + D_V)), const),
                  pl.BlockSpec((tm, LANE), table),
                  pl.BlockSpec((tm, LANE), table)],
        out_specs=[pl.BlockSpec((1, tm, w), tok) for w, _ in outs],
        out_shape=[jax.ShapeDtypeStruct((b, t, w), dt) for w, dt in outs],
        compiler_params=_cparams("arbitrary", "arbitrary"),
        name="mix_in_rope" if rope else "mix_in",
    )(x, mod, g_pre1, w_in_ext, q_norm_g, kv_norm_g, wq, wqs, wukv, cos, sin)


def _conv_kernel(t, tb, tt, h_ref, w_ref, b_ref, g_ref, bb_ref, o_ref, pad_ref, sh_ref):
    zeros = jnp.zeros((CONV_HALO, C_CONV), F32)
    pad_ref[0:CONV_HALO, :] = zeros
    pad_ref[CONV_HALO + t:, :] = zeros
    pad_ref[CONV_HALO:CONV_HALO + t, :] = h_ref[0]
    first = CONV_HALO - CONV_K // 2
    rows = tb + 2 * CONV_HALO - SUBLANE

    def block(bi, carry):
        base = pl.multiple_of(bi * tb, tb)
        win = pad_ref[pl.ds(base, tb + 2 * CONV_HALO), :]
        for s in range(SUBLANE):
            sh_ref[s, 0:rows, :] = win[s:s + rows, :]
        for c in range(tb // tt):
            acc = jnp.zeros((tt, C_CONV), F32)
            for k in range(CONV_K):
                off = first + k
                r0 = c * tt + off // SUBLANE * SUBLANE
                acc = acc + sh_ref[off % SUBLANE, r0:r0 + tt, :] * w_ref[k:k + 1, :]
            y = acc + b_ref[...]
            mu = jnp.mean(y, axis=-1, keepdims=True)
            yc = y - mu
            var = jnp.mean(yc * yc, axis=-1, keepdims=True)
            z = yc * lax.rsqrt(var + EPS) * g_ref[...] + bb_ref[...]
            o_ref[0, pl.ds(base + c * tt, tt), :] = (z * _sigmoid(z)).astype(BF16)
        return carry

    lax.fori_loop(0, t // tb, block, 0)


def _conv(hglu, w_dw, b_dw, ln_g, ln_b):
    b, t, _ = hglu.shape
    tb, tt = 256, 32
    const = lambda bi: (0, 0)
    return pl.pallas_call(
        functools.partial(_conv_kernel, t, tb, tt),
        grid=(b,),
        in_specs=[pl.BlockSpec((1, t, C_CONV), lambda bi: (bi, 0, 0)),
                  pl.BlockSpec((CONV_K, C_CONV), const),
                  pl.BlockSpec((1, C_CONV), const),
                  pl.BlockSpec((1, C_CONV), const),
                  pl.BlockSpec((1, C_CONV), const)],
        out_specs=pl.BlockSpec((1, t, C_CONV), lambda bi: (bi, 0, 0)),
        out_shape=jax.ShapeDtypeStruct((b, t, C_CONV), BF16),
        scratch_shapes=[pltpu.VMEM((t + 2 * CONV_HALO, C_CONV), F32),
                        pltpu.VMEM((SUBLANE, tb + 2 * CONV_HALO, C_CONV), F32)],
        compiler_params=_cparams("arbitrary"),
        name="conv",
    )(hglu, w_dw, b_dw, ln_g, ln_b)


def _qk(q, k):
    return lax.dot_general(q, k, (((1,), (1,)), ((), ())), preferred_element_type=F32)


def _attn_kernel(cached, q_ref, k_ref, v_ref, *rest):
    if cached:
        kc_ref, vc_ref, o_ref = rest
    else:
        (o_ref,) = rest
    outs = []
    for hd in range(N_HEADS):
        ks = slice(hd * HEAD_W, (hd + 1) * HEAD_W)
        vs = slice(hd * D_V, (hd + 1) * D_V)
        qh = q_ref[0, :, ks]
        s = _qk(qh, k_ref[0, :, ks]) * ATT_SCALE
        m = jnp.max(s, axis=-1, keepdims=True)
        if cached:
            sc = _qk(qh, kc_ref[0, :, ks]) * ATT_SCALE
            m = jnp.maximum(m, jnp.max(sc, axis=-1, keepdims=True))
        p = jnp.exp(s - m)
        l = jnp.sum(p, axis=-1, keepdims=True)
        o = _dot(p.astype(BF16), v_ref[0, :, vs])
        if cached:
            pc = jnp.exp(sc - m)
            l = l + jnp.sum(pc, axis=-1, keepdims=True)
            o = o + _dot(pc.astype(BF16), vc_ref[0, :, vs])
        outs.append(o / l)
    o_ref[0] = jnp.concatenate(outs, axis=-1).astype(BF16)


def _attn(q, k, v, kc=None, vc=None, tq=256):
    b, t, _ = q.shape
    s = k.shape[1]
    cached = kc is not None
    whole = lambda bi, i: (bi, 0, 0)
    in_specs = [pl.BlockSpec((1, tq, N_HEADS * HEAD_W), lambda bi, i: (bi, i, 0)),
                pl.BlockSpec((1, s, N_HEADS * HEAD_W), whole),
                pl.BlockSpec((1, s, N_HEADS * D_V), whole)]
    args = [q, k, v]
    if cached:
        sc = kc.shape[1]
        in_specs += [pl.BlockSpec((1, sc, N_HEADS * HEAD_W), whole),
                     pl.BlockSpec((1, sc, N_HEADS * D_V), whole)]
        args += [kc, vc]
    return pl.pallas_call(
        functools.partial(_attn_kernel, cached),
        grid=(b, t // tq),
        in_specs=in_specs,
        out_specs=pl.BlockSpec((1, tq, N_HEADS * D_V), lambda bi, i: (bi, i, 0)),
        out_shape=jax.ShapeDtypeStruct((b, t, N_HEADS * D_V), BF16),
        compiler_params=_cparams("arbitrary", "arbitrary"),
        name="attn_cached" if cached else "attn",
    )(*args)


def _cache_kv_kernel(ckv_ref, kr_ref, wukv_ref, k_ref, v_ref):
    kvd = _dot(ckv_ref[0].astype(BF16), wukv_ref[...])
    kr = kr_ref[0]
    kr = jnp.concatenate([kr, jnp.zeros_like(kr)], axis=-1)
    parts = []
    for hd in range(N_HEADS):
        parts.append(kvd[:, hd * D_NOPE:(hd + 1) * D_NOPE])
        parts.append(kr)
    k_ref[0] = jnp.concatenate(parts, axis=-1).astype(BF16)
    v_ref[0] = kvd[:, N_HEADS * D_NOPE:].astype(BF16)


def _cache_kv(ckv, krope, wukv):
    b, s, _ = ckv.shape
    tok = lambda bi: (bi, 0, 0)
    return pl.pallas_call(
        _cache_kv_kernel,
        grid=(b,),
        in_specs=[pl.BlockSpec((1, s, KV_LORA), tok),
                  pl.BlockSpec((1, s, D_ROPE), tok),
                  pl.BlockSpec((KV_LORA, N_HEADS * (D_NOPE + D_V)), lambda bi: (0, 0))],
        out_specs=[pl.BlockSpec((1, s, N_HEADS * HEAD_W), tok),
                   pl.BlockSpec((1, s, N_HEADS * D_V), tok)],
        out_shape=[jax.ShapeDtypeStruct((b, s, N_HEADS * HEAD_W), BF16),
                   jax.ShapeDtypeStruct((b, s, N_HEADS * D_V), BF16)],
        compiler_params=_cparams("arbitrary"),
        name="cache_kv",
    )(ckv, krope, wukv)


def _post_kernel(x_ref, conv_ref, att_ref, mod_ref, wo_ref, gpost_ref, gpre2_ref, wr_hi_ref, wr_lo_ref,
                 br_ref, x1_ref, h2_ref, ri_ref, rw_ref, cnt_ref):
    out = _dot(conv_ref[0], wo_ref[:C_CONV, :]) + _dot(att_ref[0], wo_ref[C_CONV:, :])
    gt1 = mod_ref[0, 2:3, :]
    sh2 = mod_ref[0, 3:4, :]
    sc2 = mod_ref[0, 4:5, :]
    x1 = x_ref[0] + gt1 * _rms(out, gpost_ref[...])
    x1_ref[0] = x1
    h2 = _rms(x1, gpre2_ref[...]) * (1.0 + sc2) + sh2
    h_hi = h2.astype(BF16)
    h2_ref[0] = h_hi
    h_lo = (h2 - h_hi.astype(F32)).astype(BF16)
    w_hi = wr_hi_ref[...]
    logits = _dot(h_hi, w_hi) + _dot(h_lo, w_hi) + _dot(h_hi, wr_lo_ref[...]) + br_ref[...]

    lane = lax.broadcasted_iota(jnp.int32, logits.shape, 1)
    neg = jnp.float32(-jnp.inf)
    big = jnp.int32(LANE)
    is_g = (lane >= N_EXPERTS) & (lane < N_EXPERTS + N_GROUPS)
    lg = jnp.where(is_g, logits, neg)
    gmax = jnp.max(lg, axis=-1, keepdims=True)
    gidx = jnp.min(jnp.where(lg == gmax, lane, big), axis=-1, keepdims=True) - N_EXPERTS
    g_top = 1.0 / jnp.sum(jnp.exp(lg - gmax), axis=-1, keepdims=True)

    in_grp = (lane >= gidx * E_PER_GROUP) & (lane < (gidx + 1) * E_PER_GROUP)
    le = jnp.where(in_grp, logits, neg)
    m1 = jnp.max(le, axis=-1, keepdims=True)
    i1 = jnp.min(jnp.where(le == m1, lane, big), axis=-1, keepdims=True)
    le2 = jnp.where(lane == i1, neg, le)
    m2 = jnp.max(le2, axis=-1, keepdims=True)
    i2 = jnp.min(jnp.where(le2 == m2, lane, big), axis=-1, keepdims=True)
    r = jnp.exp(m2 - m1)
    w1 = g_top / (1.0 + r)
    w2 = g_top * r / (1.0 + r)
    ri_ref[0] = jnp.where(lane == 0, i1, jnp.where(lane == 1, i2, 0))
    rw_ref[0] = jnp.where(lane == 0, w1, jnp.where(lane == 1, w2, 0.0))
    pairs = jnp.sum(jnp.where((lane == i1) | (lane == i2), 1.0, 0.0), axis=0, keepdims=True)
    cnt_ref[0, 0] = jnp.broadcast_to(pairs, (SUBLANE, LANE))


def _post(x, conv_out, att, mod, mod_row0, w_out, g_post1, g_pre2, wr_hi, wr_lo, br):
    b, t, _ = x.shape
    tm = MOE_TD
    const = lambda bi, i: (0, 0)
    tok = lambda bi, i: (bi, i, 0)
    return pl.pallas_call(
        _post_kernel,
        grid=(b, t // tm),
        in_specs=[pl.BlockSpec((1, tm, D_MODEL), tok),
                  pl.BlockSpec((1, tm, C_CONV), tok),
                  pl.BlockSpec((1, tm, N_HEADS * D_V), tok),
                  pl.BlockSpec((1, 6, D_MODEL), lambda bi, i: (mod_row0 + bi, 0, 0)),
                  pl.BlockSpec((D_MODEL, D_MODEL), const),
                  pl.BlockSpec((1, D_MODEL), const),
                  pl.BlockSpec((1, D_MODEL), const),
                  pl.BlockSpec((D_MODEL, LANE), const),
                  pl.BlockSpec((D_MODEL, LANE), const),
                  pl.BlockSpec((1, LANE), const)],
        out_specs=[pl.BlockSpec((1, tm, D_MODEL), tok),
                   pl.BlockSpec((1, tm, D_MODEL), tok),
                   pl.BlockSpec((1, tm, LANE), tok),
                   pl.BlockSpec((1, tm, LANE), tok),
                   pl.BlockSpec((1, 1, SUBLANE, LANE), lambda bi, i: (bi, i, 0, 0))],
        out_shape=[jax.ShapeDtypeStruct((b, t, D_MODEL), F32),
                   jax.ShapeDtypeStruct((b, t, D_MODEL), BF16),
                   jax.ShapeDtypeStruct((b, t, LANE), jnp.int32),
                   jax.ShapeDtypeStruct((b, t, LANE), F32),
                   jax.ShapeDtypeStruct((b, t // tm, SUBLANE, LANE), F32)],
        compiler_params=_cparams("arbitrary", "arbitrary"),
        name="post",
    )(x, conv_out, att, mod, w_out, g_post1, g_pre2, wr_hi, wr_lo, br)


def _one_hots(ri):
    lane = lax.broadcasted_iota(jnp.int32, ri.shape, 1)
    oh1 = lane == ri[:, 0:1]
    oh2 = lane == ri[:, 1:2]
    return oh1, oh2, jnp.where(oh1 | oh2, 1.0, 0.0)


def _local_positions(oh1, oh2, oh):
    td = oh.shape[0]
    cnt = jnp.sum(oh, axis=0, keepdims=True)
    nch = jnp.floor((cnt + (CHUNK - 1)) * (1.0 / CHUNK))
    a = lax.broadcasted_iota(jnp.int32, (LANE, LANE), 0)
    b = lax.broadcasted_iota(jnp.int32, (LANE, LANE), 1)
    lower_experts = jnp.where(a < b, 1.0, 0.0).astype(BF16)
    run_start = _dot(jnp.broadcast_to(nch, (SUBLANE, LANE)).astype(BF16), lower_experts)[0:1, :] * CHUNK
    rr = lax.broadcasted_iota(jnp.int32, (td, td), 0)
    cc = lax.broadcasted_iota(jnp.int32, (td, td), 1)
    earlier = jnp.where(cc < rr, 1.0, 0.0).astype(BF16)
    pos = _dot(earlier, oh.astype(BF16)) + run_start
    lp1 = jnp.sum(jnp.where(oh1, pos, 0.0), axis=-1, keepdims=True)
    lp2 = jnp.sum(jnp.where(oh2, pos, 0.0), axis=-1, keepdims=True)
    return lp1, lp2


def _plan(cnt, n_tiles_max):
    k = (cnt + (CHUNK - 1)) // CHUNK
    run_end = jnp.cumsum(k, axis=1)
    run_start = run_end - k
    n_chunks = run_end[:, -1]
    total = jnp.sum(k, axis=0)
    padded = (total + (TILE_CH - 1)) // TILE_CH * TILE_CH
    seg_end = jnp.cumsum(padded)
    seg_start = seg_end - padded
    base = seg_start[None, :] + jnp.cumsum(k, axis=0) - k
    c = jnp.arange(MOE_NCH, dtype=jnp.int32)[None, :, None]
    in_run = (run_start[:, None, :] <= c) & (c < run_end[:, None, :])
    dst = jnp.sum(jnp.where(in_run, base[:, None, :] + c - run_start[:, None, :], 0), axis=2)
    n_tiles = seg_end[-1] // TILE_CH
    tail_start = seg_start + total
    tail_len = padded - total
    i32 = lambda t: t.astype(jnp.int32)
    return (i32(dst), i32(n_chunks), i32(tail_start), i32(tail_len), i32(seg_start // TILE_CH),
            i32(seg_end // TILE_CH), i32(n_tiles.reshape(1)))


def _dispatch_kernel(steps_a, dst_ref, nch_ref, ts_ref, tl_ref, nt_ref, h2a_ref, ria_ref, rwa_ref, h2b_ref, rib_ref,
                     rwb_ref, xs_hbm, ws_hbm, lp_ref, xbuf, wbuf, zx, zw, sem, zsem):
    i = pl.program_id(0)
    n = pl.num_programs(0)
    slot = i % 2
    td = MOE_TD
    n_tiles_max = xs_hbm.shape[0] // MOE_TILE

    def copies(src_x, src_w, src_chunk, dst_chunk, s):
        return (_chunk_copy(src_x, src_chunk, xs_hbm, dst_chunk, s), _chunk_copy(src_w, src_chunk, ws_hbm, dst_chunk, s))

    def zero_tile(m):
        rows = pl.ds(pl.multiple_of(m * MOE_TILE, MOE_TILE), MOE_TILE)
        return (pltpu.make_async_copy(zx, xs_hbm.at[rows, :], zsem.at[1]),
                pltpu.make_async_copy(zw, ws_hbm.at[rows, :], zsem.at[1]))

    @pl.when(i == 0)
    def _():
        zx[...] = jnp.zeros_like(zx)
        zw[...] = jnp.zeros_like(zw)
        for e in range(N_EXPERTS):
            def fill(m, carry, e=e):
                for cp in copies(zx, zw, 0, ts_ref[e] + m, zsem.at[0]):
                    cp.start()
                return carry
            lax.fori_loop(0, tl_ref[e], fill, 0)

        def fill_tile(m, carry):
            for cp in zero_tile(m):
                cp.start()
            return carry
        lax.fori_loop(nt_ref[0], n_tiles_max, fill_tile, 0)

    from_a = i < steps_a
    ri = jnp.where(from_a, ria_ref[...], rib_ref[...])
    rw = jnp.where(from_a, rwa_ref[...], rwb_ref[...])
    h2 = jnp.where(from_a, h2a_ref[...], h2b_ref[...])
    oh1, oh2, oh = _one_hots(ri)
    lp1, lp2 = _local_positions(oh1, oh2, oh)
    lane = lax.broadcasted_iota(jnp.int32, ri.shape, 1)
    lp_ref[...] = jnp.where(lane == 0, lp1, jnp.where(lane == 1, lp2, 0.0)).astype(jnp.int32)

    rr = lax.broadcasted_iota(jnp.int32, (td, td), 0)
    cc = lax.broadcasted_iota(jnp.int32, (td, td), 1)
    to_lanes = lambda col: jnp.sum(jnp.where(rr == cc, col, 0.0), axis=0, keepdims=True)
    row = lax.broadcasted_iota(jnp.int32, (MOE_L, td), 0).astype(F32)
    p1 = row == to_lanes(lp1)
    p2 = row == to_lanes(lp2)
    perm = jnp.where(p1 | p2, 1.0, 0.0).astype(BF16)
    xbuf[slot] = _dot(perm, h2).astype(BF16)
    w = jnp.sum(jnp.where(p1, to_lanes(rw[:, 0:1]), 0.0) + jnp.where(p2, to_lanes(rw[:, 1:2]), 0.0),
                axis=-1, keepdims=True)
    wbuf[slot] = jnp.broadcast_to(w, (MOE_L, LANE))

    def wait_chunks(count, s):
        def body(c, carry):
            for cp in copies(xbuf.at[s], wbuf.at[s], 0, 0, sem.at[s]):
                cp.wait()
            return carry
        lax.fori_loop(0, count, body, 0)

    @pl.when(i > 0)
    def _():
        wait_chunks(nch_ref[i - 1], 1 - slot)

    def send(c, carry):
        for cp in copies(xbuf.at[slot], wbuf.at[slot], c, dst_ref[i, c], sem.at[slot]):
            cp.start()
        return carry
    lax.fori_loop(0, nch_ref[i], send, 0)

    @pl.when(i == n - 1)
    def _():
        wait_chunks(nch_ref[i], slot)
        for e in range(N_EXPERTS):
            def drain(m, carry):
                for cp in copies(zx, zw, 0, 0, zsem.at[0]):
                    cp.wait()
                return carry
            lax.fori_loop(0, tl_ref[e], drain, 0)

        def drain_tile(m, carry):
            for cp in zero_tile(m):
                cp.wait()
            return carry
        lax.fori_loop(nt_ref[0], n_tiles_max, drain_tile, 0)


def _dispatch(dst, n_chunks, tail_start, tail_len, n_tiles, part_a, part_b, n_tiles_max):
    steps_a = part_a[0].shape[0] // MOE_TD
    steps_b = part_b[0].shape[0] // MOE_TD
    in_a = lambda i, *_: (jnp.minimum(i, steps_a - 1), 0)
    in_b = lambda i, *_: (jnp.maximum(i - steps_a, 0), 0)
    specs = lambda f: [pl.BlockSpec((MOE_TD, D_MODEL), f), pl.BlockSpec((MOE_TD, LANE), f),
                       pl.BlockSpec((MOE_TD, LANE), f)]
    rows = n_tiles_max * MOE_TILE
    return pl.pallas_call(
        functools.partial(_dispatch_kernel, steps_a),
        grid_spec=pltpu.PrefetchScalarGridSpec(
            num_scalar_prefetch=5,
            grid=(steps_a + steps_b,),
            in_specs=specs(in_a) + specs(in_b),
            out_specs=[pl.BlockSpec(memory_space=pl.ANY),
                       pl.BlockSpec(memory_space=pl.ANY),
                       pl.BlockSpec((MOE_TD, LANE), lambda i, *_: (i, 0))],
            scratch_shapes=[pltpu.VMEM((2, MOE_L, D_MODEL), BF16), pltpu.VMEM((2, MOE_L, LANE), F32),
                            pltpu.VMEM((MOE_TILE, D_MODEL), BF16), pltpu.VMEM((MOE_TILE, LANE), F32),
                            pltpu.SemaphoreType.DMA((2,)), pltpu.SemaphoreType.DMA((2,))]),
        out_shape=[jax.ShapeDtypeStruct((rows, D_MODEL), BF16),
                   jax.ShapeDtypeStruct((rows, LANE), F32),
                   jax.ShapeDtypeStruct(((steps_a + steps_b) * MOE_TD, LANE), jnp.int32)],
        compiler_params=_cparams("arbitrary"),
        name="moe_dispatch",
    )(dst, n_chunks, tail_start, tail_len, n_tiles, *part_a, *part_b)


def _moe_gemm_kernel(t0_ref, t1_ref, nt_ref, wg_ref, wu_ref, wd_ref, xs_hbm, ws_hbm, y_hbm,
                     wg_b, wu_b, wd_b, xbuf, wbuf, ybuf, zbuf, isem, osem, zsem):
    e = pl.program_id(0)
    nt = nt_ref[0]
    n_tiles_max = y_hbm.shape[0] // MOE_TILE
    tile_rows = lambda t: pl.ds(pl.multiple_of(t * MOE_TILE, MOE_TILE), MOE_TILE)

    def load(t, s):
        return (pltpu.make_async_copy(xs_hbm.at[tile_rows(t), :], xbuf.at[s], isem.at[s]),
                pltpu.make_async_copy(ws_hbm.at[tile_rows(t), :], wbuf.at[s], isem.at[s]))

    def store(t, s):
        return pltpu.make_async_copy(ybuf.at[s], y_hbm.at[tile_rows(t), :], osem.at[s])

    def zero_tile(t):
        return pltpu.make_async_copy(zbuf, y_hbm.at[tile_rows(t), :], zsem.at[0])

    @pl.when(e == 0)
    def _():
        @pl.when(nt > 0)
        def _():
            for cp in load(0, 0):
                cp.start()
        zbuf[...] = jnp.zeros_like(zbuf)

        def fill(t, carry):
            zero_tile(t).start()
            return carry
        lax.fori_loop(nt, n_tiles_max, fill, 0)

    wg_b[...] = wg_ref[0].astype(BF16)
    wu_b[...] = wu_ref[0].astype(BF16)
    wd_b[...] = wd_ref[0].astype(BF16)

    def tile(t, carry):
        s = t % 2
        for cp in load(t, s):
            cp.wait()

        @pl.when(t + 1 < nt)
        def _():
            for cp in load(t + 1, 1 - s):
                cp.start()

        x = xbuf[s]
        a = _dot(x, wg_b[...])
        u = _dot(x, wu_b[...])
        he = (a * _sigmoid(a)) * u
        y = _dot(he.astype(BF16), wd_b[...]) * wbuf[s][:, 0:1]

        @pl.when(t >= 2)
        def _():
            store(t - 2, s).wait()
        ybuf[s] = y.astype(BF16)
        store(t, s).start()
        return carry

    lax.fori_loop(t0_ref[e], t1_ref[e], tile, 0)

    @pl.when(e == pl.num_programs(0) - 1)
    def _():
        @pl.when(nt >= 1)
        def _():
            store(nt - 1, (nt - 1) % 2).wait()

        @pl.when(nt >= 2)
        def _():
            store(nt - 2, nt % 2).wait()

        def drain(t, carry):
            zero_tile(t).wait()
            return carry
        lax.fori_loop(nt, n_tiles_max, drain, 0)


def _moe_gemm(tile0, tile1, n_tiles, xs, ws, w_gate, w_up, w_down):
    wmap = lambda e, *_: (e, 0, 0)
    return pl.pallas_call(
        _moe_gemm_kernel,
        grid_spec=pltpu.PrefetchScalarGridSpec(
            num_scalar_prefetch=3,
            grid=(N_EXPERTS,),
            in_specs=[pl.BlockSpec((1, D_MODEL, D_EXPERT), wmap),
                      pl.BlockSpec((1, D_MODEL, D_EXPERT), wmap),
                      pl.BlockSpec((1, D_EXPERT, D_MODEL), wmap),
                      pl.BlockSpec(memory_space=pl.ANY),
                      pl.BlockSpec(memory_space=pl.ANY)],
            out_specs=pl.BlockSpec(memory_space=pl.ANY),
            scratch_shapes=[pltpu.VMEM((D_MODEL, D_EXPERT), BF16), pltpu.VMEM((D_MODEL, D_EXPERT), BF16),
                            pltpu.VMEM((D_EXPERT, D_MODEL), BF16),
                            pltpu.VMEM((2, MOE_TILE, D_MODEL), BF16), pltpu.VMEM((2, MOE_TILE, LANE), F32),
                            pltpu.VMEM((2, MOE_TILE, D_MODEL), BF16), pltpu.VMEM((MOE_TILE, D_MODEL), BF16),
                            pltpu.SemaphoreType.DMA((2,)), pltpu.SemaphoreType.DMA((2,)),
                            pltpu.SemaphoreType.DMA((1,))]),
        out_shape=jax.ShapeDtypeStruct(xs.shape, BF16),
        compiler_params=_cparams("arbitrary"),
        name="moe_gemm",
    )(tile0, tile1, n_tiles, w_gate, w_up, w_down, xs, ws)


def _moe_combine_kernel(step0, dst_ref, nch_ref, lp_ref, x1_ref, mod_ref, gpost_ref, y_hbm, o_ref, ybuf, sem):
    i = pl.program_id(0)
    n = pl.num_programs(0)
    slot = i % 2

    def fetch(step, s):
        def body(c, carry):
            _chunk_copy(y_hbm, dst_ref[step, c], ybuf.at[s], c, sem.at[s]).start()
            return carry
        lax.fori_loop(0, nch_ref[step], body, 0)

    @pl.when(i == 0)
    def _():
        ybuf[...] = jnp.zeros_like(ybuf)
        fetch(step0, 0)

    @pl.when(i + 1 < n)
    def _():
        fetch(step0 + i + 1, 1 - slot)

    def wait(c, carry):
        _chunk_copy(y_hbm, 0, ybuf.at[slot], 0, sem.at[slot]).wait()
        return carry
    lax.fori_loop(0, nch_ref[step0 + i], wait, 0)

    lp = lp_ref[...]
    col = lax.broadcasted_iota(jnp.int32, (MOE_TD, MOE_L), 1)
    unperm = jnp.where((col == lp[:, 0:1]) | (col == lp[:, 1:2]), 1.0, 0.0).astype(BF16)
    moe = _dot(unperm, ybuf[slot])
    gt2 = mod_ref[0, 5:6, :]
    o_ref[...] = x1_ref[...] + gt2 * _rms(moe, gpost_ref[...])


def _moe_combine(step0, dst, n_chunks, lp, x1, mod, mod_row, y_sorted, g_post2):
    n = x1.shape[0]
    tok = lambda i, *_: (i, 0)
    return pl.pallas_call(
        functools.partial(_moe_combine_kernel, step0),
        grid_spec=pltpu.PrefetchScalarGridSpec(
            num_scalar_prefetch=2,
            grid=(n // MOE_TD,),
            in_specs=[pl.BlockSpec((MOE_TD, LANE), lambda i, *_: (step0 + i, 0)),
                      pl.BlockSpec((MOE_TD, D_MODEL), tok),
                      pl.BlockSpec((1, 6, D_MODEL), lambda i, *_: (mod_row(i), 0, 0)),
                      pl.BlockSpec((1, D_MODEL), lambda i, *_: (0, 0)),
                      pl.BlockSpec(memory_space=pl.ANY)],
            out_specs=pl.BlockSpec((MOE_TD, D_MODEL), tok),
            scratch_shapes=[pltpu.VMEM((2, MOE_L, D_MODEL), BF16), pltpu.SemaphoreType.DMA((2,))]),
        out_shape=jax.ShapeDtypeStruct((n, D_MODEL), F32),
        compiler_params=_cparams("arbitrary"),
        name="moe_combine",
    )(dst, n_chunks, lp, x1, mod, g_post2, y_sorted)


def _moe(part_a, part_b, mod, mod_row_a, mod_row_b, w_gate, w_up, w_down, g_post2):
    cnt = jnp.concatenate([part_a[4], part_b[4]], axis=0)[:, 0, :N_EXPERTS].astype(jnp.int32)
    steps_a = part_a[4].shape[0]
    n_tiles_max = (cnt.shape[0] * MOE_NCH + N_EXPERTS * (TILE_CH - 1)) // TILE_CH + 1
    dst, n_chunks, tail_start, tail_len, tile0, tile1, n_tiles = _plan(cnt, n_tiles_max)
    xs, ws, lp = _dispatch(dst, n_chunks, tail_start, tail_len, n_tiles, part_a[1:4], part_b[1:4], n_tiles_max)
    y_sorted = _moe_gemm(tile0, tile1, n_tiles, xs, ws, w_gate, w_up, w_down)
    ya = _moe_combine(0, dst, n_chunks, lp, part_a[0], mod, mod_row_a, y_sorted, g_post2)
    yb = _moe_combine(steps_a, dst, n_chunks, lp, part_b[0], mod, mod_row_b, y_sorted, g_post2)
    return ya, yb


def _rotate_half_cols(w):
    n = w.shape[-1]
    w4 = w.reshape(w.shape[:-1] + (n // 32, 2, 16))
    return jnp.stack([-w4[..., 1, :], w4[..., 0, :]], axis=-2).reshape(w.shape)


def _pad_lanes(w):
    return jnp.concatenate([w, jnp.zeros(w.shape[:-1] + (LANE - w.shape[-1],), w.dtype)], axis=-1)


def _rope_tables(t):
    rows = t // GRID_W
    n_freq = D_ROPE // 4
    freqs = ROPE_BASE ** (-jnp.arange(n_freq, dtype=F32) / n_freq)
    row = jnp.repeat(jnp.arange(rows, dtype=F32), GRID_W)
    col = jnp.tile(jnp.arange(GRID_W, dtype=F32), rows)
    ang_r = row[:, None] * freqs
    ang_c = col[:, None] * freqs
    cos = jnp.concatenate([jnp.cos(ang_r), jnp.cos(ang_r), jnp.cos(ang_c), jnp.cos(ang_c)], axis=-1)
    sin = jnp.concatenate([jnp.sin(ang_r), jnp.sin(ang_r), jnp.sin(ang_c), jnp.sin(ang_c)], axis=-1)
    return _pad_lanes(cos), _pad_lanes(sin)


def kernel(x_prompt, x_sample, cache_ckv, cache_krope, c, c_ctx, w_ada, b_ada, g_pre1, g_post1, g_pre2, g_post2, w_in, w_dw, b_dw, conv_ln_g, conv_ln_b, q_norm_g, kv_norm_g, w_uq, w_ukv, w_out, w_rg, b_rg, w_re, b_re, w_gate, w_up, w_down):
    nb, seq, d = x_prompt.shape
    db, dseq, _ = x_sample.shape
    l = 0

    cvec = jnp.concatenate([c_ctx[None, :], c, jnp.zeros((8 - 1 - db, d), F32)], axis=0)
    mod = _ada(cvec, w_ada[l], b_ada[l]).reshape(8, 6, d)

    o_kr = 2 * C_CONV + Q_LORA + KV_LORA
    w_in_l = w_in[l]
    w_kr = w_in_l[:, o_kr:]
    w_in_ctx = jnp.concatenate([w_in_l[:, :o_kr], _pad_lanes(w_kr)], axis=-1).astype(BF16)
    w_in_lat = jnp.concatenate([w_in_l[:, :o_kr], _pad_lanes(w_kr), _pad_lanes(_rotate_half_cols(w_kr))],
                               axis=-1).astype(BF16)
    wuq = w_uq[l].reshape(Q_LORA, N_HEADS, D_NOPE + D_ROPE)
    wq_rope = wuq[:, :, D_NOPE:]
    wq = jnp.concatenate([wuq[:, :, :D_NOPE], _pad_lanes(wq_rope)], axis=-1)
    wq = wq.reshape(Q_LORA, N_HEADS * HEAD_W).astype(BF16)
    wqs = _pad_lanes(_rotate_half_cols(wq_rope)).reshape(Q_LORA, N_HEADS * LANE).astype(BF16)
    wukv4 = w_ukv[l].reshape(KV_LORA, N_HEADS, D_NOPE + D_V)
    wukv = jnp.concatenate([wukv4[:, :, :D_NOPE].reshape(KV_LORA, N_HEADS * D_NOPE),
                            wukv4[:, :, D_NOPE:].reshape(KV_LORA, N_HEADS * D_V)], axis=-1).astype(BF16)
    w_out_b = w_out[l].astype(BF16)
    w_r = jnp.concatenate([w_re[l], w_rg[l], jnp.zeros((d, LANE - N_EXPERTS - N_GROUPS), F32)], axis=-1)
    wr_hi = w_r.astype(BF16)
    wr_lo = (w_r - wr_hi.astype(F32)).astype(BF16)
    b_r = jnp.concatenate([b_re[l], b_rg[l], jnp.zeros((LANE - N_EXPERTS - N_GROUPS,), F32)])[None, :]
    cos, sin = _rope_tables(dseq)
    row = lambda v: v[l][None, :]

    tm_c = 512
    xp_flat = x_prompt.reshape(1, nb * seq, d)
    hglu, q, k, v, ckv, kr = _mix_in(xp_flat, mod, 0, row(g_pre1), w_in_ctx, row(q_norm_g), row(kv_norm_g),
                                     wq, wqs, wukv, cos[:tm_c], sin[:tm_c], False, tm_c)
    per_seq = lambda a: a.reshape(nb, seq, a.shape[-1])
    conv_out = _conv(per_seq(hglu), w_dw[l], row(b_dw), row(conv_ln_g), row(conv_ln_b))
    att = _attn(per_seq(q), per_seq(k), per_seq(v))
    flat = lambda a: a.reshape(1, nb * seq, a.shape[-1])
    post_c = _post(xp_flat, flat(conv_out), flat(att), mod, 0, w_out_b, row(g_post1), row(g_pre2),
                   wr_hi, wr_lo, b_r)
    state_ckv = ckv.reshape(nb, 1, seq, KV_LORA)
    state_krope = kr.reshape(nb, 1, seq, D_ROPE)

    tm_s = 512
    kc, vc = _cache_kv(cache_ckv[:, l], cache_krope[:, l], wukv)
    hglu, q, k, v, _, _ = _mix_in(x_sample, mod, 1, row(g_pre1), w_in_lat, row(q_norm_g), row(kv_norm_g),
                                  wq, wqs, wukv, cos, sin, True, tm_s)
    conv_out = _conv(hglu, w_dw[l], row(b_dw), row(conv_ln_g), row(conv_ln_b))
    att = _attn(q, k, v, kc, vc)
    post_s = _post(x_sample, conv_out, att, mod, 1, w_out_b, row(g_post1), row(g_pre2), wr_hi, wr_lo, b_r)

    tokens = lambda parts: tuple(a.reshape((-1,) + a.shape[2:]) for a in parts)
    steps_per_req = dseq // MOE_TD
    yp, ys = _moe(tokens(post_c), tokens(post_s), mod, lambda i: 0, lambda i: 1 + i // steps_per_req,
                  w_gate[l], w_up[l], w_down[l], row(g_post2))

    return (yp.reshape(nb, seq, d), ys.reshape(db, dseq, d), state_ckv, state_krope)
```

```python
import functools

import jax
import jax.numpy as jnp
import numpy as np
from jax import lax
from jax.experimental import pallas as pl
from jax.experimental.pallas import tpu as pltpu

D_MODEL = 1024
GRID_W = 64
C_CONV = 512
CONV_K = 31
N_HEADS = 4
D_NOPE = 128
D_ROPE = 64
D_V = 128
Q_LORA = 384
KV_LORA = 256
N_GROUPS = 4
E_PER_GROUP = 8
N_EXPERTS = 32
D_EXPERT = 256
ROPE_BASE = 10000.0
EPS = 1e-6
ATT_SCALE = (D_NOPE + D_ROPE) ** -0.5

LANE = 128
SUBLANE = 8
HEAD_W = 2 * LANE
CONV_HALO = 16
MOE_TD = 512
CHUNK = 2 * SUBLANE
MOE_NCH = (2 * MOE_TD + N_EXPERTS * (CHUNK - 1)) // CHUNK
MOE_L = MOE_NCH * CHUNK
MOE_TILE = 256
TILE_CH = MOE_TILE // CHUNK
GEMM_RING = 4
VMEM_LIMIT = 56 * 1024 * 1024

BF16 = jnp.bfloat16
F32 = jnp.float32


def _cparams(*sem):
    return pltpu.CompilerParams(dimension_semantics=sem, vmem_limit_bytes=VMEM_LIMIT)


def _rms(x, g):
    return x * lax.rsqrt(jnp.mean(x * x, axis=-1, keepdims=True) + EPS) * g


def _sigmoid(x):
    return 1.0 / (1.0 + jnp.exp(-x))


def _dot(a, b):
    return jnp.dot(a, b, preferred_element_type=F32)


def _chunk_copy(src, src_chunk, dst, dst_chunk, sem):
    rows = lambda c: pl.ds(c * CHUNK if isinstance(c, int) else pl.multiple_of(c * CHUNK, CHUNK), CHUNK)
    return pltpu.make_async_copy(src.at[rows(src_chunk), :], dst.at[rows(dst_chunk), :], sem)


def _ada_kernel(c_ref, w_ref, b_ref, o_ref):
    c = c_ref[...]
    s = (c * _sigmoid(c)).astype(BF16)
    o_ref[...] = _dot(s, w_ref[...].astype(BF16)) + b_ref[...]


def _ada(cvec, w_ada, b_ada):
    n = w_ada.shape[1]
    tn = 1536
    return pl.pallas_call(
        _ada_kernel,
        grid=(n // tn,),
        in_specs=[pl.BlockSpec((8, D_MODEL), lambda j: (0, 0)),
                  pl.BlockSpec((D_MODEL, tn), lambda j: (0, j)),
                  pl.BlockSpec((1, tn), lambda j: (0, j))],
        out_specs=pl.BlockSpec((8, tn), lambda j: (0, j)),
        out_shape=jax.ShapeDtypeStruct((8, n), F32),
        compiler_params=_cparams("arbitrary"),
        name="ada",
    )(cvec, w_ada, b_ada.reshape(1, n))


def _mix_in_kernel(rope, x_ref, mod_ref, gpre_ref, win_ref, qg_ref, kvg_ref, wq_ref, wqs_ref,
                   wukv_ref, cos_ref, sin_ref, hglu_ref, q_ref, k_ref, v_ref, ckv_ref, kr_ref):
    x = x_ref[0]
    sh1 = mod_ref[0, 0:1, :]
    sc1 = mod_ref[0, 1:2, :]
    h = _rms(x, gpre_ref[...]) * (1.0 + sc1) + sh1
    u = _dot(h.astype(BF16), win_ref[...])

    a = u[:, :C_CONV]
    g = u[:, C_CONV:2 * C_CONV]
    hglu_ref[0] = a * _sigmoid(g)

    o_q = 2 * C_CONV
    o_kv = o_q + Q_LORA
    o_kr = o_kv + KV_LORA
    qn = _rms(u[:, o_q:o_kv], qg_ref[...]).astype(BF16)
    qf = _dot(qn, wq_ref[...])
    ckv = _rms(u[:, o_kv:o_kr], kvg_ref[...])
    ckv_ref[0] = ckv
    kvd = _dot(ckv.astype(BF16), wukv_ref[...])
    kr = u[:, o_kr:o_kr + LANE]
    kr_ref[0] = kr[:, :D_ROPE]
    if rope:
        cos = cos_ref[...]
        sin = sin_ref[...]
        qs = _dot(qn, wqs_ref[...])
        kr = kr * cos + u[:, o_kr + LANE:o_kr + 2 * LANE] * sin
    q_parts = []
    k_parts = []
    for hd in range(N_HEADS):
        q_parts.append(qf[:, hd * HEAD_W:hd * HEAD_W + LANE])
        qr = qf[:, hd * HEAD_W + LANE:(hd + 1) * HEAD_W]
        if rope:
            qr = qr * cos + qs[:, hd * LANE:(hd + 1) * LANE] * sin
        q_parts.append(qr)
        k_parts.append(kvd[:, hd * D_NOPE:(hd + 1) * D_NOPE])
        k_parts.append(kr)
    q_ref[0] = jnp.concatenate(q_parts, axis=-1).astype(BF16)
    k_ref[0] = jnp.concatenate(k_parts, axis=-1).astype(BF16)
    v_ref[0] = kvd[:, N_HEADS * D_NOPE:].astype(BF16)


def _mix_in(x, mod, mod_row0, g_pre1, w_in_ext, q_norm_g, kv_norm_g, wq, wqs, wukv, cos, sin, rope, tm):
    b, t, _ = x.shape
    ncol = w_in_ext.shape[1]
    const = lambda bi, i: (0, 0)
    tok = lambda bi, i: (bi, i, 0)
    table = (lambda bi, i: (i, 0)) if rope else const
    outs = [(C_CONV, F32), (N_HEADS * HEAD_W, BF16), (N_HEADS * HEAD_W, BF16), (N_HEADS * D_V, BF16),
            (KV_LORA, F32), (D_ROPE, F32)]
    return pl.pallas_call(
        functools.partial(_mix_in_kernel, rope),
        grid=(b, t // tm),
        in_specs=[pl.BlockSpec((1, tm, D_MODEL), tok),
                  pl.BlockSpec((1, 6, D_MODEL), lambda bi, i: (mod_row0 + bi, 0, 0)),
                  pl.BlockSpec((1, D_MODEL), const),
                  pl.BlockSpec((D_MODEL, ncol), const),
                  pl.BlockSpec((1, Q_LORA), const),
                  pl.BlockSpec((1, KV_LORA), const),
                  pl.BlockSpec((Q_LORA, N_HEADS * HEAD_W), const),
                  pl.BlockSpec((Q_LORA, N_HEADS * LANE), const),
                  pl.BlockSpec((KV_LORA, N_HEADS * (D_NOPE + D_V)), const),
                  pl.BlockSpec((tm, LANE), table),
                  pl.BlockSpec((tm, LANE), table)],
        out_specs=[pl.BlockSpec((1, tm, w), tok) for w, _ in outs],
        out_shape=[jax.ShapeDtypeStruct((b, t, w), dt) for w, dt in outs],
        compiler_params=_cparams("arbitrary", "arbitrary"),
        name="mix_in_rope" if rope else "mix_in",
    )(x, mod, g_pre1, w_in_ext, q_norm_g, kv_norm_g, wq, wqs, wukv, cos, sin)


def _conv_kernel(t, tb, tt, h_ref, w_ref, b_ref, g_ref, bb_ref, o_ref, pad_ref, sh_ref):
    zeros = jnp.zeros((CONV_HALO, C_CONV), F32)
    pad_ref[0:CONV_HALO, :] = zeros
    pad_ref[CONV_HALO + t:, :] = zeros
    pad_ref[CONV_HALO:CONV_HALO + t, :] = h_ref[0]
    first = CONV_HALO - CONV_K // 2
    rows = tb + 2 * CONV_HALO - SUBLANE

    def block(bi, carry):
        base = pl.multiple_of(bi * tb, tb)
        win = pad_ref[pl.ds(base, tb + 2 * CONV_HALO), :]
        for s in range(SUBLANE):
            sh_ref[s, 0:rows, :] = win[s:s + rows, :]
        for c in range(tb // tt):
            acc = jnp.zeros((tt, C_CONV), F32)
            for k in range(CONV_K):
                off = first + k
                r0 = c * tt + off // SUBLANE * SUBLANE
                acc = acc + sh_ref[off % SUBLANE, r0:r0 + tt, :] * w_ref[k:k + 1, :]
            y = acc + b_ref[...]
            mu = jnp.mean(y, axis=-1, keepdims=True)
            yc = y - mu
            var = jnp.mean(yc * yc, axis=-1, keepdims=True)
            z = yc * lax.rsqrt(var + EPS) * g_ref[...] + bb_ref[...]
            o_ref[0, pl.ds(base + c * tt, tt), :] = (z * _sigmoid(z)).astype(BF16)
        return carry

    lax.fori_loop(0, t // tb, block, 0)


def _conv(hglu, w_dw, b_dw, ln_g, ln_b):
    b, t, _ = hglu.shape
    tb, tt = 256, 32
    const = lambda bi: (0, 0)
    return pl.pallas_call(
        functools.partial(_conv_kernel, t, tb, tt),
        grid=(b,),
        in_specs=[pl.BlockSpec((1, t, C_CONV), lambda bi: (bi, 0, 0)),
                  pl.BlockSpec((CONV_K, C_CONV), const),
                  pl.BlockSpec((1, C_CONV), const),
                  pl.BlockSpec((1, C_CONV), const),
                  pl.BlockSpec((1, C_CONV), const)],
        out_specs=pl.BlockSpec((1, t, C_CONV), lambda bi: (bi, 0, 0)),
        out_shape=jax.ShapeDtypeStruct((b, t, C_CONV), BF16),
        scratch_shapes=[pltpu.VMEM((t + 2 * CONV_HALO, C_CONV), F32),
                        pltpu.VMEM((SUBLANE, tb + 2 * CONV_HALO, C_CONV), F32)],
        compiler_params=_cparams("arbitrary"),
        name="conv",
    )(hglu, w_dw, b_dw, ln_g, ln_b)


def _qk(q, k):
    return lax.dot_general(q, k, (((1,), (1,)), ((), ())), preferred_element_type=F32)


def _attn_kernel(cached, q_ref, k_ref, v_ref, *rest):
    if cached:
        kc_ref, vc_ref, o_ref = rest
    else:
        (o_ref,) = rest
    outs = []
    for hd in range(N_HEADS):
        ks = slice(hd * HEAD_W, (hd + 1) * HEAD_W)
        vs = slice(hd * D_V, (hd + 1) * D_V)
        qh = q_ref[0, :, ks]
        s = _qk(qh, k_ref[0, :, ks]) * ATT_SCALE
        m = jnp.max(s, axis=-1, keepdims=True)
        if cached:
            sc = _qk(qh, kc_ref[0, :, ks]) * ATT_SCALE
            m = jnp.maximum(m, jnp.max(sc, axis=-1, keepdims=True))
        p = jnp.exp(s - m)
        l = jnp.sum(p, axis=-1, keepdims=True)
        o = _dot(p.astype(BF16), v_ref[0, :, vs])
        if cached:
            pc = jnp.exp(sc - m)
            l = l + jnp.sum(pc, axis=-1, keepdims=True)
            o = o + _dot(pc.astype(BF16), vc_ref[0, :, vs])
        outs.append(o / l)
    o_ref[0] = jnp.concatenate(outs, axis=-1).astype(BF16)


def _attn(q, k, v, kc=None, vc=None, tq=256):
    b, t, _ = q.shape
    s = k.shape[1]
    cached = kc is not None
    whole = lambda bi, i: (bi, 0, 0)
    in_specs = [pl.BlockSpec((1, tq, N_HEADS * HEAD_W), lambda bi, i: (bi, i, 0)),
                pl.BlockSpec((1, s, N_HEADS * HEAD_W), whole),
                pl.BlockSpec((1, s, N_HEADS * D_V), whole)]
    args = [q, k, v]
    if cached:
        sc = kc.shape[1]
        in_specs += [pl.BlockSpec((1, sc, N_HEADS * HEAD_W), whole),
                     pl.BlockSpec((1, sc, N_HEADS * D_V), whole)]
        args += [kc, vc]
    return pl.pallas_call(
        functools.partial(_attn_kernel, cached),
        grid=(b, t // tq),
        in_specs=in_specs,
        out_specs=pl.BlockSpec((1, tq, N_HEADS * D_V), lambda bi, i: (bi, i, 0)),
        out_shape=jax.ShapeDtypeStruct((b, t, N_HEADS * D_V), BF16),
        compiler_params=_cparams("arbitrary", "arbitrary"),
        name="attn_cached" if cached else "attn",
    )(*args)


def _cache_kv_kernel(ckv_ref, kr_ref, wukv_ref, k_ref, v_ref):
    kvd = _dot(ckv_ref[0].astype(BF16), wukv_ref[...])
    kr = kr_ref[0]
    kr = jnp.concatenate([kr, jnp.zeros_like(kr)], axis=-1)
    parts = []
    for hd in range(N_HEADS):
        parts.append(kvd[:, hd * D_NOPE:(hd + 1) * D_NOPE])
        parts.append(kr)
    k_ref[0] = jnp.concatenate(parts, axis=-1).astype(BF16)
    v_ref[0] = kvd[:, N_HEADS * D_NOPE:].astype(BF16)


def _cache_kv(ckv, krope, wukv):
    b, s, _ = ckv.shape
    tok = lambda bi: (bi, 0, 0)
    return pl.pallas_call(
        _cache_kv_kernel,
        grid=(b,),
        in_specs=[pl.BlockSpec((1, s, KV_LORA), tok),
                  pl.BlockSpec((1, s, D_ROPE), tok),
                  pl.BlockSpec((KV_LORA, N_HEADS * (D_NOPE + D_V)), lambda bi: (0, 0))],
        out_specs=[pl.BlockSpec((1, s, N_HEADS * HEAD_W), tok),
                   pl.BlockSpec((1, s, N_HEADS * D_V), tok)],
        out_shape=[jax.ShapeDtypeStruct((b, s, N_HEADS * HEAD_W), BF16),
                   jax.ShapeDtypeStruct((b, s, N_HEADS * D_V), BF16)],
        compiler_params=_cparams("arbitrary"),
        name="cache_kv",
    )(ckv, krope, wukv)


def _post_kernel(x_ref, conv_ref, att_ref, mod_ref, wo_ref, gpost_ref, gpre2_ref, wr_hi_ref, wr_lo_ref,
                 br_ref, x1_ref, h2_ref, ri_ref, rw_ref, cnt_ref):
    out = _dot(conv_ref[0], wo_ref[:C_CONV, :]) + _dot(att_ref[0], wo_ref[C_CONV:, :])
    gt1 = mod_ref[0, 2:3, :]
    sh2 = mod_ref[0, 3:4, :]
    sc2 = mod_ref[0, 4:5, :]
    x1 = x_ref[0] + gt1 * _rms(out, gpost_ref[...])
    x1_ref[0] = x1
    h2 = _rms(x1, gpre2_ref[...]) * (1.0 + sc2) + sh2
    h_hi = h2.astype(BF16)
    h2_ref[0] = h_hi
    h_lo = (h2 - h_hi.astype(F32)).astype(BF16)
    w_hi = wr_hi_ref[...]
    logits = _dot(h_hi, w_hi) + _dot(h_lo, w_hi) + _dot(h_hi, wr_lo_ref[...]) + br_ref[...]

    lane = lax.broadcasted_iota(jnp.int32, logits.shape, 1)
    neg = jnp.float32(-jnp.inf)
    big = jnp.int32(LANE)
    is_g = (lane >= N_EXPERTS) & (lane < N_EXPERTS + N_GROUPS)
    lg = jnp.where(is_g, logits, neg)
    gmax = jnp.max(lg, axis=-1, keepdims=True)
    gidx = jnp.min(jnp.where(lg == gmax, lane, big), axis=-1, keepdims=True) - N_EXPERTS
    g_top = 1.0 / jnp.sum(jnp.exp(lg - gmax), axis=-1, keepdims=True)

    in_grp = (lane >= gidx * E_PER_GROUP) & (lane < (gidx + 1) * E_PER_GROUP)
    le = jnp.where(in_grp, logits, neg)
    m1 = jnp.max(le, axis=-1, keepdims=True)
    i1 = jnp.min(jnp.where(le == m1, lane, big), axis=-1, keepdims=True)
    le2 = jnp.where(lane == i1, neg, le)
    m2 = jnp.max(le2, axis=-1, keepdims=True)
    i2 = jnp.min(jnp.where(le2 == m2, lane, big), axis=-1, keepdims=True)
    r = jnp.exp(m2 - m1)
    w1 = g_top / (1.0 + r)
    w2 = g_top * r / (1.0 + r)
    ri_ref[0] = jnp.where(lane == 0, i1, jnp.where(lane == 1, i2, 0))
    rw_ref[0] = jnp.where(lane == 0, w1, jnp.where(lane == 1, w2, 0.0))
    pairs = jnp.sum(jnp.where((lane == i1) | (lane == i2), 1.0, 0.0), axis=0, keepdims=True)
    cnt_ref[0, 0] = jnp.broadcast_to(pairs, (SUBLANE, LANE))


def _post(x, conv_out, att, mod, mod_row0, w_out, g_post1, g_pre2, wr_hi, wr_lo, br):
    b, t, _ = x.shape
    tm = MOE_TD
    const = lambda bi, i: (0, 0)
    tok = lambda bi, i: (bi, i, 0)
    return pl.pallas_call(
        _post_kernel,
        grid=(b, t // tm),
        in_specs=[pl.BlockSpec((1, tm, D_MODEL), tok),
                  pl.BlockSpec((1, tm, C_CONV), tok),
                  pl.BlockSpec((1, tm, N_HEADS * D_V), tok),
                  pl.BlockSpec((1, 6, D_MODEL), lambda bi, i: (mod_row0 + bi, 0, 0)),
                  pl.BlockSpec((D_MODEL, D_MODEL), const),
                  pl.BlockSpec((1, D_MODEL), const),
                  pl.BlockSpec((1, D_MODEL), const),
                  pl.BlockSpec((D_MODEL, LANE), const),
                  pl.BlockSpec((D_MODEL, LANE), const),
                  pl.BlockSpec((1, LANE), const)],
        out_specs=[pl.BlockSpec((1, tm, D_MODEL), tok),
                   pl.BlockSpec((1, tm, D_MODEL), tok),
                   pl.BlockSpec((1, tm, LANE), tok),
                   pl.BlockSpec((1, tm, LANE), tok),
                   pl.BlockSpec((1, 1, SUBLANE, LANE), lambda bi, i: (bi, i, 0, 0))],
        out_shape=[jax.ShapeDtypeStruct((b, t, D_MODEL), F32),
                   jax.ShapeDtypeStruct((b, t, D_MODEL), BF16),
                   jax.ShapeDtypeStruct((b, t, LANE), jnp.int32),
                   jax.ShapeDtypeStruct((b, t, LANE), F32),
                   jax.ShapeDtypeStruct((b, t // tm, SUBLANE, LANE), F32)],
        compiler_params=_cparams("arbitrary", "arbitrary"),
        name="post",
    )(x, conv_out, att, mod, w_out, g_post1, g_pre2, wr_hi, wr_lo, br)


def _one_hots(ri):
    lane = lax.broadcasted_iota(jnp.int32, ri.shape, 1)
    oh1 = lane == ri[:, 0:1]
    oh2 = lane == ri[:, 1:2]
    return oh1, oh2, jnp.where(oh1 | oh2, 1.0, 0.0)


def _local_positions(oh1, oh2, oh):
    td = oh.shape[0]
    cnt = jnp.sum(oh, axis=0, keepdims=True)
    nch = jnp.floor((cnt + (CHUNK - 1)) * (1.0 / CHUNK))
    a = lax.broadcasted_iota(jnp.int32, (LANE, LANE), 0)
    b = lax.broadcasted_iota(jnp.int32, (LANE, LANE), 1)
    lower_experts = jnp.where(a < b, 1.0, 0.0).astype(BF16)
    run_start = _dot(jnp.broadcast_to(nch, (SUBLANE, LANE)).astype(BF16), lower_experts)[0:1, :] * CHUNK
    rr = lax.broadcasted_iota(jnp.int32, (td, td), 0)
    cc = lax.broadcasted_iota(jnp.int32, (td, td), 1)
    earlier = jnp.where(cc < rr, 1.0, 0.0).astype(BF16)
    pos = _dot(earlier, oh.astype(BF16)) + run_start
    lp1 = jnp.sum(jnp.where(oh1, pos, 0.0), axis=-1, keepdims=True)
    lp2 = jnp.sum(jnp.where(oh2, pos, 0.0), axis=-1, keepdims=True)
    return lp1, lp2


def _plan(cnt, n_tiles_max):
    k = (cnt + (CHUNK - 1)) // CHUNK
    run_end = jnp.cumsum(k, axis=1)
    run_start = run_end - k
    n_chunks = run_end[:, -1]
    total = jnp.sum(k, axis=0)
    padded = (total + (TILE_CH - 1)) // TILE_CH * TILE_CH
    seg_end = jnp.cumsum(padded)
    seg_start = seg_end - padded
    base = seg_start[None, :] + jnp.cumsum(k, axis=0) - k
    c = jnp.arange(MOE_NCH, dtype=jnp.int32)[None, :, None]
    in_run = (run_start[:, None, :] <= c) & (c < run_end[:, None, :])
    dst = jnp.sum(jnp.where(in_run, base[:, None, :] + c - run_start[:, None, :], 0), axis=2)
    n_tiles = seg_end[-1] // TILE_CH
    tail_start = seg_start + total
    tail_len = padded - total
    i32 = lambda t: t.astype(jnp.int32)
    return (i32(dst), i32(n_chunks), i32(tail_start), i32(tail_len), i32(seg_start // TILE_CH),
            i32(seg_end // TILE_CH), i32(n_tiles.reshape(1)))


def _dispatch_kernel(steps_a, dst_ref, nch_ref, ts_ref, tl_ref, nt_ref, h2a_ref, ria_ref, rwa_ref, h2b_ref, rib_ref,
                     rwb_ref, xs_hbm, ws_hbm, lp_ref, xbuf, wbuf, zx, zw, sem, zsem):
    i = pl.program_id(0)
    n = pl.num_programs(0)
    slot = i % 2
    td = MOE_TD
    n_tiles_max = xs_hbm.shape[0] // MOE_TILE

    def copies(src_x, src_w, src_chunk, dst_chunk, s):
        return (_chunk_copy(src_x, src_chunk, xs_hbm, dst_chunk, s), _chunk_copy(src_w, src_chunk, ws_hbm, dst_chunk, s))

    def zero_tile(m):
        rows = pl.ds(pl.multiple_of(m * MOE_TILE, MOE_TILE), MOE_TILE)
        return (pltpu.make_async_copy(zx, xs_hbm.at[rows, :], zsem.at[1]),
                pltpu.make_async_copy(zw, ws_hbm.at[rows, :], zsem.at[1]))

    @pl.when(i == 0)
    def _():
        zx[...] = jnp.zeros_like(zx)
        zw[...] = jnp.zeros_like(zw)
        for e in range(N_EXPERTS):
            def fill(m, carry, e=e):
                for cp in copies(zx, zw, 0, ts_ref[e] + m, zsem.at[0]):
                    cp.start()
                return carry
            lax.fori_loop(0, tl_ref[e], fill, 0)

        def fill_tile(m, carry):
            for cp in zero_tile(m):
                cp.start()
            return carry
        lax.fori_loop(nt_ref[0], n_tiles_max, fill_tile, 0)

    from_a = i < steps_a
    ri = jnp.where(from_a, ria_ref[...], rib_ref[...])
    rw = jnp.where(from_a, rwa_ref[...], rwb_ref[...])
    h2 = jnp.where(from_a, h2a_ref[...], h2b_ref[...])
    oh1, oh2, oh = _one_hots(ri)
    lp1, lp2 = _local_positions(oh1, oh2, oh)
    lane = lax.broadcasted_iota(jnp.int32, ri.shape, 1)
    lp_ref[...] = jnp.where(lane == 0, lp1, jnp.where(lane == 1, lp2, 0.0)).astype(jnp.int32)

    rr = lax.broadcasted_iota(jnp.int32, (td, td), 0)
    cc = lax.broadcasted_iota(jnp.int32, (td, td), 1)
    to_lanes = lambda col: jnp.sum(jnp.where(rr == cc, col, 0.0), axis=0, keepdims=True)
    row = lax.broadcasted_iota(jnp.int32, (MOE_L, td), 0).astype(F32)
    p1 = row == to_lanes(lp1)
    p2 = row == to_lanes(lp2)
    perm = jnp.where(p1 | p2, 1.0, 0.0).astype(BF16)
    xbuf[slot] = _dot(perm, h2).astype(BF16)
    w = jnp.sum(jnp.where(p1, to_lanes(rw[:, 0:1]), 0.0) + jnp.where(p2, to_lanes(rw[:, 1:2]), 0.0),
                axis=-1, keepdims=True)
    wbuf[slot] = jnp.broadcast_to(w, (MOE_L, LANE))

    def wait_chunks(count, s):
        def body(c, carry):
            for cp in copies(xbuf.at[s], wbuf.at[s], 0, 0, sem.at[s]):
                cp.wait()
            return carry
        lax.fori_loop(0, count, body, 0)

    @pl.when(i > 0)
    def _():
        wait_chunks(nch_ref[i - 1], 1 - slot)

    def send(c, carry):
        for cp in copies(xbuf.at[slot], wbuf.at[slot], c, dst_ref[i, c], sem.at[slot]):
            cp.start()
        return carry
    lax.fori_loop(0, nch_ref[i], send, 0)

    @pl.when(i == n - 1)
    def _():
        wait_chunks(nch_ref[i], slot)
        for e in range(N_EXPERTS):
            def drain(m, carry):
                for cp in copies(zx, zw, 0, 0, zsem.at[0]):
                    cp.wait()
                return carry
            lax.fori_loop(0, tl_ref[e], drain, 0)

        def drain_tile(m, carry):
            for cp in zero_tile(m):
                cp.wait()
            return carry
        lax.fori_loop(nt_ref[0], n_tiles_max, drain_tile, 0)


def _dispatch(dst, n_chunks, tail_start, tail_len, n_tiles, part_a, part_b, n_tiles_max):
    steps_a = part_a[0].shape[0] // MOE_TD
    steps_b = part_b[0].shape[0] // MOE_TD
    in_a = lambda i, *_: (jnp.minimum(i, steps_a - 1), 0)
    in_b = lambda i, *_: (jnp.maximum(i - steps_a, 0), 0)
    specs = lambda f: [pl.BlockSpec((MOE_TD, D_MODEL), f), pl.BlockSpec((MOE_TD, LANE), f),
                       pl.BlockSpec((MOE_TD, LANE), f)]
    rows = n_tiles_max * MOE_TILE
    return pl.pallas_call(
        functools.partial(_dispatch_kernel, steps_a),
        grid_spec=pltpu.PrefetchScalarGridSpec(
            num_scalar_prefetch=5,
            grid=(steps_a + steps_b,),
            in_specs=specs(in_a) + specs(in_b),
            out_specs=[pl.BlockSpec(memory_space=pl.ANY),
                       pl.BlockSpec(memory_space=pl.ANY),
                       pl.BlockSpec((MOE_TD, LANE), lambda i, *_: (i, 0))],
            scratch_shapes=[pltpu.VMEM((2, MOE_L, D_MODEL), BF16), pltpu.VMEM((2, MOE_L, LANE), F32),
                            pltpu.VMEM((MOE_TILE, D_MODEL), BF16), pltpu.VMEM((MOE_TILE, LANE), F32),
                            pltpu.SemaphoreType.DMA((2,)), pltpu.SemaphoreType.DMA((2,))]),
        out_shape=[jax.ShapeDtypeStruct((rows, D_MODEL), BF16),
                   jax.ShapeDtypeStruct((rows, LANE), F32),
                   jax.ShapeDtypeStruct(((steps_a + steps_b) * MOE_TD, LANE), jnp.int32)],
        compiler_params=_cparams("arbitrary"),
        name="moe_dispatch",
    )(dst, n_chunks, tail_start, tail_len, n_tiles, *part_a, *part_b)


def _moe_gemm_kernel(t0_ref, t1_ref, nt_ref, wg_ref, wu_ref, wd_ref, xs_hbm, ws_hbm, y_hbm,
                     wg_b, wu_b, wd_b, xbuf, wbuf, ybuf, zbuf, isem, osem, zsem):
    e = pl.program_id(0)
    nt = nt_ref[0]
    n_tiles_max = y_hbm.shape[0] // MOE_TILE
    tile_rows = lambda t: pl.ds(pl.multiple_of(t * MOE_TILE, MOE_TILE), MOE_TILE)

    def load(t, s):
        return (pltpu.make_async_copy(xs_hbm.at[tile_rows(t), :], xbuf.at[s], isem.at[s]),
                pltpu.make_async_copy(ws_hbm.at[tile_rows(t), :], wbuf.at[s], isem.at[s]))

    def store(t, s):
        return pltpu.make_async_copy(ybuf.at[s], y_hbm.at[tile_rows(t), :], osem.at[s])

    def zero_tile(t):
        return pltpu.make_async_copy(zbuf, y_hbm.at[tile_rows(t), :], zsem.at[0])

    @pl.when(e == 0)
    def _():
        for t in range(GEMM_RING - 1):
            @pl.when(t < nt)
            def _(t=t):
                for cp in load(t, t):
                    cp.start()
        zbuf[...] = jnp.zeros_like(zbuf)

        def fill(t, carry):
            zero_tile(t).start()
            return carry
        lax.fori_loop(nt, n_tiles_max, fill, 0)

    wg_b[...] = wg_ref[0].astype(BF16)
    wu_b[...] = wu_ref[0].astype(BF16)
    wd_b[...] = wd_ref[0].astype(BF16)

    def tile(t, carry):
        r = t % GEMM_RING
        s = t % 2
        for cp in load(t, r):
            cp.wait()

        @pl.when(t + (GEMM_RING - 1) < nt)
        def _():
            for cp in load(t + (GEMM_RING - 1), (t + (GEMM_RING - 1)) % GEMM_RING):
                cp.start()

        x = xbuf[r]
        a = _dot(x, wg_b[...])
        u = _dot(x, wu_b[...])
        he = (a * _sigmoid(a)) * u
        y = _dot(he.astype(BF16), wd_b[...]) * wbuf[r][:, 0:1]

        @pl.when(t >= 2)
        def _():
            store(t - 2, s).wait()
        ybuf[s] = y.astype(BF16)
        store(t, s).start()
        return carry

    lax.fori_loop(t0_ref[e], t1_ref[e], tile, 0)

    @pl.when(e == pl.num_programs(0) - 1)
    def _():
        @pl.when(nt >= 1)
        def _():
            store(nt - 1, (nt - 1) % 2).wait()

        @pl.when(nt >= 2)
        def _():
            store(nt - 2, nt % 2).wait()

        def drain(t, carry):
            zero_tile(t).wait()
            return carry
        lax.fori_loop(nt, n_tiles_max, drain, 0)


def _moe_gemm(tile0, tile1, n_tiles, xs, ws, w_gate, w_up, w_down):
    wmap = lambda e, *_: (e, 0, 0)
    return pl.pallas_call(
        _moe_gemm_kernel,
        grid_spec=pltpu.PrefetchScalarGridSpec(
            num_scalar_prefetch=3,
            grid=(N_EXPERTS,),
            in_specs=[pl.BlockSpec((1, D_MODEL, D_EXPERT), wmap),
                      pl.BlockSpec((1, D_MODEL, D_EXPERT), wmap),
                      pl.BlockSpec((1, D_EXPERT, D_MODEL), wmap),
                      pl.BlockSpec(memory_space=pl.ANY),
                      pl.BlockSpec(memory_space=pl.ANY)],
            out_specs=pl.BlockSpec(memory_space=pl.ANY),
            scratch_shapes=[pltpu.VMEM((D_MODEL, D_EXPERT), BF16), pltpu.VMEM((D_MODEL, D_EXPERT), BF16),
                            pltpu.VMEM((D_EXPERT, D_MODEL), BF16),
                            pltpu.VMEM((GEMM_RING, MOE_TILE, D_MODEL), BF16),
                            pltpu.VMEM((GEMM_RING, MOE_TILE, LANE), F32),
                            pltpu.VMEM((2, MOE_TILE, D_MODEL), BF16), pltpu.VMEM((MOE_TILE, D_MODEL), BF16),
                            pltpu.SemaphoreType.DMA((GEMM_RING,)), pltpu.SemaphoreType.DMA((2,)),
                            pltpu.SemaphoreType.DMA((1,))]),
        out_shape=jax.ShapeDtypeStruct(xs.shape, BF16),
        compiler_params=_cparams("arbitrary"),
        name="moe_gemm",
    )(tile0, tile1, n_tiles, w_gate, w_up, w_down, xs, ws)


def _moe_combine_kernel(step0, dst_ref, nch_ref, lp_ref, x1_ref, mod_ref, gpost_ref, y_hbm, o_ref, ybuf, sem):
    i = pl.program_id(0)
    n = pl.num_programs(0)
    slot = i % 2

    def fetch(step, s):
        def body(c, carry):
            _chunk_copy(y_hbm, dst_ref[step, c], ybuf.at[s], c, sem.at[s]).start()
            return carry
        lax.fori_loop(0, nch_ref[step], body, 0)

    @pl.when(i == 0)
    def _():
        ybuf[...] = jnp.zeros_like(ybuf)
        fetch(step0, 0)

    @pl.when(i + 1 < n)
    def _():
        fetch(step0 + i + 1, 1 - slot)

    def wait(c, carry):
        _chunk_copy(y_hbm, 0, ybuf.at[slot], 0, sem.at[slot]).wait()
        return carry
    lax.fori_loop(0, nch_ref[step0 + i], wait, 0)

    lp = lp_ref[...]
    col = lax.broadcasted_iota(jnp.int32, (MOE_TD, MOE_L), 1)
    unperm = jnp.where((col == lp[:, 0:1]) | (col == lp[:, 1:2]), 1.0, 0.0).astype(BF16)
    moe = _dot(unperm, ybuf[slot])
    gt2 = mod_ref[0, 5:6, :]
    o_ref[...] = x1_ref[...] + gt2 * _rms(moe, gpost_ref[...])


def _moe_combine(step0, dst, n_chunks, lp, x1, mod, mod_row, y_sorted, g_post2):
    n = x1.shape[0]
    tok = lambda i, *_: (i, 0)
    return pl.pallas_call(
        functools.partial(_moe_combine_kernel, step0),
        grid_spec=pltpu.PrefetchScalarGridSpec(
            num_scalar_prefetch=2,
            grid=(n // MOE_TD,),
            in_specs=[pl.BlockSpec((MOE_TD, LANE), lambda i, *_: (step0 + i, 0)),
                      pl.BlockSpec((MOE_TD, D_MODEL), tok),
                      pl.BlockSpec((1, 6, D_MODEL), lambda i, *_: (mod_row(i), 0, 0)),
                      pl.BlockSpec((1, D_MODEL), lambda i, *_: (0, 0)),
                      pl.BlockSpec(memory_space=pl.ANY)],
            out_specs=pl.BlockSpec((MOE_TD, D_MODEL), tok),
            scratch_shapes=[pltpu.VMEM((2, MOE_L, D_MODEL), BF16), pltpu.SemaphoreType.DMA((2,))]),
        out_shape=jax.ShapeDtypeStruct((n, D_MODEL), F32),
        compiler_params=_cparams("arbitrary"),
        name="moe_combine",
    )(dst, n_chunks, lp, x1, mod, g_post2, y_sorted)


def _moe(part_a, part_b, mod, mod_row_a, mod_row_b, w_gate, w_up, w_down, g_post2):
    cnt = jnp.concatenate([part_a[4], part_b[4]], axis=0)[:, 0, :N_EXPERTS].astype(jnp.int32)
    steps_a = part_a[4].shape[0]
    n_tiles_max = (cnt.shape[0] * MOE_NCH + N_EXPERTS * (TILE_CH - 1)) // TILE_CH + 1
    dst, n_chunks, tail_start, tail_len, tile0, tile1, n_tiles = _plan(cnt, n_tiles_max)
    xs, ws, lp = _dispatch(dst, n_chunks, tail_start, tail_len, n_tiles, part_a[1:4], part_b[1:4], n_tiles_max)
    y_sorted = _moe_gemm(tile0, tile1, n_tiles, xs, ws, w_gate, w_up, w_down)
    ya = _moe_combine(0, dst, n_chunks, lp, part_a[0], mod, mod_row_a, y_sorted, g_post2)
    yb = _moe_combine(steps_a, dst, n_chunks, lp, part_b[0], mod, mod_row_b, y_sorted, g_post2)
    return ya, yb


def _rotate_half_cols(w):
    n = w.shape[-1]
    w4 = w.reshape(w.shape[:-1] + (n // 32, 2, 16))
    return jnp.stack([-w4[..., 1, :], w4[..., 0, :]], axis=-2).reshape(w.shape)


def _pad_lanes(w):
    return jnp.concatenate([w, jnp.zeros(w.shape[:-1] + (LANE - w.shape[-1],), w.dtype)], axis=-1)


def _rope_tables(t):
    rows = t // GRID_W
    n_freq = D_ROPE // 4
    freqs = ROPE_BASE ** (-jnp.arange(n_freq, dtype=F32) / n_freq)
    row = jnp.repeat(jnp.arange(rows, dtype=F32), GRID_W)
    col = jnp.tile(jnp.arange(GRID_W, dtype=F32), rows)
    ang_r = row[:, None] * freqs
    ang_c = col[:, None] * freqs
    cos = jnp.concatenate([jnp.cos(ang_r), jnp.cos(ang_r), jnp.cos(ang_c), jnp.cos(ang_c)], axis=-1)
    sin = jnp.concatenate([jnp.sin(ang_r), jnp.sin(ang_r), jnp.sin(ang_c), jnp.sin(ang_c)], axis=-1)
    return _pad_lanes(cos), _pad_lanes(sin)


def kernel(x_prompt, x_sample, cache_ckv, cache_krope, c, c_ctx, w_ada, b_ada, g_pre1, g_post1, g_pre2, g_post2, w_in, w_dw, b_dw, conv_ln_g, conv_ln_b, q_norm_g, kv_norm_g, w_uq, w_ukv, w_out, w_rg, b_rg, w_re, b_re, w_gate, w_up, w_down):
    nb, seq, d = x_prompt.shape
    db, dseq, _ = x_sample.shape
    l = 0

    cvec = jnp.concatenate([c_ctx[None, :], c, jnp.zeros((8 - 1 - db, d), F32)], axis=0)
    mod = _ada(cvec, w_ada[l], b_ada[l]).reshape(8, 6, d)

    o_kr = 2 * C_CONV + Q_LORA + KV_LORA
    w_in_l = w_in[l]
    w_kr = w_in_l[:, o_kr:]
    w_in_ctx = jnp.concatenate([w_in_l[:, :o_kr], _pad_lanes(w_kr)], axis=-1).astype(BF16)
    w_in_lat = jnp.concatenate([w_in_l[:, :o_kr], _pad_lanes(w_kr), _pad_lanes(_rotate_half_cols(w_kr))],
                               axis=-1).astype(BF16)
    wuq = w_uq[l].reshape(Q_LORA, N_HEADS, D_NOPE + D_ROPE)
    wq_rope = wuq[:, :, D_NOPE:]
    wq = jnp.concatenate([wuq[:, :, :D_NOPE], _pad_lanes(wq_rope)], axis=-1)
    wq = wq.reshape(Q_LORA, N_HEADS * HEAD_W).astype(BF16)
    wqs = _pad_lanes(_rotate_half_cols(wq_rope)).reshape(Q_LORA, N_HEADS * LANE).astype(BF16)
    wukv4 = w_ukv[l].reshape(KV_LORA, N_HEADS, D_NOPE + D_V)
    wukv = jnp.concatenate([wukv4[:, :, :D_NOPE].reshape(KV_LORA, N_HEADS * D_NOPE),
                            wukv4[:, :, D_NOPE:].reshape(KV_LORA, N_HEADS * D_V)], axis=-1).astype(BF16)
    w_out_b = w_out[l].astype(BF16)
    w_r = jnp.concatenate([w_re[l], w_rg[l], jnp.zeros((d, LANE - N_EXPERTS - N_GROUPS), F32)], axis=-1)
    wr_hi = w_r.astype(BF16)
    wr_lo = (w_r - wr_hi.astype(F32)).astype(BF16)
    b_r = jnp.concatenate([b_re[l], b_rg[l], jnp.zeros((LANE - N_EXPERTS - N_GROUPS,), F32)])[None, :]
    cos, sin = _rope_tables(dseq)
    row = lambda v: v[l][None, :]

    tm_c = 512
    xp_flat = x_prompt.reshape(1, nb * seq, d)
    hglu, q, k, v, ckv, kr = _mix_in(xp_flat, mod, 0, row(g_pre1), w_in_ctx, row(q_norm_g), row(kv_norm_g),
                                     wq, wqs, wukv, cos[:tm_c], sin[:tm_c], False, tm_c)
    per_seq = lambda a: a.reshape(nb, seq, a.shape[-1])
    conv_out = _conv(per_seq(hglu), w_dw[l], row(b_dw), row(conv_ln_g), row(conv_ln_b))
    att = _attn(per_seq(q), per_seq(k), per_seq(v))
    flat = lambda a: a.reshape(1, nb * seq, a.shape[-1])
    post_c = _post(xp_flat, flat(conv_out), flat(att), mod, 0, w_out_b, row(g_post1), row(g_pre2),
                   wr_hi, wr_lo, b_r)
    state_ckv = ckv.reshape(nb, 1, seq, KV_LORA)
    state_krope = kr.reshape(nb, 1, seq, D_ROPE)

    tm_s = 512
    kc, vc = _cache_kv(cache_ckv[:, l], cache_krope[:, l], wukv)
    hglu, q, k, v, _, _ = _mix_in(x_sample, mod, 1, row(g_pre1), w_in_lat, row(q_norm_g), row(kv_norm_g),
                                  wq, wqs, wukv, cos, sin, True, tm_s)
    conv_out = _conv(hglu, w_dw[l], row(b_dw), row(conv_ln_g), row(conv_ln_b))
    att = _attn(q, k, v, kc, vc)
    post_s = _post(x_sample, conv_out, att, mod, 1, w_out_b, row(g_post1), row(g_pre2), wr_hi, wr_lo, b_r)

    tokens = lambda parts: tuple(a.reshape((-1,) + a.shape[2:]) for a in parts)
    steps_per_req = dseq // MOE_TD
    yp, ys = _moe(tokens(post_c), tokens(post_s), mod, lambda i: 0, lambda i: 1 + i // steps_per_req,
                  w_gate[l], w_up[l], w_down[l], row(g_post2))

    return (yp.reshape(nb, seq, d), ys.reshape(db, dseq, d), state_ckv, state_krope)
```

```python
import functools

import jax
import jax.numpy as jnp
import numpy as np
from jax import lax
from jax.experimental import pallas as pl
from jax.experimental.pallas import tpu as pltpu

D_MODEL = 1024
GRID_W = 64
C_CONV = 512
CONV_K = 31
N_HEADS = 4
D_NOPE = 128
D_ROPE = 64
D_V = 128
Q_LORA = 384
KV_LORA = 256
N_GROUPS = 4
E_PER_GROUP = 8
N_EXPERTS = 32
D_EXPERT = 256
ROPE_BASE = 10000.0
EPS = 1e-6
ATT_SCALE = (D_NOPE + D_ROPE) ** -0.5
LOG2E = 1.4426950408889634

LANE = 128
SUBLANE = 8
HEAD_W = 2 * LANE
CONV_HALO = 16
MOE_TD = 512
CHUNK = 2 * SUBLANE
MOE_NCH = (2 * MOE_TD + N_EXPERTS * (CHUNK - 1)) // CHUNK
MOE_L = MOE_NCH * CHUNK
MOE_TILE = 256
TILE_CH = MOE_TILE // CHUNK
GEMM_RING = 4
XS_W = D_MODEL + LANE
VMEM_LIMIT = 56 * 1024 * 1024

BF16 = jnp.bfloat16
F32 = jnp.float32


def _cparams(*sem):
    return pltpu.CompilerParams(dimension_semantics=sem, vmem_limit_bytes=VMEM_LIMIT)


def _rms(x, g):
    return x * lax.rsqrt(jnp.mean(x * x, axis=-1, keepdims=True) + EPS) * g


def _sigmoid(x):
    return 1.0 / (1.0 + jnp.exp(-x))


def _dot(a, b):
    return jnp.dot(a, b, preferred_element_type=F32)


def _chunk_copy(src, src_chunk, dst, dst_chunk, sem):
    rows = lambda c: pl.ds(c * CHUNK if isinstance(c, int) else pl.multiple_of(c * CHUNK, CHUNK), CHUNK)
    return pltpu.make_async_copy(src.at[rows(src_chunk), :], dst.at[rows(dst_chunk), :], sem)


def _ada_kernel(c_ref, w_ref, b_ref, o_ref):
    c = c_ref[...]
    s = (c * _sigmoid(c)).astype(BF16)
    o_ref[...] = _dot(s, w_ref[...].astype(BF16)) + b_ref[...]


def _ada(cvec, w_ada, b_ada):
    n = w_ada.shape[1]
    tn = 1536
    return pl.pallas_call(
        _ada_kernel,
        grid=(n // tn,),
        in_specs=[pl.BlockSpec((8, D_MODEL), lambda j: (0, 0)),
                  pl.BlockSpec((D_MODEL, tn), lambda j: (0, j)),
                  pl.BlockSpec((1, tn), lambda j: (0, j))],
        out_specs=pl.BlockSpec((8, tn), lambda j: (0, j)),
        out_shape=jax.ShapeDtypeStruct((8, n), F32),
        compiler_params=_cparams("arbitrary"),
        name="ada",
    )(cvec, w_ada, b_ada.reshape(1, n))


def _mix_in_kernel(rope, x_ref, mod_ref, gpre_ref, win_ref, qg_ref, kvg_ref, wq_ref, wqs_ref,
                   wukv_ref, cos_ref, sin_ref, hglu_ref, q_ref, k_ref, v_ref, ckv_ref, kr_ref):
    x = x_ref[0]
    sh1 = mod_ref[0, 0:1, :]
    sc1 = mod_ref[0, 1:2, :]
    h = _rms(x, gpre_ref[...]) * (1.0 + sc1) + sh1
    u = _dot(h.astype(BF16), win_ref[...])

    a = u[:, :C_CONV]
    g = u[:, C_CONV:2 * C_CONV]
    hglu_ref[0] = a * _sigmoid(g)

    o_q = 2 * C_CONV
    o_kv = o_q + Q_LORA
    o_kr = o_kv + KV_LORA
    qn = _rms(u[:, o_q:o_kv], qg_ref[...]).astype(BF16)
    qf = _dot(qn, wq_ref[...])
    ckv = _rms(u[:, o_kv:o_kr], kvg_ref[...])
    ckv_ref[0] = ckv
    kvd = _dot(ckv.astype(BF16), wukv_ref[...])
    kr = u[:, o_kr:o_kr + LANE]
    kr_ref[0] = kr[:, :D_ROPE]
    if rope:
        cos = cos_ref[...]
        sin = sin_ref[...]
        qs = _dot(qn, wqs_ref[...])
        kr = kr * cos + u[:, o_kr + LANE:o_kr + 2 * LANE] * sin
    q_parts = []
    k_parts = []
    for hd in range(N_HEADS):
        q_parts.append(qf[:, hd * HEAD_W:hd * HEAD_W + LANE])
        qr = qf[:, hd * HEAD_W + LANE:(hd + 1) * HEAD_W]
        if rope:
            qr = qr * cos + qs[:, hd * LANE:(hd + 1) * LANE] * sin
        q_parts.append(qr)
        k_parts.append(kvd[:, hd * D_NOPE:(hd + 1) * D_NOPE])
        k_parts.append(kr)
    q_ref[0] = jnp.concatenate(q_parts, axis=-1).astype(BF16)
    k_ref[0] = jnp.concatenate(k_parts, axis=-1).astype(BF16)
    v_ref[0] = kvd[:, N_HEADS * D_NOPE:].astype(BF16)


def _mix_in(x, mod, mod_row0, g_pre1, w_in_ext, q_norm_g, kv_norm_g, wq, wqs, wukv, cos, sin, rope, tm):
    b, t, _ = x.shape
    ncol = w_in_ext.shape[1]
    const = lambda bi, i: (0, 0)
    tok = lambda bi, i: (bi, i, 0)
    table = (lambda bi, i: (i, 0)) if rope else const
    outs = [(C_CONV, F32), (N_HEADS * HEAD_W, BF16), (N_HEADS * HEAD_W, BF16), (N_HEADS * D_V, BF16),
            (KV_LORA, F32), (D_ROPE, F32)]
    return pl.pallas_call(
        functools.partial(_mix_in_kernel, rope),
        grid=(b, t // tm),
        in_specs=[pl.BlockSpec((1, tm, D_MODEL), tok),
                  pl.BlockSpec((1, 6, D_MODEL), lambda bi, i: (mod_row0 + bi, 0, 0)),
                  pl.BlockSpec((1, D_MODEL), const),
                  pl.BlockSpec((D_MODEL, ncol), const),
                  pl.BlockSpec((1, Q_LORA), const),
                  pl.BlockSpec((1, KV_LORA), const),
                  pl.BlockSpec((Q_LORA, N_HEADS * HEAD_W), const),
                  pl.BlockSpec((Q_LORA, N_HEADS * LANE), const),
                  pl.BlockSpec((KV_LORA, N_HEADS * (D_NOPE + D_V)), const),
                  pl.BlockSpec((tm, LANE), table),
                  pl.BlockSpec((tm, LANE), table)],
        out_specs=[pl.BlockSpec((1, tm, w), tok) for w, _ in outs],
        out_shape=[jax.ShapeDtypeStruct((b, t, w), dt) for w, dt in outs],
        compiler_params=_cparams("arbitrary", "arbitrary"),
        name="mix_in_rope" if rope else "mix_in",
    )(x, mod, g_pre1, w_in_ext, q_norm_g, kv_norm_g, wq, wqs, wukv, cos, sin)


def _conv_kernel(t, tb, tt, h_ref, w_ref, b_ref, g_ref, bb_ref, o_ref, pad_ref, sh_ref):
    zeros = jnp.zeros((CONV_HALO, C_CONV), F32)
    pad_ref[0:CONV_HALO, :] = zeros
    pad_ref[CONV_HALO + t:, :] = zeros
    pad_ref[CONV_HALO:CONV_HALO + t, :] = h_ref[0]
    first = CONV_HALO - CONV_K // 2
    rows = tb + 2 * CONV_HALO - SUBLANE

    def block(bi, carry):
        base = pl.multiple_of(bi * tb, tb)
        win = pad_ref[pl.ds(base, tb + 2 * CONV_HALO), :]
        for s in range(SUBLANE):
            sh_ref[s, 0:rows, :] = win[s:s + rows, :]
        for c in range(tb // tt):
            acc = jnp.zeros((tt, C_CONV), F32)
            for k in range(CONV_K):
                off = first + k
                r0 = c * tt + off // SUBLANE * SUBLANE
                acc = acc + sh_ref[off % SUBLANE, r0:r0 + tt, :] * w_ref[k:k + 1, :]
            y = acc + b_ref[...]
            mu = jnp.mean(y, axis=-1, keepdims=True)
            yc = y - mu
            var = jnp.mean(yc * yc, axis=-1, keepdims=True)
            z = yc * lax.rsqrt(var + EPS) * g_ref[...] + bb_ref[...]
            o_ref[0, pl.ds(base + c * tt, tt), :] = (z * _sigmoid(z)).astype(BF16)
        return carry

    lax.fori_loop(0, t // tb, block, 0)


def _conv(hglu, w_dw, b_dw, ln_g, ln_b):
    b, t, _ = hglu.shape
    tb, tt = 256, 32
    const = lambda bi: (0, 0)
    return pl.pallas_call(
        functools.partial(_conv_kernel, t, tb, tt),
        grid=(b,),
        in_specs=[pl.BlockSpec((1, t, C_CONV), lambda bi: (bi, 0, 0)),
                  pl.BlockSpec((CONV_K, C_CONV), const),
                  pl.BlockSpec((1, C_CONV), const),
                  pl.BlockSpec((1, C_CONV), const),
                  pl.BlockSpec((1, C_CONV), const)],
        out_specs=pl.BlockSpec((1, t, C_CONV), lambda bi: (bi, 0, 0)),
        out_shape=jax.ShapeDtypeStruct((b, t, C_CONV), BF16),
        scratch_shapes=[pltpu.VMEM((t + 2 * CONV_HALO, C_CONV), F32),
                        pltpu.VMEM((SUBLANE, tb + 2 * CONV_HALO, C_CONV), F32)],
        compiler_params=_cparams("arbitrary"),
        name="conv",
    )(hglu, w_dw, b_dw, ln_g, ln_b)


def _qk(q, k):
    return lax.dot_general(q, k, (((1,), (1,)), ((), ())), preferred_element_type=F32)


def _attn_kernel(cached, q_ref, k_ref, v_ref, *rest):
    if cached:
        kc_ref, vc_ref, o_ref = rest
    else:
        (o_ref,) = rest
    outs = []
    for hd in range(N_HEADS):
        ks = slice(hd * HEAD_W, (hd + 1) * HEAD_W)
        vs = slice(hd * D_V, (hd + 1) * D_V)
        qh = q_ref[0, :, ks]
        s = _qk(qh, k_ref[0, :, ks]) * (ATT_SCALE * LOG2E)
        m = jnp.max(s, axis=-1, keepdims=True)
        if cached:
            sc = _qk(qh, kc_ref[0, :, ks]) * (ATT_SCALE * LOG2E)
            m = jnp.maximum(m, jnp.max(sc, axis=-1, keepdims=True))
        p = jnp.exp2(s - m)
        l = jnp.sum(p, axis=-1, keepdims=True)
        o = _dot(p.astype(BF16), v_ref[0, :, vs])
        if cached:
            pc = jnp.exp2(sc - m)
            l = l + jnp.sum(pc, axis=-1, keepdims=True)
            o = o + _dot(pc.astype(BF16), vc_ref[0, :, vs])
        outs.append(o / l)
    o_ref[0] = jnp.concatenate(outs, axis=-1).astype(BF16)


def _attn(q, k, v, kc=None, vc=None, tq=256):
    b, t, _ = q.shape
    s = k.shape[1]
    cached = kc is not None
    whole = lambda bi, i: (bi, 0, 0)
    in_specs = [pl.BlockSpec((1, tq, N_HEADS * HEAD_W), lambda bi, i: (bi, i, 0)),
                pl.BlockSpec((1, s, N_HEADS * HEAD_W), whole),
                pl.BlockSpec((1, s, N_HEADS * D_V), whole)]
    args = [q, k, v]
    if cached:
        sc = kc.shape[1]
        in_specs += [pl.BlockSpec((1, sc, N_HEADS * HEAD_W), whole),
                     pl.BlockSpec((1, sc, N_HEADS * D_V), whole)]
        args += [kc, vc]
    return pl.pallas_call(
        functools.partial(_attn_kernel, cached),
        grid=(b, t // tq),
        in_specs=in_specs,
        out_specs=pl.BlockSpec((1, tq, N_HEADS * D_V), lambda bi, i: (bi, i, 0)),
        out_shape=jax.ShapeDtypeStruct((b, t, N_HEADS * D_V), BF16),
        compiler_params=_cparams("arbitrary", "arbitrary"),
        name="attn_cached" if cached else "attn",
    )(*args)


def _cache_kv_kernel(ckv_ref, kr_ref, wukv_ref, k_ref, v_ref):
    kvd = _dot(ckv_ref[0].astype(BF16), wukv_ref[...])
    kr = kr_ref[0]
    kr = jnp.concatenate([kr, jnp.zeros_like(kr)], axis=-1)
    parts = []
    for hd in range(N_HEADS):
        parts.append(kvd[:, hd * D_NOPE:(hd + 1) * D_NOPE])
        parts.append(kr)
    k_ref[0] = jnp.concatenate(parts, axis=-1).astype(BF16)
    v_ref[0] = kvd[:, N_HEADS * D_NOPE:].astype(BF16)


def _cache_kv(ckv, krope, wukv):
    b, s, _ = ckv.shape
    tok = lambda bi: (bi, 0, 0)
    return pl.pallas_call(
        _cache_kv_kernel,
        grid=(b,),
        in_specs=[pl.BlockSpec((1, s, KV_LORA), tok),
                  pl.BlockSpec((1, s, D_ROPE), tok),
                  pl.BlockSpec((KV_LORA, N_HEADS * (D_NOPE + D_V)), lambda bi: (0, 0))],
        out_specs=[pl.BlockSpec((1, s, N_HEADS * HEAD_W), tok),
                   pl.BlockSpec((1, s, N_HEADS * D_V), tok)],
        out_shape=[jax.ShapeDtypeStruct((b, s, N_HEADS * HEAD_W), BF16),
                   jax.ShapeDtypeStruct((b, s, N_HEADS * D_V), BF16)],
        compiler_params=_cparams("arbitrary"),
        name="cache_kv",
    )(ckv, krope, wukv)


def _post_kernel(x_ref, conv_ref, att_ref, mod_ref, wo_ref, gpost_ref, gpre2_ref, wr_hi_ref, wr_cat_ref,
                 br_ref, x1_ref, h2_ref, ri_ref, rw_ref, cnt_ref):
    out = _dot(conv_ref[0], wo_ref[:C_CONV, :]) + _dot(att_ref[0], wo_ref[C_CONV:, :])
    gt1 = mod_ref[0, 2:3, :]
    sh2 = mod_ref[0, 3:4, :]
    sc2 = mod_ref[0, 4:5, :]
    x1 = x_ref[0] + gt1 * _rms(out, gpost_ref[...])
    x1_ref[0] = x1
    h2 = _rms(x1, gpre2_ref[...]) * (1.0 + sc2) + sh2
    h_hi = h2.astype(BF16)
    h2_ref[0] = h_hi
    h_lo = (h2 - h_hi.astype(F32)).astype(BF16)
    hi_terms = _dot(h_hi, wr_cat_ref[...])
    logits = hi_terms + pltpu.roll(hi_terms, LANE // 2, axis=1) + _dot(h_lo, wr_hi_ref[...]) + br_ref[...]

    lane = lax.broadcasted_iota(jnp.int32, logits.shape, 1)
    neg = jnp.float32(-jnp.inf)
    big = jnp.int32(LANE)
    is_g = (lane >= N_EXPERTS) & (lane < N_EXPERTS + N_GROUPS)
    lg = jnp.where(is_g, logits, neg)
    gmax = jnp.max(lg, axis=-1, keepdims=True)
    gidx = jnp.min(jnp.where(lg == gmax, lane, big), axis=-1, keepdims=True) - N_EXPERTS
    g_top = 1.0 / jnp.sum(jnp.exp(lg - gmax), axis=-1, keepdims=True)

    in_grp = (lane >= gidx * E_PER_GROUP) & (lane < (gidx + 1) * E_PER_GROUP)
    le = jnp.where(in_grp, logits, neg)
    m1 = jnp.max(le, axis=-1, keepdims=True)
    i1 = jnp.min(jnp.where(le == m1, lane, big), axis=-1, keepdims=True)
    le2 = jnp.where(lane == i1, neg, le)
    m2 = jnp.max(le2, axis=-1, keepdims=True)
    i2 = jnp.min(jnp.where(le2 == m2, lane, big), axis=-1, keepdims=True)
    r = jnp.exp(m2 - m1)
    w1 = g_top / (1.0 + r)
    w2 = g_top * r / (1.0 + r)
    ri_ref[0] = jnp.where(lane == 0, i1, jnp.where(lane == 1, i2, 0))
    rw_ref[0] = jnp.where(lane == 0, w1, jnp.where(lane == 1, w2, 0.0))
    pairs = jnp.sum(jnp.where((lane == i1) | (lane == i2), 1.0, 0.0), axis=0, keepdims=True)
    cnt_ref[0, 0] = jnp.broadcast_to(pairs, (SUBLANE, LANE))


def _post(x, conv_out, att, mod, mod_row0, w_out, g_post1, g_pre2, wr_hi, wr_lo, br):
    b, t, _ = x.shape
    tm = MOE_TD
    const = lambda bi, i: (0, 0)
    tok = lambda bi, i: (bi, i, 0)
    return pl.pallas_call(
        _post_kernel,
        grid=(b, t // tm),
        in_specs=[pl.BlockSpec((1, tm, D_MODEL), tok),
                  pl.BlockSpec((1, tm, C_CONV), tok),
                  pl.BlockSpec((1, tm, N_HEADS * D_V), tok),
                  pl.BlockSpec((1, 6, D_MODEL), lambda bi, i: (mod_row0 + bi, 0, 0)),
                  pl.BlockSpec((D_MODEL, D_MODEL), const),
                  pl.BlockSpec((1, D_MODEL), const),
                  pl.BlockSpec((1, D_MODEL), const),
                  pl.BlockSpec((D_MODEL, LANE), const),
                  pl.BlockSpec((D_MODEL, LANE), const),
                  pl.BlockSpec((1, LANE), const)],
        out_specs=[pl.BlockSpec((1, tm, D_MODEL), tok),
                   pl.BlockSpec((1, tm, D_MODEL), tok),
                   pl.BlockSpec((1, tm, LANE), tok),
                   pl.BlockSpec((1, tm, LANE), tok),
                   pl.BlockSpec((1, 1, SUBLANE, LANE), lambda bi, i: (bi, i, 0, 0))],
        out_shape=[jax.ShapeDtypeStruct((b, t, D_MODEL), F32),
                   jax.ShapeDtypeStruct((b, t, D_MODEL), BF16),
                   jax.ShapeDtypeStruct((b, t, LANE), jnp.int32),
                   jax.ShapeDtypeStruct((b, t, LANE), F32),
                   jax.ShapeDtypeStruct((b, t // tm, SUBLANE, LANE), F32)],
        compiler_params=_cparams("arbitrary", "arbitrary"),
        name="post",
    )(x, conv_out, att, mod, w_out, g_post1, g_pre2, wr_hi, wr_lo, br)


def _one_hots(ri):
    lane = lax.broadcasted_iota(jnp.int32, ri.shape, 1)
    oh1 = lane == ri[:, 0:1]
    oh2 = lane == ri[:, 1:2]
    return oh1, oh2, jnp.where(oh1 | oh2, 1.0, 0.0)


def _local_positions(oh1, oh2, oh):
    td = oh.shape[0]
    cnt = jnp.sum(oh, axis=0, keepdims=True)
    nch = jnp.floor((cnt + (CHUNK - 1)) * (1.0 / CHUNK))
    a = lax.broadcasted_iota(jnp.int32, (LANE, LANE), 0)
    b = lax.broadcasted_iota(jnp.int32, (LANE, LANE), 1)
    lower_experts = jnp.where(a < b, 1.0, 0.0).astype(BF16)
    run_start = _dot(jnp.broadcast_to(nch, (SUBLANE, LANE)).astype(BF16), lower_experts)[0:1, :] * CHUNK
    rr = lax.broadcasted_iota(jnp.int32, (td, td), 0)
    cc = lax.broadcasted_iota(jnp.int32, (td, td), 1)
    earlier = jnp.where(cc < rr, 1.0, 0.0).astype(BF16)
    pos = _dot(earlier, oh.astype(BF16)) + run_start
    lp1 = jnp.sum(jnp.where(oh1, pos, 0.0), axis=-1, keepdims=True)
    lp2 = jnp.sum(jnp.where(oh2, pos, 0.0), axis=-1, keepdims=True)
    return lp1, lp2


def _plan(cnt, n_tiles_max):
    k = (cnt + (CHUNK - 1)) // CHUNK
    run_end = jnp.cumsum(k, axis=1)
    run_start = run_end - k
    n_chunks = run_end[:, -1]
    total = jnp.sum(k, axis=0)
    padded = (total + (TILE_CH - 1)) // TILE_CH * TILE_CH
    seg_end = jnp.cumsum(padded)
    seg_start = seg_end - padded
    base = seg_start[None, :] + jnp.cumsum(k, axis=0) - k
    c = jnp.arange(MOE_NCH, dtype=jnp.int32)[None, :, None]
    in_run = (run_start[:, None, :] <= c) & (c < run_end[:, None, :])
    dst = jnp.sum(jnp.where(in_run, base[:, None, :] + c - run_start[:, None, :], 0), axis=2)
    n_tiles = seg_end[-1] // TILE_CH
    tail_start = seg_start + total
    tail_len = padded - total
    i32 = lambda t: t.astype(jnp.int32)
    return (i32(dst), i32(n_chunks), i32(tail_start), i32(tail_len), i32(seg_start // TILE_CH),
            i32(seg_end // TILE_CH), i32(n_tiles.reshape(1)))


def _dispatch_kernel(steps_a, dst_ref, nch_ref, ts_ref, tl_ref, nt_ref, h2a_ref, ria_ref, rwa_ref, h2b_ref, rib_ref,
                     rwb_ref, xs_hbm, lp_ref, xbuf, zx, sem, zsem):
    i = pl.program_id(0)
    n = pl.num_programs(0)
    slot = i % 2
    td = MOE_TD
    n_tiles_max = xs_hbm.shape[0] // MOE_TILE

    def zero_tile(m):
        rows = pl.ds(pl.multiple_of(m * MOE_TILE, MOE_TILE), MOE_TILE)
        return pltpu.make_async_copy(zx, xs_hbm.at[rows, :], zsem.at[1])

    @pl.when(i == 0)
    def _():
        zx[...] = jnp.zeros_like(zx)
        for e in range(N_EXPERTS):
            def fill(m, carry, e=e):
                _chunk_copy(zx, 0, xs_hbm, ts_ref[e] + m, zsem.at[0]).start()
                return carry
            lax.fori_loop(0, tl_ref[e], fill, 0)

        def fill_tile(m, carry):
            zero_tile(m).start()
            return carry
        lax.fori_loop(nt_ref[0], n_tiles_max, fill_tile, 0)

    from_a = i < steps_a
    ri = jnp.where(from_a, ria_ref[...], rib_ref[...])
    rw = jnp.where(from_a, rwa_ref[...], rwb_ref[...])
    h2 = jnp.where(from_a, h2a_ref[...], h2b_ref[...])
    oh1, oh2, oh = _one_hots(ri)
    lp1, lp2 = _local_positions(oh1, oh2, oh)
    lane = lax.broadcasted_iota(jnp.int32, ri.shape, 1)
    lp_ref[...] = jnp.where(lane == 0, lp1, jnp.where(lane == 1, lp2, 0.0)).astype(jnp.int32)

    rr = lax.broadcasted_iota(jnp.int32, (td, td), 0)
    cc = lax.broadcasted_iota(jnp.int32, (td, td), 1)
    to_lanes = lambda col: jnp.sum(jnp.where(rr == cc, col, 0.0), axis=0, keepdims=True)
    row = lax.broadcasted_iota(jnp.int32, (MOE_L, td), 0).astype(F32)
    p1 = row == to_lanes(lp1)
    p2 = row == to_lanes(lp2)
    perm = jnp.where(p1 | p2, 1.0, 0.0).astype(BF16)
    xbuf[slot, :, 0:D_MODEL] = _dot(perm, h2).astype(BF16)
    w = jnp.sum(jnp.where(p1, to_lanes(rw[:, 0:1]), 0.0) + jnp.where(p2, to_lanes(rw[:, 1:2]), 0.0),
                axis=-1, keepdims=True)
    w0 = w.astype(BF16).astype(F32)
    w1 = (w - w0).astype(BF16).astype(F32)
    w2 = (w - w0 - w1).astype(BF16).astype(F32)
    wl = lax.broadcasted_iota(jnp.int32, (MOE_L, LANE), 1)
    terms = jnp.where(wl == 0, w0, jnp.where(wl == 1, w1, jnp.where(wl == 2, w2, 0.0)))
    xbuf[slot, :, D_MODEL:XS_W] = terms.astype(BF16)

    def wait_chunks(count, s):
        def body(c, carry):
            _chunk_copy(xbuf.at[s], 0, xs_hbm, 0, sem.at[s]).wait()
            return carry
        lax.fori_loop(0, count, body, 0)

    @pl.when(i > 0)
    def _():
        wait_chunks(nch_ref[i - 1], 1 - slot)

    def send(c, carry):
        _chunk_copy(xbuf.at[slot], c, xs_hbm, dst_ref[i, c], sem.at[slot]).start()
        return carry
    lax.fori_loop(0, nch_ref[i], send, 0)

    @pl.when(i == n - 1)
    def _():
        wait_chunks(nch_ref[i], slot)
        for e in range(N_EXPERTS):
            def drain(m, carry):
                _chunk_copy(zx, 0, xs_hbm, 0, zsem.at[0]).wait()
                return carry
            lax.fori_loop(0, tl_ref[e], drain, 0)

        def drain_tile(m, carry):
            zero_tile(m).wait()
            return carry
        lax.fori_loop(nt_ref[0], n_tiles_max, drain_tile, 0)


def _dispatch(dst, n_chunks, tail_start, tail_len, n_tiles, part_a, part_b, n_tiles_max):
    steps_a = part_a[0].shape[0] // MOE_TD
    steps_b = part_b[0].shape[0] // MOE_TD
    in_a = lambda i, *_: (jnp.minimum(i, steps_a - 1), 0)
    in_b = lambda i, *_: (jnp.maximum(i - steps_a, 0), 0)
    specs = lambda f: [pl.BlockSpec((MOE_TD, D_MODEL), f), pl.BlockSpec((MOE_TD, LANE), f),
                       pl.BlockSpec((MOE_TD, LANE), f)]
    rows = n_tiles_max * MOE_TILE
    return pl.pallas_call(
        functools.partial(_dispatch_kernel, steps_a),
        grid_spec=pltpu.PrefetchScalarGridSpec(
            num_scalar_prefetch=5,
            grid=(steps_a + steps_b,),
            in_specs=specs(in_a) + specs(in_b),
            out_specs=[pl.BlockSpec(memory_space=pl.ANY),
                       pl.BlockSpec((MOE_TD, LANE), lambda i, *_: (i, 0))],
            scratch_shapes=[pltpu.VMEM((2, MOE_L, XS_W), BF16), pltpu.VMEM((MOE_TILE, XS_W), BF16),
                            pltpu.SemaphoreType.DMA((2,)), pltpu.SemaphoreType.DMA((2,))]),
        out_shape=[jax.ShapeDtypeStruct((rows, XS_W), BF16),
                   jax.ShapeDtypeStruct(((steps_a + steps_b) * MOE_TD, LANE), jnp.int32)],
        compiler_params=_cparams("arbitrary"),
        name="moe_dispatch",
    )(dst, n_chunks, tail_start, tail_len, n_tiles, *part_a, *part_b)


def _moe_gemm_kernel(t0_ref, t1_ref, nt_ref, wg_ref, wu_ref, wd_ref, xs_hbm, y_hbm,
                     wg_b, wu_b, wd_b, xbuf, ybuf, zbuf, isem, osem, zsem):
    e = pl.program_id(0)
    nt = nt_ref[0]
    n_tiles_max = y_hbm.shape[0] // MOE_TILE
    tile_rows = lambda t: pl.ds(pl.multiple_of(t * MOE_TILE, MOE_TILE), MOE_TILE)

    def load(t, s):
        return (pltpu.make_async_copy(xs_hbm.at[tile_rows(t), :], xbuf.at[s], isem.at[s]),)

    def store(t, s):
        return pltpu.make_async_copy(ybuf.at[s], y_hbm.at[tile_rows(t), :], osem.at[s])

    def zero_tile(t):
        return pltpu.make_async_copy(zbuf, y_hbm.at[tile_rows(t), :], zsem.at[0])

    @pl.when(e == 0)
    def _():
        for t in range(GEMM_RING - 1):
            @pl.when(t < nt)
            def _(t=t):
                for cp in load(t, t):
                    cp.start()
        zbuf[...] = jnp.zeros_like(zbuf)

        def fill(t, carry):
            zero_tile(t).start()
            return carry
        lax.fori_loop(nt, n_tiles_max, fill, 0)

    wg_b[...] = wg_ref[0].astype(BF16)
    wu_b[...] = wu_ref[0].astype(BF16)
    wd_b[...] = wd_ref[0].astype(BF16)

    def tile(t, carry):
        r = t % GEMM_RING
        s = t % 2
        for cp in load(t, r):
            cp.wait()

        @pl.when(t + (GEMM_RING - 1) < nt)
        def _():
            for cp in load(t + (GEMM_RING - 1), (t + (GEMM_RING - 1)) % GEMM_RING):
                cp.start()

        x = xbuf[r, :, 0:D_MODEL]
        wt = xbuf[r, :, D_MODEL:XS_W].astype(F32)
        w = wt[:, 0:1] + wt[:, 1:2] + wt[:, 2:3]
        a = _dot(x, wg_b[...])
        u = _dot(x, wu_b[...])
        he = (a * _sigmoid(a)) * u
        y = _dot(he.astype(BF16), wd_b[...]) * w

        @pl.when(t >= 2)
        def _():
            store(t - 2, s).wait()
        ybuf[s] = y.astype(BF16)
        store(t, s).start()
        return carry

    lax.fori_loop(t0_ref[e], t1_ref[e], tile, 0)

    @pl.when(e == pl.num_programs(0) - 1)
    def _():
        @pl.when(nt >= 1)
        def _():
            store(nt - 1, (nt - 1) % 2).wait()

        @pl.when(nt >= 2)
        def _():
            store(nt - 2, nt % 2).wait()

        def drain(t, carry):
            zero_tile(t).wait()
            return carry
        lax.fori_loop(nt, n_tiles_max, drain, 0)


def _moe_gemm(tile0, tile1, n_tiles, xs, w_gate, w_up, w_down):
    wmap = lambda e, *_: (e, 0, 0)
    return pl.pallas_call(
        _moe_gemm_kernel,
        grid_spec=pltpu.PrefetchScalarGridSpec(
            num_scalar_prefetch=3,
            grid=(N_EXPERTS,),
            in_specs=[pl.BlockSpec((1, D_MODEL, D_EXPERT), wmap),
                      pl.BlockSpec((1, D_MODEL, D_EXPERT), wmap),
                      pl.BlockSpec((1, D_EXPERT, D_MODEL), wmap),
                      pl.BlockSpec(memory_space=pl.ANY)],
            out_specs=pl.BlockSpec(memory_space=pl.ANY),
            scratch_shapes=[pltpu.VMEM((D_MODEL, D_EXPERT), BF16), pltpu.VMEM((D_MODEL, D_EXPERT), BF16),
                            pltpu.VMEM((D_EXPERT, D_MODEL), BF16),
                            pltpu.VMEM((GEMM_RING, MOE_TILE, XS_W), BF16),
                            pltpu.VMEM((2, MOE_TILE, D_MODEL), BF16), pltpu.VMEM((MOE_TILE, D_MODEL), BF16),
                            pltpu.SemaphoreType.DMA((GEMM_RING,)), pltpu.SemaphoreType.DMA((2,)),
                            pltpu.SemaphoreType.DMA((1,))]),
        out_shape=jax.ShapeDtypeStruct((xs.shape[0], D_MODEL), BF16),
        compiler_params=_cparams("arbitrary"),
        name="moe_gemm",
    )(tile0, tile1, n_tiles, w_gate, w_up, w_down, xs)


def _moe_combine_kernel(step0, dst_ref, nch_ref, lp_ref, x1_ref, mod_ref, gpost_ref, y_hbm, o_ref, ybuf, sem):
    i = pl.program_id(0)
    n = pl.num_programs(0)
    slot = i % 2

    def fetch(step, s):
        def body(c, carry):
            _chunk_copy(y_hbm, dst_ref[step, c], ybuf.at[s], c, sem.at[s]).start()
            return carry
        lax.fori_loop(0, nch_ref[step], body, 0)

    @pl.when(i == 0)
    def _():
        ybuf[...] = jnp.zeros_like(ybuf)
        fetch(step0, 0)

    @pl.when(i + 1 < n)
    def _():
        fetch(step0 + i + 1, 1 - slot)

    def wait(c, carry):
        _chunk_copy(y_hbm, 0, ybuf.at[slot], 0, sem.at[slot]).wait()
        return carry
    lax.fori_loop(0, nch_ref[step0 + i], wait, 0)

    lp = lp_ref[...]
    col = lax.broadcasted_iota(jnp.int32, (MOE_TD, MOE_L), 1)
    unperm = jnp.where((col == lp[:, 0:1]) | (col == lp[:, 1:2]), 1.0, 0.0).astype(BF16)
    moe = _dot(unperm, ybuf[slot])
    gt2 = mod_ref[0, 5:6, :]
    o_ref[...] = x1_ref[...] + gt2 * _rms(moe, gpost_ref[...])


def _moe_combine(step0, dst, n_chunks, lp, x1, mod, mod_row, y_sorted, g_post2):
    n = x1.shape[0]
    tok = lambda i, *_: (i, 0)
    return pl.pallas_call(
        functools.partial(_moe_combine_kernel, step0),
        grid_spec=pltpu.PrefetchScalarGridSpec(
            num_scalar_prefetch=2,
            grid=(n // MOE_TD,),
            in_specs=[pl.BlockSpec((MOE_TD, LANE), lambda i, *_: (step0 + i, 0)),
                      pl.BlockSpec((MOE_TD, D_MODEL), tok),
                      pl.BlockSpec((1, 6, D_MODEL), lambda i, *_: (mod_row(i), 0, 0)),
                      pl.BlockSpec((1, D_MODEL), lambda i, *_: (0, 0)),
                      pl.BlockSpec(memory_space=pl.ANY)],
            out_specs=pl.BlockSpec((MOE_TD, D_MODEL), tok),
            scratch_shapes=[pltpu.VMEM((2, MOE_L, D_MODEL), BF16), pltpu.SemaphoreType.DMA((2,))]),
        out_shape=jax.ShapeDtypeStruct((n, D_MODEL), F32),
        compiler_params=_cparams("arbitrary"),
        name="moe_combine",
    )(dst, n_chunks, lp, x1, mod, g_post2, y_sorted)


def _moe(part_a, part_b, mod, mod_row_a, mod_row_b, w_gate, w_up, w_down, g_post2):
    cnt = jnp.concatenate([part_a[4], part_b[4]], axis=0)[:, 0, :N_EXPERTS].astype(jnp.int32)
    steps_a = part_a[4].shape[0]
    n_tiles_max = (cnt.shape[0] * MOE_NCH + N_EXPERTS * (TILE_CH - 1)) // TILE_CH + 1
    dst, n_chunks, tail_start, tail_len, tile0, tile1, n_tiles = _plan(cnt, n_tiles_max)
    xs, lp = _dispatch(dst, n_chunks, tail_start, tail_len, n_tiles, part_a[1:4], part_b[1:4], n_tiles_max)
    y_sorted = _moe_gemm(tile0, tile1, n_tiles, xs, w_gate, w_up, w_down)
    ya = _moe_combine(0, dst, n_chunks, lp, part_a[0], mod, mod_row_a, y_sorted, g_post2)
    yb = _moe_combine(steps_a, dst, n_chunks, lp, part_b[0], mod, mod_row_b, y_sorted, g_post2)
    return ya, yb


def _rotate_half_cols(w):
    n = w.shape[-1]
    w4 = w.reshape(w.shape[:-1] + (n // 32, 2, 16))
    return jnp.stack([-w4[..., 1, :], w4[..., 0, :]], axis=-2).reshape(w.shape)


def _pad_lanes(w):
    return jnp.concatenate([w, jnp.zeros(w.shape[:-1] + (LANE - w.shape[-1],), w.dtype)], axis=-1)


def _rope_tables(t):
    rows = t // GRID_W
    n_freq = D_ROPE // 4
    freqs = ROPE_BASE ** (-jnp.arange(n_freq, dtype=F32) / n_freq)
    ang_r = jnp.arange(rows, dtype=F32)[:, None] * freqs
    ang_c = jnp.arange(GRID_W, dtype=F32)[:, None] * freqs
    per_row = lambda a: jnp.repeat(a, GRID_W, axis=0)
    per_col = lambda a: jnp.tile(a, (rows, 1))

    def table(fn):
        r, c = per_row(fn(ang_r)), per_col(fn(ang_c))
        return _pad_lanes(jnp.concatenate([r, r, c, c], axis=-1))
    return table(jnp.cos), table(jnp.sin)


def kernel(x_prompt, x_sample, cache_ckv, cache_krope, c, c_ctx, w_ada, b_ada, g_pre1, g_post1, g_pre2, g_post2, w_in, w_dw, b_dw, conv_ln_g, conv_ln_b, q_norm_g, kv_norm_g, w_uq, w_ukv, w_out, w_rg, b_rg, w_re, b_re, w_gate, w_up, w_down):
    nb, seq, d = x_prompt.shape
    db, dseq, _ = x_sample.shape
    l = 0

    cvec = jnp.concatenate([c_ctx[None, :], c, jnp.zeros((8 - 1 - db, d), F32)], axis=0)
    mod = _ada(cvec, w_ada[l], b_ada[l]).reshape(8, 6, d)

    o_kr = 2 * C_CONV + Q_LORA + KV_LORA
    w_in_l = w_in[l]
    w_kr = w_in_l[:, o_kr:]
    w_in_ctx = jnp.concatenate([w_in_l[:, :o_kr], _pad_lanes(w_kr)], axis=-1).astype(BF16)
    w_in_lat = jnp.concatenate([w_in_l[:, :o_kr], _pad_lanes(w_kr), _pad_lanes(_rotate_half_cols(w_kr))],
                               axis=-1).astype(BF16)
    wuq = w_uq[l].reshape(Q_LORA, N_HEADS, D_NOPE + D_ROPE)
    wq_rope = wuq[:, :, D_NOPE:]
    wq = jnp.concatenate([wuq[:, :, :D_NOPE], _pad_lanes(wq_rope)], axis=-1)
    wq = wq.reshape(Q_LORA, N_HEADS * HEAD_W).astype(BF16)
    wqs = _pad_lanes(_rotate_half_cols(wq_rope)).reshape(Q_LORA, N_HEADS * LANE).astype(BF16)
    wukv4 = w_ukv[l].reshape(KV_LORA, N_HEADS, D_NOPE + D_V)
    wukv = jnp.concatenate([wukv4[:, :, :D_NOPE].reshape(KV_LORA, N_HEADS * D_NOPE),
                            wukv4[:, :, D_NOPE:].reshape(KV_LORA, N_HEADS * D_V)], axis=-1).astype(BF16)
    w_out_b = w_out[l].astype(BF16)
    w_r = jnp.concatenate([w_re[l], w_rg[l], jnp.zeros((d, LANE - N_EXPERTS - N_GROUPS), F32)], axis=-1)
    wr_hi = w_r.astype(BF16)
    wr_lo = jnp.concatenate([wr_hi[:, :LANE // 2], (w_r - wr_hi.astype(F32)).astype(BF16)[:, :LANE // 2]], axis=-1)
    b_r = jnp.concatenate([b_re[l], b_rg[l], jnp.zeros((LANE - N_EXPERTS - N_GROUPS,), F32)])[None, :]
    cos, sin = _rope_tables(dseq)
    row = lambda v: v[l][None, :]

    tm_c = 512
    xp_flat = x_prompt.reshape(1, nb * seq, d)
    hglu, q, k, v, ckv, kr = _mix_in(xp_flat, mod, 0, row(g_pre1), w_in_ctx, row(q_norm_g), row(kv_norm_g),
                                     wq, wqs, wukv, cos[:tm_c], sin[:tm_c], False, tm_c)
    per_seq = lambda a: a.reshape(nb, seq, a.shape[-1])
    conv_out = _conv(per_seq(hglu), w_dw[l], row(b_dw), row(conv_ln_g), row(conv_ln_b))
    att = _attn(per_seq(q), per_seq(k), per_seq(v))
    flat = lambda a: a.reshape(1, nb * seq, a.shape[-1])
    post_c = _post(xp_flat, flat(conv_out), flat(att), mod, 0, w_out_b, row(g_post1), row(g_pre2),
                   wr_hi, wr_lo, b_r)
    state_ckv = ckv.reshape(nb, 1, seq, KV_LORA)
    state_krope = kr.reshape(nb, 1, seq, D_ROPE)

    tm_s = 512
    kc, vc = _cache_kv(cache_ckv[:, l], cache_krope[:, l], wukv)
    hglu, q, k, v, _, _ = _mix_in(x_sample, mod, 1, row(g_pre1), w_in_lat, row(q_norm_g), row(kv_norm_g),
                                  wq, wqs, wukv, cos, sin, True, tm_s)
    conv_out = _conv(hglu, w_dw[l], row(b_dw), row(conv_ln_g), row(conv_ln_b))
    att = _attn(q, k, v, kc, vc)
    post_s = _post(x_sample, conv_out, att, mod, 1, w_out_b, row(g_post1), row(g_pre2), wr_hi, wr_lo, b_r)

    tokens = lambda parts: tuple(a.reshape((-1,) + a.shape[2:]) for a in parts)
    steps_per_req = dseq // MOE_TD
    yp, ys = _moe(tokens(post_c), tokens(post_s), mod, lambda i: 0, lambda i: 1 + i // steps_per_req,
                  w_gate[l], w_up[l], w_down[l], row(g_post2))

    return (yp.reshape(nb, seq, d), ys.reshape(db, dseq, d), state_ckv, state_krope)
```

```python
import functools

import jax
import jax.numpy as jnp
import numpy as np
from jax import lax
from jax.experimental import pallas as pl
from jax.experimental.pallas import tpu as pltpu

D_MODEL = 1024
GRID_W = 64
C_CONV = 512
CONV_K = 31
N_HEADS = 4
D_NOPE = 128
D_ROPE = 64
D_V = 128
Q_LORA = 384
KV_LORA = 256
N_GROUPS = 4
E_PER_GROUP = 8
N_EXPERTS = 32
D_EXPERT = 256
ROPE_BASE = 10000.0
EPS = 1e-6
ATT_SCALE = (D_NOPE + D_ROPE) ** -0.5
LOG2E = 1.4426950408889634

LANE = 128
SUBLANE = 8
HEAD_W = 2 * LANE
CONV_HALO = 16
MOE_TD = 512
CHUNK = 2 * SUBLANE
MOE_NCH = (2 * MOE_TD + N_EXPERTS * (CHUNK - 1)) // CHUNK
MOE_L = MOE_NCH * CHUNK
MOE_TILE = 256
TILE_CH = MOE_TILE // CHUNK
GEMM_RING = 4
XS_W = D_MODEL + LANE
VMEM_LIMIT = 56 * 1024 * 1024

BF16 = jnp.bfloat16
F32 = jnp.float32


def _cparams(*sem):
    return pltpu.CompilerParams(dimension_semantics=sem, vmem_limit_bytes=VMEM_LIMIT)


def _rms(x, g):
    return x * lax.rsqrt(jnp.mean(x * x, axis=-1, keepdims=True) + EPS) * g


def _sigmoid(x):
    return 1.0 / (1.0 + jnp.exp(-x))


def _dot(a, b):
    return jnp.dot(a, b, preferred_element_type=F32)


def _chunk_copy(src, src_chunk, dst, dst_chunk, sem):
    rows = lambda c: pl.ds(c * CHUNK if isinstance(c, int) else pl.multiple_of(c * CHUNK, CHUNK), CHUNK)
    return pltpu.make_async_copy(src.at[rows(src_chunk), :], dst.at[rows(dst_chunk), :], sem)


def _ada_kernel(c_ref, w_ref, b_ref, o_ref):
    c = c_ref[...]
    s = (c * _sigmoid(c)).astype(BF16)
    o_ref[...] = _dot(s, w_ref[...].astype(BF16)) + b_ref[...]


def _ada(cvec, w_ada, b_ada):
    n = w_ada.shape[1]
    tn = 1536
    return pl.pallas_call(
        _ada_kernel,
        grid=(n // tn,),
        in_specs=[pl.BlockSpec((8, D_MODEL), lambda j: (0, 0)),
                  pl.BlockSpec((D_MODEL, tn), lambda j: (0, j)),
                  pl.BlockSpec((1, tn), lambda j: (0, j))],
        out_specs=pl.BlockSpec((8, tn), lambda j: (0, j)),
        out_shape=jax.ShapeDtypeStruct((8, n), F32),
        compiler_params=_cparams("arbitrary"),
        name="ada",
    )(cvec, w_ada, b_ada.reshape(1, n))


CONV_TB = 256
CONV_WIN = CONV_TB + 2 * CONV_HALO
CONV_TT = 32


def _conv_window(win_ref, sh_ref, w_ref, b_ref, g_ref, bb_ref, o_ref, row0):
    first = CONV_HALO - CONV_K // 2
    rows = CONV_WIN - SUBLANE
    win = win_ref[...]
    for s in range(SUBLANE):
        sh_ref[s, 0:rows, :] = win[s:s + rows, :]
    for c in range(CONV_TB // CONV_TT):
        acc = jnp.zeros((CONV_TT, C_CONV), F32)
        for k in range(CONV_K):
            off = first + k
            r0 = c * CONV_TT + off // SUBLANE * SUBLANE
            acc = acc + sh_ref[off % SUBLANE, r0:r0 + CONV_TT, :] * w_ref[k:k + 1, :]
        y = acc + b_ref[...]
        mu = jnp.mean(y, axis=-1, keepdims=True)
        yc = y - mu
        var = jnp.mean(yc * yc, axis=-1, keepdims=True)
        z = yc * lax.rsqrt(var + EPS) * g_ref[...] + bb_ref[...]
        o_ref[0, row0 + c * CONV_TT:row0 + (c + 1) * CONV_TT, :] = (z * _sigmoid(z)).astype(BF16)


def _mix_in_kernel(latent, x_ref, xp_ref, xn_ref, mod_ref, gpre_ref, win_ref, qg_ref, kvg_ref, wq_ref, wqs_ref,
                   wukv_ref, cos_ref, sin_ref, wdw_ref, bdw_ref, lng_ref, lnb_ref,
                   conv_ref, q_ref, k_ref, v_ref, ckv_ref, kr_ref, cwin, csh):
    rope = latent
    x = x_ref[0]
    tm = x.shape[0]
    sh1 = mod_ref[0, 0:1, :]
    sc1 = mod_ref[0, 1:2, :]
    modulate = lambda v: _rms(v, gpre_ref[...]) * (1.0 + sc1) + sh1
    glu = lambda uc: uc[:, :C_CONV] * _sigmoid(uc[:, C_CONV:2 * C_CONV])
    u = _dot(modulate(x).astype(BF16), win_ref[...])
    hglu = glu(u)

    zeros = jnp.zeros((CONV_HALO, C_CONV), F32)
    n_win = tm // CONV_TB
    if latent:
        i = pl.program_id(1)
        xh = jnp.concatenate([xp_ref[0], xn_ref[0]], axis=0)
        hh = glu(_dot(modulate(xh).astype(BF16), win_ref[:, 0:2 * C_CONV]))
        before = jnp.where(i > 0, hh[0:CONV_HALO], zeros)
        after = jnp.where(i < pl.num_programs(1) - 1, hh[CONV_HALO:2 * CONV_HALO], zeros)
    for j in range(n_win):
        if latent:
            lo, hi, d0 = j * CONV_TB - CONV_HALO, j * CONV_TB + CONV_TB + CONV_HALO, 0
            if lo < 0:
                cwin[j, 0:CONV_HALO, :] = before
                lo, d0 = 0, CONV_HALO
            if hi > tm:
                cwin[j, CONV_WIN - CONV_HALO:CONV_WIN, :] = after
                hi = tm
            cwin[j, d0:d0 + hi - lo, :] = hglu[lo:hi, :]
        else:
            cwin[j, 0:CONV_HALO, :] = zeros
            cwin[j, CONV_HALO:CONV_HALO + CONV_TB, :] = hglu[j * CONV_TB:(j + 1) * CONV_TB, :]
            cwin[j, CONV_HALO + CONV_TB:CONV_WIN, :] = zeros
    for j in range(n_win):
        _conv_window(cwin.at[j], csh, wdw_ref, bdw_ref, lng_ref, lnb_ref, conv_ref, j * CONV_TB)

    o_q = 2 * C_CONV
    o_kv = o_q + Q_LORA
    o_kr = o_kv + KV_LORA
    qn = _rms(u[:, o_q:o_kv], qg_ref[...]).astype(BF16)
    qf = _dot(qn, wq_ref[...])
    ckv = _rms(u[:, o_kv:o_kr], kvg_ref[...])
    ckv_ref[0] = ckv
    kvd = _dot(ckv.astype(BF16), wukv_ref[...])
    kr = u[:, o_kr:o_kr + LANE]
    kr_ref[0] = kr[:, :D_ROPE]
    if rope:
        cos = cos_ref[...]
        sin = sin_ref[...]
        qs = _dot(qn, wqs_ref[...])
        kr = kr * cos + u[:, o_kr + LANE:o_kr + 2 * LANE] * sin
    q_parts = []
    k_parts = []
    for hd in range(N_HEADS):
        q_parts.append(qf[:, hd * HEAD_W:hd * HEAD_W + LANE])
        qr = qf[:, hd * HEAD_W + LANE:(hd + 1) * HEAD_W]
        if rope:
            qr = qr * cos + qs[:, hd * LANE:(hd + 1) * LANE] * sin
        q_parts.append(qr)
        k_parts.append(kvd[:, hd * D_NOPE:(hd + 1) * D_NOPE])
        k_parts.append(kr)
    q_ref[0] = jnp.concatenate(q_parts, axis=-1).astype(BF16)
    k_ref[0] = jnp.concatenate(k_parts, axis=-1).astype(BF16)
    v_ref[0] = kvd[:, N_HEADS * D_NOPE:].astype(BF16)


def _mix_in(x, mod, mod_row0, g_pre1, w_in_ext, q_norm_g, kv_norm_g, wq, wqs, wukv, cos, sin, conv_w, latent, tm):
    b, t, _ = x.shape
    ncol = w_in_ext.shape[1]
    const = lambda bi, i: (0, 0)
    tok = lambda bi, i: (bi, i, 0)
    table = (lambda bi, i: (i, 0)) if latent else const
    halo_blocks = tm // CONV_HALO
    prev_rows = lambda bi, i: (bi, jnp.maximum(i * halo_blocks - 1, 0), 0)
    next_rows = lambda bi, i: (bi, jnp.minimum((i + 1) * halo_blocks, t // CONV_HALO - 1), 0)
    outs = [(C_CONV, BF16), (N_HEADS * HEAD_W, BF16), (N_HEADS * HEAD_W, BF16), (N_HEADS * D_V, BF16),
            (KV_LORA, F32), (D_ROPE, F32)]
    return pl.pallas_call(
        functools.partial(_mix_in_kernel, latent),
        grid=(b, t // tm),
        in_specs=[pl.BlockSpec((1, tm, D_MODEL), tok),
                  pl.BlockSpec((1, CONV_HALO, D_MODEL), prev_rows),
                  pl.BlockSpec((1, CONV_HALO, D_MODEL), next_rows),
                  pl.BlockSpec((1, 6, D_MODEL), lambda bi, i: (mod_row0 + bi, 0, 0)),
                  pl.BlockSpec((1, D_MODEL), const),
                  pl.BlockSpec((D_MODEL, ncol), const),
                  pl.BlockSpec((1, Q_LORA), const),
                  pl.BlockSpec((1, KV_LORA), const),
                  pl.BlockSpec((Q_LORA, N_HEADS * HEAD_W), const),
                  pl.BlockSpec((Q_LORA, N_HEADS * LANE), const),
                  pl.BlockSpec((KV_LORA, N_HEADS * (D_NOPE + D_V)), const),
                  pl.BlockSpec((tm, LANE), table),
                  pl.BlockSpec((tm, LANE), table),
                  pl.BlockSpec((CONV_K, C_CONV), const),
                  pl.BlockSpec((1, C_CONV), const),
                  pl.BlockSpec((1, C_CONV), const),
                  pl.BlockSpec((1, C_CONV), const)],
        out_specs=[pl.BlockSpec((1, tm, w), tok) for w, _ in outs],
        out_shape=[jax.ShapeDtypeStruct((b, t, w), dt) for w, dt in outs],
        scratch_shapes=[pltpu.VMEM((tm // CONV_TB, CONV_WIN, C_CONV), F32),
                        pltpu.VMEM((SUBLANE, CONV_WIN, C_CONV), F32)],
        compiler_params=_cparams("arbitrary", "arbitrary"),
        name="mix_in_latent" if latent else "mix_in",
    )(x, x, x, mod, g_pre1, w_in_ext, q_norm_g, kv_norm_g, wq, wqs, wukv, cos, sin, *conv_w)


def _qk(q, k):
    return lax.dot_general(q, k, (((1,), (1,)), ((), ())), preferred_element_type=F32)


def _attn_kernel(cached, q_ref, k_ref, v_ref, *rest):
    if cached:
        kc_ref, vc_ref, o_ref = rest
    else:
        (o_ref,) = rest
    outs = []
    for hd in range(N_HEADS):
        ks = slice(hd * HEAD_W, (hd + 1) * HEAD_W)
        vs = slice(hd * D_V, (hd + 1) * D_V)
        qh = q_ref[0, :, ks]
        s = _qk(qh, k_ref[0, :, ks]) * (ATT_SCALE * LOG2E)
        m = jnp.max(s, axis=-1, keepdims=True)
        if cached:
            sc = _qk(qh, kc_ref[0, :, ks]) * (ATT_SCALE * LOG2E)
            m = jnp.maximum(m, jnp.max(sc, axis=-1, keepdims=True))
        p = jnp.exp2(s - m)
        l = jnp.sum(p, axis=-1, keepdims=True)
        o = _dot(p.astype(BF16), v_ref[0, :, vs])
        if cached:
            pc = jnp.exp2(sc - m)
            l = l + jnp.sum(pc, axis=-1, keepdims=True)
            o = o + _dot(pc.astype(BF16), vc_ref[0, :, vs])
        outs.append(o / l)
    o_ref[0] = jnp.concatenate(outs, axis=-1).astype(BF16)


def _attn(q, k, v, kc=None, vc=None, tq=256):
    b, t, _ = q.shape
    s = k.shape[1]
    cached = kc is not None
    whole = lambda bi, i: (bi, 0, 0)
    in_specs = [pl.BlockSpec((1, tq, N_HEADS * HEAD_W), lambda bi, i: (bi, i, 0)),
                pl.BlockSpec((1, s, N_HEADS * HEAD_W), whole),
                pl.BlockSpec((1, s, N_HEADS * D_V), whole)]
    args = [q, k, v]
    if cached:
        sc = kc.shape[1]
        in_specs += [pl.BlockSpec((1, sc, N_HEADS * HEAD_W), whole),
                     pl.BlockSpec((1, sc, N_HEADS * D_V), whole)]
        args += [kc, vc]
    return pl.pallas_call(
        functools.partial(_attn_kernel, cached),
        grid=(b, t // tq),
        in_specs=in_specs,
        out_specs=pl.BlockSpec((1, tq, N_HEADS * D_V), lambda bi, i: (bi, i, 0)),
        out_shape=jax.ShapeDtypeStruct((b, t, N_HEADS * D_V), BF16),
        compiler_params=_cparams("arbitrary", "arbitrary"),
        name="attn_cached" if cached else "attn",
    )(*args)


def _cache_kv_kernel(ckv_ref, kr_ref, wukv_ref, k_ref, v_ref):
    kvd = _dot(ckv_ref[0].astype(BF16), wukv_ref[...])
    kr = kr_ref[0]
    kr = jnp.concatenate([kr, jnp.zeros_like(kr)], axis=-1)
    parts = []
    for hd in range(N_HEADS):
        parts.append(kvd[:, hd * D_NOPE:(hd + 1) * D_NOPE])
        parts.append(kr)
    k_ref[0] = jnp.concatenate(parts, axis=-1).astype(BF16)
    v_ref[0] = kvd[:, N_HEADS * D_NOPE:].astype(BF16)


def _cache_kv(ckv, krope, wukv):
    b, s, _ = ckv.shape
    tok = lambda bi: (bi, 0, 0)
    return pl.pallas_call(
        _cache_kv_kernel,
        grid=(b,),
        in_specs=[pl.BlockSpec((1, s, KV_LORA), tok),
                  pl.BlockSpec((1, s, D_ROPE), tok),
                  pl.BlockSpec((KV_LORA, N_HEADS * (D_NOPE + D_V)), lambda bi: (0, 0))],
        out_specs=[pl.BlockSpec((1, s, N_HEADS * HEAD_W), tok),
                   pl.BlockSpec((1, s, N_HEADS * D_V), tok)],
        out_shape=[jax.ShapeDtypeStruct((b, s, N_HEADS * HEAD_W), BF16),
                   jax.ShapeDtypeStruct((b, s, N_HEADS * D_V), BF16)],
        compiler_params=_cparams("arbitrary"),
        name="cache_kv",
    )(ckv, krope, wukv)


def _post_kernel(x_ref, conv_ref, att_ref, mod_ref, wo_ref, gpost_ref, gpre2_ref, wr_hi_ref, wr_cat_ref,
                 br_ref, x1_ref, h2_ref, ri_ref, rw_ref, cnt_ref):
    out = _dot(conv_ref[0], wo_ref[:C_CONV, :]) + _dot(att_ref[0], wo_ref[C_CONV:, :])
    gt1 = mod_ref[0, 2:3, :]
    sh2 = mod_ref[0, 3:4, :]
    sc2 = mod_ref[0, 4:5, :]
    x1 = x_ref[0] + gt1 * _rms(out, gpost_ref[...])
    x1_ref[0] = x1
    h2 = _rms(x1, gpre2_ref[...]) * (1.0 + sc2) + sh2
    h_hi = h2.astype(BF16)
    h2_ref[0] = h_hi
    h_lo = (h2 - h_hi.astype(F32)).astype(BF16)
    hi_terms = _dot(h_hi, wr_cat_ref[...])
    logits = hi_terms + pltpu.roll(hi_terms, LANE // 2, axis=1) + _dot(h_lo, wr_hi_ref[...]) + br_ref[...]

    lane = lax.broadcasted_iota(jnp.int32, logits.shape, 1)
    neg = jnp.float32(-jnp.inf)
    big = jnp.int32(LANE)
    is_g = (lane >= N_EXPERTS) & (lane < N_EXPERTS + N_GROUPS)
    lg = jnp.where(is_g, logits, neg)
    gmax = jnp.max(lg, axis=-1, keepdims=True)
    gidx = jnp.min(jnp.where(lg == gmax, lane, big), axis=-1, keepdims=True) - N_EXPERTS
    g_top = 1.0 / jnp.sum(jnp.exp(lg - gmax), axis=-1, keepdims=True)

    in_grp = (lane >= gidx * E_PER_GROUP) & (lane < (gidx + 1) * E_PER_GROUP)
    le = jnp.where(in_grp, logits, neg)
    m1 = jnp.max(le, axis=-1, keepdims=True)
    i1 = jnp.min(jnp.where(le == m1, lane, big), axis=-1, keepdims=True)
    le2 = jnp.where(lane == i1, neg, le)
    m2 = jnp.max(le2, axis=-1, keepdims=True)
    i2 = jnp.min(jnp.where(le2 == m2, lane, big), axis=-1, keepdims=True)
    r = jnp.exp(m2 - m1)
    w1 = g_top / (1.0 + r)
    w2 = g_top * r / (1.0 + r)
    ri_ref[0] = jnp.where(lane == 0, i1, jnp.where(lane == 1, i2, 0))
    rw_ref[0] = jnp.where(lane == 0, w1, jnp.where(lane == 1, w2, 0.0))
    pairs = jnp.sum(jnp.where((lane == i1) | (lane == i2), 1.0, 0.0), axis=0, keepdims=True)
    cnt_ref[0, 0] = jnp.broadcast_to(pairs, (SUBLANE, LANE))


def _post(x, conv_out, att, mod, mod_row0, w_out, g_post1, g_pre2, wr_hi, wr_lo, br):
    b, t, _ = x.shape
    tm = MOE_TD
    const = lambda bi, i: (0, 0)
    tok = lambda bi, i: (bi, i, 0)
    return pl.pallas_call(
        _post_kernel,
        grid=(b, t // tm),
        in_specs=[pl.BlockSpec((1, tm, D_MODEL), tok),
                  pl.BlockSpec((1, tm, C_CONV), tok),
                  pl.BlockSpec((1, tm, N_HEADS * D_V), tok),
                  pl.BlockSpec((1, 6, D_MODEL), lambda bi, i: (mod_row0 + bi, 0, 0)),
                  pl.BlockSpec((D_MODEL, D_MODEL), const),
                  pl.BlockSpec((1, D_MODEL), const),
                  pl.BlockSpec((1, D_MODEL), const),
                  pl.BlockSpec((D_MODEL, LANE), const),
                  pl.BlockSpec((D_MODEL, LANE), const),
                  pl.BlockSpec((1, LANE), const)],
        out_specs=[pl.BlockSpec((1, tm, D_MODEL), tok),
                   pl.BlockSpec((1, tm, D_MODEL), tok),
                   pl.BlockSpec((1, tm, LANE), tok),
                   pl.BlockSpec((1, tm, LANE), tok),
                   pl.BlockSpec((1, 1, SUBLANE, LANE), lambda bi, i: (bi, i, 0, 0))],
        out_shape=[jax.ShapeDtypeStruct((b, t, D_MODEL), F32),
                   jax.ShapeDtypeStruct((b, t, D_MODEL), BF16),
                   jax.ShapeDtypeStruct((b, t, LANE), jnp.int32),
                   jax.ShapeDtypeStruct((b, t, LANE), F32),
                   jax.ShapeDtypeStruct((b, t // tm, SUBLANE, LANE), F32)],
        compiler_params=_cparams("arbitrary", "arbitrary"),
        name="post",
    )(x, conv_out, att, mod, w_out, g_post1, g_pre2, wr_hi, wr_lo, br)


def _one_hots(ri):
    lane = lax.broadcasted_iota(jnp.int32, ri.shape, 1)
    oh1 = lane == ri[:, 0:1]
    oh2 = lane == ri[:, 1:2]
    return oh1, oh2, jnp.where(oh1 | oh2, 1.0, 0.0)


def _local_positions(oh1, oh2, oh):
    td = oh.shape[0]
    cnt = jnp.sum(oh, axis=0, keepdims=True)
    nch = jnp.floor((cnt + (CHUNK - 1)) * (1.0 / CHUNK))
    a = lax.broadcasted_iota(jnp.int32, (LANE, LANE), 0)
    b = lax.broadcasted_iota(jnp.int32, (LANE, LANE), 1)
    lower_experts = jnp.where(a < b, 1.0, 0.0).astype(BF16)
    run_start = _dot(jnp.broadcast_to(nch, (SUBLANE, LANE)).astype(BF16), lower_experts)[0:1, :] * CHUNK
    rr = lax.broadcasted_iota(jnp.int32, (td, td), 0)
    cc = lax.broadcasted_iota(jnp.int32, (td, td), 1)
    earlier = jnp.where(cc < rr, 1.0, 0.0).astype(BF16)
    pos = _dot(earlier, oh.astype(BF16)) + run_start
    lp1 = jnp.sum(jnp.where(oh1, pos, 0.0), axis=-1, keepdims=True)
    lp2 = jnp.sum(jnp.where(oh2, pos, 0.0), axis=-1, keepdims=True)
    return lp1, lp2


def _plan(cnt, n_tiles_max):
    k = (cnt + (CHUNK - 1)) // CHUNK
    run_end = jnp.cumsum(k, axis=1)
    run_start = run_end - k
    n_chunks = run_end[:, -1]
    total = jnp.sum(k, axis=0)
    padded = (total + (TILE_CH - 1)) // TILE_CH * TILE_CH
    seg_end = jnp.cumsum(padded)
    seg_start = seg_end - padded
    base = seg_start[None, :] + jnp.cumsum(k, axis=0) - k
    c = jnp.arange(MOE_NCH, dtype=jnp.int32)[None, :, None]
    in_run = (run_start[:, None, :] <= c) & (c < run_end[:, None, :])
    dst = jnp.sum(jnp.where(in_run, base[:, None, :] + c - run_start[:, None, :], 0), axis=2)
    n_tiles = seg_end[-1] // TILE_CH
    tail_start = seg_start + total
    tail_len = padded - total
    i32 = lambda t: t.astype(jnp.int32)
    return (i32(dst), i32(n_chunks), i32(tail_start), i32(tail_len), i32(seg_start // TILE_CH),
            i32(seg_end // TILE_CH), i32(n_tiles.reshape(1)))


def _dispatch_kernel(steps_a, dst_ref, nch_ref, ts_ref, tl_ref, nt_ref, h2a_ref, ria_ref, rwa_ref, h2b_ref, rib_ref,
                     rwb_ref, xs_hbm, lp_ref, xbuf, zx, sem, zsem):
    i = pl.program_id(0)
    n = pl.num_programs(0)
    slot = i % 2
    td = MOE_TD
    n_tiles_max = xs_hbm.shape[0] // MOE_TILE

    def zero_tile(m):
        rows = pl.ds(pl.multiple_of(m * MOE_TILE, MOE_TILE), MOE_TILE)
        return pltpu.make_async_copy(zx, xs_hbm.at[rows, :], zsem.at[1])

    @pl.when(i == 0)
    def _():
        zx[...] = jnp.zeros_like(zx)
        for e in range(N_EXPERTS):
            def fill(m, carry, e=e):
                _chunk_copy(zx, 0, xs_hbm, ts_ref[e] + m, zsem.at[0]).start()
                return carry
            lax.fori_loop(0, tl_ref[e], fill, 0)

        def fill_tile(m, carry):
            zero_tile(m).start()
            return carry
        lax.fori_loop(nt_ref[0], n_tiles_max, fill_tile, 0)

    from_a = i < steps_a
    ri = jnp.where(from_a, ria_ref[...], rib_ref[...])
    rw = jnp.where(from_a, rwa_ref[...], rwb_ref[...])
    h2 = jnp.where(from_a, h2a_ref[...], h2b_ref[...])
    oh1, oh2, oh = _one_hots(ri)
    lp1, lp2 = _local_positions(oh1, oh2, oh)
    lane = lax.broadcasted_iota(jnp.int32, ri.shape, 1)
    lp_ref[...] = jnp.where(lane == 0, lp1, jnp.where(lane == 1, lp2, 0.0)).astype(jnp.int32)

    rr = lax.broadcasted_iota(jnp.int32, (td, td), 0)
    cc = lax.broadcasted_iota(jnp.int32, (td, td), 1)
    to_lanes = lambda col: jnp.sum(jnp.where(rr == cc, col, 0.0), axis=0, keepdims=True)
    row = lax.broadcasted_iota(jnp.int32, (MOE_L, td), 0).astype(F32)
    p1 = row == to_lanes(lp1)
    p2 = row == to_lanes(lp2)
    perm = jnp.where(p1 | p2, 1.0, 0.0).astype(BF16)
    xbuf[slot, :, 0:D_MODEL] = _dot(perm, h2).astype(BF16)
    w = jnp.sum(jnp.where(p1, to_lanes(rw[:, 0:1]), 0.0) + jnp.where(p2, to_lanes(rw[:, 1:2]), 0.0),
                axis=-1, keepdims=True)
    w0 = w.astype(BF16).astype(F32)
    w1 = (w - w0).astype(BF16).astype(F32)
    w2 = (w - w0 - w1).astype(BF16).astype(F32)
    wl = lax.broadcasted_iota(jnp.int32, (MOE_L, LANE), 1)
    terms = jnp.where(wl == 0, w0, jnp.where(wl == 1, w1, jnp.where(wl == 2, w2, 0.0)))
    xbuf[slot, :, D_MODEL:XS_W] = terms.astype(BF16)

    def wait_chunks(count, s):
        def body(c, carry):
            _chunk_copy(xbuf.at[s], 0, xs_hbm, 0, sem.at[s]).wait()
            return carry
        lax.fori_loop(0, count, body, 0)

    @pl.when(i > 0)
    def _():
        wait_chunks(nch_ref[i - 1], 1 - slot)

    def send(c, carry):
        _chunk_copy(xbuf.at[slot], c, xs_hbm, dst_ref[i, c], sem.at[slot]).start()
        return carry
    lax.fori_loop(0, nch_ref[i], send, 0)

    @pl.when(i == n - 1)
    def _():
        wait_chunks(nch_ref[i], slot)
        for e in range(N_EXPERTS):
            def drain(m, carry):
                _chunk_copy(zx, 0, xs_hbm, 0, zsem.at[0]).wait()
                return carry
            lax.fori_loop(0, tl_ref[e], drain, 0)

        def drain_tile(m, carry):
            zero_tile(m).wait()
            return carry
        lax.fori_loop(nt_ref[0], n_tiles_max, drain_tile, 0)


def _dispatch(dst, n_chunks, tail_start, tail_len, n_tiles, part_a, part_b, n_tiles_max):
    steps_a = part_a[0].shape[0] // MOE_TD
    steps_b = part_b[0].shape[0] // MOE_TD
    in_a = lambda i, *_: (jnp.minimum(i, steps_a - 1), 0)
    in_b = lambda i, *_: (jnp.maximum(i - steps_a, 0), 0)
    specs = lambda f: [pl.BlockSpec((MOE_TD, D_MODEL), f), pl.BlockSpec((MOE_TD, LANE), f),
                       pl.BlockSpec((MOE_TD, LANE), f)]
    rows = n_tiles_max * MOE_TILE
    return pl.pallas_call(
        functools.partial(_dispatch_kernel, steps_a),
        grid_spec=pltpu.PrefetchScalarGridSpec(
            num_scalar_prefetch=5,
            grid=(steps_a + steps_b,),
            in_specs=specs(in_a) + specs(in_b),
            out_specs=[pl.BlockSpec(memory_space=pl.ANY),
                       pl.BlockSpec((MOE_TD, LANE), lambda i, *_: (i, 0))],
            scratch_shapes=[pltpu.VMEM((2, MOE_L, XS_W), BF16), pltpu.VMEM((MOE_TILE, XS_W), BF16),
                            pltpu.SemaphoreType.DMA((2,)), pltpu.SemaphoreType.DMA((2,))]),
        out_shape=[jax.ShapeDtypeStruct((rows, XS_W), BF16),
                   jax.ShapeDtypeStruct(((steps_a + steps_b) * MOE_TD, LANE), jnp.int32)],
        compiler_params=_cparams("arbitrary"),
        name="moe_dispatch",
    )(dst, n_chunks, tail_start, tail_len, n_tiles, *part_a, *part_b)


def _moe_gemm_kernel(t0_ref, t1_ref, nt_ref, wg_ref, wu_ref, wd_ref, xs_hbm, y_hbm,
                     wg_b, wu_b, wd_b, xbuf, ybuf, zbuf, isem, osem, zsem):
    e = pl.program_id(0)
    nt = nt_ref[0]
    n_tiles_max = y_hbm.shape[0] // MOE_TILE
    tile_rows = lambda t: pl.ds(pl.multiple_of(t * MOE_TILE, MOE_TILE), MOE_TILE)

    def load(t, s):
        return (pltpu.make_async_copy(xs_hbm.at[tile_rows(t), :], xbuf.at[s], isem.at[s]),)

    def store(t, s):
        return pltpu.make_async_copy(ybuf.at[s], y_hbm.at[tile_rows(t), :], osem.at[s])

    def zero_tile(t):
        return pltpu.make_async_copy(zbuf, y_hbm.at[tile_rows(t), :], zsem.at[0])

    @pl.when(e == 0)
    def _():
        for t in range(GEMM_RING - 1):
            @pl.when(t < nt)
            def _(t=t):
                for cp in load(t, t):
                    cp.start()
        zbuf[...] = jnp.zeros_like(zbuf)

        def fill(t, carry):
            zero_tile(t).start()
            return carry
        lax.fori_loop(nt, n_tiles_max, fill, 0)

    wg_b[...] = wg_ref[0].astype(BF16)
    wu_b[...] = wu_ref[0].astype(BF16)
    wd_b[...] = wd_ref[0].astype(BF16)

    def tile(t, carry):
        r = t % GEMM_RING
        s = t % 2
        for cp in load(t, r):
            cp.wait()

        @pl.when(t + (GEMM_RING - 1) < nt)
        def _():
            for cp in load(t + (GEMM_RING - 1), (t + (GEMM_RING - 1)) % GEMM_RING):
                cp.start()

        x = xbuf[r, :, 0:D_MODEL]
        wt = xbuf[r, :, D_MODEL:XS_W].astype(F32)
        w = wt[:, 0:1] + wt[:, 1:2] + wt[:, 2:3]
        a = _dot(x, wg_b[...])
        u = _dot(x, wu_b[...])
        he = (a * _sigmoid(a)) * u
        y = _dot(he.astype(BF16), wd_b[...]) * w

        @pl.when(t >= 2)
        def _():
            store(t - 2, s).wait()
        ybuf[s] = y.astype(BF16)
        store(t, s).start()
        return carry

    lax.fori_loop(t0_ref[e], t1_ref[e], tile, 0)

    @pl.when(e == pl.num_programs(0) - 1)
    def _():
        @pl.when(nt >= 1)
        def _():
            store(nt - 1, (nt - 1) % 2).wait()

        @pl.when(nt >= 2)
        def _():
            store(nt - 2, nt % 2).wait()

        def drain(t, carry):
            zero_tile(t).wait()
            return carry
        lax.fori_loop(nt, n_tiles_max, drain, 0)


def _moe_gemm(tile0, tile1, n_tiles, xs, w_gate, w_up, w_down):
    wmap = lambda e, *_: (e, 0, 0)
    return pl.pallas_call(
        _moe_gemm_kernel,
        grid_spec=pltpu.PrefetchScalarGridSpec(
            num_scalar_prefetch=3,
            grid=(N_EXPERTS,),
            in_specs=[pl.BlockSpec((1, D_MODEL, D_EXPERT), wmap),
                      pl.BlockSpec((1, D_MODEL, D_EXPERT), wmap),
                      pl.BlockSpec((1, D_EXPERT, D_MODEL), wmap),
                      pl.BlockSpec(memory_space=pl.ANY)],
            out_specs=pl.BlockSpec(memory_space=pl.ANY),
            scratch_shapes=[pltpu.VMEM((D_MODEL, D_EXPERT), BF16), pltpu.VMEM((D_MODEL, D_EXPERT), BF16),
                            pltpu.VMEM((D_EXPERT, D_MODEL), BF16),
                            pltpu.VMEM((GEMM_RING, MOE_TILE, XS_W), BF16),
                            pltpu.VMEM((2, MOE_TILE, D_MODEL), BF16), pltpu.VMEM((MOE_TILE, D_MODEL), BF16),
                            pltpu.SemaphoreType.DMA((GEMM_RING,)), pltpu.SemaphoreType.DMA((2,)),
                            pltpu.SemaphoreType.DMA((1,))]),
        out_shape=jax.ShapeDtypeStruct((xs.shape[0], D_MODEL), BF16),
        compiler_params=_cparams("arbitrary"),
        name="moe_gemm",
    )(tile0, tile1, n_tiles, w_gate, w_up, w_down, xs)


def _moe_combine_kernel(step0, dst_ref, nch_ref, lp_ref, x1_ref, mod_ref, gpost_ref, y_hbm, o_ref, ybuf, sem):
    i = pl.program_id(0)
    n = pl.num_programs(0)
    slot = i % 2

    def fetch(step, s):
        def body(c, carry):
            _chunk_copy(y_hbm, dst_ref[step, c], ybuf.at[s], c, sem.at[s]).start()
            return carry
        lax.fori_loop(0, nch_ref[step], body, 0)

    @pl.when(i == 0)
    def _():
        ybuf[...] = jnp.zeros_like(ybuf)
        fetch(step0, 0)

    @pl.when(i + 1 < n)
    def _():
        fetch(step0 + i + 1, 1 - slot)

    def wait(c, carry):
        _chunk_copy(y_hbm, 0, ybuf.at[slot], 0, sem.at[slot]).wait()
        return carry
    lax.fori_loop(0, nch_ref[step0 + i], wait, 0)

    lp = lp_ref[...]
    col = lax.broadcasted_iota(jnp.int32, (MOE_TD, MOE_L), 1)
    unperm = jnp.where((col == lp[:, 0:1]) | (col == lp[:, 1:2]), 1.0, 0.0).astype(BF16)
    moe = _dot(unperm, ybuf[slot])
    gt2 = mod_ref[0, 5:6, :]
    o_ref[...] = x1_ref[...] + gt2 * _rms(moe, gpost_ref[...])


def _moe_combine(step0, dst, n_chunks, lp, x1, mod, mod_row, y_sorted, g_post2):
    n = x1.shape[0]
    tok = lambda i, *_: (i, 0)
    return pl.pallas_call(
        functools.partial(_moe_combine_kernel, step0),
        grid_spec=pltpu.PrefetchScalarGridSpec(
            num_scalar_prefetch=2,
            grid=(n // MOE_TD,),
            in_specs=[pl.BlockSpec((MOE_TD, LANE), lambda i, *_: (step0 + i, 0)),
                      pl.BlockSpec((MOE_TD, D_MODEL), tok),
                      pl.BlockSpec((1, 6, D_MODEL), lambda i, *_: (mod_row(i), 0, 0)),
                      pl.BlockSpec((1, D_MODEL), lambda i, *_: (0, 0)),
                      pl.BlockSpec(memory_space=pl.ANY)],
            out_specs=pl.BlockSpec((MOE_TD, D_MODEL), tok),
            scratch_shapes=[pltpu.VMEM((2, MOE_L, D_MODEL), BF16), pltpu.SemaphoreType.DMA((2,))]),
        out_shape=jax.ShapeDtypeStruct((n, D_MODEL), F32),
        compiler_params=_cparams("arbitrary"),
        name="moe_combine",
    )(dst, n_chunks, lp, x1, mod, g_post2, y_sorted)


def _moe(part_a, part_b, mod, mod_row_a, mod_row_b, w_gate, w_up, w_down, g_post2):
    cnt = jnp.concatenate([part_a[4], part_b[4]], axis=0)[:, 0, :N_EXPERTS].astype(jnp.int32)
    steps_a = part_a[4].shape[0]
    n_tiles_max = (cnt.shape[0] * MOE_NCH + N_EXPERTS * (TILE_CH - 1)) // TILE_CH + 1
    dst, n_chunks, tail_start, tail_len, tile0, tile1, n_tiles = _plan(cnt, n_tiles_max)
    xs, lp = _dispatch(dst, n_chunks, tail_start, tail_len, n_tiles, part_a[1:4], part_b[1:4], n_tiles_max)
    y_sorted = _moe_gemm(tile0, tile1, n_tiles, xs, w_gate, w_up, w_down)
    ya = _moe_combine(0, dst, n_chunks, lp, part_a[0], mod, mod_row_a, y_sorted, g_post2)
    yb = _moe_combine(steps_a, dst, n_chunks, lp, part_b[0], mod, mod_row_b, y_sorted, g_post2)
    return ya, yb


def _rotate_half_cols(w):
    n = w.shape[-1]
    w4 = w.reshape(w.shape[:-1] + (n // 32, 2, 16))
    return jnp.stack([-w4[..., 1, :], w4[..., 0, :]], axis=-2).reshape(w.shape)


def _pad_lanes(w):
    return jnp.concatenate([w, jnp.zeros(w.shape[:-1] + (LANE - w.shape[-1],), w.dtype)], axis=-1)


def _rope_tables(t):
    rows = t // GRID_W
    n_freq = D_ROPE // 4
    freqs = ROPE_BASE ** (-jnp.arange(n_freq, dtype=F32) / n_freq)
    ang_r = jnp.arange(rows, dtype=F32)[:, None] * freqs
    ang_c = jnp.arange(GRID_W, dtype=F32)[:, None] * freqs
    per_row = lambda a: jnp.repeat(a, GRID_W, axis=0)
    per_col = lambda a: jnp.tile(a, (rows, 1))

    def table(fn):
        r, c = per_row(fn(ang_r)), per_col(fn(ang_c))
        return _pad_lanes(jnp.concatenate([r, r, c, c], axis=-1))
    return table(jnp.cos), table(jnp.sin)


def kernel(x_prompt, x_sample, cache_ckv, cache_krope, c, c_ctx, w_ada, b_ada, g_pre1, g_post1, g_pre2, g_post2, w_in, w_dw, b_dw, conv_ln_g, conv_ln_b, q_norm_g, kv_norm_g, w_uq, w_ukv, w_out, w_rg, b_rg, w_re, b_re, w_gate, w_up, w_down):
    nb, seq, d = x_prompt.shape
    db, dseq, _ = x_sample.shape
    l = 0

    cvec = jnp.concatenate([c_ctx[None, :], c, jnp.zeros((8 - 1 - db, d), F32)], axis=0)
    mod = _ada(cvec, w_ada[l], b_ada[l]).reshape(8, 6, d)

    o_kr = 2 * C_CONV + Q_LORA + KV_LORA
    w_in_l = w_in[l]
    w_kr = w_in_l[:, o_kr:]
    w_in_ctx = jnp.concatenate([w_in_l[:, :o_kr], _pad_lanes(w_kr)], axis=-1).astype(BF16)
    w_in_lat = jnp.concatenate([w_in_l[:, :o_kr], _pad_lanes(w_kr), _pad_lanes(_rotate_half_cols(w_kr))],
                               axis=-1).astype(BF16)
    wuq = w_uq[l].reshape(Q_LORA, N_HEADS, D_NOPE + D_ROPE)
    wq_rope = wuq[:, :, D_NOPE:]
    wq = jnp.concatenate([wuq[:, :, :D_NOPE], _pad_lanes(wq_rope)], axis=-1)
    wq = wq.reshape(Q_LORA, N_HEADS * HEAD_W).astype(BF16)
    wqs = _pad_lanes(_rotate_half_cols(wq_rope)).reshape(Q_LORA, N_HEADS * LANE).astype(BF16)
    wukv4 = w_ukv[l].reshape(KV_LORA, N_HEADS, D_NOPE + D_V)
    wukv = jnp.concatenate([wukv4[:, :, :D_NOPE].reshape(KV_LORA, N_HEADS * D_NOPE),
                            wukv4[:, :, D_NOPE:].reshape(KV_LORA, N_HEADS * D_V)], axis=-1).astype(BF16)
    w_out_b = w_out[l].astype(BF16)
    w_r = jnp.concatenate([w_re[l], w_rg[l], jnp.zeros((d, LANE - N_EXPERTS - N_GROUPS), F32)], axis=-1)
    wr_hi = w_r.astype(BF16)
    wr_lo = jnp.concatenate([wr_hi[:, :LANE // 2], (w_r - wr_hi.astype(F32)).astype(BF16)[:, :LANE // 2]], axis=-1)
    b_r = jnp.concatenate([b_re[l], b_rg[l], jnp.zeros((LANE - N_EXPERTS - N_GROUPS,), F32)])[None, :]
    cos, sin = _rope_tables(dseq)
    row = lambda v: v[l][None, :]

    tm_c = 512
    xp_flat = x_prompt.reshape(1, nb * seq, d)
    conv_w = (w_dw[l], row(b_dw), row(conv_ln_g), row(conv_ln_b))
    assert seq == CONV_TB, "a context step must hold whole sequences of CONV_TB tokens"
    conv_out, q, k, v, ckv, kr = _mix_in(xp_flat, mod, 0, row(g_pre1), w_in_ctx, row(q_norm_g), row(kv_norm_g),
                                         wq, wqs, wukv, cos[:tm_c], sin[:tm_c], conv_w, False, tm_c)
    per_seq = lambda a: a.reshape(nb, seq, a.shape[-1])
    att = _attn(per_seq(q), per_seq(k), per_seq(v))
    flat = lambda a: a.reshape(1, nb * seq, a.shape[-1])
    post_c = _post(xp_flat, conv_out, flat(att), mod, 0, w_out_b, row(g_post1), row(g_pre2),
                   wr_hi, wr_lo, b_r)
    state_ckv = ckv.reshape(nb, 1, seq, KV_LORA)
    state_krope = kr.reshape(nb, 1, seq, D_ROPE)

    tm_s = 512
    kc, vc = _cache_kv(cache_ckv[:, l], cache_krope[:, l], wukv)
    conv_out, q, k, v, _, _ = _mix_in(x_sample, mod, 1, row(g_pre1), w_in_lat, row(q_norm_g), row(kv_norm_g),
                                      wq, wqs, wukv, cos, sin, conv_w, True, tm_s)
    att = _attn(q, k, v, kc, vc)
    post_s = _post(x_sample, conv_out, att, mod, 1, w_out_b, row(g_post1), row(g_pre2), wr_hi, wr_lo, b_r)

    tokens = lambda parts: tuple(a.reshape((-1,) + a.shape[2:]) for a in parts)
    steps_per_req = dseq // MOE_TD
    yp, ys = _moe(tokens(post_c), tokens(post_s), mod, lambda i: 0, lambda i: 1 + i // steps_per_req,
                  w_gate[l], w_up[l], w_down[l], row(g_post2))

    return (yp.reshape(nb, seq, d), ys.reshape(db, dseq, d), state_ckv, state_krope)
```

```python
import functools

import jax
import jax.numpy as jnp
import numpy as np
from jax import lax
from jax.experimental import pallas as pl
from jax.experimental.pallas import tpu as pltpu

D_MODEL = 1024
GRID_W = 64
C_CONV = 512
CONV_K = 31
N_HEADS = 4
D_NOPE = 128
D_ROPE = 64
D_V = 128
Q_LORA = 384
KV_LORA = 256
N_GROUPS = 4
E_PER_GROUP = 8
N_EXPERTS = 32
D_EXPERT = 256
ROPE_BASE = 10000.0
EPS = 1e-6
ATT_SCALE = (D_NOPE + D_ROPE) ** -0.5
LOG2E = 1.4426950408889634

LANE = 128
SUBLANE = 8
HEAD_W = 2 * LANE
CONV_HALO = 16
MOE_TD = 512
CHUNK = 2 * SUBLANE
MOE_NCH = (2 * MOE_TD + N_EXPERTS * (CHUNK - 1)) // CHUNK
MOE_L = MOE_NCH * CHUNK
MOE_TILE = 256
TILE_CH = MOE_TILE // CHUNK
GEMM_RING = 4
XS_W = D_MODEL + LANE
VMEM_LIMIT = 56 * 1024 * 1024

BF16 = jnp.bfloat16
F32 = jnp.float32


def _cparams(*sem):
    return pltpu.CompilerParams(dimension_semantics=sem, vmem_limit_bytes=VMEM_LIMIT)


def _rms(x, g):
    return x * lax.rsqrt(jnp.mean(x * x, axis=-1, keepdims=True) + EPS) * g


def _sigmoid(x):
    return 1.0 / (1.0 + jnp.exp(-x))


def _dot(a, b):
    return jnp.dot(a, b, preferred_element_type=F32)


def _chunk_copy(src, src_chunk, dst, dst_chunk, sem):
    rows = lambda c: pl.ds(c * CHUNK if isinstance(c, int) else pl.multiple_of(c * CHUNK, CHUNK), CHUNK)
    return pltpu.make_async_copy(src.at[rows(src_chunk), :], dst.at[rows(dst_chunk), :], sem)


def _ada_kernel(c_ref, w_ref, b_ref, o_ref):
    c = c_ref[...]
    s = (c * _sigmoid(c)).astype(BF16)
    o_ref[...] = _dot(s, w_ref[...].astype(BF16)) + b_ref[...]


def _ada(cvec, w_ada, b_ada):
    n = w_ada.shape[1]
    tn = 1536
    return pl.pallas_call(
        _ada_kernel,
        grid=(n // tn,),
        in_specs=[pl.BlockSpec((8, D_MODEL), lambda j: (0, 0)),
                  pl.BlockSpec((D_MODEL, tn), lambda j: (0, j)),
                  pl.BlockSpec((1, tn), lambda j: (0, j))],
        out_specs=pl.BlockSpec((8, tn), lambda j: (0, j)),
        out_shape=jax.ShapeDtypeStruct((8, n), F32),
        compiler_params=_cparams("arbitrary"),
        name="ada",
    )(cvec, w_ada, b_ada.reshape(1, n))


CONV_TB = 256
CONV_WIN = CONV_TB + 2 * CONV_HALO
CONV_TT = 32


def _conv_window(win_ref, sh_ref, w_ref, b_ref, g_ref, bb_ref, o_ref, row0):
    first = CONV_HALO - CONV_K // 2
    rows = CONV_WIN - SUBLANE
    win = win_ref[...]
    for s in range(SUBLANE):
        sh_ref[s, 0:rows, :] = win[s:s + rows, :]
    for c in range(CONV_TB // CONV_TT):
        acc = jnp.zeros((CONV_TT, C_CONV), F32)
        for k in range(CONV_K):
            off = first + k
            r0 = c * CONV_TT + off // SUBLANE * SUBLANE
            acc = acc + sh_ref[off % SUBLANE, r0:r0 + CONV_TT, :] * w_ref[k:k + 1, :]
        y = acc + b_ref[...]
        mu = jnp.mean(y, axis=-1, keepdims=True)
        yc = y - mu
        var = jnp.mean(yc * yc, axis=-1, keepdims=True)
        z = yc * lax.rsqrt(var + EPS) * g_ref[...] + bb_ref[...]
        o_ref[0, row0 + c * CONV_TT:row0 + (c + 1) * CONV_TT, :] = (z * _sigmoid(z)).astype(BF16)


def _mix_in_kernel(latent, x_ref, xp_ref, xn_ref, mod_ref, gpre_ref, win_ref, qg_ref, kvg_ref, wq_ref, wqs_ref,
                   wukv_ref, cos_ref, sin_ref, wdw_ref, bdw_ref, lng_ref, lnb_ref,
                   conv_ref, q_ref, k_ref, v_ref, ckv_ref, kr_ref, cwin, csh):
    rope = latent
    x = x_ref[0]
    tm = x.shape[0]
    sh1 = mod_ref[0, 0:1, :]
    sc1 = mod_ref[0, 1:2, :]
    modulate = lambda v: _rms(v, gpre_ref[...]) * (1.0 + sc1) + sh1
    glu = lambda uc: uc[:, :C_CONV] * _sigmoid(uc[:, C_CONV:2 * C_CONV])
    u = _dot(modulate(x).astype(BF16), win_ref[...])
    hglu = glu(u)

    zeros = jnp.zeros((CONV_HALO, C_CONV), F32)
    n_win = tm // CONV_TB
    if latent:
        i = pl.program_id(1)
        xh = jnp.concatenate([xp_ref[0], xn_ref[0]], axis=0)
        hh = glu(_dot(modulate(xh).astype(BF16), win_ref[:, 0:2 * C_CONV]))
        before = jnp.where(i > 0, hh[0:CONV_HALO], zeros)
        after = jnp.where(i < pl.num_programs(1) - 1, hh[CONV_HALO:2 * CONV_HALO], zeros)
    for j in range(n_win):
        if latent:
            lo, hi, d0 = j * CONV_TB - CONV_HALO, j * CONV_TB + CONV_TB + CONV_HALO, 0
            if lo < 0:
                cwin[j, 0:CONV_HALO, :] = before
                lo, d0 = 0, CONV_HALO
            if hi > tm:
                cwin[j, CONV_WIN - CONV_HALO:CONV_WIN, :] = after
                hi = tm
            cwin[j, d0:d0 + hi - lo, :] = hglu[lo:hi, :]
        else:
            cwin[j, 0:CONV_HALO, :] = zeros
            cwin[j, CONV_HALO:CONV_HALO + CONV_TB, :] = hglu[j * CONV_TB:(j + 1) * CONV_TB, :]
            cwin[j, CONV_HALO + CONV_TB:CONV_WIN, :] = zeros
    for j in range(n_win):
        _conv_window(cwin.at[j], csh, wdw_ref, bdw_ref, lng_ref, lnb_ref, conv_ref, j * CONV_TB)

    o_q = 2 * C_CONV
    o_kv = o_q + Q_LORA
    o_kr = o_kv + KV_LORA
    qn = _rms(u[:, o_q:o_kv], qg_ref[...]).astype(BF16)
    qf = _dot(qn, wq_ref[...])
    ckv = _rms(u[:, o_kv:o_kr], kvg_ref[...])
    ckv_ref[0] = ckv
    kvd = _dot(ckv.astype(BF16), wukv_ref[...])
    kr = u[:, o_kr:o_kr + LANE]
    kr_ref[0] = kr[:, :D_ROPE]
    if rope:
        cos = cos_ref[...]
        sin = sin_ref[...]
        qs = _dot(qn, wqs_ref[...])
        kr = kr * cos + u[:, o_kr + LANE:o_kr + 2 * LANE] * sin
    q_parts = []
    k_parts = []
    for hd in range(N_HEADS):
        q_parts.append(qf[:, hd * HEAD_W:hd * HEAD_W + LANE])
        qr = qf[:, hd * HEAD_W + LANE:(hd + 1) * HEAD_W]
        if rope:
            qr = qr * cos + qs[:, hd * LANE:(hd + 1) * LANE] * sin
        q_parts.append(qr)
        k_parts.append(kvd[:, hd * D_NOPE:(hd + 1) * D_NOPE])
        k_parts.append(kr)
    q_ref[0] = jnp.concatenate(q_parts, axis=-1).astype(BF16)
    k_ref[0] = jnp.concatenate(k_parts, axis=-1).astype(BF16)
    v_ref[0] = kvd[:, N_HEADS * D_NOPE:].astype(BF16)


def _mix_in(x, mod, mod_row0, g_pre1, w_in_ext, q_norm_g, kv_norm_g, wq, wqs, wukv, cos, sin, conv_w, latent, tm):
    b, t, _ = x.shape
    ncol = w_in_ext.shape[1]
    const = lambda bi, i: (0, 0)
    tok = lambda bi, i: (bi, i, 0)
    table = (lambda bi, i: (i, 0)) if latent else const
    halo_blocks = tm // CONV_HALO
    prev_rows = lambda bi, i: (bi, jnp.maximum(i * halo_blocks - 1, 0), 0)
    next_rows = lambda bi, i: (bi, jnp.minimum((i + 1) * halo_blocks, t // CONV_HALO - 1), 0)
    outs = [(C_CONV, BF16), (N_HEADS * HEAD_W, BF16), (N_HEADS * HEAD_W, BF16), (N_HEADS * D_V, BF16),
            (KV_LORA, F32), (D_ROPE, F32)]
    return pl.pallas_call(
        functools.partial(_mix_in_kernel, latent),
        grid=(b, t // tm),
        in_specs=[pl.BlockSpec((1, tm, D_MODEL), tok),
                  pl.BlockSpec((1, CONV_HALO, D_MODEL), prev_rows),
                  pl.BlockSpec((1, CONV_HALO, D_MODEL), next_rows),
                  pl.BlockSpec((1, 6, D_MODEL), lambda bi, i: (mod_row0 + bi, 0, 0)),
                  pl.BlockSpec((1, D_MODEL), const),
                  pl.BlockSpec((D_MODEL, ncol), const),
                  pl.BlockSpec((1, Q_LORA), const),
                  pl.BlockSpec((1, KV_LORA), const),
                  pl.BlockSpec((Q_LORA, N_HEADS * HEAD_W), const),
                  pl.BlockSpec((Q_LORA, N_HEADS * LANE), const),
                  pl.BlockSpec((KV_LORA, N_HEADS * (D_NOPE + D_V)), const),
                  pl.BlockSpec((tm, LANE), table),
                  pl.BlockSpec((tm, LANE), table),
                  pl.BlockSpec((CONV_K, C_CONV), const),
                  pl.BlockSpec((1, C_CONV), const),
                  pl.BlockSpec((1, C_CONV), const),
                  pl.BlockSpec((1, C_CONV), const)],
        out_specs=[pl.BlockSpec((1, tm, w), tok) for w, _ in outs],
        out_shape=[jax.ShapeDtypeStruct((b, t, w), dt) for w, dt in outs],
        scratch_shapes=[pltpu.VMEM((tm // CONV_TB, CONV_WIN, C_CONV), F32),
                        pltpu.VMEM((SUBLANE, CONV_WIN, C_CONV), F32)],
        compiler_params=_cparams("arbitrary", "arbitrary"),
        name="mix_in_latent" if latent else "mix_in",
    )(x, x, x, mod, g_pre1, w_in_ext, q_norm_g, kv_norm_g, wq, wqs, wukv, cos, sin, *conv_w)


def _qk(q, k):
    return lax.dot_general(q, k, (((1,), (1,)), ((), ())), preferred_element_type=F32)


ATT_KCHUNK = 256


def _attn_kernel(cached, q_ref, k_ref, v_ref, *rest):
    if cached:
        kc_ref, vc_ref, o_ref, s_scr = rest
        sources = [(kc_ref, vc_ref), (k_ref, v_ref)]
    else:
        o_ref, s_scr = rest
        sources = [(k_ref, v_ref)]
    n_seq, tq, _ = q_ref.shape
    chunks = []
    for kr, vr in sources:
        for c0 in range(0, kr.shape[1], ATT_KCHUNK):
            c1 = min(c0 + ATT_KCHUNK, kr.shape[1])
            col = chunks[-1][4] + chunks[-1][3] - chunks[-1][2] if chunks else 0
            chunks.append((kr, vr, c0, c1, col))
    for b in range(n_seq):
        outs = []
        for hd in range(N_HEADS):
            ks = slice(hd * HEAD_W, (hd + 1) * HEAD_W)
            vs = slice(hd * D_V, (hd + 1) * D_V)
            qh = q_ref[b, :, ks]
            m = jnp.full((tq, 1), -jnp.inf, F32)
            for kr, vr, c0, c1, col in chunks:
                s = _qk(qh, kr[b, c0:c1, ks]) * (ATT_SCALE * LOG2E)
                s_scr[hd, :, col:col + c1 - c0] = s
                m = jnp.maximum(m, jnp.max(s, axis=-1, keepdims=True))
            l = jnp.zeros((tq, 1), F32)
            o = jnp.zeros((tq, D_V), F32)
            for kr, vr, c0, c1, col in chunks:
                p = jnp.exp2(s_scr[hd, :, col:col + c1 - c0] - m)
                l = l + jnp.sum(p, axis=-1, keepdims=True)
                o = o + _dot(p.astype(BF16), vr[b, c0:c1, vs])
            outs.append(o / l)
        o_ref[b] = jnp.concatenate(outs, axis=-1).astype(BF16)


def _attn(q, k, v, kc=None, vc=None, tq=256, n_seq=1):
    b, t, _ = q.shape
    s = k.shape[1]
    cached = kc is not None
    whole = lambda bi, i: (bi, 0, 0)
    in_specs = [pl.BlockSpec((n_seq, tq, N_HEADS * HEAD_W), lambda bi, i: (bi, i, 0)),
                pl.BlockSpec((n_seq, s, N_HEADS * HEAD_W), whole),
                pl.BlockSpec((n_seq, s, N_HEADS * D_V), whole)]
    args = [q, k, v]
    if cached:
        sc = kc.shape[1]
        in_specs += [pl.BlockSpec((n_seq, sc, N_HEADS * HEAD_W), whole),
                     pl.BlockSpec((n_seq, sc, N_HEADS * D_V), whole)]
        args += [kc, vc]
        s += sc
    return pl.pallas_call(
        functools.partial(_attn_kernel, cached),
        grid=(b // n_seq, t // tq),
        in_specs=in_specs,
        out_specs=pl.BlockSpec((n_seq, tq, N_HEADS * D_V), lambda bi, i: (bi, i, 0)),
        out_shape=jax.ShapeDtypeStruct((b, t, N_HEADS * D_V), BF16),
        scratch_shapes=[pltpu.VMEM((N_HEADS, tq, s), F32)],
        compiler_params=_cparams("arbitrary", "arbitrary"),
        name="attn_cached" if cached else "attn",
    )(*args)


def _cache_kv_kernel(ckv_ref, kr_ref, wukv_ref, k_ref, v_ref):
    kvd = _dot(ckv_ref[0].astype(BF16), wukv_ref[...])
    kr = kr_ref[0]
    kr = jnp.concatenate([kr, jnp.zeros_like(kr)], axis=-1)
    parts = []
    for hd in range(N_HEADS):
        parts.append(kvd[:, hd * D_NOPE:(hd + 1) * D_NOPE])
        parts.append(kr)
    k_ref[0] = jnp.concatenate(parts, axis=-1).astype(BF16)
    v_ref[0] = kvd[:, N_HEADS * D_NOPE:].astype(BF16)


def _cache_kv(ckv, krope, wukv):
    b, s, _ = ckv.shape
    tok = lambda bi: (bi, 0, 0)
    return pl.pallas_call(
        _cache_kv_kernel,
        grid=(b,),
        in_specs=[pl.BlockSpec((1, s, KV_LORA), tok),
                  pl.BlockSpec((1, s, D_ROPE), tok),
                  pl.BlockSpec((KV_LORA, N_HEADS * (D_NOPE + D_V)), lambda bi: (0, 0))],
        out_specs=[pl.BlockSpec((1, s, N_HEADS * HEAD_W), tok),
                   pl.BlockSpec((1, s, N_HEADS * D_V), tok)],
        out_shape=[jax.ShapeDtypeStruct((b, s, N_HEADS * HEAD_W), BF16),
                   jax.ShapeDtypeStruct((b, s, N_HEADS * D_V), BF16)],
        compiler_params=_cparams("arbitrary"),
        name="cache_kv",
    )(ckv, krope, wukv)


def _post_kernel(x_ref, conv_ref, att_ref, mod_ref, wo_ref, gpost_ref, gpre2_ref, wr_hi_ref, wr_cat_ref,
                 br_ref, x1_ref, h2_ref, ri_ref, rw_ref, cnt_ref):
    out = _dot(conv_ref[0], wo_ref[:C_CONV, :]) + _dot(att_ref[0], wo_ref[C_CONV:, :])
    gt1 = mod_ref[0, 2:3, :]
    sh2 = mod_ref[0, 3:4, :]
    sc2 = mod_ref[0, 4:5, :]
    x1 = x_ref[0] + gt1 * _rms(out, gpost_ref[...])
    x1_ref[0] = x1
    h2 = _rms(x1, gpre2_ref[...]) * (1.0 + sc2) + sh2
    h_hi = h2.astype(BF16)
    h2_ref[0] = h_hi
    h_lo = (h2 - h_hi.astype(F32)).astype(BF16)
    hi_terms = _dot(h_hi, wr_cat_ref[...])
    logits = hi_terms + pltpu.roll(hi_terms, LANE // 2, axis=1) + _dot(h_lo, wr_hi_ref[...]) + br_ref[...]

    lane = lax.broadcasted_iota(jnp.int32, logits.shape, 1)
    neg = jnp.float32(-jnp.inf)
    big = jnp.int32(LANE)
    is_g = (lane >= N_EXPERTS) & (lane < N_EXPERTS + N_GROUPS)
    lg = jnp.where(is_g, logits, neg)
    gmax = jnp.max(lg, axis=-1, keepdims=True)
    gidx = jnp.min(jnp.where(lg == gmax, lane, big), axis=-1, keepdims=True) - N_EXPERTS
    g_top = 1.0 / jnp.sum(jnp.exp(lg - gmax), axis=-1, keepdims=True)

    in_grp = (lane >= gidx * E_PER_GROUP) & (lane < (gidx + 1) * E_PER_GROUP)
    le = jnp.where(in_grp, logits, neg)
    m1 = jnp.max(le, axis=-1, keepdims=True)
    i1 = jnp.min(jnp.where(le == m1, lane, big), axis=-1, keepdims=True)
    le2 = jnp.where(lane == i1, neg, le)
    m2 = jnp.max(le2, axis=-1, keepdims=True)
    i2 = jnp.min(jnp.where(le2 == m2, lane, big), axis=-1, keepdims=True)
    r = jnp.exp(m2 - m1)
    w1 = g_top / (1.0 + r)
    w2 = g_top * r / (1.0 + r)
    ri_ref[0] = jnp.where(lane == 0, i1, jnp.where(lane == 1, i2, 0))
    rw_ref[0] = jnp.where(lane == 0, w1, jnp.where(lane == 1, w2, 0.0))
    pairs = jnp.sum(jnp.where((lane == i1) | (lane == i2), 1.0, 0.0), axis=0, keepdims=True)
    cnt_ref[0, 0] = jnp.broadcast_to(pairs, (SUBLANE, LANE))


def _post(x, conv_out, att, mod, mod_row0, w_out, g_post1, g_pre2, wr_hi, wr_lo, br):
    b, t, _ = x.shape
    tm = MOE_TD
    const = lambda bi, i: (0, 0)
    tok = lambda bi, i: (bi, i, 0)
    return pl.pallas_call(
        _post_kernel,
        grid=(b, t // tm),
        in_specs=[pl.BlockSpec((1, tm, D_MODEL), tok),
                  pl.BlockSpec((1, tm, C_CONV), tok),
                  pl.BlockSpec((1, tm, N_HEADS * D_V), tok),
                  pl.BlockSpec((1, 6, D_MODEL), lambda bi, i: (mod_row0 + bi, 0, 0)),
                  pl.BlockSpec((D_MODEL, D_MODEL), const),
                  pl.BlockSpec((1, D_MODEL), const),
                  pl.BlockSpec((1, D_MODEL), const),
                  pl.BlockSpec((D_MODEL, LANE), const),
                  pl.BlockSpec((D_MODEL, LANE), const),
                  pl.BlockSpec((1, LANE), const)],
        out_specs=[pl.BlockSpec((1, tm, D_MODEL), tok),
                   pl.BlockSpec((1, tm, D_MODEL), tok),
                   pl.BlockSpec((1, tm, LANE), tok),
                   pl.BlockSpec((1, tm, LANE), tok),
                   pl.BlockSpec((1, 1, SUBLANE, LANE), lambda bi, i: (bi, i, 0, 0))],
        out_shape=[jax.ShapeDtypeStruct((b, t, D_MODEL), F32),
                   jax.ShapeDtypeStruct((b, t, D_MODEL), BF16),
                   jax.ShapeDtypeStruct((b, t, LANE), jnp.int32),
                   jax.ShapeDtypeStruct((b, t, LANE), F32),
                   jax.ShapeDtypeStruct((b, t // tm, SUBLANE, LANE), F32)],
        compiler_params=_cparams("arbitrary", "arbitrary"),
        name="post",
    )(x, conv_out, att, mod, w_out, g_post1, g_pre2, wr_hi, wr_lo, br)


def _one_hots(ri):
    lane = lax.broadcasted_iota(jnp.int32, ri.shape, 1)
    oh1 = lane == ri[:, 0:1]
    oh2 = lane == ri[:, 1:2]
    return oh1, oh2, jnp.where(oh1 | oh2, 1.0, 0.0)


def _local_positions(oh1, oh2, oh):
    td = oh.shape[0]
    cnt = jnp.sum(oh, axis=0, keepdims=True)
    nch = jnp.floor((cnt + (CHUNK - 1)) * (1.0 / CHUNK))
    a = lax.broadcasted_iota(jnp.int32, (LANE, LANE), 0)
    b = lax.broadcasted_iota(jnp.int32, (LANE, LANE), 1)
    lower_experts = jnp.where(a < b, 1.0, 0.0).astype(BF16)
    run_start = _dot(jnp.broadcast_to(nch, (SUBLANE, LANE)).astype(BF16), lower_experts)[0:1, :] * CHUNK
    rr = lax.broadcasted_iota(jnp.int32, (td, td), 0)
    cc = lax.broadcasted_iota(jnp.int32, (td, td), 1)
    earlier = jnp.where(cc < rr, 1.0, 0.0).astype(BF16)
    pos = _dot(earlier, oh.astype(BF16)) + run_start
    lp1 = jnp.sum(jnp.where(oh1, pos, 0.0), axis=-1, keepdims=True)
    lp2 = jnp.sum(jnp.where(oh2, pos, 0.0), axis=-1, keepdims=True)
    return lp1, lp2


def _plan(cnt, n_tiles_max):
    k = (cnt + (CHUNK - 1)) // CHUNK
    run_end = jnp.cumsum(k, axis=1)
    run_start = run_end - k
    n_chunks = run_end[:, -1]
    total = jnp.sum(k, axis=0)
    padded = (total + (TILE_CH - 1)) // TILE_CH * TILE_CH
    seg_end = jnp.cumsum(padded)
    seg_start = seg_end - padded
    base = seg_start[None, :] + jnp.cumsum(k, axis=0) - k
    c = jnp.arange(MOE_NCH, dtype=jnp.int32)[None, :, None]
    in_run = (run_start[:, None, :] <= c) & (c < run_end[:, None, :])
    dst = jnp.sum(jnp.where(in_run, base[:, None, :] + c - run_start[:, None, :], 0), axis=2)
    n_tiles = seg_end[-1] // TILE_CH
    tail_start = seg_start + total
    tail_len = padded - total
    i32 = lambda t: t.astype(jnp.int32)
    return (i32(dst), i32(n_chunks), i32(tail_start), i32(tail_len), i32(seg_start // TILE_CH),
            i32(seg_end // TILE_CH), i32(n_tiles.reshape(1)))


def _dispatch_kernel(steps_a, dst_ref, nch_ref, ts_ref, tl_ref, nt_ref, h2a_ref, ria_ref, rwa_ref, h2b_ref, rib_ref,
                     rwb_ref, xs_hbm, lp_ref, xbuf, zx, sem, zsem):
    i = pl.program_id(0)
    n = pl.num_programs(0)
    slot = i % 2
    td = MOE_TD
    n_tiles_max = xs_hbm.shape[0] // MOE_TILE

    def zero_tile(m):
        rows = pl.ds(pl.multiple_of(m * MOE_TILE, MOE_TILE), MOE_TILE)
        return pltpu.make_async_copy(zx, xs_hbm.at[rows, :], zsem.at[1])

    @pl.when(i == 0)
    def _():
        zx[...] = jnp.zeros_like(zx)
        for e in range(N_EXPERTS):
            def fill(m, carry, e=e):
                _chunk_copy(zx, 0, xs_hbm, ts_ref[e] + m, zsem.at[0]).start()
                return carry
            lax.fori_loop(0, tl_ref[e], fill, 0)

        def fill_tile(m, carry):
            zero_tile(m).start()
            return carry
        lax.fori_loop(nt_ref[0], n_tiles_max, fill_tile, 0)

    from_a = i < steps_a
    ri = jnp.where(from_a, ria_ref[...], rib_ref[...])
    rw = jnp.where(from_a, rwa_ref[...], rwb_ref[...])
    h2 = jnp.where(from_a, h2a_ref[...], h2b_ref[...])
    oh1, oh2, oh = _one_hots(ri)
    lp1, lp2 = _local_positions(oh1, oh2, oh)
    lane = lax.broadcasted_iota(jnp.int32, ri.shape, 1)
    lp_ref[...] = jnp.where(lane == 0, lp1, jnp.where(lane == 1, lp2, 0.0)).astype(jnp.int32)

    rr = lax.broadcasted_iota(jnp.int32, (td, td), 0)
    cc = lax.broadcasted_iota(jnp.int32, (td, td), 1)
    to_lanes = lambda col: jnp.sum(jnp.where(rr == cc, col, 0.0), axis=0, keepdims=True)
    row = lax.broadcasted_iota(jnp.int32, (MOE_L, td), 0).astype(F32)
    p1 = row == to_lanes(lp1)
    p2 = row == to_lanes(lp2)
    perm = jnp.where(p1 | p2, 1.0, 0.0).astype(BF16)
    xbuf[slot, :, 0:D_MODEL] = _dot(perm, h2).astype(BF16)
    w = jnp.sum(jnp.where(p1, to_lanes(rw[:, 0:1]), 0.0) + jnp.where(p2, to_lanes(rw[:, 1:2]), 0.0),
                axis=-1, keepdims=True)
    w0 = w.astype(BF16).astype(F32)
    w1 = (w - w0).astype(BF16).astype(F32)
    w2 = (w - w0 - w1).astype(BF16).astype(F32)
    wl = lax.broadcasted_iota(jnp.int32, (MOE_L, LANE), 1)
    terms = jnp.where(wl == 0, w0, jnp.where(wl == 1, w1, jnp.where(wl == 2, w2, 0.0)))
    xbuf[slot, :, D_MODEL:XS_W] = terms.astype(BF16)

    def wait_chunks(count, s):
        def body(c, carry):
            _chunk_copy(xbuf.at[s], 0, xs_hbm, 0, sem.at[s]).wait()
            return carry
        lax.fori_loop(0, count, body, 0)

    @pl.when(i > 0)
    def _():
        wait_chunks(nch_ref[i - 1], 1 - slot)

    def send(c, carry):
        _chunk_copy(xbuf.at[slot], c, xs_hbm, dst_ref[i, c], sem.at[slot]).start()
        return carry
    lax.fori_loop(0, nch_ref[i], send, 0)

    @pl.when(i == n - 1)
    def _():
        wait_chunks(nch_ref[i], slot)
        for e in range(N_EXPERTS):
            def drain(m, carry):
                _chunk_copy(zx, 0, xs_hbm, 0, zsem.at[0]).wait()
                return carry
            lax.fori_loop(0, tl_ref[e], drain, 0)

        def drain_tile(m, carry):
            zero_tile(m).wait()
            return carry
        lax.fori_loop(nt_ref[0], n_tiles_max, drain_tile, 0)


def _dispatch(dst, n_chunks, tail_start, tail_len, n_tiles, part_a, part_b, n_tiles_max):
    steps_a = part_a[0].shape[0] // MOE_TD
    steps_b = part_b[0].shape[0] // MOE_TD
    in_a = lambda i, *_: (jnp.minimum(i, steps_a - 1), 0)
    in_b = lambda i, *_: (jnp.maximum(i - steps_a, 0), 0)
    specs = lambda f: [pl.BlockSpec((MOE_TD, D_MODEL), f), pl.BlockSpec((MOE_TD, LANE), f),
                       pl.BlockSpec((MOE_TD, LANE), f)]
    rows = n_tiles_max * MOE_TILE
    return pl.pallas_call(
        functools.partial(_dispatch_kernel, steps_a),
        grid_spec=pltpu.PrefetchScalarGridSpec(
            num_scalar_prefetch=5,
            grid=(steps_a + steps_b,),
            in_specs=specs(in_a) + specs(in_b),
            out_specs=[pl.BlockSpec(memory_space=pl.ANY),
                       pl.BlockSpec((MOE_TD, LANE), lambda i, *_: (i, 0))],
            scratch_shapes=[pltpu.VMEM((2, MOE_L, XS_W), BF16), pltpu.VMEM((MOE_TILE, XS_W), BF16),
                            pltpu.SemaphoreType.DMA((2,)), pltpu.SemaphoreType.DMA((2,))]),
        out_shape=[jax.ShapeDtypeStruct((rows, XS_W), BF16),
                   jax.ShapeDtypeStruct(((steps_a + steps_b) * MOE_TD, LANE), jnp.int32)],
        compiler_params=_cparams("arbitrary"),
        name="moe_dispatch",
    )(dst, n_chunks, tail_start, tail_len, n_tiles, *part_a, *part_b)


def _moe_gemm_kernel(t0_ref, t1_ref, nt_ref, wg_ref, wu_ref, wd_ref, xs_hbm, y_hbm,
                     wg_b, wu_b, wd_b, xbuf, ybuf, zbuf, isem, osem, zsem):
    e = pl.program_id(0)
    nt = nt_ref[0]
    n_tiles_max = y_hbm.shape[0] // MOE_TILE
    tile_rows = lambda t: pl.ds(pl.multiple_of(t * MOE_TILE, MOE_TILE), MOE_TILE)

    def load(t, s):
        return (pltpu.make_async_copy(xs_hbm.at[tile_rows(t), :], xbuf.at[s], isem.at[s]),)

    def store(t, s):
        return pltpu.make_async_copy(ybuf.at[s], y_hbm.at[tile_rows(t), :], osem.at[s])

    def zero_tile(t):
        return pltpu.make_async_copy(zbuf, y_hbm.at[tile_rows(t), :], zsem.at[0])

    @pl.when(e == 0)
    def _():
        for t in range(GEMM_RING - 1):
            @pl.when(t < nt)
            def _(t=t):
                for cp in load(t, t):
                    cp.start()
        zbuf[...] = jnp.zeros_like(zbuf)

        def fill(t, carry):
            zero_tile(t).start()
            return carry
        lax.fori_loop(nt, n_tiles_max, fill, 0)

    wg_b[...] = wg_ref[0].astype(BF16)
    wu_b[...] = wu_ref[0].astype(BF16)
    wd_b[...] = wd_ref[0].astype(BF16)

    def tile(t, carry):
        r = t % GEMM_RING
        s = t % 2
        for cp in load(t, r):
            cp.wait()

        @pl.when(t + (GEMM_RING - 1) < nt)
        def _():
            for cp in load(t + (GEMM_RING - 1), (t + (GEMM_RING - 1)) % GEMM_RING):
                cp.start()

        x = xbuf[r, :, 0:D_MODEL]
        wt = xbuf[r, :, D_MODEL:XS_W].astype(F32)
        w = wt[:, 0:1] + wt[:, 1:2] + wt[:, 2:3]
        a = _dot(x, wg_b[...])
        u = _dot(x, wu_b[...])
        he = (a * _sigmoid(a)) * u
        y = _dot(he.astype(BF16), wd_b[...]) * w

        @pl.when(t >= 2)
        def _():
            store(t - 2, s).wait()
        ybuf[s] = y.astype(BF16)
        store(t, s).start()
        return carry

    lax.fori_loop(t0_ref[e], t1_ref[e], tile, 0)

    @pl.when(e == pl.num_programs(0) - 1)
    def _():
        @pl.when(nt >= 1)
        def _():
            store(nt - 1, (nt - 1) % 2).wait()

        @pl.when(nt >= 2)
        def _():
            store(nt - 2, nt % 2).wait()

        def drain(t, carry):
            zero_tile(t).wait()
            return carry
        lax.fori_loop(nt, n_tiles_max, drain, 0)


def _moe_gemm(tile0, tile1, n_tiles, xs, w_gate, w_up, w_down):
    wmap = lambda e, *_: (e, 0, 0)
    return pl.pallas_call(
        _moe_gemm_kernel,
        grid_spec=pltpu.PrefetchScalarGridSpec(
            num_scalar_prefetch=3,
            grid=(N_EXPERTS,),
            in_specs=[pl.BlockSpec((1, D_MODEL, D_EXPERT), wmap),
                      pl.BlockSpec((1, D_MODEL, D_EXPERT), wmap),
                      pl.BlockSpec((1, D_EXPERT, D_MODEL), wmap),
                      pl.BlockSpec(memory_space=pl.ANY)],
            out_specs=pl.BlockSpec(memory_space=pl.ANY),
            scratch_shapes=[pltpu.VMEM((D_MODEL, D_EXPERT), BF16), pltpu.VMEM((D_MODEL, D_EXPERT), BF16),
                            pltpu.VMEM((D_EXPERT, D_MODEL), BF16),
                            pltpu.VMEM((GEMM_RING, MOE_TILE, XS_W), BF16),
                            pltpu.VMEM((2, MOE_TILE, D_MODEL), BF16), pltpu.VMEM((MOE_TILE, D_MODEL), BF16),
                            pltpu.SemaphoreType.DMA((GEMM_RING,)), pltpu.SemaphoreType.DMA((2,)),
                            pltpu.SemaphoreType.DMA((1,))]),
        out_shape=jax.ShapeDtypeStruct((xs.shape[0], D_MODEL), BF16),
        compiler_params=_cparams("arbitrary"),
        name="moe_gemm",
    )(tile0, tile1, n_tiles, w_gate, w_up, w_down, xs)


def _moe_combine_kernel(step0, dst_ref, nch_ref, lp_ref, x1_ref, mod_ref, gpost_ref, y_hbm, o_ref, ybuf, sem):
    i = pl.program_id(0)
    n = pl.num_programs(0)
    slot = i % 2

    def fetch(step, s):
        def body(c, carry):
            _chunk_copy(y_hbm, dst_ref[step, c], ybuf.at[s], c, sem.at[s]).start()
            return carry
        lax.fori_loop(0, nch_ref[step], body, 0)

    @pl.when(i == 0)
    def _():
        ybuf[...] = jnp.zeros_like(ybuf)
        fetch(step0, 0)

    @pl.when(i + 1 < n)
    def _():
        fetch(step0 + i + 1, 1 - slot)

    def wait(c, carry):
        _chunk_copy(y_hbm, 0, ybuf.at[slot], 0, sem.at[slot]).wait()
        return carry
    lax.fori_loop(0, nch_ref[step0 + i], wait, 0)

    lp = lp_ref[...]
    col = lax.broadcasted_iota(jnp.int32, (MOE_TD, MOE_L), 1)
    unperm = jnp.where((col == lp[:, 0:1]) | (col == lp[:, 1:2]), 1.0, 0.0).astype(BF16)
    moe = _dot(unperm, ybuf[slot])
    gt2 = mod_ref[0, 5:6, :]
    o_ref[...] = x1_ref[...] + gt2 * _rms(moe, gpost_ref[...])


def _moe_combine(step0, dst, n_chunks, lp, x1, mod, mod_row, y_sorted, g_post2):
    n = x1.shape[0]
    tok = lambda i, *_: (i, 0)
    return pl.pallas_call(
        functools.partial(_moe_combine_kernel, step0),
        grid_spec=pltpu.PrefetchScalarGridSpec(
            num_scalar_prefetch=2,
            grid=(n // MOE_TD,),
            in_specs=[pl.BlockSpec((MOE_TD, LANE), lambda i, *_: (step0 + i, 0)),
                      pl.BlockSpec((MOE_TD, D_MODEL), tok),
                      pl.BlockSpec((1, 6, D_MODEL), lambda i, *_: (mod_row(i), 0, 0)),
                      pl.BlockSpec((1, D_MODEL), lambda i, *_: (0, 0)),
                      pl.BlockSpec(memory_space=pl.ANY)],
            out_specs=pl.BlockSpec((MOE_TD, D_MODEL), tok),
            scratch_shapes=[pltpu.VMEM((2, MOE_L, D_MODEL), BF16), pltpu.SemaphoreType.DMA((2,))]),
        out_shape=jax.ShapeDtypeStruct((n, D_MODEL), F32),
        compiler_params=_cparams("arbitrary"),
        name="moe_combine",
    )(dst, n_chunks, lp, x1, mod, g_post2, y_sorted)


def _moe(part_a, part_b, mod, mod_row_a, mod_row_b, w_gate, w_up, w_down, g_post2):
    cnt = jnp.concatenate([part_a[4], part_b[4]], axis=0)[:, 0, :N_EXPERTS].astype(jnp.int32)
    steps_a = part_a[4].shape[0]
    n_tiles_max = (cnt.shape[0] * MOE_NCH + N_EXPERTS * (TILE_CH - 1)) // TILE_CH + 1
    dst, n_chunks, tail_start, tail_len, tile0, tile1, n_tiles = _plan(cnt, n_tiles_max)
    xs, lp = _dispatch(dst, n_chunks, tail_start, tail_len, n_tiles, part_a[1:4], part_b[1:4], n_tiles_max)
    y_sorted = _moe_gemm(tile0, tile1, n_tiles, xs, w_gate, w_up, w_down)
    ya = _moe_combine(0, dst, n_chunks, lp, part_a[0], mod, mod_row_a, y_sorted, g_post2)
    yb = _moe_combine(steps_a, dst, n_chunks, lp, part_b[0], mod, mod_row_b, y_sorted, g_post2)
    return ya, yb


def _rotate_half_cols(w):
    n = w.shape[-1]
    w4 = w.reshape(w.shape[:-1] + (n // 32, 2, 16))
    return jnp.stack([-w4[..., 1, :], w4[..., 0, :]], axis=-2).reshape(w.shape)


def _pad_lanes(w):
    return jnp.concatenate([w, jnp.zeros(w.shape[:-1] + (LANE - w.shape[-1],), w.dtype)], axis=-1)


def _rope_tables(t):
    rows = t // GRID_W
    n_freq = D_ROPE // 4
    freqs = ROPE_BASE ** (-jnp.arange(n_freq, dtype=F32) / n_freq)
    ang_r = jnp.arange(rows, dtype=F32)[:, None] * freqs
    ang_c = jnp.arange(GRID_W, dtype=F32)[:, None] * freqs
    per_row = lambda a: jnp.repeat(a, GRID_W, axis=0)
    per_col = lambda a: jnp.tile(a, (rows, 1))

    def table(fn):
        r, c = per_row(fn(ang_r)), per_col(fn(ang_c))
        return _pad_lanes(jnp.concatenate([r, r, c, c], axis=-1))
    return table(jnp.cos), table(jnp.sin)


def kernel(x_prompt, x_sample, cache_ckv, cache_krope, c, c_ctx, w_ada, b_ada, g_pre1, g_post1, g_pre2, g_post2, w_in, w_dw, b_dw, conv_ln_g, conv_ln_b, q_norm_g, kv_norm_g, w_uq, w_ukv, w_out, w_rg, b_rg, w_re, b_re, w_gate, w_up, w_down):
    nb, seq, d = x_prompt.shape
    db, dseq, _ = x_sample.shape
    l = 0

    cvec = jnp.concatenate([c_ctx[None, :], c, jnp.zeros((8 - 1 - db, d), F32)], axis=0)
    mod = _ada(cvec, w_ada[l], b_ada[l]).reshape(8, 6, d)

    o_kr = 2 * C_CONV + Q_LORA + KV_LORA
    w_in_l = w_in[l]
    w_kr = w_in_l[:, o_kr:]
    w_in_ctx = jnp.concatenate([w_in_l[:, :o_kr], _pad_lanes(w_kr)], axis=-1).astype(BF16)
    w_in_lat = jnp.concatenate([w_in_l[:, :o_kr], _pad_lanes(w_kr), _pad_lanes(_rotate_half_cols(w_kr))],
                               axis=-1).astype(BF16)
    wuq = w_uq[l].reshape(Q_LORA, N_HEADS, D_NOPE + D_ROPE)
    wq_rope = wuq[:, :, D_NOPE:]
    wq = jnp.concatenate([wuq[:, :, :D_NOPE], _pad_lanes(wq_rope)], axis=-1)
    wq = wq.reshape(Q_LORA, N_HEADS * HEAD_W).astype(BF16)
    wqs = _pad_lanes(_rotate_half_cols(wq_rope)).reshape(Q_LORA, N_HEADS * LANE).astype(BF16)
    wukv4 = w_ukv[l].reshape(KV_LORA, N_HEADS, D_NOPE + D_V)
    wukv = jnp.concatenate([wukv4[:, :, :D_NOPE].reshape(KV_LORA, N_HEADS * D_NOPE),
                            wukv4[:, :, D_NOPE:].reshape(KV_LORA, N_HEADS * D_V)], axis=-1).astype(BF16)
    w_out_b = w_out[l].astype(BF16)
    w_r = jnp.concatenate([w_re[l], w_rg[l], jnp.zeros((d, LANE - N_EXPERTS - N_GROUPS), F32)], axis=-1)
    wr_hi = w_r.astype(BF16)
    wr_lo = jnp.concatenate([wr_hi[:, :LANE // 2], (w_r - wr_hi.astype(F32)).astype(BF16)[:, :LANE // 2]], axis=-1)
    b_r = jnp.concatenate([b_re[l], b_rg[l], jnp.zeros((LANE - N_EXPERTS - N_GROUPS,), F32)])[None, :]
    cos, sin = _rope_tables(dseq)
    row = lambda v: v[l][None, :]

    tm_c = 512
    xp_flat = x_prompt.reshape(1, nb * seq, d)
    conv_w = (w_dw[l], row(b_dw), row(conv_ln_g), row(conv_ln_b))
    assert seq == CONV_TB, "a context step must hold whole sequences of CONV_TB tokens"
    conv_out, q, k, v, ckv, kr = _mix_in(xp_flat, mod, 0, row(g_pre1), w_in_ctx, row(q_norm_g), row(kv_norm_g),
                                         wq, wqs, wukv, cos[:tm_c], sin[:tm_c], conv_w, False, tm_c)
    per_seq = lambda a: a.reshape(nb, seq, a.shape[-1])
    att = _attn(per_seq(q), per_seq(k), per_seq(v), n_seq=4)
    flat = lambda a: a.reshape(1, nb * seq, a.shape[-1])
    post_c = _post(xp_flat, conv_out, flat(att), mod, 0, w_out_b, row(g_post1), row(g_pre2),
                   wr_hi, wr_lo, b_r)
    state_ckv = ckv.reshape(nb, 1, seq, KV_LORA)
    state_krope = kr.reshape(nb, 1, seq, D_ROPE)

    tm_s = 512
    kc, vc = _cache_kv(cache_ckv[:, l], cache_krope[:, l], wukv)
    conv_out, q, k, v, _, _ = _mix_in(x_sample, mod, 1, row(g_pre1), w_in_lat, row(q_norm_g), row(kv_norm_g),
                                      wq, wqs, wukv, cos, sin, conv_w, True, tm_s)
    att = _attn(q, k, v, kc, vc)
    post_s = _post(x_sample, conv_out, att, mod, 1, w_out_b, row(g_post1), row(g_pre2), wr_hi, wr_lo, b_r)

    tokens = lambda parts: tuple(a.reshape((-1,) + a.shape[2:]) for a in parts)
    steps_per_req = dseq // MOE_TD
    yp, ys = _moe(tokens(post_c), tokens(post_s), mod, lambda i: 0, lambda i: 1 + i // steps_per_req,
                  w_gate[l], w_up[l], w_down[l], row(g_post2))

    return (yp.reshape(nb, seq, d), ys.reshape(db, dseq, d), state_ckv, state_krope)
```

```python
import functools

import jax
import jax.numpy as jnp
import numpy as np
from jax import lax
from jax.experimental import pallas as pl
from jax.experimental.pallas import tpu as pltpu

D_MODEL = 1024
GRID_W = 64
C_CONV = 512
CONV_K = 31
N_HEADS = 4
D_NOPE = 128
D_ROPE = 64
D_V = 128
Q_LORA = 384
KV_LORA = 256
N_GROUPS = 4
E_PER_GROUP = 8
N_EXPERTS = 32
D_EXPERT = 256
ROPE_BASE = 10000.0
EPS = 1e-6
ATT_SCALE = (D_NOPE + D_ROPE) ** -0.5
LOG2E = 1.4426950408889634

LANE = 128
SUBLANE = 8
HEAD_W = 2 * LANE
CONV_HALO = 16
MOE_TD = 512
CHUNK = 2 * SUBLANE
MOE_NCH = (2 * MOE_TD + N_EXPERTS * (CHUNK - 1)) // CHUNK
MOE_L = MOE_NCH * CHUNK
MOE_TILE = 256
TILE_CH = MOE_TILE // CHUNK
GEMM_RING = 6
GEMM_OUT = 4
XS_W = D_MODEL + LANE
VMEM_LIMIT = 56 * 1024 * 1024

BF16 = jnp.bfloat16
F32 = jnp.float32


def _cparams(*sem):
    return pltpu.CompilerParams(dimension_semantics=sem, vmem_limit_bytes=VMEM_LIMIT)


def _rms(x, g):
    return x * lax.rsqrt(jnp.mean(x * x, axis=-1, keepdims=True) + EPS) * g


def _sigmoid(x):
    return 1.0 / (1.0 + jnp.exp(-x))


def _dot(a, b):
    return jnp.dot(a, b, preferred_element_type=F32)


def _chunk_copy(src, src_chunk, dst, dst_chunk, sem):
    rows = lambda c: pl.ds(c * CHUNK if isinstance(c, int) else pl.multiple_of(c * CHUNK, CHUNK), CHUNK)
    return pltpu.make_async_copy(src.at[rows(src_chunk), :], dst.at[rows(dst_chunk), :], sem)


def _ada_kernel(c_ref, w_ref, b_ref, o_ref):
    c = c_ref[...]
    s = (c * _sigmoid(c)).astype(BF16)
    o_ref[...] = _dot(s, w_ref[...].astype(BF16)) + b_ref[...]


def _ada(cvec, w_ada, b_ada):
    n = w_ada.shape[1]
    tn = 1536
    return pl.pallas_call(
        _ada_kernel,
        grid=(n // tn,),
        in_specs=[pl.BlockSpec((8, D_MODEL), lambda j: (0, 0)),
                  pl.BlockSpec((D_MODEL, tn), lambda j: (0, j)),
                  pl.BlockSpec((1, tn), lambda j: (0, j))],
        out_specs=pl.BlockSpec((8, tn), lambda j: (0, j)),
        out_shape=jax.ShapeDtypeStruct((8, n), F32),
        compiler_params=_cparams("arbitrary"),
        name="ada",
    )(cvec, w_ada, b_ada.reshape(1, n))


CONV_TB = 256
CONV_WIN = CONV_TB + 2 * CONV_HALO
CONV_TT = 32


def _conv_window(win_ref, sh_ref, w_ref, b_ref, g_ref, bb_ref, o_ref, row0):
    first = CONV_HALO - CONV_K // 2
    rows = CONV_WIN - SUBLANE
    win = win_ref[...]
    for s in range(SUBLANE):
        sh_ref[s, 0:rows, :] = win[s:s + rows, :]
    for c in range(CONV_TB // CONV_TT):
        acc = jnp.zeros((CONV_TT, C_CONV), F32)
        for k in range(CONV_K):
            off = first + k
            r0 = c * CONV_TT + off // SUBLANE * SUBLANE
            acc = acc + sh_ref[off % SUBLANE, r0:r0 + CONV_TT, :] * w_ref[k:k + 1, :]
        y = acc + b_ref[...]
        mu = jnp.mean(y, axis=-1, keepdims=True)
        yc = y - mu
        var = jnp.mean(yc * yc, axis=-1, keepdims=True)
        z = yc * lax.rsqrt(var + EPS) * g_ref[...] + bb_ref[...]
        o_ref[0, row0 + c * CONV_TT:row0 + (c + 1) * CONV_TT, :] = (z * _sigmoid(z)).astype(BF16)


def _mix_in_kernel(latent, x_ref, xp_ref, xn_ref, mod_ref, gpre_ref, win_ref, qg_ref, kvg_ref, wq_ref, wqs_ref,
                   wukv_ref, cos_ref, sin_ref, wdw_ref, bdw_ref, lng_ref, lnb_ref,
                   conv_ref, q_ref, k_ref, v_ref, ckv_ref, kr_ref, cwin, csh):
    rope = latent
    x = x_ref[0]
    tm = x.shape[0]
    sh1 = mod_ref[0, 0:1, :]
    sc1 = mod_ref[0, 1:2, :]
    modulate = lambda v: _rms(v, gpre_ref[...]) * (1.0 + sc1) + sh1
    glu = lambda uc: uc[:, :C_CONV] * _sigmoid(uc[:, C_CONV:2 * C_CONV])
    u = _dot(modulate(x).astype(BF16), win_ref[...])
    hglu = glu(u)

    zeros = jnp.zeros((CONV_HALO, C_CONV), F32)
    n_win = tm // CONV_TB
    if latent:
        i = pl.program_id(1)
        xh = jnp.concatenate([xp_ref[0], xn_ref[0]], axis=0)
        hh = glu(_dot(modulate(xh).astype(BF16), win_ref[:, 0:2 * C_CONV]))
        before = jnp.where(i > 0, hh[0:CONV_HALO], zeros)
        after = jnp.where(i < pl.num_programs(1) - 1, hh[CONV_HALO:2 * CONV_HALO], zeros)
    for j in range(n_win):
        if latent:
            lo, hi, d0 = j * CONV_TB - CONV_HALO, j * CONV_TB + CONV_TB + CONV_HALO, 0
            if lo < 0:
                cwin[j, 0:CONV_HALO, :] = before
                lo, d0 = 0, CONV_HALO
            if hi > tm:
                cwin[j, CONV_WIN - CONV_HALO:CONV_WIN, :] = after
                hi = tm
            cwin[j, d0:d0 + hi - lo, :] = hglu[lo:hi, :]
        else:
            cwin[j, 0:CONV_HALO, :] = zeros
            cwin[j, CONV_HALO:CONV_HALO + CONV_TB, :] = hglu[j * CONV_TB:(j + 1) * CONV_TB, :]
            cwin[j, CONV_HALO + CONV_TB:CONV_WIN, :] = zeros
    for j in range(n_win):
        _conv_window(cwin.at[j], csh, wdw_ref, bdw_ref, lng_ref, lnb_ref, conv_ref, j * CONV_TB)

    o_q = 2 * C_CONV
    o_kv = o_q + Q_LORA
    o_kr = o_kv + KV_LORA
    qn = _rms(u[:, o_q:o_kv], qg_ref[...]).astype(BF16)
    qf = _dot(qn, wq_ref[...])
    ckv = _rms(u[:, o_kv:o_kr], kvg_ref[...])
    ckv_ref[0] = ckv
    kvd = _dot(ckv.astype(BF16), wukv_ref[...])
    kr = u[:, o_kr:o_kr + LANE]
    kr_ref[0] = kr[:, :D_ROPE]
    if rope:
        cos = cos_ref[...]
        sin = sin_ref[...]
        qs = _dot(qn, wqs_ref[...])
        kr = kr * cos + u[:, o_kr + LANE:o_kr + 2 * LANE] * sin
    q_parts = []
    k_parts = []
    for hd in range(N_HEADS):
        q_parts.append(qf[:, hd * HEAD_W:hd * HEAD_W + LANE])
        qr = qf[:, hd * HEAD_W + LANE:(hd + 1) * HEAD_W]
        if rope:
            qr = qr * cos + qs[:, hd * LANE:(hd + 1) * LANE] * sin
        q_parts.append(qr)
        k_parts.append(kvd[:, hd * D_NOPE:(hd + 1) * D_NOPE])
        k_parts.append(kr)
    q_ref[0] = jnp.concatenate(q_parts, axis=-1).astype(BF16)
    k_ref[0] = jnp.concatenate(k_parts, axis=-1).astype(BF16)
    v_ref[0] = kvd[:, N_HEADS * D_NOPE:].astype(BF16)


def _mix_in(x, mod, mod_row0, g_pre1, w_in_ext, q_norm_g, kv_norm_g, wq, wqs, wukv, cos, sin, conv_w, latent, tm):
    b, t, _ = x.shape
    ncol = w_in_ext.shape[1]
    const = lambda bi, i: (0, 0)
    tok = lambda bi, i: (bi, i, 0)
    table = (lambda bi, i: (i, 0)) if latent else const
    halo_blocks = tm // CONV_HALO
    prev_rows = lambda bi, i: (bi, jnp.maximum(i * halo_blocks - 1, 0), 0)
    next_rows = lambda bi, i: (bi, jnp.minimum((i + 1) * halo_blocks, t // CONV_HALO - 1), 0)
    outs = [(C_CONV, BF16), (N_HEADS * HEAD_W, BF16), (N_HEADS * HEAD_W, BF16), (N_HEADS * D_V, BF16),
            (KV_LORA, F32), (D_ROPE, F32)]
    return pl.pallas_call(
        functools.partial(_mix_in_kernel, latent),
        grid=(b, t // tm),
        in_specs=[pl.BlockSpec((1, tm, D_MODEL), tok),
                  pl.BlockSpec((1, CONV_HALO, D_MODEL), prev_rows),
                  pl.BlockSpec((1, CONV_HALO, D_MODEL), next_rows),
                  pl.BlockSpec((1, 6, D_MODEL), lambda bi, i: (mod_row0 + bi, 0, 0)),
                  pl.BlockSpec((1, D_MODEL), const),
                  pl.BlockSpec((D_MODEL, ncol), const),
                  pl.BlockSpec((1, Q_LORA), const),
                  pl.BlockSpec((1, KV_LORA), const),
                  pl.BlockSpec((Q_LORA, N_HEADS * HEAD_W), const),
                  pl.BlockSpec((Q_LORA, N_HEADS * LANE), const),
                  pl.BlockSpec((KV_LORA, N_HEADS * (D_NOPE + D_V)), const),
                  pl.BlockSpec((tm, LANE), table),
                  pl.BlockSpec((tm, LANE), table),
                  pl.BlockSpec((CONV_K, C_CONV), const),
                  pl.BlockSpec((1, C_CONV), const),
                  pl.BlockSpec((1, C_CONV), const),
                  pl.BlockSpec((1, C_CONV), const)],
        out_specs=[pl.BlockSpec((1, tm, w), tok) for w, _ in outs],
        out_shape=[jax.ShapeDtypeStruct((b, t, w), dt) for w, dt in outs],
        scratch_shapes=[pltpu.VMEM((tm // CONV_TB, CONV_WIN, C_CONV), F32),
                        pltpu.VMEM((SUBLANE, CONV_WIN, C_CONV), F32)],
        compiler_params=_cparams("arbitrary", "arbitrary"),
        name="mix_in_latent" if latent else "mix_in",
    )(x, x, x, mod, g_pre1, w_in_ext, q_norm_g, kv_norm_g, wq, wqs, wukv, cos, sin, *conv_w)


def _qk(q, k):
    return lax.dot_general(q, k, (((1,), (1,)), ((), ())), preferred_element_type=F32)


ATT_KCHUNK = 256


def _attn_kernel(cached, q_ref, k_ref, v_ref, *rest):
    if cached:
        kc_ref, vc_ref, o_ref, s_scr = rest
        sources = [(kc_ref, vc_ref), (k_ref, v_ref)]
    else:
        o_ref, s_scr = rest
        sources = [(k_ref, v_ref)]
    n_seq, tq, _ = q_ref.shape
    chunks = []
    for kr, vr in sources:
        for c0 in range(0, kr.shape[1], ATT_KCHUNK):
            c1 = min(c0 + ATT_KCHUNK, kr.shape[1])
            col = chunks[-1][4] + chunks[-1][3] - chunks[-1][2] if chunks else 0
            chunks.append((kr, vr, c0, c1, col))
    for b in range(n_seq):
        outs = []
        for hd in range(N_HEADS):
            ks = slice(hd * HEAD_W, (hd + 1) * HEAD_W)
            vs = slice(hd * D_V, (hd + 1) * D_V)
            qh = q_ref[b, :, ks]
            m = jnp.full((tq, 1), -jnp.inf, F32)
            for kr, vr, c0, c1, col in chunks:
                s = _qk(qh, kr[b, c0:c1, ks]) * (ATT_SCALE * LOG2E)
                s_scr[hd, :, col:col + c1 - c0] = s
                m = jnp.maximum(m, jnp.max(s, axis=-1, keepdims=True))
            l = jnp.zeros((tq, 1), F32)
            o = jnp.zeros((tq, D_V), F32)
            for kr, vr, c0, c1, col in chunks:
                p = jnp.exp2(s_scr[hd, :, col:col + c1 - c0] - m)
                l = l + jnp.sum(p, axis=-1, keepdims=True)
                o = o + _dot(p.astype(BF16), vr[b, c0:c1, vs])
            outs.append(o / l)
        o_ref[b] = jnp.concatenate(outs, axis=-1).astype(BF16)


def _attn(q, k, v, kc=None, vc=None, tq=256, n_seq=1):
    b, t, _ = q.shape
    s = k.shape[1]
    cached = kc is not None
    whole = lambda bi, i: (bi, 0, 0)
    in_specs = [pl.BlockSpec((n_seq, tq, N_HEADS * HEAD_W), lambda bi, i: (bi, i, 0)),
                pl.BlockSpec((n_seq, s, N_HEADS * HEAD_W), whole),
                pl.BlockSpec((n_seq, s, N_HEADS * D_V), whole)]
    args = [q, k, v]
    if cached:
        sc = kc.shape[1]
        in_specs += [pl.BlockSpec((n_seq, sc, N_HEADS * HEAD_W), whole),
                     pl.BlockSpec((n_seq, sc, N_HEADS * D_V), whole)]
        args += [kc, vc]
        s += sc
    return pl.pallas_call(
        functools.partial(_attn_kernel, cached),
        grid=(b // n_seq, t // tq),
        in_specs=in_specs,
        out_specs=pl.BlockSpec((n_seq, tq, N_HEADS * D_V), lambda bi, i: (bi, i, 0)),
        out_shape=jax.ShapeDtypeStruct((b, t, N_HEADS * D_V), BF16),
        scratch_shapes=[pltpu.VMEM((N_HEADS, tq, s), F32)],
        compiler_params=_cparams("arbitrary", "arbitrary"),
        name="attn_cached" if cached else "attn",
    )(*args)


def _cache_kv_kernel(ckv_ref, kr_ref, wukv_ref, k_ref, v_ref):
    kvd = _dot(ckv_ref[0].astype(BF16), wukv_ref[...])
    kr = kr_ref[0]
    kr = jnp.concatenate([kr, jnp.zeros_like(kr)], axis=-1)
    parts = []
    for hd in range(N_HEADS):
        parts.append(kvd[:, hd * D_NOPE:(hd + 1) * D_NOPE])
        parts.append(kr)
    k_ref[0] = jnp.concatenate(parts, axis=-1).astype(BF16)
    v_ref[0] = kvd[:, N_HEADS * D_NOPE:].astype(BF16)


def _cache_kv(ckv, krope, wukv):
    b, s, _ = ckv.shape
    tok = lambda bi: (bi, 0, 0)
    return pl.pallas_call(
        _cache_kv_kernel,
        grid=(b,),
        in_specs=[pl.BlockSpec((1, s, KV_LORA), tok),
                  pl.BlockSpec((1, s, D_ROPE), tok),
                  pl.BlockSpec((KV_LORA, N_HEADS * (D_NOPE + D_V)), lambda bi: (0, 0))],
        out_specs=[pl.BlockSpec((1, s, N_HEADS * HEAD_W), tok),
                   pl.BlockSpec((1, s, N_HEADS * D_V), tok)],
        out_shape=[jax.ShapeDtypeStruct((b, s, N_HEADS * HEAD_W), BF16),
                   jax.ShapeDtypeStruct((b, s, N_HEADS * D_V), BF16)],
        compiler_params=_cparams("arbitrary"),
        name="cache_kv",
    )(ckv, krope, wukv)


def _post_kernel(x_ref, conv_ref, att_ref, mod_ref, wo_ref, gpost_ref, gpre2_ref, wr_hi_ref, wr_cat_ref,
                 br_ref, x1_ref, h2_ref, ri_ref, rw_ref, cnt_ref):
    out = _dot(conv_ref[0], wo_ref[:C_CONV, :]) + _dot(att_ref[0], wo_ref[C_CONV:, :])
    gt1 = mod_ref[0, 2:3, :]
    sh2 = mod_ref[0, 3:4, :]
    sc2 = mod_ref[0, 4:5, :]
    x1 = x_ref[0] + gt1 * _rms(out, gpost_ref[...])
    x1_ref[0] = x1
    h2 = _rms(x1, gpre2_ref[...]) * (1.0 + sc2) + sh2
    h_hi = h2.astype(BF16)
    h2_ref[0] = h_hi
    h_lo = (h2 - h_hi.astype(F32)).astype(BF16)
    hi_terms = _dot(h_hi, wr_cat_ref[...])
    logits = hi_terms + pltpu.roll(hi_terms, LANE // 2, axis=1) + _dot(h_lo, wr_hi_ref[...]) + br_ref[...]

    lane = lax.broadcasted_iota(jnp.int32, logits.shape, 1)
    neg = jnp.float32(-jnp.inf)
    big = jnp.int32(LANE)
    is_g = (lane >= N_EXPERTS) & (lane < N_EXPERTS + N_GROUPS)
    lg = jnp.where(is_g, logits, neg)
    gmax = jnp.max(lg, axis=-1, keepdims=True)
    gidx = jnp.min(jnp.where(lg == gmax, lane, big), axis=-1, keepdims=True) - N_EXPERTS
    g_top = 1.0 / jnp.sum(jnp.exp(lg - gmax), axis=-1, keepdims=True)

    in_grp = (lane >= gidx * E_PER_GROUP) & (lane < (gidx + 1) * E_PER_GROUP)
    le = jnp.where(in_grp, logits, neg)
    m1 = jnp.max(le, axis=-1, keepdims=True)
    i1 = jnp.min(jnp.where(le == m1, lane, big), axis=-1, keepdims=True)
    le2 = jnp.where(lane == i1, neg, le)
    m2 = jnp.max(le2, axis=-1, keepdims=True)
    i2 = jnp.min(jnp.where(le2 == m2, lane, big), axis=-1, keepdims=True)
    r = jnp.exp(m2 - m1)
    w1 = g_top / (1.0 + r)
    w2 = g_top * r / (1.0 + r)
    ri_ref[0] = jnp.where(lane == 0, i1, jnp.where(lane == 1, i2, 0))
    rw_ref[0] = jnp.where(lane == 0, w1, jnp.where(lane == 1, w2, 0.0))
    pairs = jnp.sum(jnp.where((lane == i1) | (lane == i2), 1.0, 0.0), axis=0, keepdims=True)
    cnt_ref[0, 0] = jnp.broadcast_to(pairs, (SUBLANE, LANE))


def _post(x, conv_out, att, mod, mod_row0, w_out, g_post1, g_pre2, wr_hi, wr_lo, br):
    b, t, _ = x.shape
    tm = MOE_TD
    const = lambda bi, i: (0, 0)
    tok = lambda bi, i: (bi, i, 0)
    return pl.pallas_call(
        _post_kernel,
        grid=(b, t // tm),
        in_specs=[pl.BlockSpec((1, tm, D_MODEL), tok),
                  pl.BlockSpec((1, tm, C_CONV), tok),
                  pl.BlockSpec((1, tm, N_HEADS * D_V), tok),
                  pl.BlockSpec((1, 6, D_MODEL), lambda bi, i: (mod_row0 + bi, 0, 0)),
                  pl.BlockSpec((D_MODEL, D_MODEL), const),
                  pl.BlockSpec((1, D_MODEL), const),
                  pl.BlockSpec((1, D_MODEL), const),
                  pl.BlockSpec((D_MODEL, LANE), const),
                  pl.BlockSpec((D_MODEL, LANE), const),
                  pl.BlockSpec((1, LANE), const)],
        out_specs=[pl.BlockSpec((1, tm, D_MODEL), tok),
                   pl.BlockSpec((1, tm, D_MODEL), tok),
                   pl.BlockSpec((1, tm, LANE), tok),
                   pl.BlockSpec((1, tm, LANE), tok),
                   pl.BlockSpec((1, 1, SUBLANE, LANE), lambda bi, i: (bi, i, 0, 0))],
        out_shape=[jax.ShapeDtypeStruct((b, t, D_MODEL), F32),
                   jax.ShapeDtypeStruct((b, t, D_MODEL), BF16),
                   jax.ShapeDtypeStruct((b, t, LANE), jnp.int32),
                   jax.ShapeDtypeStruct((b, t, LANE), F32),
                   jax.ShapeDtypeStruct((b, t // tm, SUBLANE, LANE), F32)],
        compiler_params=_cparams("arbitrary", "arbitrary"),
        name="post",
    )(x, conv_out, att, mod, w_out, g_post1, g_pre2, wr_hi, wr_lo, br)


def _one_hots(ri):
    lane = lax.broadcasted_iota(jnp.int32, ri.shape, 1)
    oh1 = lane == ri[:, 0:1]
    oh2 = lane == ri[:, 1:2]
    return oh1, oh2, jnp.where(oh1 | oh2, 1.0, 0.0)


def _local_positions(oh1, oh2, oh):
    td = oh.shape[0]
    cnt = jnp.sum(oh, axis=0, keepdims=True)
    nch = jnp.floor((cnt + (CHUNK - 1)) * (1.0 / CHUNK))
    a = lax.broadcasted_iota(jnp.int32, (LANE, LANE), 0)
    b = lax.broadcasted_iota(jnp.int32, (LANE, LANE), 1)
    lower_experts = jnp.where(a < b, 1.0, 0.0).astype(BF16)
    run_start = _dot(jnp.broadcast_to(nch, (SUBLANE, LANE)).astype(BF16), lower_experts)[0:1, :] * CHUNK
    rr = lax.broadcasted_iota(jnp.int32, (td, td), 0)
    cc = lax.broadcasted_iota(jnp.int32, (td, td), 1)
    earlier = jnp.where(cc < rr, 1.0, 0.0).astype(BF16)
    pos = _dot(earlier, oh.astype(BF16)) + run_start
    lp1 = jnp.sum(jnp.where(oh1, pos, 0.0), axis=-1, keepdims=True)
    lp2 = jnp.sum(jnp.where(oh2, pos, 0.0), axis=-1, keepdims=True)
    return lp1, lp2


def _plan(cnt, n_tiles_max):
    k = (cnt + (CHUNK - 1)) // CHUNK
    run_end = jnp.cumsum(k, axis=1)
    run_start = run_end - k
    n_chunks = run_end[:, -1]
    total = jnp.sum(k, axis=0)
    padded = (total + (TILE_CH - 1)) // TILE_CH * TILE_CH
    seg_end = jnp.cumsum(padded)
    seg_start = seg_end - padded
    base = seg_start[None, :] + jnp.cumsum(k, axis=0) - k
    c = jnp.arange(MOE_NCH, dtype=jnp.int32)[None, :, None]
    in_run = (run_start[:, None, :] <= c) & (c < run_end[:, None, :])
    dst = jnp.sum(jnp.where(in_run, base[:, None, :] + c - run_start[:, None, :], 0), axis=2)
    n_tiles = seg_end[-1] // TILE_CH
    tail_start = seg_start + total
    tail_len = padded - total
    i32 = lambda t: t.astype(jnp.int32)
    return (i32(dst), i32(n_chunks), i32(tail_start), i32(tail_len), i32(seg_start // TILE_CH),
            i32(seg_end // TILE_CH), i32(n_tiles.reshape(1)))


def _dispatch_kernel(steps_a, dst_ref, nch_ref, ts_ref, tl_ref, nt_ref, h2a_ref, ria_ref, rwa_ref, h2b_ref, rib_ref,
                     rwb_ref, xs_hbm, lp_ref, xbuf, zx, sem, zsem):
    i = pl.program_id(0)
    n = pl.num_programs(0)
    slot = i % 2
    td = MOE_TD
    n_tiles_max = xs_hbm.shape[0] // MOE_TILE

    def zero_tile(m):
        rows = pl.ds(pl.multiple_of(m * MOE_TILE, MOE_TILE), MOE_TILE)
        return pltpu.make_async_copy(zx, xs_hbm.at[rows, :], zsem.at[1])

    @pl.when(i == 0)
    def _():
        zx[...] = jnp.zeros_like(zx)
        for e in range(N_EXPERTS):
            def fill(m, carry, e=e):
                _chunk_copy(zx, 0, xs_hbm, ts_ref[e] + m, zsem.at[0]).start()
                return carry
            lax.fori_loop(0, tl_ref[e], fill, 0)

        def fill_tile(m, carry):
            zero_tile(m).start()
            return carry
        lax.fori_loop(nt_ref[0], n_tiles_max, fill_tile, 0)

    from_a = i < steps_a
    ri = jnp.where(from_a, ria_ref[...], rib_ref[...])
    rw = jnp.where(from_a, rwa_ref[...], rwb_ref[...])
    h2 = jnp.where(from_a, h2a_ref[...], h2b_ref[...])
    oh1, oh2, oh = _one_hots(ri)
    lp1, lp2 = _local_positions(oh1, oh2, oh)
    lane = lax.broadcasted_iota(jnp.int32, ri.shape, 1)
    lp_ref[...] = jnp.where(lane == 0, lp1, jnp.where(lane == 1, lp2, 0.0)).astype(jnp.int32)

    rr = lax.broadcasted_iota(jnp.int32, (td, td), 0)
    cc = lax.broadcasted_iota(jnp.int32, (td, td), 1)
    to_lanes = lambda col: jnp.sum(jnp.where(rr == cc, col, 0.0), axis=0, keepdims=True)
    row = lax.broadcasted_iota(jnp.int32, (MOE_L, td), 0).astype(F32)
    p1 = row == to_lanes(lp1)
    p2 = row == to_lanes(lp2)
    perm = jnp.where(p1 | p2, 1.0, 0.0).astype(BF16)
    xbuf[slot, :, 0:D_MODEL] = _dot(perm, h2).astype(BF16)
    w = jnp.sum(jnp.where(p1, to_lanes(rw[:, 0:1]), 0.0) + jnp.where(p2, to_lanes(rw[:, 1:2]), 0.0),
                axis=-1, keepdims=True)
    w0 = w.astype(BF16).astype(F32)
    w1 = (w - w0).astype(BF16).astype(F32)
    w2 = (w - w0 - w1).astype(BF16).astype(F32)
    wl = lax.broadcasted_iota(jnp.int32, (MOE_L, LANE), 1)
    terms = jnp.where(wl == 0, w0, jnp.where(wl == 1, w1, jnp.where(wl == 2, w2, 0.0)))
    xbuf[slot, :, D_MODEL:XS_W] = terms.astype(BF16)

    def wait_chunks(count, s):
        def body(c, carry):
            _chunk_copy(xbuf.at[s], 0, xs_hbm, 0, sem.at[s]).wait()
            return carry
        lax.fori_loop(0, count, body, 0)

    @pl.when(i > 0)
    def _():
        wait_chunks(nch_ref[i - 1], 1 - slot)

    def send(c, carry):
        _chunk_copy(xbuf.at[slot], c, xs_hbm, dst_ref[i, c], sem.at[slot]).start()
        return carry
    lax.fori_loop(0, nch_ref[i], send, 0)

    @pl.when(i == n - 1)
    def _():
        wait_chunks(nch_ref[i], slot)
        for e in range(N_EXPERTS):
            def drain(m, carry):
                _chunk_copy(zx, 0, xs_hbm, 0, zsem.at[0]).wait()
                return carry
            lax.fori_loop(0, tl_ref[e], drain, 0)

        def drain_tile(m, carry):
            zero_tile(m).wait()
            return carry
        lax.fori_loop(nt_ref[0], n_tiles_max, drain_tile, 0)


def _dispatch(dst, n_chunks, tail_start, tail_len, n_tiles, part_a, part_b, n_tiles_max):
    steps_a = part_a[0].shape[0] // MOE_TD
    steps_b = part_b[0].shape[0] // MOE_TD
    in_a = lambda i, *_: (jnp.minimum(i, steps_a - 1), 0)
    in_b = lambda i, *_: (jnp.maximum(i - steps_a, 0), 0)
    specs = lambda f: [pl.BlockSpec((MOE_TD, D_MODEL), f), pl.BlockSpec((MOE_TD, LANE), f),
                       pl.BlockSpec((MOE_TD, LANE), f)]
    rows = n_tiles_max * MOE_TILE
    return pl.pallas_call(
        functools.partial(_dispatch_kernel, steps_a),
        grid_spec=pltpu.PrefetchScalarGridSpec(
            num_scalar_prefetch=5,
            grid=(steps_a + steps_b,),
            in_specs=specs(in_a) + specs(in_b),
            out_specs=[pl.BlockSpec(memory_space=pl.ANY),
                       pl.BlockSpec((MOE_TD, LANE), lambda i, *_: (i, 0))],
            scratch_shapes=[pltpu.VMEM((2, MOE_L, XS_W), BF16), pltpu.VMEM((MOE_TILE, XS_W), BF16),
                            pltpu.SemaphoreType.DMA((2,)), pltpu.SemaphoreType.DMA((2,))]),
        out_shape=[jax.ShapeDtypeStruct((rows, XS_W), BF16),
                   jax.ShapeDtypeStruct(((steps_a + steps_b) * MOE_TD, LANE), jnp.int32)],
        compiler_params=_cparams("arbitrary"),
        name="moe_dispatch",
    )(dst, n_chunks, tail_start, tail_len, n_tiles, *part_a, *part_b)


def _moe_gemm_kernel(t0_ref, t1_ref, nt_ref, wg_ref, wu_ref, wd_ref, xs_hbm, y_hbm,
                     wg_b, wu_b, wd_b, xbuf, ybuf, zbuf, isem, osem, zsem, issued):
    e = pl.program_id(0)
    nt = nt_ref[0]
    n_tiles_max = y_hbm.shape[0] // MOE_TILE
    tile_rows = lambda t: pl.ds(pl.multiple_of(t * MOE_TILE, MOE_TILE), MOE_TILE)

    def load(t, s):
        return (pltpu.make_async_copy(xs_hbm.at[tile_rows(t), :], xbuf.at[s], isem.at[s]),)

    def store(t, s):
        return pltpu.make_async_copy(ybuf.at[s], y_hbm.at[tile_rows(t), :], osem.at[s])

    def zero_tile(t):
        return pltpu.make_async_copy(zbuf, y_hbm.at[tile_rows(t), :], zsem.at[0])

    @pl.when(e == 0)
    def _():
        issued[0] = 0
        zbuf[...] = jnp.zeros_like(zbuf)

        def fill(t, carry):
            zero_tile(t).start()
            return carry
        lax.fori_loop(nt, n_tiles_max, fill, 0)

    wg_b[...] = wg_ref[0].astype(BF16)
    wu_b[...] = wu_ref[0].astype(BF16)
    wd_b[...] = wd_ref[0].astype(BF16)

    def top_up(t):
        upto = jnp.minimum(t + GEMM_RING, nt)

        def start(u, carry):
            for cp in load(u, u % GEMM_RING):
                cp.start()
            return carry
        lax.fori_loop(issued[0], upto, start, 0)
        issued[0] = jnp.maximum(issued[0], upto)

    def process(t, n):
        top_up(t)
        for k in range(n):
            for cp in load(t + k, (t + k) % GEMM_RING):
                cp.wait()
        rows = [xbuf[(t + k) % GEMM_RING] for k in range(n)]
        xw = rows[0] if n == 1 else jnp.concatenate(rows, axis=0)
        x = xw[:, 0:D_MODEL]
        wt = xw[:, D_MODEL:XS_W].astype(F32)
        w = wt[:, 0:1] + wt[:, 1:2] + wt[:, 2:3]
        a = _dot(x, wg_b[...])
        u = _dot(x, wu_b[...])
        he = (a * _sigmoid(a)) * u
        y = (_dot(he.astype(BF16), wd_b[...]) * w).astype(BF16)
        for k in range(n):
            s = (t + k) % GEMM_OUT

            @pl.when(t + k >= GEMM_OUT)
            def _(k=k, s=s):
                store(t + k - GEMM_OUT, s).wait()
            ybuf[s] = y[k * MOE_TILE:(k + 1) * MOE_TILE, :]
            store(t + k, s).start()

    t0 = t0_ref[e]
    t1 = t1_ref[e]

    def pair(p, carry):
        process(t0 + 2 * p, 2)
        return carry
    lax.fori_loop(0, (t1 - t0) // 2, pair, 0)

    @pl.when((t1 - t0) % 2 == 1)
    def _():
        process(t1 - 1, 1)

    @pl.when(e == pl.num_programs(0) - 1)
    def _():
        for k in range(1, GEMM_OUT + 1):
            @pl.when(nt >= k)
            def _(k=k):
                store(nt - k, (nt - k) % GEMM_OUT).wait()

        def drain(t, carry):
            zero_tile(t).wait()
            return carry
        lax.fori_loop(nt, n_tiles_max, drain, 0)


def _moe_gemm(tile0, tile1, n_tiles, xs, w_gate, w_up, w_down):
    wmap = lambda e, *_: (e, 0, 0)
    return pl.pallas_call(
        _moe_gemm_kernel,
        grid_spec=pltpu.PrefetchScalarGridSpec(
            num_scalar_prefetch=3,
            grid=(N_EXPERTS,),
            in_specs=[pl.BlockSpec((1, D_MODEL, D_EXPERT), wmap),
                      pl.BlockSpec((1, D_MODEL, D_EXPERT), wmap),
                      pl.BlockSpec((1, D_EXPERT, D_MODEL), wmap),
                      pl.BlockSpec(memory_space=pl.ANY)],
            out_specs=pl.BlockSpec(memory_space=pl.ANY),
            scratch_shapes=[pltpu.VMEM((D_MODEL, D_EXPERT), BF16), pltpu.VMEM((D_MODEL, D_EXPERT), BF16),
                            pltpu.VMEM((D_EXPERT, D_MODEL), BF16),
                            pltpu.VMEM((GEMM_RING, MOE_TILE, XS_W), BF16),
                            pltpu.VMEM((GEMM_OUT, MOE_TILE, D_MODEL), BF16), pltpu.VMEM((MOE_TILE, D_MODEL), BF16),
                            pltpu.SemaphoreType.DMA((GEMM_RING,)), pltpu.SemaphoreType.DMA((GEMM_OUT,)),
                            pltpu.SemaphoreType.DMA((1,)), pltpu.SMEM((1,), jnp.int32)]),
        out_shape=jax.ShapeDtypeStruct((xs.shape[0], D_MODEL), BF16),
        compiler_params=_cparams("arbitrary"),
        name="moe_gemm",
    )(tile0, tile1, n_tiles, w_gate, w_up, w_down, xs)


def _moe_combine_kernel(step0, dst_ref, nch_ref, lp_ref, x1_ref, mod_ref, gpost_ref, y_hbm, o_ref, ybuf, sem):
    i = pl.program_id(0)
    n = pl.num_programs(0)
    slot = i % 2

    def fetch(step, s):
        def body(c, carry):
            _chunk_copy(y_hbm, dst_ref[step, c], ybuf.at[s], c, sem.at[s]).start()
            return carry
        lax.fori_loop(0, nch_ref[step], body, 0)

    @pl.when(i == 0)
    def _():
        ybuf[...] = jnp.zeros_like(ybuf)
        fetch(step0, 0)

    @pl.when(i + 1 < n)
    def _():
        fetch(step0 + i + 1, 1 - slot)

    def wait(c, carry):
        _chunk_copy(y_hbm, 0, ybuf.at[slot], 0, sem.at[slot]).wait()
        return carry
    lax.fori_loop(0, nch_ref[step0 + i], wait, 0)

    lp = lp_ref[...]
    col = lax.broadcasted_iota(jnp.int32, (MOE_TD, MOE_L), 1)
    unperm = jnp.where((col == lp[:, 0:1]) | (col == lp[:, 1:2]), 1.0, 0.0).astype(BF16)
    moe = _dot(unperm, ybuf[slot])
    gt2 = mod_ref[0, 5:6, :]
    o_ref[...] = x1_ref[...] + gt2 * _rms(moe, gpost_ref[...])


def _moe_combine(step0, dst, n_chunks, lp, x1, mod, mod_row, y_sorted, g_post2):
    n = x1.shape[0]
    tok = lambda i, *_: (i, 0)
    return pl.pallas_call(
        functools.partial(_moe_combine_kernel, step0),
        grid_spec=pltpu.PrefetchScalarGridSpec(
            num_scalar_prefetch=2,
            grid=(n // MOE_TD,),
            in_specs=[pl.BlockSpec((MOE_TD, LANE), lambda i, *_: (step0 + i, 0)),
                      pl.BlockSpec((MOE_TD, D_MODEL), tok),
                      pl.BlockSpec((1, 6, D_MODEL), lambda i, *_: (mod_row(i), 0, 0)),
                      pl.BlockSpec((1, D_MODEL), lambda i, *_: (0, 0)),
                      pl.BlockSpec(memory_space=pl.ANY)],
            out_specs=pl.BlockSpec((MOE_TD, D_MODEL), tok),
            scratch_shapes=[pltpu.VMEM((2, MOE_L, D_MODEL), BF16), pltpu.SemaphoreType.DMA((2,))]),
        out_shape=jax.ShapeDtypeStruct((n, D_MODEL), F32),
        compiler_params=_cparams("arbitrary"),
        name="moe_combine",
    )(dst, n_chunks, lp, x1, mod, g_post2, y_sorted)


def _moe(part_a, part_b, mod, mod_row_a, mod_row_b, w_gate, w_up, w_down, g_post2):
    cnt = jnp.concatenate([part_a[4], part_b[4]], axis=0)[:, 0, :N_EXPERTS].astype(jnp.int32)
    steps_a = part_a[4].shape[0]
    n_tiles_max = (cnt.shape[0] * MOE_NCH + N_EXPERTS * (TILE_CH - 1)) // TILE_CH + 1
    dst, n_chunks, tail_start, tail_len, tile0, tile1, n_tiles = _plan(cnt, n_tiles_max)
    xs, lp = _dispatch(dst, n_chunks, tail_start, tail_len, n_tiles, part_a[1:4], part_b[1:4], n_tiles_max)
    y_sorted = _moe_gemm(tile0, tile1, n_tiles, xs, w_gate, w_up, w_down)
    ya = _moe_combine(0, dst, n_chunks, lp, part_a[0], mod, mod_row_a, y_sorted, g_post2)
    yb = _moe_combine(steps_a, dst, n_chunks, lp, part_b[0], mod, mod_row_b, y_sorted, g_post2)
    return ya, yb


def _rotate_half_cols(w):
    n = w.shape[-1]
    w4 = w.reshape(w.shape[:-1] + (n // 32, 2, 16))
    return jnp.stack([-w4[..., 1, :], w4[..., 0, :]], axis=-2).reshape(w.shape)


def _pad_lanes(w):
    return jnp.concatenate([w, jnp.zeros(w.shape[:-1] + (LANE - w.shape[-1],), w.dtype)], axis=-1)


def _rope_tables(t):
    rows = t // GRID_W
    n_freq = D_ROPE // 4
    freqs = ROPE_BASE ** (-jnp.arange(n_freq, dtype=F32) / n_freq)
    ang_r = jnp.arange(rows, dtype=F32)[:, None] * freqs
    ang_c = jnp.arange(GRID_W, dtype=F32)[:, None] * freqs
    per_row = lambda a: jnp.repeat(a, GRID_W, axis=0)
    per_col = lambda a: jnp.tile(a, (rows, 1))

    def table(fn):
        r, c = per_row(fn(ang_r)), per_col(fn(ang_c))
        return _pad_lanes(jnp.concatenate([r, r, c, c], axis=-1))
    return table(jnp.cos), table(jnp.sin)


def kernel(x_prompt, x_sample, cache_ckv, cache_krope, c, c_ctx, w_ada, b_ada, g_pre1, g_post1, g_pre2, g_post2, w_in, w_dw, b_dw, conv_ln_g, conv_ln_b, q_norm_g, kv_norm_g, w_uq, w_ukv, w_out, w_rg, b_rg, w_re, b_re, w_gate, w_up, w_down):
    nb, seq, d = x_prompt.shape
    db, dseq, _ = x_sample.shape
    l = 0

    cvec = jnp.concatenate([c_ctx[None, :], c, jnp.zeros((8 - 1 - db, d), F32)], axis=0)
    mod = _ada(cvec, w_ada[l], b_ada[l]).reshape(8, 6, d)

    o_kr = 2 * C_CONV + Q_LORA + KV_LORA
    w_in_l = w_in[l]
    w_kr = w_in_l[:, o_kr:]
    w_in_ctx = jnp.concatenate([w_in_l[:, :o_kr], _pad_lanes(w_kr)], axis=-1).astype(BF16)
    w_in_lat = jnp.concatenate([w_in_l[:, :o_kr], _pad_lanes(w_kr), _pad_lanes(_rotate_half_cols(w_kr))],
                               axis=-1).astype(BF16)
    wuq = w_uq[l].reshape(Q_LORA, N_HEADS, D_NOPE + D_ROPE)
    wq_rope = wuq[:, :, D_NOPE:]
    wq = jnp.concatenate([wuq[:, :, :D_NOPE], _pad_lanes(wq_rope)], axis=-1)
    wq = wq.reshape(Q_LORA, N_HEADS * HEAD_W).astype(BF16)
    wqs = _pad_lanes(_rotate_half_cols(wq_rope)).reshape(Q_LORA, N_HEADS * LANE).astype(BF16)
    wukv4 = w_ukv[l].reshape(KV_LORA, N_HEADS, D_NOPE + D_V)
    wukv = jnp.concatenate([wukv4[:, :, :D_NOPE].reshape(KV_LORA, N_HEADS * D_NOPE),
                            wukv4[:, :, D_NOPE:].reshape(KV_LORA, N_HEADS * D_V)], axis=-1).astype(BF16)
    w_out_b = w_out[l].astype(BF16)
    w_r = jnp.concatenate([w_re[l], w_rg[l], jnp.zeros((d, LANE - N_EXPERTS - N_GROUPS), F32)], axis=-1)
    wr_hi = w_r.astype(BF16)
    wr_lo = jnp.concatenate([wr_hi[:, :LANE // 2], (w_r - wr_hi.astype(F32)).astype(BF16)[:, :LANE // 2]], axis=-1)
    b_r = jnp.concatenate([b_re[l], b_rg[l], jnp.zeros((LANE - N_EXPERTS - N_GROUPS,), F32)])[None, :]
    cos, sin = _rope_tables(dseq)
    row = lambda v: v[l][None, :]

    tm_c = 512
    xp_flat = x_prompt.reshape(1, nb * seq, d)
    conv_w = (w_dw[l], row(b_dw), row(conv_ln_g), row(conv_ln_b))
    assert seq == CONV_TB, "a context step must hold whole sequences of CONV_TB tokens"
    conv_out, q, k, v, ckv, kr = _mix_in(xp_flat, mod, 0, row(g_pre1), w_in_ctx, row(q_norm_g), row(kv_norm_g),
                                         wq, wqs, wukv, cos[:tm_c], sin[:tm_c], conv_w, False, tm_c)
    per_seq = lambda a: a.reshape(nb, seq, a.shape[-1])
    att = _attn(per_seq(q), per_seq(k), per_seq(v), n_seq=4)
    flat = lambda a: a.reshape(1, nb * seq, a.shape[-1])
    post_c = _post(xp_flat, conv_out, flat(att), mod, 0, w_out_b, row(g_post1), row(g_pre2),
                   wr_hi, wr_lo, b_r)
    state_ckv = ckv.reshape(nb, 1, seq, KV_LORA)
    state_krope = kr.reshape(nb, 1, seq, D_ROPE)

    tm_s = 512
    kc, vc = _cache_kv(cache_ckv[:, l], cache_krope[:, l], wukv)
    conv_out, q, k, v, _, _ = _mix_in(x_sample, mod, 1, row(g_pre1), w_in_lat, row(q_norm_g), row(kv_norm_g),
                                      wq, wqs, wukv, cos, sin, conv_w, True, tm_s)
    att = _attn(q, k, v, kc, vc)
    post_s = _post(x_sample, conv_out, att, mod, 1, w_out_b, row(g_post1), row(g_pre2), wr_hi, wr_lo, b_r)

    tokens = lambda parts: tuple(a.reshape((-1,) + a.shape[2:]) for a in parts)
    steps_per_req = dseq // MOE_TD
    yp, ys = _moe(tokens(post_c), tokens(post_s), mod, lambda i: 0, lambda i: 1 + i // steps_per_req,
                  w_gate[l], w_up[l], w_down[l], row(g_post2))

    return (yp.reshape(nb, seq, d), ys.reshape(db, dseq, d), state_ckv, state_krope)
```

```python
import functools

import jax
import jax.numpy as jnp
import numpy as np
from jax import lax
from jax.experimental import pallas as pl
from jax.experimental.pallas import tpu as pltpu

D_MODEL = 1024
GRID_W = 64
C_CONV = 512
CONV_K = 31
N_HEADS = 4
D_NOPE = 128
D_ROPE = 64
D_V = 128
Q_LORA = 384
KV_LORA = 256
N_GROUPS = 4
E_PER_GROUP = 8
N_EXPERTS = 32
D_EXPERT = 256
ROPE_BASE = 10000.0
EPS = 1e-6
ATT_SCALE = (D_NOPE + D_ROPE) ** -0.5
LOG2E = 1.4426950408889634

LANE = 128
SUBLANE = 8
HEAD_W = 2 * LANE
CONV_HALO = 16
MOE_TD = 512
CHUNK = 2 * SUBLANE
MOE_NCH = (2 * MOE_TD + N_EXPERTS * (CHUNK - 1)) // CHUNK
MOE_L = MOE_NCH * CHUNK
MOE_TILE = 256
TILE_CH = MOE_TILE // CHUNK
GEMM_RING = 8
GEMM_OUT = 8
XS_W = D_MODEL + LANE
VMEM_LIMIT = 56 * 1024 * 1024

BF16 = jnp.bfloat16
F32 = jnp.float32


def _cparams(*sem):
    return pltpu.CompilerParams(dimension_semantics=sem, vmem_limit_bytes=VMEM_LIMIT)


def _rms(x, g):
    return x * lax.rsqrt(jnp.mean(x * x, axis=-1, keepdims=True) + EPS) * g


def _sigmoid(x):
    return 1.0 / (1.0 + jnp.exp(-x))


def _dot(a, b):
    return jnp.dot(a, b, preferred_element_type=F32)


def _chunk_copy(src, src_chunk, dst, dst_chunk, sem):
    rows = lambda c: pl.ds(c * CHUNK if isinstance(c, int) else pl.multiple_of(c * CHUNK, CHUNK), CHUNK)
    return pltpu.make_async_copy(src.at[rows(src_chunk), :], dst.at[rows(dst_chunk), :], sem)


def _ada_kernel(c_ref, w_ref, b_ref, o_ref):
    c = c_ref[...]
    s = (c * _sigmoid(c)).astype(BF16)
    o_ref[...] = _dot(s, w_ref[...].astype(BF16)) + b_ref[...]


def _ada(cvec, w_ada, b_ada):
    n = w_ada.shape[1]
    tn = 1536
    return pl.pallas_call(
        _ada_kernel,
        grid=(n // tn,),
        in_specs=[pl.BlockSpec((8, D_MODEL), lambda j: (0, 0)),
                  pl.BlockSpec((D_MODEL, tn), lambda j: (0, j)),
                  pl.BlockSpec((1, tn), lambda j: (0, j))],
        out_specs=pl.BlockSpec((8, tn), lambda j: (0, j)),
        out_shape=jax.ShapeDtypeStruct((8, n), F32),
        compiler_params=_cparams("arbitrary"),
        name="ada",
    )(cvec, w_ada, b_ada.reshape(1, n))


CONV_TB = 256
CONV_WIN = CONV_TB + 2 * CONV_HALO
CONV_TT = 32


def _conv_window(win_ref, sh_ref, w_ref, b_ref, g_ref, bb_ref, o_ref, row0):
    first = CONV_HALO - CONV_K // 2
    rows = CONV_WIN - SUBLANE
    win = win_ref[...]
    for s in range(SUBLANE):
        sh_ref[s, 0:rows, :] = win[s:s + rows, :]
    for c in range(CONV_TB // CONV_TT):
        acc = jnp.zeros((CONV_TT, C_CONV), F32)
        for k in range(CONV_K):
            off = first + k
            r0 = c * CONV_TT + off // SUBLANE * SUBLANE
            acc = acc + sh_ref[off % SUBLANE, r0:r0 + CONV_TT, :] * w_ref[k:k + 1, :]
        y = acc + b_ref[...]
        mu = jnp.mean(y, axis=-1, keepdims=True)
        yc = y - mu
        var = jnp.mean(yc * yc, axis=-1, keepdims=True)
        z = yc * lax.rsqrt(var + EPS) * g_ref[...] + bb_ref[...]
        o_ref[0, row0 + c * CONV_TT:row0 + (c + 1) * CONV_TT, :] = (z * _sigmoid(z)).astype(BF16)


def _mix_in_kernel(latent, x_ref, xp_ref, xn_ref, mod_ref, gpre_ref, win_ref, qg_ref, kvg_ref, wq_ref, wqs_ref,
                   wukv_ref, cos_ref, sin_ref, wdw_ref, bdw_ref, lng_ref, lnb_ref,
                   conv_ref, q_ref, k_ref, v_ref, ckv_ref, kr_ref, cwin, csh):
    rope = latent
    x = x_ref[0]
    tm = x.shape[0]
    sh1 = mod_ref[0, 0:1, :]
    sc1 = mod_ref[0, 1:2, :]
    modulate = lambda v: _rms(v, gpre_ref[...]) * (1.0 + sc1) + sh1
    glu = lambda uc: uc[:, :C_CONV] * _sigmoid(uc[:, C_CONV:2 * C_CONV])
    u = _dot(modulate(x).astype(BF16), win_ref[...])
    hglu = glu(u)

    zeros = jnp.zeros((CONV_HALO, C_CONV), F32)
    n_win = tm // CONV_TB
    if latent:
        i = pl.program_id(1)
        xh = jnp.concatenate([xp_ref[0], xn_ref[0]], axis=0)
        hh = glu(_dot(modulate(xh).astype(BF16), win_ref[:, 0:2 * C_CONV]))
        before = jnp.where(i > 0, hh[0:CONV_HALO], zeros)
        after = jnp.where(i < pl.num_programs(1) - 1, hh[CONV_HALO:2 * CONV_HALO], zeros)
    for j in range(n_win):
        if latent:
            lo, hi, d0 = j * CONV_TB - CONV_HALO, j * CONV_TB + CONV_TB + CONV_HALO, 0
            if lo < 0:
                cwin[j, 0:CONV_HALO, :] = before
                lo, d0 = 0, CONV_HALO
            if hi > tm:
                cwin[j, CONV_WIN - CONV_HALO:CONV_WIN, :] = after
                hi = tm
            cwin[j, d0:d0 + hi - lo, :] = hglu[lo:hi, :]
        else:
            cwin[j, 0:CONV_HALO, :] = zeros
            cwin[j, CONV_HALO:CONV_HALO + CONV_TB, :] = hglu[j * CONV_TB:(j + 1) * CONV_TB, :]
            cwin[j, CONV_HALO + CONV_TB:CONV_WIN, :] = zeros
    for j in range(n_win):
        _conv_window(cwin.at[j], csh, wdw_ref, bdw_ref, lng_ref, lnb_ref, conv_ref, j * CONV_TB)

    o_q = 2 * C_CONV
    o_kv = o_q + Q_LORA
    o_kr = o_kv + KV_LORA
    qn = _rms(u[:, o_q:o_kv], qg_ref[...]).astype(BF16)
    qf = _dot(qn, wq_ref[...])
    ckv = _rms(u[:, o_kv:o_kr], kvg_ref[...])
    ckv_ref[0] = ckv
    kvd = _dot(ckv.astype(BF16), wukv_ref[...])
    kr = u[:, o_kr:o_kr + LANE]
    kr_ref[0] = kr[:, :D_ROPE]
    if rope:
        cos = cos_ref[...]
        sin = sin_ref[...]
        qs = _dot(qn, wqs_ref[...])
        kr = kr * cos + u[:, o_kr + LANE:o_kr + 2 * LANE] * sin
    q_parts = []
    k_parts = []
    for hd in range(N_HEADS):
        q_parts.append(qf[:, hd * HEAD_W:hd * HEAD_W + LANE])
        qr = qf[:, hd * HEAD_W + LANE:(hd + 1) * HEAD_W]
        if rope:
            qr = qr * cos + qs[:, hd * LANE:(hd + 1) * LANE] * sin
        q_parts.append(qr)
        k_parts.append(kvd[:, hd * D_NOPE:(hd + 1) * D_NOPE])
        k_parts.append(kr)
    q_ref[0] = jnp.concatenate(q_parts, axis=-1).astype(BF16)
    k_ref[0] = jnp.concatenate(k_parts, axis=-1).astype(BF16)
    v_ref[0] = kvd[:, N_HEADS * D_NOPE:].astype(BF16)


def _mix_in(x, mod, mod_row0, g_pre1, w_in_ext, q_norm_g, kv_norm_g, wq, wqs, wukv, cos, sin, conv_w, latent, tm):
    b, t, _ = x.shape
    ncol = w_in_ext.shape[1]
    const = lambda bi, i: (0, 0)
    tok = lambda bi, i: (bi, i, 0)
    table = (lambda bi, i: (i, 0)) if latent else const
    halo_blocks = tm // CONV_HALO
    prev_rows = lambda bi, i: (bi, jnp.maximum(i * halo_blocks - 1, 0), 0)
    next_rows = lambda bi, i: (bi, jnp.minimum((i + 1) * halo_blocks, t // CONV_HALO - 1), 0)
    outs = [(C_CONV, BF16), (N_HEADS * HEAD_W, BF16), (N_HEADS * HEAD_W, BF16), (N_HEADS * D_V, BF16),
            (KV_LORA, F32), (D_ROPE, F32)]
    return pl.pallas_call(
        functools.partial(_mix_in_kernel, latent),
        grid=(b, t // tm),
        in_specs=[pl.BlockSpec((1, tm, D_MODEL), tok),
                  pl.BlockSpec((1, CONV_HALO, D_MODEL), prev_rows),
                  pl.BlockSpec((1, CONV_HALO, D_MODEL), next_rows),
                  pl.BlockSpec((1, 6, D_MODEL), lambda bi, i: (mod_row0 + bi, 0, 0)),
                  pl.BlockSpec((1, D_MODEL), const),
                  pl.BlockSpec((D_MODEL, ncol), const),
                  pl.BlockSpec((1, Q_LORA), const),
                  pl.BlockSpec((1, KV_LORA), const),
                  pl.BlockSpec((Q_LORA, N_HEADS * HEAD_W), const),
                  pl.BlockSpec((Q_LORA, N_HEADS * LANE), const),
                  pl.BlockSpec((KV_LORA, N_HEADS * (D_NOPE + D_V)), const),
                  pl.BlockSpec((tm, LANE), table),
                  pl.BlockSpec((tm, LANE), table),
                  pl.BlockSpec((CONV_K, C_CONV), const),
                  pl.BlockSpec((1, C_CONV), const),
                  pl.BlockSpec((1, C_CONV), const),
                  pl.BlockSpec((1, C_CONV), const)],
        out_specs=[pl.BlockSpec((1, tm, w), tok) for w, _ in outs],
        out_shape=[jax.ShapeDtypeStruct((b, t, w), dt) for w, dt in outs],
        scratch_shapes=[pltpu.VMEM((tm // CONV_TB, CONV_WIN, C_CONV), F32),
                        pltpu.VMEM((SUBLANE, CONV_WIN, C_CONV), F32)],
        compiler_params=_cparams("arbitrary", "arbitrary"),
        name="mix_in_latent" if latent else "mix_in",
    )(x, x, x, mod, g_pre1, w_in_ext, q_norm_g, kv_norm_g, wq, wqs, wukv, cos, sin, *conv_w)


def _qk(q, k):
    return lax.dot_general(q, k, (((1,), (1,)), ((), ())), preferred_element_type=F32)


ATT_KCHUNK = 256


def _attn_kernel(cached, q_ref, k_ref, v_ref, *rest):
    if cached:
        kc_ref, vc_ref, o_ref, s_scr = rest
        sources = [(kc_ref, vc_ref), (k_ref, v_ref)]
    else:
        o_ref, s_scr = rest
        sources = [(k_ref, v_ref)]
    n_seq, tq, _ = q_ref.shape
    chunks = []
    for kr, vr in sources:
        for c0 in range(0, kr.shape[1], ATT_KCHUNK):
            c1 = min(c0 + ATT_KCHUNK, kr.shape[1])
            col = chunks[-1][4] + chunks[-1][3] - chunks[-1][2] if chunks else 0
            chunks.append((kr, vr, c0, c1, col))
    for b in range(n_seq):
        outs = []
        for hd in range(N_HEADS):
            ks = slice(hd * HEAD_W, (hd + 1) * HEAD_W)
            vs = slice(hd * D_V, (hd + 1) * D_V)
            qh = q_ref[b, :, ks]
            m = jnp.full((tq, 1), -jnp.inf, F32)
            for kr, vr, c0, c1, col in chunks:
                s = _qk(qh, kr[b, c0:c1, ks]) * (ATT_SCALE * LOG2E)
                s_scr[hd, :, col:col + c1 - c0] = s
                m = jnp.maximum(m, jnp.max(s, axis=-1, keepdims=True))
            l = jnp.zeros((tq, 1), F32)
            o = jnp.zeros((tq, D_V), F32)
            for kr, vr, c0, c1, col in chunks:
                p = jnp.exp2(s_scr[hd, :, col:col + c1 - c0] - m)
                l = l + jnp.sum(p, axis=-1, keepdims=True)
                o = o + _dot(p.astype(BF16), vr[b, c0:c1, vs])
            outs.append(o / l)
        o_ref[b] = jnp.concatenate(outs, axis=-1).astype(BF16)


def _attn(q, k, v, kc=None, vc=None, tq=256, n_seq=1):
    b, t, _ = q.shape
    s = k.shape[1]
    cached = kc is not None
    whole = lambda bi, i: (bi, 0, 0)
    in_specs = [pl.BlockSpec((n_seq, tq, N_HEADS * HEAD_W), lambda bi, i: (bi, i, 0)),
                pl.BlockSpec((n_seq, s, N_HEADS * HEAD_W), whole),
                pl.BlockSpec((n_seq, s, N_HEADS * D_V), whole)]
    args = [q, k, v]
    if cached:
        sc = kc.shape[1]
        in_specs += [pl.BlockSpec((n_seq, sc, N_HEADS * HEAD_W), whole),
                     pl.BlockSpec((n_seq, sc, N_HEADS * D_V), whole)]
        args += [kc, vc]
        s += sc
    return pl.pallas_call(
        functools.partial(_attn_kernel, cached),
        grid=(b // n_seq, t // tq),
        in_specs=in_specs,
        out_specs=pl.BlockSpec((n_seq, tq, N_HEADS * D_V), lambda bi, i: (bi, i, 0)),
        out_shape=jax.ShapeDtypeStruct((b, t, N_HEADS * D_V), BF16),
        scratch_shapes=[pltpu.VMEM((N_HEADS, tq, s), F32)],
        compiler_params=_cparams("arbitrary", "arbitrary"),
        name="attn_cached" if cached else "attn",
    )(*args)


def _cache_kv_kernel(ckv_ref, kr_ref, wukv_ref, k_ref, v_ref):
    kvd = _dot(ckv_ref[0].astype(BF16), wukv_ref[...])
    kr = kr_ref[0]
    kr = jnp.concatenate([kr, jnp.zeros_like(kr)], axis=-1)
    parts = []
    for hd in range(N_HEADS):
        parts.append(kvd[:, hd * D_NOPE:(hd + 1) * D_NOPE])
        parts.append(kr)
    k_ref[0] = jnp.concatenate(parts, axis=-1).astype(BF16)
    v_ref[0] = kvd[:, N_HEADS * D_NOPE:].astype(BF16)


def _cache_kv(ckv, krope, wukv):
    b, s, _ = ckv.shape
    tok = lambda bi: (bi, 0, 0)
    return pl.pallas_call(
        _cache_kv_kernel,
        grid=(b,),
        in_specs=[pl.BlockSpec((1, s, KV_LORA), tok),
                  pl.BlockSpec((1, s, D_ROPE), tok),
                  pl.BlockSpec((KV_LORA, N_HEADS * (D_NOPE + D_V)), lambda bi: (0, 0))],
        out_specs=[pl.BlockSpec((1, s, N_HEADS * HEAD_W), tok),
                   pl.BlockSpec((1, s, N_HEADS * D_V), tok)],
        out_shape=[jax.ShapeDtypeStruct((b, s, N_HEADS * HEAD_W), BF16),
                   jax.ShapeDtypeStruct((b, s, N_HEADS * D_V), BF16)],
        compiler_params=_cparams("arbitrary"),
        name="cache_kv",
    )(ckv, krope, wukv)


def _post_kernel(x_ref, conv_ref, att_ref, mod_ref, wo_ref, gpost_ref, gpre2_ref, wr_hi_ref, wr_cat_ref,
                 br_ref, x1_ref, h2_ref, ri_ref, rw_ref, cnt_ref):
    out = _dot(conv_ref[0], wo_ref[:C_CONV, :]) + _dot(att_ref[0], wo_ref[C_CONV:, :])
    gt1 = mod_ref[0, 2:3, :]
    sh2 = mod_ref[0, 3:4, :]
    sc2 = mod_ref[0, 4:5, :]
    x1 = x_ref[0] + gt1 * _rms(out, gpost_ref[...])
    x1_ref[0] = x1
    h2 = _rms(x1, gpre2_ref[...]) * (1.0 + sc2) + sh2
    h_hi = h2.astype(BF16)
    h2_ref[0] = h_hi
    h_lo = (h2 - h_hi.astype(F32)).astype(BF16)
    hi_terms = _dot(h_hi, wr_cat_ref[...])
    logits = hi_terms + pltpu.roll(hi_terms, LANE // 2, axis=1) + _dot(h_lo, wr_hi_ref[...]) + br_ref[...]

    lane = lax.broadcasted_iota(jnp.int32, logits.shape, 1)
    neg = jnp.float32(-jnp.inf)
    big = jnp.int32(LANE)
    is_g = (lane >= N_EXPERTS) & (lane < N_EXPERTS + N_GROUPS)
    lg = jnp.where(is_g, logits, neg)
    gmax = jnp.max(lg, axis=-1, keepdims=True)
    gidx = jnp.min(jnp.where(lg == gmax, lane, big), axis=-1, keepdims=True) - N_EXPERTS
    g_top = 1.0 / jnp.sum(jnp.exp(lg - gmax), axis=-1, keepdims=True)

    in_grp = (lane >= gidx * E_PER_GROUP) & (lane < (gidx + 1) * E_PER_GROUP)
    le = jnp.where(in_grp, logits, neg)
    m1 = jnp.max(le, axis=-1, keepdims=True)
    i1 = jnp.min(jnp.where(le == m1, lane, big), axis=-1, keepdims=True)
    le2 = jnp.where(lane == i1, neg, le)
    m2 = jnp.max(le2, axis=-1, keepdims=True)
    i2 = jnp.min(jnp.where(le2 == m2, lane, big), axis=-1, keepdims=True)
    r = jnp.exp(m2 - m1)
    w1 = g_top / (1.0 + r)
    w2 = g_top * r / (1.0 + r)
    ri_ref[0] = jnp.where(lane == 0, i1, jnp.where(lane == 1, i2, 0))
    rw_ref[0] = jnp.where(lane == 0, w1, jnp.where(lane == 1, w2, 0.0))
    pairs = jnp.sum(jnp.where((lane == i1) | (lane == i2), 1.0, 0.0), axis=0, keepdims=True)
    cnt_ref[0, 0] = jnp.broadcast_to(pairs, (SUBLANE, LANE))


def _post(x, conv_out, att, mod, mod_row0, w_out, g_post1, g_pre2, wr_hi, wr_lo, br):
    b, t, _ = x.shape
    tm = MOE_TD
    const = lambda bi, i: (0, 0)
    tok = lambda bi, i: (bi, i, 0)
    return pl.pallas_call(
        _post_kernel,
        grid=(b, t // tm),
        in_specs=[pl.BlockSpec((1, tm, D_MODEL), tok),
                  pl.BlockSpec((1, tm, C_CONV), tok),
                  pl.BlockSpec((1, tm, N_HEADS * D_V), tok),
                  pl.BlockSpec((1, 6, D_MODEL), lambda bi, i: (mod_row0 + bi, 0, 0)),
                  pl.BlockSpec((D_MODEL, D_MODEL), const),
                  pl.BlockSpec((1, D_MODEL), const),
                  pl.BlockSpec((1, D_MODEL), const),
                  pl.BlockSpec((D_MODEL, LANE), const),
                  pl.BlockSpec((D_MODEL, LANE), const),
                  pl.BlockSpec((1, LANE), const)],
        out_specs=[pl.BlockSpec((1, tm, D_MODEL), tok),
                   pl.BlockSpec((1, tm, D_MODEL), tok),
                   pl.BlockSpec((1, tm, LANE), tok),
                   pl.BlockSpec((1, tm, LANE), tok),
                   pl.BlockSpec((1, 1, SUBLANE, LANE), lambda bi, i: (bi, i, 0, 0))],
        out_shape=[jax.ShapeDtypeStruct((b, t, D_MODEL), F32),
                   jax.ShapeDtypeStruct((b, t, D_MODEL), BF16),
                   jax.ShapeDtypeStruct((b, t, LANE), jnp.int32),
                   jax.ShapeDtypeStruct((b, t, LANE), F32),
                   jax.ShapeDtypeStruct((b, t // tm, SUBLANE, LANE), F32)],
        compiler_params=_cparams("arbitrary", "arbitrary"),
        name="post",
    )(x, conv_out, att, mod, w_out, g_post1, g_pre2, wr_hi, wr_lo, br)


def _one_hots(ri):
    lane = lax.broadcasted_iota(jnp.int32, ri.shape, 1)
    oh1 = lane == ri[:, 0:1]
    oh2 = lane == ri[:, 1:2]
    return oh1, oh2, jnp.where(oh1 | oh2, 1.0, 0.0)


def _local_positions(oh1, oh2, oh):
    td = oh.shape[0]
    cnt = jnp.sum(oh, axis=0, keepdims=True)
    nch = jnp.floor((cnt + (CHUNK - 1)) * (1.0 / CHUNK))
    a = lax.broadcasted_iota(jnp.int32, (LANE, LANE), 0)
    b = lax.broadcasted_iota(jnp.int32, (LANE, LANE), 1)
    lower_experts = jnp.where(a < b, 1.0, 0.0).astype(BF16)
    run_start = _dot(jnp.broadcast_to(nch, (SUBLANE, LANE)).astype(BF16), lower_experts)[0:1, :] * CHUNK
    rr = lax.broadcasted_iota(jnp.int32, (td, td), 0)
    cc = lax.broadcasted_iota(jnp.int32, (td, td), 1)
    earlier = jnp.where(cc < rr, 1.0, 0.0).astype(BF16)
    pos = _dot(earlier, oh.astype(BF16)) + run_start
    lp1 = jnp.sum(jnp.where(oh1, pos, 0.0), axis=-1, keepdims=True)
    lp2 = jnp.sum(jnp.where(oh2, pos, 0.0), axis=-1, keepdims=True)
    return lp1, lp2


def _plan(cnt, n_tiles_max):
    k = (cnt + (CHUNK - 1)) // CHUNK
    run_end = jnp.cumsum(k, axis=1)
    run_start = run_end - k
    n_chunks = run_end[:, -1]
    total = jnp.sum(k, axis=0)
    padded = (total + (TILE_CH - 1)) // TILE_CH * TILE_CH
    seg_end = jnp.cumsum(padded)
    seg_start = seg_end - padded
    base = seg_start[None, :] + jnp.cumsum(k, axis=0) - k
    c = jnp.arange(MOE_NCH, dtype=jnp.int32)[None, :, None]
    in_run = (run_start[:, None, :] <= c) & (c < run_end[:, None, :])
    dst = jnp.sum(jnp.where(in_run, base[:, None, :] + c - run_start[:, None, :], 0), axis=2)
    n_tiles = seg_end[-1] // TILE_CH
    tail_start = seg_start + total
    tail_len = padded - total
    i32 = lambda t: t.astype(jnp.int32)
    return (i32(dst), i32(n_chunks), i32(tail_start), i32(tail_len), i32(seg_start // TILE_CH),
            i32(seg_end // TILE_CH), i32(n_tiles.reshape(1)))


def _dispatch_kernel(steps_a, dst_ref, nch_ref, ts_ref, tl_ref, nt_ref, h2a_ref, ria_ref, rwa_ref, h2b_ref, rib_ref,
                     rwb_ref, xs_hbm, lp_ref, xbuf, zx, sem, zsem):
    i = pl.program_id(0)
    n = pl.num_programs(0)
    slot = i % 2
    td = MOE_TD
    n_tiles_max = xs_hbm.shape[0] // MOE_TILE

    def zero_tile(m):
        rows = pl.ds(pl.multiple_of(m * MOE_TILE, MOE_TILE), MOE_TILE)
        return pltpu.make_async_copy(zx, xs_hbm.at[rows, :], zsem.at[1])

    @pl.when(i == 0)
    def _():
        zx[...] = jnp.zeros_like(zx)
        for e in range(N_EXPERTS):
            def fill(m, carry, e=e):
                _chunk_copy(zx, 0, xs_hbm, ts_ref[e] + m, zsem.at[0]).start()
                return carry
            lax.fori_loop(0, tl_ref[e], fill, 0)

        def fill_tile(m, carry):
            zero_tile(m).start()
            return carry
        lax.fori_loop(nt_ref[0], n_tiles_max, fill_tile, 0)

    from_a = i < steps_a
    ri = jnp.where(from_a, ria_ref[...], rib_ref[...])
    rw = jnp.where(from_a, rwa_ref[...], rwb_ref[...])
    h2 = jnp.where(from_a, h2a_ref[...], h2b_ref[...])
    oh1, oh2, oh = _one_hots(ri)
    lp1, lp2 = _local_positions(oh1, oh2, oh)
    lane = lax.broadcasted_iota(jnp.int32, ri.shape, 1)
    lp_ref[...] = jnp.where(lane == 0, lp1, jnp.where(lane == 1, lp2, 0.0)).astype(jnp.int32)

    rr = lax.broadcasted_iota(jnp.int32, (td, td), 0)
    cc = lax.broadcasted_iota(jnp.int32, (td, td), 1)
    to_lanes = lambda col: jnp.sum(jnp.where(rr == cc, col, 0.0), axis=0, keepdims=True)
    row = lax.broadcasted_iota(jnp.int32, (MOE_L, td), 0).astype(F32)
    p1 = row == to_lanes(lp1)
    p2 = row == to_lanes(lp2)
    perm = jnp.where(p1 | p2, 1.0, 0.0).astype(BF16)
    xbuf[slot, :, 0:D_MODEL] = _dot(perm, h2).astype(BF16)
    w = jnp.sum(jnp.where(p1, to_lanes(rw[:, 0:1]), 0.0) + jnp.where(p2, to_lanes(rw[:, 1:2]), 0.0),
                axis=-1, keepdims=True)
    w0 = w.astype(BF16).astype(F32)
    w1 = (w - w0).astype(BF16).astype(F32)
    w2 = (w - w0 - w1).astype(BF16).astype(F32)
    wl = lax.broadcasted_iota(jnp.int32, (MOE_L, LANE), 1)
    terms = jnp.where(wl == 0, w0, jnp.where(wl == 1, w1, jnp.where(wl == 2, w2, 0.0)))
    xbuf[slot, :, D_MODEL:XS_W] = terms.astype(BF16)

    def wait_chunks(count, s):
        def body(c, carry):
            _chunk_copy(xbuf.at[s], 0, xs_hbm, 0, sem.at[s]).wait()
            return carry
        lax.fori_loop(0, count, body, 0)

    @pl.when(i > 0)
    def _():
        wait_chunks(nch_ref[i - 1], 1 - slot)

    def send(c, carry):
        _chunk_copy(xbuf.at[slot], c, xs_hbm, dst_ref[i, c], sem.at[slot]).start()
        return carry
    lax.fori_loop(0, nch_ref[i], send, 0)

    @pl.when(i == n - 1)
    def _():
        wait_chunks(nch_ref[i], slot)
        for e in range(N_EXPERTS):
            def drain(m, carry):
                _chunk_copy(zx, 0, xs_hbm, 0, zsem.at[0]).wait()
                return carry
            lax.fori_loop(0, tl_ref[e], drain, 0)

        def drain_tile(m, carry):
            zero_tile(m).wait()
            return carry
        lax.fori_loop(nt_ref[0], n_tiles_max, drain_tile, 0)


def _dispatch(dst, n_chunks, tail_start, tail_len, n_tiles, part_a, part_b, n_tiles_max):
    steps_a = part_a[0].shape[0] // MOE_TD
    steps_b = part_b[0].shape[0] // MOE_TD
    in_a = lambda i, *_: (jnp.minimum(i, steps_a - 1), 0)
    in_b = lambda i, *_: (jnp.maximum(i - steps_a, 0), 0)
    specs = lambda f: [pl.BlockSpec((MOE_TD, D_MODEL), f), pl.BlockSpec((MOE_TD, LANE), f),
                       pl.BlockSpec((MOE_TD, LANE), f)]
    rows = n_tiles_max * MOE_TILE
    return pl.pallas_call(
        functools.partial(_dispatch_kernel, steps_a),
        grid_spec=pltpu.PrefetchScalarGridSpec(
            num_scalar_prefetch=5,
            grid=(steps_a + steps_b,),
            in_specs=specs(in_a) + specs(in_b),
            out_specs=[pl.BlockSpec(memory_space=pl.ANY),
                       pl.BlockSpec((MOE_TD, LANE), lambda i, *_: (i, 0))],
            scratch_shapes=[pltpu.VMEM((2, MOE_L, XS_W), BF16), pltpu.VMEM((MOE_TILE, XS_W), BF16),
                            pltpu.SemaphoreType.DMA((2,)), pltpu.SemaphoreType.DMA((2,))]),
        out_shape=[jax.ShapeDtypeStruct((rows, XS_W), BF16),
                   jax.ShapeDtypeStruct(((steps_a + steps_b) * MOE_TD, LANE), jnp.int32)],
        compiler_params=_cparams("arbitrary"),
        name="moe_dispatch",
    )(dst, n_chunks, tail_start, tail_len, n_tiles, *part_a, *part_b)


def _moe_gemm_kernel(t0_ref, t1_ref, nt_ref, wg_ref, wu_ref, wd_ref, xs_hbm, y_hbm,
                     wg_b, wu_b, wd_b, xbuf, ybuf, zbuf, isem, osem, zsem, issued):
    e = pl.program_id(0)
    nt = nt_ref[0]
    n_tiles_max = y_hbm.shape[0] // MOE_TILE
    tile_rows = lambda t: pl.ds(pl.multiple_of(t * MOE_TILE, MOE_TILE), MOE_TILE)

    def load(t, s):
        return (pltpu.make_async_copy(xs_hbm.at[tile_rows(t), :], xbuf.at[s], isem.at[s]),)

    def store(t, s):
        return pltpu.make_async_copy(ybuf.at[s], y_hbm.at[tile_rows(t), :], osem.at[s])

    def zero_tile(t):
        return pltpu.make_async_copy(zbuf, y_hbm.at[tile_rows(t), :], zsem.at[0])

    @pl.when(e == 0)
    def _():
        issued[0] = 0
        zbuf[...] = jnp.zeros_like(zbuf)

        def fill(t, carry):
            zero_tile(t).start()
            return carry
        lax.fori_loop(nt, n_tiles_max, fill, 0)

    wg_b[...] = wg_ref[0].astype(BF16)
    wu_b[...] = wu_ref[0].astype(BF16)
    wd_b[...] = wd_ref[0].astype(BF16)

    def top_up(t):
        upto = jnp.minimum(t + GEMM_RING, nt)

        def start(u, carry):
            for cp in load(u, u % GEMM_RING):
                cp.start()
            return carry
        lax.fori_loop(issued[0], upto, start, 0)
        issued[0] = jnp.maximum(issued[0], upto)

    def process(t, n):
        top_up(t)
        for k in range(n):
            for cp in load(t + k, (t + k) % GEMM_RING):
                cp.wait()
        rows = [xbuf[(t + k) % GEMM_RING] for k in range(n)]
        xw = rows[0] if n == 1 else jnp.concatenate(rows, axis=0)
        x = xw[:, 0:D_MODEL]
        wt = xw[:, D_MODEL:XS_W].astype(F32)
        w = wt[:, 0:1] + wt[:, 1:2] + wt[:, 2:3]
        a = _dot(x, wg_b[...])
        u = _dot(x, wu_b[...])
        he = (a * _sigmoid(a)) * u
        y = (_dot(he.astype(BF16), wd_b[...]) * w).astype(BF16)
        for k in range(n):
            s = (t + k) % GEMM_OUT

            @pl.when(t + k >= GEMM_OUT)
            def _(k=k, s=s):
                store(t + k - GEMM_OUT, s).wait()
            ybuf[s] = y[k * MOE_TILE:(k + 1) * MOE_TILE, :]
            store(t + k, s).start()

    t0 = t0_ref[e]
    t1 = t1_ref[e]

    n_quads = (t1 - t0) // 4
    rest = t0 + 4 * n_quads

    def quad(p, carry):
        process(t0 + 4 * p, 4)
        return carry
    lax.fori_loop(0, n_quads, quad, 0)

    @pl.when(t1 - rest >= 2)
    def _():
        process(rest, 2)

    @pl.when((t1 - rest) % 2 == 1)
    def _():
        process(t1 - 1, 1)

    @pl.when(e == pl.num_programs(0) - 1)
    def _():
        for k in range(1, GEMM_OUT + 1):
            @pl.when(nt >= k)
            def _(k=k):
                store(nt - k, (nt - k) % GEMM_OUT).wait()

        def drain(t, carry):
            zero_tile(t).wait()
            return carry
        lax.fori_loop(nt, n_tiles_max, drain, 0)


def _moe_gemm(tile0, tile1, n_tiles, xs, w_gate, w_up, w_down):
    wmap = lambda e, *_: (e, 0, 0)
    return pl.pallas_call(
        _moe_gemm_kernel,
        grid_spec=pltpu.PrefetchScalarGridSpec(
            num_scalar_prefetch=3,
            grid=(N_EXPERTS,),
            in_specs=[pl.BlockSpec((1, D_MODEL, D_EXPERT), wmap),
                      pl.BlockSpec((1, D_MODEL, D_EXPERT), wmap),
                      pl.BlockSpec((1, D_EXPERT, D_MODEL), wmap),
                      pl.BlockSpec(memory_space=pl.ANY)],
            out_specs=pl.BlockSpec(memory_space=pl.ANY),
            scratch_shapes=[pltpu.VMEM((D_MODEL, D_EXPERT), BF16), pltpu.VMEM((D_MODEL, D_EXPERT), BF16),
                            pltpu.VMEM((D_EXPERT, D_MODEL), BF16),
                            pltpu.VMEM((GEMM_RING, MOE_TILE, XS_W), BF16),
                            pltpu.VMEM((GEMM_OUT, MOE_TILE, D_MODEL), BF16), pltpu.VMEM((MOE_TILE, D_MODEL), BF16),
                            pltpu.SemaphoreType.DMA((GEMM_RING,)), pltpu.SemaphoreType.DMA((GEMM_OUT,)),
                            pltpu.SemaphoreType.DMA((1,)), pltpu.SMEM((1,), jnp.int32)]),
        out_shape=jax.ShapeDtypeStruct((xs.shape[0], D_MODEL), BF16),
        compiler_params=_cparams("arbitrary"),
        name="moe_gemm",
    )(tile0, tile1, n_tiles, w_gate, w_up, w_down, xs)


def _moe_combine_kernel(step0, dst_ref, nch_ref, lp_ref, x1_ref, mod_ref, gpost_ref, y_hbm, o_ref, ybuf, sem):
    i = pl.program_id(0)
    n = pl.num_programs(0)
    slot = i % 2

    def fetch(step, s):
        def body(c, carry):
            _chunk_copy(y_hbm, dst_ref[step, c], ybuf.at[s], c, sem.at[s]).start()
            return carry
        lax.fori_loop(0, nch_ref[step], body, 0)

    @pl.when(i == 0)
    def _():
        ybuf[...] = jnp.zeros_like(ybuf)
        fetch(step0, 0)

    @pl.when(i + 1 < n)
    def _():
        fetch(step0 + i + 1, 1 - slot)

    def wait(c, carry):
        _chunk_copy(y_hbm, 0, ybuf.at[slot], 0, sem.at[slot]).wait()
        return carry
    lax.fori_loop(0, nch_ref[step0 + i], wait, 0)

    lp = lp_ref[...]
    col = lax.broadcasted_iota(jnp.int32, (MOE_TD, MOE_L), 1)
    unperm = jnp.where((col == lp[:, 0:1]) | (col == lp[:, 1:2]), 1.0, 0.0).astype(BF16)
    moe = _dot(unperm, ybuf[slot])
    gt2 = mod_ref[0, 5:6, :]
    o_ref[...] = x1_ref[...] + gt2 * _rms(moe, gpost_ref[...])


def _moe_combine(step0, dst, n_chunks, lp, x1, mod, mod_row, y_sorted, g_post2):
    n = x1.shape[0]
    tok = lambda i, *_: (i, 0)
    return pl.pallas_call(
        functools.partial(_moe_combine_kernel, step0),
        grid_spec=pltpu.PrefetchScalarGridSpec(
            num_scalar_prefetch=2,
            grid=(n // MOE_TD,),
            in_specs=[pl.BlockSpec((MOE_TD, LANE), lambda i, *_: (step0 + i, 0)),
                      pl.BlockSpec((MOE_TD, D_MODEL), tok),
                      pl.BlockSpec((1, 6, D_MODEL), lambda i, *_: (mod_row(i), 0, 0)),
                      pl.BlockSpec((1, D_MODEL), lambda i, *_: (0, 0)),
                      pl.BlockSpec(memory_space=pl.ANY)],
            out_specs=pl.BlockSpec((MOE_TD, D_MODEL), tok),
            scratch_shapes=[pltpu.VMEM((2, MOE_L, D_MODEL), BF16), pltpu.SemaphoreType.DMA((2,))]),
        out_shape=jax.ShapeDtypeStruct((n, D_MODEL), F32),
        compiler_params=_cparams("arbitrary"),
        name="moe_combine",
    )(dst, n_chunks, lp, x1, mod, g_post2, y_sorted)


def _moe(part_a, part_b, mod, mod_row_a, mod_row_b, w_gate, w_up, w_down, g_post2):
    cnt = jnp.concatenate([part_a[4], part_b[4]], axis=0)[:, 0, :N_EXPERTS].astype(jnp.int32)
    steps_a = part_a[4].shape[0]
    n_tiles_max = (cnt.shape[0] * MOE_NCH + N_EXPERTS * (TILE_CH - 1)) // TILE_CH + 1
    dst, n_chunks, tail_start, tail_len, tile0, tile1, n_tiles = _plan(cnt, n_tiles_max)
    xs, lp = _dispatch(dst, n_chunks, tail_start, tail_len, n_tiles, part_a[1:4], part_b[1:4], n_tiles_max)
    y_sorted = _moe_gemm(tile0, tile1, n_tiles, xs, w_gate, w_up, w_down)
    ya = _moe_combine(0, dst, n_chunks, lp, part_a[0], mod, mod_row_a, y_sorted, g_post2)
    yb = _moe_combine(steps_a, dst, n_chunks, lp, part_b[0], mod, mod_row_b, y_sorted, g_post2)
    return ya, yb


def _rotate_half_cols(w):
    n = w.shape[-1]
    w4 = w.reshape(w.shape[:-1] + (n // 32, 2, 16))
    return jnp.stack([-w4[..., 1, :], w4[..., 0, :]], axis=-2).reshape(w.shape)


def _pad_lanes(w):
    return jnp.concatenate([w, jnp.zeros(w.shape[:-1] + (LANE - w.shape[-1],), w.dtype)], axis=-1)


def _rope_tables(t):
    rows = t // GRID_W
    n_freq = D_ROPE // 4
    freqs = ROPE_BASE ** (-jnp.arange(n_freq, dtype=F32) / n_freq)
    ang_r = jnp.arange(rows, dtype=F32)[:, None] * freqs
    ang_c = jnp.arange(GRID_W, dtype=F32)[:, None] * freqs
    per_row = lambda a: jnp.repeat(a, GRID_W, axis=0)
    per_col = lambda a: jnp.tile(a, (rows, 1))

    def table(fn):
        r, c = per_row(fn(ang_r)), per_col(fn(ang_c))
        return _pad_lanes(jnp.concatenate([r, r, c, c], axis=-1))
    return table(jnp.cos), table(jnp.sin)


def kernel(x_prompt, x_sample, cache_ckv, cache_krope, c, c_ctx, w_ada, b_ada, g_pre1, g_post1, g_pre2, g_post2, w_in, w_dw, b_dw, conv_ln_g, conv_ln_b, q_norm_g, kv_norm_g, w_uq, w_ukv, w_out, w_rg, b_rg, w_re, b_re, w_gate, w_up, w_down):
    nb, seq, d = x_prompt.shape
    db, dseq, _ = x_sample.shape
    l = 0

    cvec = jnp.concatenate([c_ctx[None, :], c, jnp.zeros((8 - 1 - db, d), F32)], axis=0)
    mod = _ada(cvec, w_ada[l], b_ada[l]).reshape(8, 6, d)

    o_kr = 2 * C_CONV + Q_LORA + KV_LORA
    w_in_l = w_in[l]
    w_kr = w_in_l[:, o_kr:]
    w_in_ctx = jnp.concatenate([w_in_l[:, :o_kr], _pad_lanes(w_kr)], axis=-1).astype(BF16)
    w_in_lat = jnp.concatenate([w_in_l[:, :o_kr], _pad_lanes(w_kr), _pad_lanes(_rotate_half_cols(w_kr))],
                               axis=-1).astype(BF16)
    wuq = w_uq[l].reshape(Q_LORA, N_HEADS, D_NOPE + D_ROPE)
    wq_rope = wuq[:, :, D_NOPE:]
    wq = jnp.concatenate([wuq[:, :, :D_NOPE], _pad_lanes(wq_rope)], axis=-1)
    wq = wq.reshape(Q_LORA, N_HEADS * HEAD_W).astype(BF16)
    wqs = _pad_lanes(_rotate_half_cols(wq_rope)).reshape(Q_LORA, N_HEADS * LANE).astype(BF16)
    wukv4 = w_ukv[l].reshape(KV_LORA, N_HEADS, D_NOPE + D_V)
    wukv = jnp.concatenate([wukv4[:, :, :D_NOPE].reshape(KV_LORA, N_HEADS * D_NOPE),
                            wukv4[:, :, D_NOPE:].reshape(KV_LORA, N_HEADS * D_V)], axis=-1).astype(BF16)
    w_out_b = w_out[l].astype(BF16)
    w_r = jnp.concatenate([w_re[l], w_rg[l], jnp.zeros((d, LANE - N_EXPERTS - N_GROUPS), F32)], axis=-1)
    wr_hi = w_r.astype(BF16)
    wr_lo = jnp.concatenate([wr_hi[:, :LANE // 2], (w_r - wr_hi.astype(F32)).astype(BF16)[:, :LANE // 2]], axis=-1)
    b_r = jnp.concatenate([b_re[l], b_rg[l], jnp.zeros((LANE - N_EXPERTS - N_GROUPS,), F32)])[None, :]
    cos, sin = _rope_tables(dseq)
    row = lambda v: v[l][None, :]

    tm_c = 512
    xp_flat = x_prompt.reshape(1, nb * seq, d)
    conv_w = (w_dw[l], row(b_dw), row(conv_ln_g), row(conv_ln_b))
    assert seq == CONV_TB, "a context step must hold whole sequences of CONV_TB tokens"
    conv_out, q, k, v, ckv, kr = _mix_in(xp_flat, mod, 0, row(g_pre1), w_in_ctx, row(q_norm_g), row(kv_norm_g),
                                         wq, wqs, wukv, cos[:tm_c], sin[:tm_c], conv_w, False, tm_c)
    per_seq = lambda a: a.reshape(nb, seq, a.shape[-1])
    att = _attn(per_seq(q), per_seq(k), per_seq(v), n_seq=4)
    flat = lambda a: a.reshape(1, nb * seq, a.shape[-1])
    post_c = _post(xp_flat, conv_out, flat(att), mod, 0, w_out_b, row(g_post1), row(g_pre2),
                   wr_hi, wr_lo, b_r)
    state_ckv = ckv.reshape(nb, 1, seq, KV_LORA)
    state_krope = kr.reshape(nb, 1, seq, D_ROPE)

    tm_s = 512
    kc, vc = _cache_kv(cache_ckv[:, l], cache_krope[:, l], wukv)
    conv_out, q, k, v, _, _ = _mix_in(x_sample, mod, 1, row(g_pre1), w_in_lat, row(q_norm_g), row(kv_norm_g),
                                      wq, wqs, wukv, cos, sin, conv_w, True, tm_s)
    att = _attn(q, k, v, kc, vc)
    post_s = _post(x_sample, conv_out, att, mod, 1, w_out_b, row(g_post1), row(g_pre2), wr_hi, wr_lo, b_r)

    tokens = lambda parts: tuple(a.reshape((-1,) + a.shape[2:]) for a in parts)
    steps_per_req = dseq // MOE_TD
    yp, ys = _moe(tokens(post_c), tokens(post_s), mod, lambda i: 0, lambda i: 1 + i // steps_per_req,
                  w_gate[l], w_up[l], w_down[l], row(g_post2))

    return (yp.reshape(nb, seq, d), ys.reshape(db, dseq, d), state_ckv, state_krope)
```

```python
import functools

import jax
import jax.numpy as jnp
import numpy as np
from jax import lax
from jax.experimental import pallas as pl
from jax.experimental.pallas import tpu as pltpu

D_MODEL = 1024
GRID_W = 64
C_CONV = 512
CONV_K = 31
N_HEADS = 4
D_NOPE = 128
D_ROPE = 64
D_V = 128
Q_LORA = 384
KV_LORA = 256
N_GROUPS = 4
E_PER_GROUP = 8
N_EXPERTS = 32
D_EXPERT = 256
ROPE_BASE = 10000.0
EPS = 1e-6
ATT_SCALE = (D_NOPE + D_ROPE) ** -0.5
LOG2E = 1.4426950408889634

LANE = 128
SUBLANE = 8
HEAD_W = 2 * LANE
CONV_HALO = 16
MOE_TD = 512
CHUNK = 2 * SUBLANE
MOE_NCH = (2 * MOE_TD + N_EXPERTS * (CHUNK - 1)) // CHUNK
MOE_L = MOE_NCH * CHUNK
MOE_TILE = 256
TILE_CH = MOE_TILE // CHUNK
GEMM_RING = 6
GEMM_OUT = 4
XS_W = D_MODEL + LANE
VMEM_LIMIT = 56 * 1024 * 1024

BF16 = jnp.bfloat16
F32 = jnp.float32


def _cparams(*sem):
    return pltpu.CompilerParams(dimension_semantics=sem, vmem_limit_bytes=VMEM_LIMIT)


def _rms(x, g):
    return x * lax.rsqrt(jnp.mean(x * x, axis=-1, keepdims=True) + EPS) * g


def _sigmoid(x):
    return 1.0 / (1.0 + jnp.exp(-x))


def _dot(a, b):
    return jnp.dot(a, b, preferred_element_type=F32)


def _chunk_copy(src, src_chunk, dst, dst_chunk, sem):
    rows = lambda c: pl.ds(c * CHUNK if isinstance(c, int) else pl.multiple_of(c * CHUNK, CHUNK), CHUNK)
    return pltpu.make_async_copy(src.at[rows(src_chunk), :], dst.at[rows(dst_chunk), :], sem)


def _ada_kernel(c_ref, w_ref, b_ref, o_ref):
    c = c_ref[...]
    s = (c * _sigmoid(c)).astype(BF16)
    o_ref[...] = _dot(s, w_ref[...].astype(BF16)) + b_ref[...]


def _ada(cvec, w_ada, b_ada):
    n = w_ada.shape[1]
    tn = 1536
    return pl.pallas_call(
        _ada_kernel,
        grid=(n // tn,),
        in_specs=[pl.BlockSpec((8, D_MODEL), lambda j: (0, 0)),
                  pl.BlockSpec((D_MODEL, tn), lambda j: (0, j)),
                  pl.BlockSpec((1, tn), lambda j: (0, j))],
        out_specs=pl.BlockSpec((8, tn), lambda j: (0, j)),
        out_shape=jax.ShapeDtypeStruct((8, n), F32),
        compiler_params=_cparams("arbitrary"),
        name="ada",
    )(cvec, w_ada, b_ada.reshape(1, n))


CONV_TB = 256
CONV_WIN = CONV_TB + 2 * CONV_HALO
CONV_TT = 32


def _conv_window(win_ref, sh_ref, w_ref, b_ref, g_ref, bb_ref, o_ref, row0):
    first = CONV_HALO - CONV_K // 2
    rows = CONV_WIN - SUBLANE
    win = win_ref[...]
    for s in range(SUBLANE):
        sh_ref[s, 0:rows, :] = win[s:s + rows, :]
    for c in range(CONV_TB // CONV_TT):
        acc = jnp.zeros((CONV_TT, C_CONV), F32)
        for k in range(CONV_K):
            off = first + k
            r0 = c * CONV_TT + off // SUBLANE * SUBLANE
            acc = acc + sh_ref[off % SUBLANE, r0:r0 + CONV_TT, :] * w_ref[k:k + 1, :]
        y = acc + b_ref[...]
        mu = jnp.mean(y, axis=-1, keepdims=True)
        yc = y - mu
        var = jnp.mean(yc * yc, axis=-1, keepdims=True)
        z = yc * lax.rsqrt(var + EPS) * g_ref[...] + bb_ref[...]
        o_ref[0, row0 + c * CONV_TT:row0 + (c + 1) * CONV_TT, :] = (z * _sigmoid(z)).astype(BF16)


def _mix_in_kernel(latent, x_ref, xp_ref, xn_ref, mod_ref, gpre_ref, win_ref, qg_ref, kvg_ref, wq_ref, wqs_ref,
                   wukv_ref, cos_ref, sin_ref, wdw_ref, bdw_ref, lng_ref, lnb_ref,
                   conv_ref, q_ref, k_ref, v_ref, ckv_ref, kr_ref, cwin, csh, wbf):
    rope = latent

    @pl.when((pl.program_id(0) == 0) & (pl.program_id(1) == 0))
    def _():
        wbf[...] = win_ref[...].astype(BF16)

    x = x_ref[0]
    tm = x.shape[0]
    sh1 = mod_ref[0, 0:1, :]
    sc1 = mod_ref[0, 1:2, :]
    modulate = lambda v: _rms(v, gpre_ref[...]) * (1.0 + sc1) + sh1
    glu = lambda uc: uc[:, :C_CONV] * _sigmoid(uc[:, C_CONV:2 * C_CONV])
    u = _dot(modulate(x).astype(BF16), wbf[...])
    hglu = glu(u)

    zeros = jnp.zeros((CONV_HALO, C_CONV), F32)
    n_win = tm // CONV_TB
    if latent:
        i = pl.program_id(1)
        xh = jnp.concatenate([xp_ref[0], xn_ref[0]], axis=0)
        hh = glu(_dot(modulate(xh).astype(BF16), wbf[:, 0:2 * C_CONV]))
        before = jnp.where(i > 0, hh[0:CONV_HALO], zeros)
        after = jnp.where(i < pl.num_programs(1) - 1, hh[CONV_HALO:2 * CONV_HALO], zeros)
    for j in range(n_win):
        if latent:
            lo, hi, d0 = j * CONV_TB - CONV_HALO, j * CONV_TB + CONV_TB + CONV_HALO, 0
            if lo < 0:
                cwin[j, 0:CONV_HALO, :] = before
                lo, d0 = 0, CONV_HALO
            if hi > tm:
                cwin[j, CONV_WIN - CONV_HALO:CONV_WIN, :] = after
                hi = tm
            cwin[j, d0:d0 + hi - lo, :] = hglu[lo:hi, :]
        else:
            cwin[j, 0:CONV_HALO, :] = zeros
            cwin[j, CONV_HALO:CONV_HALO + CONV_TB, :] = hglu[j * CONV_TB:(j + 1) * CONV_TB, :]
            cwin[j, CONV_HALO + CONV_TB:CONV_WIN, :] = zeros
    for j in range(n_win):
        _conv_window(cwin.at[j], csh, wdw_ref, bdw_ref, lng_ref, lnb_ref, conv_ref, j * CONV_TB)

    o_q = 2 * C_CONV
    o_kv = o_q + Q_LORA
    o_kr = o_kv + KV_LORA
    qn = _rms(u[:, o_q:o_kv], qg_ref[...]).astype(BF16)
    qf = _dot(qn, wq_ref[...])
    ckv = _rms(u[:, o_kv:o_kr], kvg_ref[...])
    ckv_ref[0] = ckv
    kvd = _dot(ckv.astype(BF16), wukv_ref[...])
    kr64 = u[:, o_kr:o_kr + D_ROPE]
    kr_ref[0] = kr64
    kr = jnp.concatenate([kr64, jnp.zeros((tm, LANE - D_ROPE), F32)], axis=-1)
    if rope:
        cos = cos_ref[...]
        sin = sin_ref[...]
        qs = _dot(qn, wqs_ref[...])
        lane = lax.broadcasted_iota(jnp.int32, kr.shape, 1)
        partner = jnp.where(lane % (D_ROPE // 2) < D_ROPE // 4,
                            -pltpu.roll(kr, LANE - D_ROPE // 4, axis=1), pltpu.roll(kr, D_ROPE // 4, axis=1))
        kr = kr * cos + partner * sin
    q_parts = []
    k_parts = []
    for hd in range(N_HEADS):
        q_parts.append(qf[:, hd * HEAD_W:hd * HEAD_W + LANE])
        qr = qf[:, hd * HEAD_W + LANE:(hd + 1) * HEAD_W]
        if rope:
            qr = qr * cos + qs[:, hd * LANE:(hd + 1) * LANE] * sin
        q_parts.append(qr)
        k_parts.append(kvd[:, hd * D_NOPE:(hd + 1) * D_NOPE])
        k_parts.append(kr)
    q_ref[0] = jnp.concatenate(q_parts, axis=-1).astype(BF16)
    k_ref[0] = jnp.concatenate(k_parts, axis=-1).astype(BF16)
    v_ref[0] = kvd[:, N_HEADS * D_NOPE:].astype(BF16)


def _mix_in(x, mod, mod_row0, g_pre1, w_in_ext, q_norm_g, kv_norm_g, wq, wqs, wukv, cos, sin, conv_w, latent, tm):
    b, t, _ = x.shape
    ncol = w_in_ext.shape[1]
    const = lambda bi, i: (0, 0)
    tok = lambda bi, i: (bi, i, 0)
    table = (lambda bi, i: (i, 0)) if latent else const
    halo_blocks = tm // CONV_HALO
    prev_rows = lambda bi, i: (bi, jnp.maximum(i * halo_blocks - 1, 0), 0)
    next_rows = lambda bi, i: (bi, jnp.minimum((i + 1) * halo_blocks, t // CONV_HALO - 1), 0)
    outs = [(C_CONV, BF16), (N_HEADS * HEAD_W, BF16), (N_HEADS * HEAD_W, BF16), (N_HEADS * D_V, BF16),
            (KV_LORA, F32), (D_ROPE, F32)]
    return pl.pallas_call(
        functools.partial(_mix_in_kernel, latent),
        grid=(b, t // tm),
        in_specs=[pl.BlockSpec((1, tm, D_MODEL), tok),
                  pl.BlockSpec((1, CONV_HALO, D_MODEL), prev_rows),
                  pl.BlockSpec((1, CONV_HALO, D_MODEL), next_rows),
                  pl.BlockSpec((1, 6, D_MODEL), lambda bi, i: (mod_row0 + bi, 0, 0)),
                  pl.BlockSpec((1, D_MODEL), const),
                  pl.BlockSpec((D_MODEL, ncol), const),
                  pl.BlockSpec((1, Q_LORA), const),
                  pl.BlockSpec((1, KV_LORA), const),
                  pl.BlockSpec((Q_LORA, N_HEADS * HEAD_W), const),
                  pl.BlockSpec((Q_LORA, N_HEADS * LANE), const),
                  pl.BlockSpec((KV_LORA, N_HEADS * (D_NOPE + D_V)), const),
                  pl.BlockSpec((tm, LANE), table),
                  pl.BlockSpec((tm, LANE), table),
                  pl.BlockSpec((CONV_K, C_CONV), const),
                  pl.BlockSpec((1, C_CONV), const),
                  pl.BlockSpec((1, C_CONV), const),
                  pl.BlockSpec((1, C_CONV), const)],
        out_specs=[pl.BlockSpec((1, tm, w), tok) for w, _ in outs],
        out_shape=[jax.ShapeDtypeStruct((b, t, w), dt) for w, dt in outs],
        scratch_shapes=[pltpu.VMEM((tm // CONV_TB, CONV_WIN, C_CONV), F32),
                        pltpu.VMEM((SUBLANE, CONV_WIN, C_CONV), F32),
                        pltpu.VMEM((D_MODEL, ncol), BF16)],
        compiler_params=_cparams("arbitrary", "arbitrary"),
        name="mix_in_latent" if latent else "mix_in",
    )(x, x, x, mod, g_pre1, w_in_ext, q_norm_g, kv_norm_g, wq, wqs, wukv, cos, sin, *conv_w)


def _qk(q, k):
    return lax.dot_general(q, k, (((1,), (1,)), ((), ())), preferred_element_type=F32)


ATT_KCHUNK = 256


def _attn_kernel(cached, q_ref, k_ref, v_ref, *rest):
    if cached:
        kc_ref, vc_ref, o_ref, s_scr = rest
        sources = [(kc_ref, vc_ref), (k_ref, v_ref)]
    else:
        o_ref, s_scr = rest
        sources = [(k_ref, v_ref)]
    n_seq, tq, _ = q_ref.shape
    chunks = []
    for kr, vr in sources:
        for c0 in range(0, kr.shape[1], ATT_KCHUNK):
            c1 = min(c0 + ATT_KCHUNK, kr.shape[1])
            col = chunks[-1][4] + chunks[-1][3] - chunks[-1][2] if chunks else 0
            chunks.append((kr, vr, c0, c1, col))
    for b in range(n_seq):
        outs = []
        for hd in range(N_HEADS):
            ks = slice(hd * HEAD_W, (hd + 1) * HEAD_W)
            vs = slice(hd * D_V, (hd + 1) * D_V)
            qh = q_ref[b, :, ks]
            m = jnp.full((tq, 1), -jnp.inf, F32)
            for kr, vr, c0, c1, col in chunks:
                s = _qk(qh, kr[b, c0:c1, ks]) * (ATT_SCALE * LOG2E)
                s_scr[hd, :, col:col + c1 - c0] = s
                m = jnp.maximum(m, jnp.max(s, axis=-1, keepdims=True))
            l = jnp.zeros((tq, 1), F32)
            o = jnp.zeros((tq, D_V), F32)
            for kr, vr, c0, c1, col in chunks:
                p = jnp.exp2(s_scr[hd, :, col:col + c1 - c0] - m)
                l = l + jnp.sum(p, axis=-1, keepdims=True)
                o = o + _dot(p.astype(BF16), vr[b, c0:c1, vs])
            outs.append(o / l)
        o_ref[b] = jnp.concatenate(outs, axis=-1).astype(BF16)


def _attn(q, k, v, kc=None, vc=None, tq=256, n_seq=1):
    b, t, _ = q.shape
    s = k.shape[1]
    cached = kc is not None
    whole = lambda bi, i: (bi, 0, 0)
    in_specs = [pl.BlockSpec((n_seq, tq, N_HEADS * HEAD_W), lambda bi, i: (bi, i, 0)),
                pl.BlockSpec((n_seq, s, N_HEADS * HEAD_W), whole),
                pl.BlockSpec((n_seq, s, N_HEADS * D_V), whole)]
    args = [q, k, v]
    if cached:
        sc = kc.shape[1]
        in_specs += [pl.BlockSpec((n_seq, sc, N_HEADS * HEAD_W), whole),
                     pl.BlockSpec((n_seq, sc, N_HEADS * D_V), whole)]
        args += [kc, vc]
        s += sc
    return pl.pallas_call(
        functools.partial(_attn_kernel, cached),
        grid=(b // n_seq, t // tq),
        in_specs=in_specs,
        out_specs=pl.BlockSpec((n_seq, tq, N_HEADS * D_V), lambda bi, i: (bi, i, 0)),
        out_shape=jax.ShapeDtypeStruct((b, t, N_HEADS * D_V), BF16),
        scratch_shapes=[pltpu.VMEM((N_HEADS, tq, s), F32)],
        compiler_params=_cparams("arbitrary", "arbitrary"),
        name="attn_cached" if cached else "attn",
    )(*args)


def _cache_kv_kernel(ckv_ref, kr_ref, wukv_ref, k_ref, v_ref):
    kvd = _dot(ckv_ref[0].astype(BF16), wukv_ref[...])
    kr = kr_ref[0]
    kr = jnp.concatenate([kr, jnp.zeros_like(kr)], axis=-1)
    parts = []
    for hd in range(N_HEADS):
        parts.append(kvd[:, hd * D_NOPE:(hd + 1) * D_NOPE])
        parts.append(kr)
    k_ref[0] = jnp.concatenate(parts, axis=-1).astype(BF16)
    v_ref[0] = kvd[:, N_HEADS * D_NOPE:].astype(BF16)


def _cache_kv(ckv, krope, wukv):
    b, s, _ = ckv.shape
    tok = lambda bi: (bi, 0, 0)
    return pl.pallas_call(
        _cache_kv_kernel,
        grid=(b,),
        in_specs=[pl.BlockSpec((1, s, KV_LORA), tok),
                  pl.BlockSpec((1, s, D_ROPE), tok),
                  pl.BlockSpec((KV_LORA, N_HEADS * (D_NOPE + D_V)), lambda bi: (0, 0))],
        out_specs=[pl.BlockSpec((1, s, N_HEADS * HEAD_W), tok),
                   pl.BlockSpec((1, s, N_HEADS * D_V), tok)],
        out_shape=[jax.ShapeDtypeStruct((b, s, N_HEADS * HEAD_W), BF16),
                   jax.ShapeDtypeStruct((b, s, N_HEADS * D_V), BF16)],
        compiler_params=_cparams("arbitrary"),
        name="cache_kv",
    )(ckv, krope, wukv)


def _post_kernel(x_ref, conv_ref, att_ref, mod_ref, wo_ref, gpost_ref, gpre2_ref, wr_hi_ref, wr_cat_ref,
                 br_ref, x1_ref, h2_ref, ri_ref, rw_ref, cnt_ref):
    out = _dot(conv_ref[0], wo_ref[:C_CONV, :]) + _dot(att_ref[0], wo_ref[C_CONV:, :])
    gt1 = mod_ref[0, 2:3, :]
    sh2 = mod_ref[0, 3:4, :]
    sc2 = mod_ref[0, 4:5, :]
    x1 = x_ref[0] + gt1 * _rms(out, gpost_ref[...])
    x1_ref[0] = x1
    h2 = _rms(x1, gpre2_ref[...]) * (1.0 + sc2) + sh2
    h_hi = h2.astype(BF16)
    h2_ref[0] = h_hi
    h_lo = (h2 - h_hi.astype(F32)).astype(BF16)
    hi_terms = _dot(h_hi, wr_cat_ref[...])
    logits = hi_terms + pltpu.roll(hi_terms, LANE // 2, axis=1) + _dot(h_lo, wr_hi_ref[...]) + br_ref[...]

    lane = lax.broadcasted_iota(jnp.int32, logits.shape, 1)
    neg = jnp.float32(-jnp.inf)
    big = jnp.int32(LANE)
    is_g = (lane >= N_EXPERTS) & (lane < N_EXPERTS + N_GROUPS)
    lg = jnp.where(is_g, logits, neg)
    gmax = jnp.max(lg, axis=-1, keepdims=True)
    gidx = jnp.min(jnp.where(lg == gmax, lane, big), axis=-1, keepdims=True) - N_EXPERTS
    g_top = 1.0 / jnp.sum(jnp.exp(lg - gmax), axis=-1, keepdims=True)

    in_grp = (lane >= gidx * E_PER_GROUP) & (lane < (gidx + 1) * E_PER_GROUP)
    le = jnp.where(in_grp, logits, neg)
    m1 = jnp.max(le, axis=-1, keepdims=True)
    i1 = jnp.min(jnp.where(le == m1, lane, big), axis=-1, keepdims=True)
    le2 = jnp.where(lane == i1, neg, le)
    m2 = jnp.max(le2, axis=-1, keepdims=True)
    i2 = jnp.min(jnp.where(le2 == m2, lane, big), axis=-1, keepdims=True)
    r = jnp.exp(m2 - m1)
    w1 = g_top / (1.0 + r)
    w2 = g_top * r / (1.0 + r)
    ri_ref[0] = jnp.where(lane == 0, i1, jnp.where(lane == 1, i2, 0))
    rw_ref[0] = jnp.where(lane == 0, w1, jnp.where(lane == 1, w2, 0.0))
    pairs = jnp.sum(jnp.where((lane == i1) | (lane == i2), 1.0, 0.0), axis=0, keepdims=True)
    cnt_ref[0, 0] = jnp.broadcast_to(pairs, (SUBLANE, LANE))


def _post(x, conv_out, att, mod, mod_row0, w_out, g_post1, g_pre2, wr_hi, wr_lo, br):
    b, t, _ = x.shape
    tm = MOE_TD
    const = lambda bi, i: (0, 0)
    tok = lambda bi, i: (bi, i, 0)
    return pl.pallas_call(
        _post_kernel,
        grid=(b, t // tm),
        in_specs=[pl.BlockSpec((1, tm, D_MODEL), tok),
                  pl.BlockSpec((1, tm, C_CONV), tok),
                  pl.BlockSpec((1, tm, N_HEADS * D_V), tok),
                  pl.BlockSpec((1, 6, D_MODEL), lambda bi, i: (mod_row0 + bi, 0, 0)),
                  pl.BlockSpec((D_MODEL, D_MODEL), const),
                  pl.BlockSpec((1, D_MODEL), const),
                  pl.BlockSpec((1, D_MODEL), const),
                  pl.BlockSpec((D_MODEL, LANE), const),
                  pl.BlockSpec((D_MODEL, LANE), const),
                  pl.BlockSpec((1, LANE), const)],
        out_specs=[pl.BlockSpec((1, tm, D_MODEL), tok),
                   pl.BlockSpec((1, tm, D_MODEL), tok),
                   pl.BlockSpec((1, tm, LANE), tok),
                   pl.BlockSpec((1, tm, LANE), tok),
                   pl.BlockSpec((1, 1, SUBLANE, LANE), lambda bi, i: (bi, i, 0, 0))],
        out_shape=[jax.ShapeDtypeStruct((b, t, D_MODEL), F32),
                   jax.ShapeDtypeStruct((b, t, D_MODEL), BF16),
                   jax.ShapeDtypeStruct((b, t, LANE), jnp.int32),
                   jax.ShapeDtypeStruct((b, t, LANE), F32),
                   jax.ShapeDtypeStruct((b, t // tm, SUBLANE, LANE), F32)],
        compiler_params=_cparams("arbitrary", "arbitrary"),
        name="post",
    )(x, conv_out, att, mod, w_out, g_post1, g_pre2, wr_hi, wr_lo, br)


def _one_hots(ri):
    lane = lax.broadcasted_iota(jnp.int32, ri.shape, 1)
    oh1 = lane == ri[:, 0:1]
    oh2 = lane == ri[:, 1:2]
    return oh1, oh2, jnp.where(oh1 | oh2, 1.0, 0.0)


def _strictly_lower(n):
    r = lax.broadcasted_iota(jnp.int32, (n, n), 0)
    c = lax.broadcasted_iota(jnp.int32, (n, n), 1)
    return jnp.where(c < r, 1.0, 0.0).astype(BF16)


def _local_positions(oh1, oh2, oh, earlier_ref):
    cnt = jnp.sum(oh, axis=0, keepdims=True)
    nch = jnp.floor((cnt + (CHUNK - 1)) * (1.0 / CHUNK))
    a = lax.broadcasted_iota(jnp.int32, (LANE, LANE), 0)
    b = lax.broadcasted_iota(jnp.int32, (LANE, LANE), 1)
    lower_experts = jnp.where(a < b, 1.0, 0.0).astype(BF16)
    run_start = _dot(jnp.broadcast_to(nch, (SUBLANE, LANE)).astype(BF16), lower_experts)[0:1, :] * CHUNK
    pos = _dot(earlier_ref[...], oh.astype(BF16)) + run_start
    lp1 = jnp.sum(jnp.where(oh1, pos, 0.0), axis=-1, keepdims=True)
    lp2 = jnp.sum(jnp.where(oh2, pos, 0.0), axis=-1, keepdims=True)
    return lp1, lp2


def _plan(cnt, n_tiles_max):
    k = (cnt + (CHUNK - 1)) // CHUNK
    run_end = jnp.cumsum(k, axis=1)
    run_start = run_end - k
    n_chunks = run_end[:, -1]
    total = jnp.sum(k, axis=0)
    padded = (total + (TILE_CH - 1)) // TILE_CH * TILE_CH
    seg_end = jnp.cumsum(padded)
    seg_start = seg_end - padded
    base = seg_start[None, :] + jnp.cumsum(k, axis=0) - k
    c = jnp.arange(MOE_NCH, dtype=jnp.int32)[None, :, None]
    in_run = (run_start[:, None, :] <= c) & (c < run_end[:, None, :])
    dst = jnp.sum(jnp.where(in_run, base[:, None, :] + c - run_start[:, None, :], 0), axis=2)
    n_tiles = seg_end[-1] // TILE_CH
    tail_start = seg_start + total
    tail_len = padded - total
    i32 = lambda t: t.astype(jnp.int32)
    return (i32(dst), i32(n_chunks), i32(tail_start), i32(tail_len), i32(seg_start // TILE_CH),
            i32(seg_end // TILE_CH), i32(n_tiles.reshape(1)))


def _dispatch_kernel(steps_a, dst_ref, nch_ref, ts_ref, tl_ref, nt_ref, h2a_ref, ria_ref, rwa_ref, h2b_ref, rib_ref,
                     rwb_ref, xs_hbm, lp_ref, xbuf, zx, sem, zsem, earlier):
    i = pl.program_id(0)
    n = pl.num_programs(0)
    slot = i % 2
    td = MOE_TD
    n_tiles_max = xs_hbm.shape[0] // MOE_TILE

    def zero_tile(m):
        rows = pl.ds(pl.multiple_of(m * MOE_TILE, MOE_TILE), MOE_TILE)
        return pltpu.make_async_copy(zx, xs_hbm.at[rows, :], zsem.at[1])

    @pl.when(i == 0)
    def _():
        zx[...] = jnp.zeros_like(zx)
        earlier[...] = _strictly_lower(MOE_TD)
        for e in range(N_EXPERTS):
            def fill(m, carry, e=e):
                _chunk_copy(zx, 0, xs_hbm, ts_ref[e] + m, zsem.at[0]).start()
                return carry
            lax.fori_loop(0, tl_ref[e], fill, 0)

        def fill_tile(m, carry):
            zero_tile(m).start()
            return carry
        lax.fori_loop(nt_ref[0], n_tiles_max, fill_tile, 0)

    from_a = i < steps_a
    ri = jnp.where(from_a, ria_ref[...], rib_ref[...])
    rw = jnp.where(from_a, rwa_ref[...], rwb_ref[...])
    h2 = jnp.where(from_a, h2a_ref[...], h2b_ref[...])
    oh1, oh2, oh = _one_hots(ri)
    lp1, lp2 = _local_positions(oh1, oh2, oh, earlier)
    lane = lax.broadcasted_iota(jnp.int32, ri.shape, 1)
    lp_ref[...] = jnp.where(lane == 0, lp1, jnp.where(lane == 1, lp2, 0.0)).astype(jnp.int32)

    cols = jnp.where(lane == 0, lp1, jnp.where(lane == 1, lp2, jnp.where(lane == 2, rw[:, 0:1],
                                                                         jnp.where(lane == 3, rw[:, 1:2], 0.0))))
    rows4 = cols.T
    row = lax.broadcasted_iota(jnp.int32, (MOE_L, td), 0).astype(F32)
    p1 = row == rows4[0:1, :]
    p2 = row == rows4[1:2, :]
    perm = jnp.where(p1 | p2, 1.0, 0.0).astype(BF16)
    xbuf[slot, :, 0:D_MODEL] = _dot(perm, h2).astype(BF16)
    w = jnp.sum(jnp.where(p1, rows4[2:3, :], 0.0) + jnp.where(p2, rows4[3:4, :], 0.0), axis=-1, keepdims=True)
    w0 = w.astype(BF16).astype(F32)
    w1 = (w - w0).astype(BF16).astype(F32)
    w2 = (w - w0 - w1).astype(BF16).astype(F32)
    wl = lax.broadcasted_iota(jnp.int32, (MOE_L, LANE), 1)
    terms = jnp.where(wl == 0, w0, jnp.where(wl == 1, w1, jnp.where(wl == 2, w2, 0.0)))
    xbuf[slot, :, D_MODEL:XS_W] = terms.astype(BF16)

    def wait_chunks(count, s):
        def body(c, carry):
            _chunk_copy(xbuf.at[s], 0, xs_hbm, 0, sem.at[s]).wait()
            return carry
        lax.fori_loop(0, count, body, 0)

    @pl.when(i > 0)
    def _():
        wait_chunks(nch_ref[i - 1], 1 - slot)

    def send(c, carry):
        _chunk_copy(xbuf.at[slot], c, xs_hbm, dst_ref[i, c], sem.at[slot]).start()
        return carry
    lax.fori_loop(0, nch_ref[i], send, 0)

    @pl.when(i == n - 1)
    def _():
        wait_chunks(nch_ref[i], slot)
        for e in range(N_EXPERTS):
            def drain(m, carry):
                _chunk_copy(zx, 0, xs_hbm, 0, zsem.at[0]).wait()
                return carry
            lax.fori_loop(0, tl_ref[e], drain, 0)

        def drain_tile(m, carry):
            zero_tile(m).wait()
            return carry
        lax.fori_loop(nt_ref[0], n_tiles_max, drain_tile, 0)


def _dispatch(dst, n_chunks, tail_start, tail_len, n_tiles, part_a, part_b, n_tiles_max):
    steps_a = part_a[0].shape[0] // MOE_TD
    steps_b = part_b[0].shape[0] // MOE_TD
    in_a = lambda i, *_: (jnp.minimum(i, steps_a - 1), 0)
    in_b = lambda i, *_: (jnp.maximum(i - steps_a, 0), 0)
    specs = lambda f: [pl.BlockSpec((MOE_TD, D_MODEL), f), pl.BlockSpec((MOE_TD, LANE), f),
                       pl.BlockSpec((MOE_TD, LANE), f)]
    rows = n_tiles_max * MOE_TILE
    return pl.pallas_call(
        functools.partial(_dispatch_kernel, steps_a),
        grid_spec=pltpu.PrefetchScalarGridSpec(
            num_scalar_prefetch=5,
            grid=(steps_a + steps_b,),
            in_specs=specs(in_a) + specs(in_b),
            out_specs=[pl.BlockSpec(memory_space=pl.ANY),
                       pl.BlockSpec((MOE_TD, LANE), lambda i, *_: (i, 0))],
            scratch_shapes=[pltpu.VMEM((2, MOE_L, XS_W), BF16), pltpu.VMEM((MOE_TILE, XS_W), BF16),
                            pltpu.SemaphoreType.DMA((2,)), pltpu.SemaphoreType.DMA((2,)),
                            pltpu.VMEM((MOE_TD, MOE_TD), BF16)]),
        out_shape=[jax.ShapeDtypeStruct((rows, XS_W), BF16),
                   jax.ShapeDtypeStruct(((steps_a + steps_b) * MOE_TD, LANE), jnp.int32)],
        compiler_params=_cparams("arbitrary"),
        name="moe_dispatch",
    )(dst, n_chunks, tail_start, tail_len, n_tiles, *part_a, *part_b)


def _moe_gemm_kernel(t0_ref, t1_ref, nt_ref, wg_ref, wu_ref, wd_ref, xs_hbm, y_hbm,
                     wg_b, wu_b, wd_b, xbuf, ybuf, zbuf, isem, osem, zsem, issued):
    e = pl.program_id(0)
    nt = nt_ref[0]
    n_tiles_max = y_hbm.shape[0] // MOE_TILE
    tile_rows = lambda t: pl.ds(pl.multiple_of(t * MOE_TILE, MOE_TILE), MOE_TILE)

    def load(t, s):
        return (pltpu.make_async_copy(xs_hbm.at[tile_rows(t), :], xbuf.at[s], isem.at[s]),)

    def store(t, s):
        return pltpu.make_async_copy(ybuf.at[s], y_hbm.at[tile_rows(t), :], osem.at[s])

    def zero_tile(t):
        return pltpu.make_async_copy(zbuf, y_hbm.at[tile_rows(t), :], zsem.at[0])

    @pl.when(e == 0)
    def _():
        issued[0] = 0
        zbuf[...] = jnp.zeros_like(zbuf)

        def fill(t, carry):
            zero_tile(t).start()
            return carry
        lax.fori_loop(nt, n_tiles_max, fill, 0)

    wg_b[...] = wg_ref[0].astype(BF16)
    wu_b[...] = wu_ref[0].astype(BF16)
    wd_b[...] = wd_ref[0].astype(BF16)

    def top_up(t):
        upto = jnp.minimum(t + GEMM_RING, nt)

        def start(u, carry):
            for cp in load(u, u % GEMM_RING):
                cp.start()
            return carry
        lax.fori_loop(issued[0], upto, start, 0)
        issued[0] = jnp.maximum(issued[0], upto)

    def process(t, n):
        top_up(t)
        for k in range(n):
            for cp in load(t + k, (t + k) % GEMM_RING):
                cp.wait()
        rows = [xbuf[(t + k) % GEMM_RING] for k in range(n)]
        xw = rows[0] if n == 1 else jnp.concatenate(rows, axis=0)
        x = xw[:, 0:D_MODEL]
        wt = xw[:, D_MODEL:XS_W].astype(F32)
        w = wt[:, 0:1] + wt[:, 1:2] + wt[:, 2:3]
        a = _dot(x, wg_b[...])
        u = _dot(x, wu_b[...])
        he = (a * _sigmoid(a)) * u
        y = (_dot(he.astype(BF16), wd_b[...]) * w).astype(BF16)
        for k in range(n):
            s = (t + k) % GEMM_OUT

            @pl.when(t + k >= GEMM_OUT)
            def _(k=k, s=s):
                store(t + k - GEMM_OUT, s).wait()
            ybuf[s] = y[k * MOE_TILE:(k + 1) * MOE_TILE, :]
            store(t + k, s).start()

    t0 = t0_ref[e]
    t1 = t1_ref[e]

    def pair(p, carry):
        process(t0 + 2 * p, 2)
        return carry
    lax.fori_loop(0, (t1 - t0) // 2, pair, 0)

    @pl.when((t1 - t0) % 2 == 1)
    def _():
        process(t1 - 1, 1)

    @pl.when(e == pl.num_programs(0) - 1)
    def _():
        for k in range(1, GEMM_OUT + 1):
            @pl.when(nt >= k)
            def _(k=k):
                store(nt - k, (nt - k) % GEMM_OUT).wait()

        def drain(t, carry):
            zero_tile(t).wait()
            return carry
        lax.fori_loop(nt, n_tiles_max, drain, 0)


def _moe_gemm(tile0, tile1, n_tiles, xs, w_gate, w_up, w_down):
    wmap = lambda e, *_: (e, 0, 0)
    return pl.pallas_call(
        _moe_gemm_kernel,
        grid_spec=pltpu.PrefetchScalarGridSpec(
            num_scalar_prefetch=3,
            grid=(N_EXPERTS,),
            in_specs=[pl.BlockSpec((1, D_MODEL, D_EXPERT), wmap),
                      pl.BlockSpec((1, D_MODEL, D_EXPERT), wmap),
                      pl.BlockSpec((1, D_EXPERT, D_MODEL), wmap),
                      pl.BlockSpec(memory_space=pl.ANY)],
            out_specs=pl.BlockSpec(memory_space=pl.ANY),
            scratch_shapes=[pltpu.VMEM((D_MODEL, D_EXPERT), BF16), pltpu.VMEM((D_MODEL, D_EXPERT), BF16),
                            pltpu.VMEM((D_EXPERT, D_MODEL), BF16),
                            pltpu.VMEM((GEMM_RING, MOE_TILE, XS_W), BF16),
                            pltpu.VMEM((GEMM_OUT, MOE_TILE, D_MODEL), BF16), pltpu.VMEM((MOE_TILE, D_MODEL), BF16),
                            pltpu.SemaphoreType.DMA((GEMM_RING,)), pltpu.SemaphoreType.DMA((GEMM_OUT,)),
                            pltpu.SemaphoreType.DMA((1,)), pltpu.SMEM((1,), jnp.int32)]),
        out_shape=jax.ShapeDtypeStruct((xs.shape[0], D_MODEL), BF16),
        compiler_params=_cparams("arbitrary"),
        name="moe_gemm",
    )(tile0, tile1, n_tiles, w_gate, w_up, w_down, xs)


def _moe_combine_kernel(step0, dst_ref, nch_ref, lp_ref, x1_ref, mod_ref, gpost_ref, y_hbm, o_ref, ybuf, sem):
    i = pl.program_id(0)
    n = pl.num_programs(0)
    slot = i % 2

    def fetch(step, s):
        def body(c, carry):
            _chunk_copy(y_hbm, dst_ref[step, c], ybuf.at[s], c, sem.at[s]).start()
            return carry
        lax.fori_loop(0, nch_ref[step], body, 0)

    @pl.when(i == 0)
    def _():
        ybuf[...] = jnp.zeros_like(ybuf)
        fetch(step0, 0)

    @pl.when(i + 1 < n)
    def _():
        fetch(step0 + i + 1, 1 - slot)

    def wait(c, carry):
        _chunk_copy(y_hbm, 0, ybuf.at[slot], 0, sem.at[slot]).wait()
        return carry
    lax.fori_loop(0, nch_ref[step0 + i], wait, 0)

    lp = lp_ref[...]
    col = lax.broadcasted_iota(jnp.int32, (MOE_TD, MOE_L), 1)
    unperm = jnp.where((col == lp[:, 0:1]) | (col == lp[:, 1:2]), 1.0, 0.0).astype(BF16)
    moe = _dot(unperm, ybuf[slot])
    gt2 = mod_ref[0, 5:6, :]
    o_ref[...] = x1_ref[...] + gt2 * _rms(moe, gpost_ref[...])


def _moe_combine(step0, dst, n_chunks, lp, x1, mod, mod_row, y_sorted, g_post2):
    n = x1.shape[0]
    tok = lambda i, *_: (i, 0)
    return pl.pallas_call(
        functools.partial(_moe_combine_kernel, step0),
        grid_spec=pltpu.PrefetchScalarGridSpec(
            num_scalar_prefetch=2,
            grid=(n // MOE_TD,),
            in_specs=[pl.BlockSpec((MOE_TD, LANE), lambda i, *_: (step0 + i, 0)),
                      pl.BlockSpec((MOE_TD, D_MODEL), tok),
                      pl.BlockSpec((1, 6, D_MODEL), lambda i, *_: (mod_row(i), 0, 0)),
                      pl.BlockSpec((1, D_MODEL), lambda i, *_: (0, 0)),
                      pl.BlockSpec(memory_space=pl.ANY)],
            out_specs=pl.BlockSpec((MOE_TD, D_MODEL), tok),
            scratch_shapes=[pltpu.VMEM((2, MOE_L, D_MODEL), BF16), pltpu.SemaphoreType.DMA((2,))]),
        out_shape=jax.ShapeDtypeStruct((n, D_MODEL), F32),
        compiler_params=_cparams("arbitrary"),
        name="moe_combine",
    )(dst, n_chunks, lp, x1, mod, g_post2, y_sorted)


def _moe(part_a, part_b, mod, mod_row_a, mod_row_b, w_gate, w_up, w_down, g_post2):
    cnt = jnp.concatenate([part_a[4], part_b[4]], axis=0)[:, 0, :N_EXPERTS].astype(jnp.int32)
    steps_a = part_a[4].shape[0]
    n_tiles_max = (cnt.shape[0] * MOE_NCH + N_EXPERTS * (TILE_CH - 1)) // TILE_CH + 1
    dst, n_chunks, tail_start, tail_len, tile0, tile1, n_tiles = _plan(cnt, n_tiles_max)
    xs, lp = _dispatch(dst, n_chunks, tail_start, tail_len, n_tiles, part_a[1:4], part_b[1:4], n_tiles_max)
    y_sorted = _moe_gemm(tile0, tile1, n_tiles, xs, w_gate, w_up, w_down)
    ya = _moe_combine(0, dst, n_chunks, lp, part_a[0], mod, mod_row_a, y_sorted, g_post2)
    yb = _moe_combine(steps_a, dst, n_chunks, lp, part_b[0], mod, mod_row_b, y_sorted, g_post2)
    return ya, yb


def _rotate_half_cols(w):
    n = w.shape[-1]
    w4 = w.reshape(w.shape[:-1] + (n // 32, 2, 16))
    return jnp.stack([-w4[..., 1, :], w4[..., 0, :]], axis=-2).reshape(w.shape)


def _pad_lanes(w):
    return jnp.concatenate([w, jnp.zeros(w.shape[:-1] + (LANE - w.shape[-1],), w.dtype)], axis=-1)


def _rope_tables(t):
    rows = t // GRID_W
    n_freq = D_ROPE // 4
    freqs = ROPE_BASE ** (-jnp.arange(n_freq, dtype=F32) / n_freq)
    ang_r = jnp.arange(rows, dtype=F32)[:, None] * freqs
    ang_c = jnp.arange(GRID_W, dtype=F32)[:, None] * freqs
    per_row = lambda a: jnp.repeat(a, GRID_W, axis=0)
    per_col = lambda a: jnp.tile(a, (rows, 1))

    def table(fn):
        r, c = per_row(fn(ang_r)), per_col(fn(ang_c))
        return _pad_lanes(jnp.concatenate([r, r, c, c], axis=-1))
    return table(jnp.cos), table(jnp.sin)


def kernel(x_prompt, x_sample, cache_ckv, cache_krope, c, c_ctx, w_ada, b_ada, g_pre1, g_post1, g_pre2, g_post2, w_in, w_dw, b_dw, conv_ln_g, conv_ln_b, q_norm_g, kv_norm_g, w_uq, w_ukv, w_out, w_rg, b_rg, w_re, b_re, w_gate, w_up, w_down):
    nb, seq, d = x_prompt.shape
    db, dseq, _ = x_sample.shape
    l = 0

    cvec = jnp.concatenate([c_ctx[None, :], c, jnp.zeros((8 - 1 - db, d), F32)], axis=0)
    mod = _ada(cvec, w_ada[l], b_ada[l]).reshape(8, 6, d)

    w_in_ctx = w_in_lat = w_in[l]
    wuq = w_uq[l].reshape(Q_LORA, N_HEADS, D_NOPE + D_ROPE)
    wq_rope = wuq[:, :, D_NOPE:]
    wq = jnp.concatenate([wuq[:, :, :D_NOPE], _pad_lanes(wq_rope)], axis=-1)
    wq = wq.reshape(Q_LORA, N_HEADS * HEAD_W).astype(BF16)
    wqs = _pad_lanes(_rotate_half_cols(wq_rope)).reshape(Q_LORA, N_HEADS * LANE).astype(BF16)
    wukv4 = w_ukv[l].reshape(KV_LORA, N_HEADS, D_NOPE + D_V)
    wukv = jnp.concatenate([wukv4[:, :, :D_NOPE].reshape(KV_LORA, N_HEADS * D_NOPE),
                            wukv4[:, :, D_NOPE:].reshape(KV_LORA, N_HEADS * D_V)], axis=-1).astype(BF16)
    w_out_b = w_out[l].astype(BF16)
    w_r = jnp.concatenate([w_re[l], w_rg[l], jnp.zeros((d, LANE - N_EXPERTS - N_GROUPS), F32)], axis=-1)
    wr_hi = w_r.astype(BF16)
    wr_lo = jnp.concatenate([wr_hi[:, :LANE // 2], (w_r - wr_hi.astype(F32)).astype(BF16)[:, :LANE // 2]], axis=-1)
    b_r = jnp.concatenate([b_re[l], b_rg[l], jnp.zeros((LANE - N_EXPERTS - N_GROUPS,), F32)])[None, :]
    cos, sin = _rope_tables(dseq)
    row = lambda v: v[l][None, :]

    tm_c = 512
    xp_flat = x_prompt.reshape(1, nb * seq, d)
    conv_w = (w_dw[l], row(b_dw), row(conv_ln_g), row(conv_ln_b))
    assert seq == CONV_TB, "a context step must hold whole sequences of CONV_TB tokens"
    conv_out, q, k, v, ckv, kr = _mix_in(xp_flat, mod, 0, row(g_pre1), w_in_ctx, row(q_norm_g), row(kv_norm_g),
                                         wq, wqs, wukv, cos[:tm_c], sin[:tm_c], conv_w, False, tm_c)
    per_seq = lambda a: a.reshape(nb, seq, a.shape[-1])
    att = _attn(per_seq(q), per_seq(k), per_seq(v), n_seq=4)
    flat = lambda a: a.reshape(1, nb * seq, a.shape[-1])
    post_c = _post(xp_flat, conv_out, flat(att), mod, 0, w_out_b, row(g_post1), row(g_pre2),
                   wr_hi, wr_lo, b_r)
    state_ckv = ckv.reshape(nb, 1, seq, KV_LORA)
    state_krope = kr.reshape(nb, 1, seq, D_ROPE)

    tm_s = 512
    kc, vc = _cache_kv(cache_ckv[:, l], cache_krope[:, l], wukv)
    conv_out, q, k, v, _, _ = _mix_in(x_sample, mod, 1, row(g_pre1), w_in_lat, row(q_norm_g), row(kv_norm_g),
                                      wq, wqs, wukv, cos, sin, conv_w, True, tm_s)
    att = _attn(q, k, v, kc, vc)
    post_s = _post(x_sample, conv_out, att, mod, 1, w_out_b, row(g_post1), row(g_pre2), wr_hi, wr_lo, b_r)

    tokens = lambda parts: tuple(a.reshape((-1,) + a.shape[2:]) for a in parts)
    steps_per_req = dseq // MOE_TD
    yp, ys = _moe(tokens(post_c), tokens(post_s), mod, lambda i: 0, lambda i: 1 + i // steps_per_req,
                  w_gate[l], w_up[l], w_down[l], row(g_post2))

    return (yp.reshape(nb, seq, d), ys.reshape(db, dseq, d), state_ckv, state_krope)
```

```python
import functools

import jax
import jax.numpy as jnp
import numpy as np
from jax import lax
from jax.experimental import pallas as pl
from jax.experimental.pallas import tpu as pltpu

D_MODEL = 1024
GRID_W = 64
C_CONV = 512
CONV_K = 31
N_HEADS = 4
D_NOPE = 128
D_ROPE = 64
D_V = 128
Q_LORA = 384
KV_LORA = 256
N_GROUPS = 4
E_PER_GROUP = 8
N_EXPERTS = 32
D_EXPERT = 256
ROPE_BASE = 10000.0
EPS = 1e-6
ATT_SCALE = (D_NOPE + D_ROPE) ** -0.5
LOG2E = 1.4426950408889634

LANE = 128
SUBLANE = 8
HEAD_W = 2 * LANE
CONV_HALO = 16
MOE_TD = 512
CHUNK = 2 * SUBLANE
MOE_NCH = (2 * MOE_TD + N_EXPERTS * (CHUNK - 1)) // CHUNK
MOE_L = MOE_NCH * CHUNK
MOE_TILE = 256
TILE_CH = MOE_TILE // CHUNK
GEMM_RING = 6
GEMM_OUT = 4
XS_W = D_MODEL + LANE
VMEM_LIMIT = 56 * 1024 * 1024

BF16 = jnp.bfloat16
F32 = jnp.float32


def _cparams(*sem):
    return pltpu.CompilerParams(dimension_semantics=sem, vmem_limit_bytes=VMEM_LIMIT)


def _rms(x, g):
    return x * lax.rsqrt(jnp.mean(x * x, axis=-1, keepdims=True) + EPS) * g


def _sigmoid(x):
    return 1.0 / (1.0 + jnp.exp(-x))


def _dot(a, b):
    return jnp.dot(a, b, preferred_element_type=F32)


def _dot_t(a, b_t):
    return lax.dot_general(a, b_t, (((1,), (1,)), ((), ())), preferred_element_type=F32)


def _chunk_copy(src, src_chunk, dst, dst_chunk, sem):
    rows = lambda c: pl.ds(c * CHUNK if isinstance(c, int) else pl.multiple_of(c * CHUNK, CHUNK), CHUNK)
    return pltpu.make_async_copy(src.at[rows(src_chunk), :], dst.at[rows(dst_chunk), :], sem)


def _ada_kernel(c_ref, w_ref, b_ref, o_ref):
    c = c_ref[...]
    s = (c * _sigmoid(c)).astype(BF16)
    o_ref[...] = _dot(s, w_ref[...].astype(BF16)) + b_ref[...]


def _ada(cvec, w_ada, b_ada):
    n = w_ada.shape[1]
    tn = 1536
    return pl.pallas_call(
        _ada_kernel,
        grid=(n // tn,),
        in_specs=[pl.BlockSpec((8, D_MODEL), lambda j: (0, 0)),
                  pl.BlockSpec((D_MODEL, tn), lambda j: (0, j)),
                  pl.BlockSpec((1, tn), lambda j: (0, j))],
        out_specs=pl.BlockSpec((8, tn), lambda j: (0, j)),
        out_shape=jax.ShapeDtypeStruct((8, n), F32),
        compiler_params=_cparams("arbitrary"),
        name="ada",
    )(cvec, w_ada, b_ada.reshape(1, n))


CONV_TB = 256
CONV_WIN = CONV_TB + 2 * CONV_HALO
CONV_TT = 32


def _conv_window(win_ref, sh_ref, w_ref, b_ref, g_ref, bb_ref, o_ref, row0):
    first = CONV_HALO - CONV_K // 2
    rows = CONV_WIN - SUBLANE
    win = win_ref[...]
    for s in range(SUBLANE):
        sh_ref[s, 0:rows, :] = win[s:s + rows, :]
    for c in range(CONV_TB // CONV_TT):
        acc = jnp.zeros((CONV_TT, C_CONV), F32)
        for k in range(CONV_K):
            off = first + k
            r0 = c * CONV_TT + off // SUBLANE * SUBLANE
            acc = acc + sh_ref[off % SUBLANE, r0:r0 + CONV_TT, :] * w_ref[k:k + 1, :]
        y = acc + b_ref[...]
        mu = jnp.mean(y, axis=-1, keepdims=True)
        yc = y - mu
        var = jnp.mean(yc * yc, axis=-1, keepdims=True)
        z = yc * lax.rsqrt(var + EPS) * g_ref[...] + bb_ref[...]
        o_ref[0, row0 + c * CONV_TT:row0 + (c + 1) * CONV_TT, :] = (z * _sigmoid(z)).astype(BF16)


def _mix_in_kernel(latent, x_ref, xp_ref, xn_ref, mod_ref, gpre_ref, win_ref, qg_ref, kvg_ref, wq_ref, wqs_ref,
                   wukv_ref, cos_ref, sin_ref, wdw_ref, bdw_ref, lng_ref, lnb_ref,
                   conv_ref, q_ref, k_ref, v_ref, ckv_ref, kr_ref, cwin, csh, wbf):
    rope = latent

    @pl.when((pl.program_id(0) == 0) & (pl.program_id(1) == 0))
    def _():
        wbf[...] = win_ref[...].astype(BF16)

    x = x_ref[0]
    tm = x.shape[0]
    sh1 = mod_ref[0, 0:1, :]
    sc1 = mod_ref[0, 1:2, :]
    modulate = lambda v: _rms(v, gpre_ref[...]) * (1.0 + sc1) + sh1
    glu = lambda uc: uc[:, :C_CONV] * _sigmoid(uc[:, C_CONV:2 * C_CONV])
    u = _dot_t(modulate(x).astype(BF16), wbf[...])
    hglu = glu(u)

    zeros = jnp.zeros((CONV_HALO, C_CONV), F32)
    n_win = tm // CONV_TB
    if latent:
        i = pl.program_id(1)
        xh = jnp.concatenate([xp_ref[0], xn_ref[0]], axis=0)
        hh = glu(_dot_t(modulate(xh).astype(BF16), wbf[0:2 * C_CONV, :]))
        before = jnp.where(i > 0, hh[0:CONV_HALO], zeros)
        after = jnp.where(i < pl.num_programs(1) - 1, hh[CONV_HALO:2 * CONV_HALO], zeros)
    for j in range(n_win):
        if latent:
            lo, hi, d0 = j * CONV_TB - CONV_HALO, j * CONV_TB + CONV_TB + CONV_HALO, 0
            if lo < 0:
                cwin[j, 0:CONV_HALO, :] = before
                lo, d0 = 0, CONV_HALO
            if hi > tm:
                cwin[j, CONV_WIN - CONV_HALO:CONV_WIN, :] = after
                hi = tm
            cwin[j, d0:d0 + hi - lo, :] = hglu[lo:hi, :]
        else:
            cwin[j, 0:CONV_HALO, :] = zeros
            cwin[j, CONV_HALO:CONV_HALO + CONV_TB, :] = hglu[j * CONV_TB:(j + 1) * CONV_TB, :]
            cwin[j, CONV_HALO + CONV_TB:CONV_WIN, :] = zeros
    for j in range(n_win):
        _conv_window(cwin.at[j], csh, wdw_ref, bdw_ref, lng_ref, lnb_ref, conv_ref, j * CONV_TB)

    o_q = 2 * C_CONV
    o_kv = o_q + Q_LORA
    o_kr = o_kv + KV_LORA
    qn = _rms(u[:, o_q:o_kv], qg_ref[...]).astype(BF16)
    qf = _dot(qn, wq_ref[...])
    ckv = _rms(u[:, o_kv:o_kr], kvg_ref[...])
    ckv_ref[0] = ckv
    kvd = _dot(ckv.astype(BF16), wukv_ref[...])
    kr64 = u[:, o_kr:o_kr + D_ROPE]
    kr_ref[0] = kr64
    kr = jnp.concatenate([kr64, jnp.zeros((tm, LANE - D_ROPE), F32)], axis=-1)
    if rope:
        cos = cos_ref[...]
        sin = sin_ref[...]
        qs = _dot(qn, wqs_ref[...])
        lane = lax.broadcasted_iota(jnp.int32, kr.shape, 1)
        partner = jnp.where(lane % (D_ROPE // 2) < D_ROPE // 4,
                            -pltpu.roll(kr, LANE - D_ROPE // 4, axis=1), pltpu.roll(kr, D_ROPE // 4, axis=1))
        kr = kr * cos + partner * sin
    q_parts = []
    k_parts = []
    for hd in range(N_HEADS):
        q_parts.append(qf[:, hd * HEAD_W:hd * HEAD_W + LANE])
        qr = qf[:, hd * HEAD_W + LANE:(hd + 1) * HEAD_W]
        if rope:
            qr = qr * cos + qs[:, hd * LANE:(hd + 1) * LANE] * sin
        q_parts.append(qr)
        k_parts.append(kvd[:, hd * D_NOPE:(hd + 1) * D_NOPE])
        k_parts.append(kr)
    q_ref[0] = jnp.concatenate(q_parts, axis=-1).astype(BF16)
    k_ref[0] = jnp.concatenate(k_parts, axis=-1).astype(BF16)
    v_ref[0] = kvd[:, N_HEADS * D_NOPE:].astype(BF16)


def _mix_in(x, mod, mod_row0, g_pre1, w_in_ext, q_norm_g, kv_norm_g, wq, wqs, wukv, cos, sin, conv_w, latent, tm):
    b, t, _ = x.shape
    ncol = w_in_ext.shape[0]
    const = lambda bi, i: (0, 0)
    tok = lambda bi, i: (bi, i, 0)
    table = (lambda bi, i: (i, 0)) if latent else const
    halo_blocks = tm // CONV_HALO
    prev_rows = lambda bi, i: (bi, jnp.maximum(i * halo_blocks - 1, 0), 0)
    next_rows = lambda bi, i: (bi, jnp.minimum((i + 1) * halo_blocks, t // CONV_HALO - 1), 0)
    outs = [(C_CONV, BF16), (N_HEADS * HEAD_W, BF16), (N_HEADS * HEAD_W, BF16), (N_HEADS * D_V, BF16),
            (KV_LORA, F32), (D_ROPE, F32)]
    return pl.pallas_call(
        functools.partial(_mix_in_kernel, latent),
        grid=(b, t // tm),
        in_specs=[pl.BlockSpec((1, tm, D_MODEL), tok),
                  pl.BlockSpec((1, CONV_HALO, D_MODEL), prev_rows),
                  pl.BlockSpec((1, CONV_HALO, D_MODEL), next_rows),
                  pl.BlockSpec((1, 6, D_MODEL), lambda bi, i: (mod_row0 + bi, 0, 0)),
                  pl.BlockSpec((1, D_MODEL), const),
                  pl.BlockSpec((ncol, D_MODEL), const),
                  pl.BlockSpec((1, Q_LORA), const),
                  pl.BlockSpec((1, KV_LORA), const),
                  pl.BlockSpec((Q_LORA, N_HEADS * HEAD_W), const),
                  pl.BlockSpec((Q_LORA, N_HEADS * LANE), const),
                  pl.BlockSpec((KV_LORA, N_HEADS * (D_NOPE + D_V)), const),
                  pl.BlockSpec((tm, LANE), table),
                  pl.BlockSpec((tm, LANE), table),
                  pl.BlockSpec((CONV_K, C_CONV), const),
                  pl.BlockSpec((1, C_CONV), const),
                  pl.BlockSpec((1, C_CONV), const),
                  pl.BlockSpec((1, C_CONV), const)],
        out_specs=[pl.BlockSpec((1, tm, w), tok) for w, _ in outs],
        out_shape=[jax.ShapeDtypeStruct((b, t, w), dt) for w, dt in outs],
        scratch_shapes=[pltpu.VMEM((tm // CONV_TB, CONV_WIN, C_CONV), F32),
                        pltpu.VMEM((SUBLANE, CONV_WIN, C_CONV), F32),
                        pltpu.VMEM((ncol, D_MODEL), BF16)],
        compiler_params=_cparams("arbitrary", "arbitrary"),
        name="mix_in_latent" if latent else "mix_in",
    )(x, x, x, mod, g_pre1, w_in_ext, q_norm_g, kv_norm_g, wq, wqs, wukv, cos, sin, *conv_w)


def _qk(q, k):
    return lax.dot_general(q, k, (((1,), (1,)), ((), ())), preferred_element_type=F32)


ATT_KCHUNK = 256


def _attn_kernel(cached, q_ref, k_ref, v_ref, *rest):
    if cached:
        kc_ref, vc_ref, o_ref, s_scr = rest
        sources = [(kc_ref, vc_ref), (k_ref, v_ref)]
    else:
        o_ref, s_scr = rest
        sources = [(k_ref, v_ref)]
    n_seq, tq, _ = q_ref.shape
    chunks = []
    for kr, vr in sources:
        for c0 in range(0, kr.shape[1], ATT_KCHUNK):
            c1 = min(c0 + ATT_KCHUNK, kr.shape[1])
            col = chunks[-1][4] + chunks[-1][3] - chunks[-1][2] if chunks else 0
            chunks.append((kr, vr, c0, c1, col))
    for b in range(n_seq):
        outs = []
        for hd in range(N_HEADS):
            ks = slice(hd * HEAD_W, (hd + 1) * HEAD_W)
            vs = slice(hd * D_V, (hd + 1) * D_V)
            qh = q_ref[b, :, ks]
            m = jnp.full((tq, 1), -jnp.inf, F32)
            for kr, vr, c0, c1, col in chunks:
                s = _qk(qh, kr[b, c0:c1, ks]) * (ATT_SCALE * LOG2E)
                s_scr[hd, :, col:col + c1 - c0] = s
                m = jnp.maximum(m, jnp.max(s, axis=-1, keepdims=True))
            l = jnp.zeros((tq, 1), F32)
            o = jnp.zeros((tq, D_V), F32)
            for kr, vr, c0, c1, col in chunks:
                p = jnp.exp2(s_scr[hd, :, col:col + c1 - c0] - m)
                l = l + jnp.sum(p, axis=-1, keepdims=True)
                o = o + _dot(p.astype(BF16), vr[b, c0:c1, vs])
            outs.append(o / l)
        o_ref[b] = jnp.concatenate(outs, axis=-1).astype(BF16)


def _attn(q, k, v, kc=None, vc=None, tq=256, n_seq=1):
    b, t, _ = q.shape
    s = k.shape[1]
    cached = kc is not None
    whole = lambda bi, i: (bi, 0, 0)
    in_specs = [pl.BlockSpec((n_seq, tq, N_HEADS * HEAD_W), lambda bi, i: (bi, i, 0)),
                pl.BlockSpec((n_seq, s, N_HEADS * HEAD_W), whole),
                pl.BlockSpec((n_seq, s, N_HEADS * D_V), whole)]
    args = [q, k, v]
    if cached:
        sc = kc.shape[1]
        in_specs += [pl.BlockSpec((n_seq, sc, N_HEADS * HEAD_W), whole),
                     pl.BlockSpec((n_seq, sc, N_HEADS * D_V), whole)]
        args += [kc, vc]
        s += sc
    return pl.pallas_call(
        functools.partial(_attn_kernel, cached),
        grid=(b // n_seq, t // tq),
        in_specs=in_specs,
        out_specs=pl.BlockSpec((n_seq, tq, N_HEADS * D_V), lambda bi, i: (bi, i, 0)),
        out_shape=jax.ShapeDtypeStruct((b, t, N_HEADS * D_V), BF16),
        scratch_shapes=[pltpu.VMEM((N_HEADS, tq, s), F32)],
        compiler_params=_cparams("arbitrary", "arbitrary"),
        name="attn_cached" if cached else "attn",
    )(*args)


def _cache_kv_kernel(ckv_ref, kr_ref, wukv_ref, k_ref, v_ref):
    kvd = _dot(ckv_ref[0].astype(BF16), wukv_ref[...])
    kr = kr_ref[0]
    kr = jnp.concatenate([kr, jnp.zeros_like(kr)], axis=-1)
    parts = []
    for hd in range(N_HEADS):
        parts.append(kvd[:, hd * D_NOPE:(hd + 1) * D_NOPE])
        parts.append(kr)
    k_ref[0] = jnp.concatenate(parts, axis=-1).astype(BF16)
    v_ref[0] = kvd[:, N_HEADS * D_NOPE:].astype(BF16)


def _cache_kv(ckv, krope, wukv):
    b, s, _ = ckv.shape
    tok = lambda bi: (bi, 0, 0)
    return pl.pallas_call(
        _cache_kv_kernel,
        grid=(b,),
        in_specs=[pl.BlockSpec((1, s, KV_LORA), tok),
                  pl.BlockSpec((1, s, D_ROPE), tok),
                  pl.BlockSpec((KV_LORA, N_HEADS * (D_NOPE + D_V)), lambda bi: (0, 0))],
        out_specs=[pl.BlockSpec((1, s, N_HEADS * HEAD_W), tok),
                   pl.BlockSpec((1, s, N_HEADS * D_V), tok)],
        out_shape=[jax.ShapeDtypeStruct((b, s, N_HEADS * HEAD_W), BF16),
                   jax.ShapeDtypeStruct((b, s, N_HEADS * D_V), BF16)],
        compiler_params=_cparams("arbitrary"),
        name="cache_kv",
    )(ckv, krope, wukv)


def _post_kernel(x_ref, conv_ref, att_ref, mod_ref, wo_ref, gpost_ref, gpre2_ref, wr_hi_ref, wr_cat_ref,
                 br_ref, x1_ref, h2_ref, ri_ref, rw_ref, cnt_ref, wo_b):
    @pl.when((pl.program_id(0) == 0) & (pl.program_id(1) == 0))
    def _():
        wo_b[...] = wo_ref[...].astype(BF16)

    out = _dot(conv_ref[0], wo_b[:C_CONV, :]) + _dot(att_ref[0], wo_b[C_CONV:, :])
    gt1 = mod_ref[0, 2:3, :]
    sh2 = mod_ref[0, 3:4, :]
    sc2 = mod_ref[0, 4:5, :]
    x1 = x_ref[0] + gt1 * _rms(out, gpost_ref[...])
    x1_ref[0] = x1
    h2 = _rms(x1, gpre2_ref[...]) * (1.0 + sc2) + sh2
    h_hi = h2.astype(BF16)
    h2_ref[0] = h_hi
    h_lo = (h2 - h_hi.astype(F32)).astype(BF16)
    hi_terms = _dot(h_hi, wr_cat_ref[...])
    logits = hi_terms + pltpu.roll(hi_terms, LANE // 2, axis=1) + _dot(h_lo, wr_hi_ref[...]) + br_ref[...]

    lane = lax.broadcasted_iota(jnp.int32, logits.shape, 1)
    neg = jnp.float32(-jnp.inf)
    big = jnp.int32(LANE)
    is_g = (lane >= N_EXPERTS) & (lane < N_EXPERTS + N_GROUPS)
    lg = jnp.where(is_g, logits, neg)
    gmax = jnp.max(lg, axis=-1, keepdims=True)
    gidx = jnp.min(jnp.where(lg == gmax, lane, big), axis=-1, keepdims=True) - N_EXPERTS
    g_top = 1.0 / jnp.sum(jnp.exp(lg - gmax), axis=-1, keepdims=True)

    in_grp = (lane >= gidx * E_PER_GROUP) & (lane < (gidx + 1) * E_PER_GROUP)
    le = jnp.where(in_grp, logits, neg)
    m1 = jnp.max(le, axis=-1, keepdims=True)
    i1 = jnp.min(jnp.where(le == m1, lane, big), axis=-1, keepdims=True)
    le2 = jnp.where(lane == i1, neg, le)
    m2 = jnp.max(le2, axis=-1, keepdims=True)
    i2 = jnp.min(jnp.where(le2 == m2, lane, big), axis=-1, keepdims=True)
    r = jnp.exp(m2 - m1)
    w1 = g_top / (1.0 + r)
    w2 = g_top * r / (1.0 + r)
    ri_ref[0] = jnp.where(lane == 0, i1, jnp.where(lane == 1, i2, 0))
    rw_ref[0] = jnp.where(lane == 0, w1, jnp.where(lane == 1, w2, 0.0))
    pairs = jnp.sum(jnp.where((lane == i1) | (lane == i2), 1.0, 0.0), axis=0, keepdims=True)
    cnt_ref[0, 0] = jnp.broadcast_to(pairs, (SUBLANE, LANE))


def _post(x, conv_out, att, mod, mod_row0, w_out, g_post1, g_pre2, wr_hi, wr_lo, br):
    b, t, _ = x.shape
    tm = MOE_TD
    const = lambda bi, i: (0, 0)
    tok = lambda bi, i: (bi, i, 0)
    return pl.pallas_call(
        _post_kernel,
        grid=(b, t // tm),
        in_specs=[pl.BlockSpec((1, tm, D_MODEL), tok),
                  pl.BlockSpec((1, tm, C_CONV), tok),
                  pl.BlockSpec((1, tm, N_HEADS * D_V), tok),
                  pl.BlockSpec((1, 6, D_MODEL), lambda bi, i: (mod_row0 + bi, 0, 0)),
                  pl.BlockSpec((D_MODEL, D_MODEL), const),
                  pl.BlockSpec((1, D_MODEL), const),
                  pl.BlockSpec((1, D_MODEL), const),
                  pl.BlockSpec((D_MODEL, LANE), const),
                  pl.BlockSpec((D_MODEL, LANE), const),
                  pl.BlockSpec((1, LANE), const)],
        out_specs=[pl.BlockSpec((1, tm, D_MODEL), tok),
                   pl.BlockSpec((1, tm, D_MODEL), tok),
                   pl.BlockSpec((1, tm, LANE), tok),
                   pl.BlockSpec((1, tm, LANE), tok),
                   pl.BlockSpec((1, 1, SUBLANE, LANE), lambda bi, i: (bi, i, 0, 0))],
        out_shape=[jax.ShapeDtypeStruct((b, t, D_MODEL), F32),
                   jax.ShapeDtypeStruct((b, t, D_MODEL), BF16),
                   jax.ShapeDtypeStruct((b, t, LANE), jnp.int32),
                   jax.ShapeDtypeStruct((b, t, LANE), F32),
                   jax.ShapeDtypeStruct((b, t // tm, SUBLANE, LANE), F32)],
        scratch_shapes=[pltpu.VMEM((D_MODEL, D_MODEL), BF16)],
        compiler_params=_cparams("arbitrary", "arbitrary"),
        name="post",
    )(x, conv_out, att, mod, w_out, g_post1, g_pre2, wr_hi, wr_lo, br)


def _one_hots(ri):
    lane = lax.broadcasted_iota(jnp.int32, ri.shape, 1)
    oh1 = lane == ri[:, 0:1]
    oh2 = lane == ri[:, 1:2]
    return oh1, oh2, jnp.where(oh1 | oh2, 1.0, 0.0)


def _strictly_lower(n):
    r = lax.broadcasted_iota(jnp.int32, (n, n), 0)
    c = lax.broadcasted_iota(jnp.int32, (n, n), 1)
    return jnp.where(c < r, 1.0, 0.0).astype(BF16)


def _local_positions(oh1, oh2, oh, earlier_ref):
    cnt = jnp.sum(oh, axis=0, keepdims=True)
    nch = jnp.floor((cnt + (CHUNK - 1)) * (1.0 / CHUNK))
    a = lax.broadcasted_iota(jnp.int32, (LANE, LANE), 0)
    b = lax.broadcasted_iota(jnp.int32, (LANE, LANE), 1)
    lower_experts = jnp.where(a < b, 1.0, 0.0).astype(BF16)
    run_start = _dot(jnp.broadcast_to(nch, (SUBLANE, LANE)).astype(BF16), lower_experts)[0:1, :] * CHUNK
    pos = _dot(earlier_ref[...], oh.astype(BF16)) + run_start
    lp1 = jnp.sum(jnp.where(oh1, pos, 0.0), axis=-1, keepdims=True)
    lp2 = jnp.sum(jnp.where(oh2, pos, 0.0), axis=-1, keepdims=True)
    return lp1, lp2


def _plan(cnt, n_tiles_max):
    k = (cnt + (CHUNK - 1)) // CHUNK
    run_end = jnp.cumsum(k, axis=1)
    run_start = run_end - k
    n_chunks = run_end[:, -1]
    total = jnp.sum(k, axis=0)
    padded = (total + (TILE_CH - 1)) // TILE_CH * TILE_CH
    seg_end = jnp.cumsum(padded)
    seg_start = seg_end - padded
    base = seg_start[None, :] + jnp.cumsum(k, axis=0) - k
    c = jnp.arange(MOE_NCH, dtype=jnp.int32)[None, :, None]
    in_run = (run_start[:, None, :] <= c) & (c < run_end[:, None, :])
    dst = jnp.sum(jnp.where(in_run, base[:, None, :] + c - run_start[:, None, :], 0), axis=2)
    n_tiles = seg_end[-1] // TILE_CH
    tail_start = seg_start + total
    tail_len = padded - total
    i32 = lambda t: t.astype(jnp.int32)
    return (i32(dst), i32(n_chunks), i32(tail_start), i32(tail_len), i32(seg_start // TILE_CH),
            i32(seg_end // TILE_CH), i32(n_tiles.reshape(1)))


def _dispatch_kernel(steps_a, dst_ref, nch_ref, ts_ref, tl_ref, nt_ref, h2a_ref, ria_ref, rwa_ref, h2b_ref, rib_ref,
                     rwb_ref, xs_hbm, lp_ref, xbuf, zx, sem, zsem, earlier):
    i = pl.program_id(0)
    n = pl.num_programs(0)
    slot = i % 2
    td = MOE_TD
    n_tiles_max = xs_hbm.shape[0] // MOE_TILE

    def zero_tile(m):
        rows = pl.ds(pl.multiple_of(m * MOE_TILE, MOE_TILE), MOE_TILE)
        return pltpu.make_async_copy(zx, xs_hbm.at[rows, :], zsem.at[1])

    @pl.when(i == 0)
    def _():
        zx[...] = jnp.zeros_like(zx)
        earlier[...] = _strictly_lower(MOE_TD)
        for e in range(N_EXPERTS):
            def fill(m, carry, e=e):
                _chunk_copy(zx, 0, xs_hbm, ts_ref[e] + m, zsem.at[0]).start()
                return carry
            lax.fori_loop(0, tl_ref[e], fill, 0)

        def fill_tile(m, carry):
            zero_tile(m).start()
            return carry
        lax.fori_loop(nt_ref[0], n_tiles_max, fill_tile, 0)

    from_a = i < steps_a
    ri = jnp.where(from_a, ria_ref[...], rib_ref[...])
    rw = jnp.where(from_a, rwa_ref[...], rwb_ref[...])
    h2 = jnp.where(from_a, h2a_ref[...], h2b_ref[...])
    oh1, oh2, oh = _one_hots(ri)
    lp1, lp2 = _local_positions(oh1, oh2, oh, earlier)
    lane = lax.broadcasted_iota(jnp.int32, ri.shape, 1)
    lp_ref[...] = jnp.where(lane == 0, lp1, jnp.where(lane == 1, lp2, 0.0)).astype(jnp.int32)

    cols = jnp.where(lane == 0, lp1, jnp.where(lane == 1, lp2, jnp.where(lane == 2, rw[:, 0:1],
                                                                         jnp.where(lane == 3, rw[:, 1:2], 0.0))))
    rows4 = cols.T
    row = lax.broadcasted_iota(jnp.int32, (MOE_L, td), 0).astype(F32)
    p1 = row == rows4[0:1, :]
    p2 = row == rows4[1:2, :]
    perm = jnp.where(p1 | p2, 1.0, 0.0).astype(BF16)
    xbuf[slot, :, 0:D_MODEL] = _dot(perm, h2).astype(BF16)
    w = jnp.sum(jnp.where(p1, rows4[2:3, :], 0.0) + jnp.where(p2, rows4[3:4, :], 0.0), axis=-1, keepdims=True)
    w0 = w.astype(BF16).astype(F32)
    w1 = (w - w0).astype(BF16).astype(F32)
    w2 = (w - w0 - w1).astype(BF16).astype(F32)
    wl = lax.broadcasted_iota(jnp.int32, (MOE_L, LANE), 1)
    terms = jnp.where(wl == 0, w0, jnp.where(wl == 1, w1, jnp.where(wl == 2, w2, 0.0)))
    xbuf[slot, :, D_MODEL:XS_W] = terms.astype(BF16)

    def wait_chunks(count, s):
        def body(c, carry):
            _chunk_copy(xbuf.at[s], 0, xs_hbm, 0, sem.at[s]).wait()
            return carry
        lax.fori_loop(0, count, body, 0)

    @pl.when(i > 0)
    def _():
        wait_chunks(nch_ref[i - 1], 1 - slot)

    def send(c, carry):
        _chunk_copy(xbuf.at[slot], c, xs_hbm, dst_ref[i, c], sem.at[slot]).start()
        return carry
    lax.fori_loop(0, nch_ref[i], send, 0)

    @pl.when(i == n - 1)
    def _():
        wait_chunks(nch_ref[i], slot)
        for e in range(N_EXPERTS):
            def drain(m, carry):
                _chunk_copy(zx, 0, xs_hbm, 0, zsem.at[0]).wait()
                return carry
            lax.fori_loop(0, tl_ref[e], drain, 0)

        def drain_tile(m, carry):
            zero_tile(m).wait()
            return carry
        lax.fori_loop(nt_ref[0], n_tiles_max, drain_tile, 0)


def _dispatch(dst, n_chunks, tail_start, tail_len, n_tiles, part_a, part_b, n_tiles_max):
    steps_a = part_a[0].shape[0] // MOE_TD
    steps_b = part_b[0].shape[0] // MOE_TD
    in_a = lambda i, *_: (jnp.minimum(i, steps_a - 1), 0)
    in_b = lambda i, *_: (jnp.maximum(i - steps_a, 0), 0)
    specs = lambda f: [pl.BlockSpec((MOE_TD, D_MODEL), f), pl.BlockSpec((MOE_TD, LANE), f),
                       pl.BlockSpec((MOE_TD, LANE), f)]
    rows = n_tiles_max * MOE_TILE
    return pl.pallas_call(
        functools.partial(_dispatch_kernel, steps_a),
        grid_spec=pltpu.PrefetchScalarGridSpec(
            num_scalar_prefetch=5,
            grid=(steps_a + steps_b,),
            in_specs=specs(in_a) + specs(in_b),
            out_specs=[pl.BlockSpec(memory_space=pl.ANY),
                       pl.BlockSpec((MOE_TD, LANE), lambda i, *_: (i, 0))],
            scratch_shapes=[pltpu.VMEM((2, MOE_L, XS_W), BF16), pltpu.VMEM((MOE_TILE, XS_W), BF16),
                            pltpu.SemaphoreType.DMA((2,)), pltpu.SemaphoreType.DMA((2,)),
                            pltpu.VMEM((MOE_TD, MOE_TD), BF16)]),
        out_shape=[jax.ShapeDtypeStruct((rows, XS_W), BF16),
                   jax.ShapeDtypeStruct(((steps_a + steps_b) * MOE_TD, LANE), jnp.int32)],
        compiler_params=_cparams("arbitrary"),
        name="moe_dispatch",
    )(dst, n_chunks, tail_start, tail_len, n_tiles, *part_a, *part_b)


def _moe_gemm_kernel(t0_ref, t1_ref, nt_ref, wg_ref, wu_ref, wd_ref, xs_hbm, y_hbm,
                     wg_b, wu_b, wd_b, xbuf, ybuf, zbuf, isem, osem, zsem, issued):
    e = pl.program_id(0)
    nt = nt_ref[0]
    n_tiles_max = y_hbm.shape[0] // MOE_TILE
    tile_rows = lambda t: pl.ds(pl.multiple_of(t * MOE_TILE, MOE_TILE), MOE_TILE)

    def load(t, s):
        return (pltpu.make_async_copy(xs_hbm.at[tile_rows(t), :], xbuf.at[s], isem.at[s]),)

    def store(t, s):
        return pltpu.make_async_copy(ybuf.at[s], y_hbm.at[tile_rows(t), :], osem.at[s])

    def zero_tile(t):
        return pltpu.make_async_copy(zbuf, y_hbm.at[tile_rows(t), :], zsem.at[0])

    @pl.when(e == 0)
    def _():
        issued[0] = 0
        zbuf[...] = jnp.zeros_like(zbuf)

        def fill(t, carry):
            zero_tile(t).start()
            return carry
        lax.fori_loop(nt, n_tiles_max, fill, 0)

    wg_b[...] = wg_ref[0].astype(BF16)
    wu_b[...] = wu_ref[0].astype(BF16)
    wd_b[...] = wd_ref[0].astype(BF16)

    def top_up(t):
        upto = jnp.minimum(t + GEMM_RING, nt)

        def start(u, carry):
            for cp in load(u, u % GEMM_RING):
                cp.start()
            return carry
        lax.fori_loop(issued[0], upto, start, 0)
        issued[0] = jnp.maximum(issued[0], upto)

    def process(t, n):
        top_up(t)
        for k in range(n):
            for cp in load(t + k, (t + k) % GEMM_RING):
                cp.wait()
        rows = [xbuf[(t + k) % GEMM_RING] for k in range(n)]
        xw = rows[0] if n == 1 else jnp.concatenate(rows, axis=0)
        x = xw[:, 0:D_MODEL]
        wt = xw[:, D_MODEL:XS_W].astype(F32)
        w = wt[:, 0:1] + wt[:, 1:2] + wt[:, 2:3]
        a = _dot(x, wg_b[...])
        u = _dot(x, wu_b[...])
        he = (a * _sigmoid(a)) * u
        y = (_dot(he.astype(BF16), wd_b[...]) * w).astype(BF16)
        for k in range(n):
            s = (t + k) % GEMM_OUT

            @pl.when(t + k >= GEMM_OUT)
            def _(k=k, s=s):
                store(t + k - GEMM_OUT, s).wait()
            ybuf[s] = y[k * MOE_TILE:(k + 1) * MOE_TILE, :]
            store(t + k, s).start()

    t0 = t0_ref[e]
    t1 = t1_ref[e]

    def pair(p, carry):
        process(t0 + 2 * p, 2)
        return carry
    lax.fori_loop(0, (t1 - t0) // 2, pair, 0)

    @pl.when((t1 - t0) % 2 == 1)
    def _():
        process(t1 - 1, 1)

    @pl.when(e == pl.num_programs(0) - 1)
    def _():
        for k in range(1, GEMM_OUT + 1):
            @pl.when(nt >= k)
            def _(k=k):
                store(nt - k, (nt - k) % GEMM_OUT).wait()

        def drain(t, carry):
            zero_tile(t).wait()
            return carry
        lax.fori_loop(nt, n_tiles_max, drain, 0)


def _moe_gemm(tile0, tile1, n_tiles, xs, w_gate, w_up, w_down):
    wmap = lambda e, *_: (e, 0, 0)
    return pl.pallas_call(
        _moe_gemm_kernel,
        grid_spec=pltpu.PrefetchScalarGridSpec(
            num_scalar_prefetch=3,
            grid=(N_EXPERTS,),
            in_specs=[pl.BlockSpec((1, D_MODEL, D_EXPERT), wmap),
                      pl.BlockSpec((1, D_MODEL, D_EXPERT), wmap),
                      pl.BlockSpec((1, D_EXPERT, D_MODEL), wmap),
                      pl.BlockSpec(memory_space=pl.ANY)],
            out_specs=pl.BlockSpec(memory_space=pl.ANY),
            scratch_shapes=[pltpu.VMEM((D_MODEL, D_EXPERT), BF16), pltpu.VMEM((D_MODEL, D_EXPERT), BF16),
                            pltpu.VMEM((D_EXPERT, D_MODEL), BF16),
                            pltpu.VMEM((GEMM_RING, MOE_TILE, XS_W), BF16),
                            pltpu.VMEM((GEMM_OUT, MOE_TILE, D_MODEL), BF16), pltpu.VMEM((MOE_TILE, D_MODEL), BF16),
                            pltpu.SemaphoreType.DMA((GEMM_RING,)), pltpu.SemaphoreType.DMA((GEMM_OUT,)),
                            pltpu.SemaphoreType.DMA((1,)), pltpu.SMEM((1,), jnp.int32)]),
        out_shape=jax.ShapeDtypeStruct((xs.shape[0], D_MODEL), BF16),
        compiler_params=_cparams("arbitrary"),
        name="moe_gemm",
    )(tile0, tile1, n_tiles, w_gate, w_up, w_down, xs)


def _moe_combine_kernel(step0, dst_ref, nch_ref, lp_ref, x1_ref, mod_ref, gpost_ref, y_hbm, o_ref, ybuf, sem):
    i = pl.program_id(0)
    n = pl.num_programs(0)
    slot = i % 2

    def fetch(step, s):
        def body(c, carry):
            _chunk_copy(y_hbm, dst_ref[step, c], ybuf.at[s], c, sem.at[s]).start()
            return carry
        lax.fori_loop(0, nch_ref[step], body, 0)

    @pl.when(i == 0)
    def _():
        ybuf[...] = jnp.zeros_like(ybuf)
        fetch(step0, 0)

    @pl.when(i + 1 < n)
    def _():
        fetch(step0 + i + 1, 1 - slot)

    def wait(c, carry):
        _chunk_copy(y_hbm, 0, ybuf.at[slot], 0, sem.at[slot]).wait()
        return carry
    lax.fori_loop(0, nch_ref[step0 + i], wait, 0)

    lp = lp_ref[...]
    col = lax.broadcasted_iota(jnp.int32, (MOE_TD, MOE_L), 1)
    unperm = jnp.where((col == lp[:, 0:1]) | (col == lp[:, 1:2]), 1.0, 0.0).astype(BF16)
    moe = _dot(unperm, ybuf[slot])
    gt2 = mod_ref[0, 5:6, :]
    o_ref[...] = x1_ref[...] + gt2 * _rms(moe, gpost_ref[...])


def _moe_combine(step0, dst, n_chunks, lp, x1, mod, mod_row, y_sorted, g_post2):
    n = x1.shape[0]
    tok = lambda i, *_: (i, 0)
    return pl.pallas_call(
        functools.partial(_moe_combine_kernel, step0),
        grid_spec=pltpu.PrefetchScalarGridSpec(
            num_scalar_prefetch=2,
            grid=(n // MOE_TD,),
            in_specs=[pl.BlockSpec((MOE_TD, LANE), lambda i, *_: (step0 + i, 0)),
                      pl.BlockSpec((MOE_TD, D_MODEL), tok),
                      pl.BlockSpec((1, 6, D_MODEL), lambda i, *_: (mod_row(i), 0, 0)),
                      pl.BlockSpec((1, D_MODEL), lambda i, *_: (0, 0)),
                      pl.BlockSpec(memory_space=pl.ANY)],
            out_specs=pl.BlockSpec((MOE_TD, D_MODEL), tok),
            scratch_shapes=[pltpu.VMEM((2, MOE_L, D_MODEL), BF16), pltpu.SemaphoreType.DMA((2,))]),
        out_shape=jax.ShapeDtypeStruct((n, D_MODEL), F32),
        compiler_params=_cparams("arbitrary"),
        name="moe_combine",
    )(dst, n_chunks, lp, x1, mod, g_post2, y_sorted)


def _moe(part_a, part_b, mod, mod_row_a, mod_row_b, w_gate, w_up, w_down, g_post2):
    cnt = jnp.concatenate([part_a[4], part_b[4]], axis=0)[:, 0, :N_EXPERTS].astype(jnp.int32)
    steps_a = part_a[4].shape[0]
    n_tiles_max = (cnt.shape[0] * MOE_NCH + N_EXPERTS * (TILE_CH - 1)) // TILE_CH + 1
    dst, n_chunks, tail_start, tail_len, tile0, tile1, n_tiles = _plan(cnt, n_tiles_max)
    xs, lp = _dispatch(dst, n_chunks, tail_start, tail_len, n_tiles, part_a[1:4], part_b[1:4], n_tiles_max)
    y_sorted = _moe_gemm(tile0, tile1, n_tiles, xs, w_gate, w_up, w_down)
    ya = _moe_combine(0, dst, n_chunks, lp, part_a[0], mod, mod_row_a, y_sorted, g_post2)
    yb = _moe_combine(steps_a, dst, n_chunks, lp, part_b[0], mod, mod_row_b, y_sorted, g_post2)
    return ya, yb


def _rotate_half_cols(w):
    n = w.shape[-1]
    w4 = w.reshape(w.shape[:-1] + (n // 32, 2, 16))
    return jnp.stack([-w4[..., 1, :], w4[..., 0, :]], axis=-2).reshape(w.shape)


def _pad_lanes(w):
    return jnp.concatenate([w, jnp.zeros(w.shape[:-1] + (LANE - w.shape[-1],), w.dtype)], axis=-1)


def _rope_tables(t):
    rows = t // GRID_W
    n_freq = D_ROPE // 4
    freqs = ROPE_BASE ** (-jnp.arange(n_freq, dtype=F32) / n_freq)
    ang_r = jnp.arange(rows, dtype=F32)[:, None] * freqs
    ang_c = jnp.arange(GRID_W, dtype=F32)[:, None] * freqs
    per_row = lambda a: jnp.repeat(a, GRID_W, axis=0)
    per_col = lambda a: jnp.tile(a, (rows, 1))

    def table(fn):
        r, c = per_row(fn(ang_r)), per_col(fn(ang_c))
        return _pad_lanes(jnp.concatenate([r, r, c, c], axis=-1))
    return table(jnp.cos), table(jnp.sin)


def kernel(x_prompt, x_sample, cache_ckv, cache_krope, c, c_ctx, w_ada, b_ada, g_pre1, g_post1, g_pre2, g_post2, w_in, w_dw, b_dw, conv_ln_g, conv_ln_b, q_norm_g, kv_norm_g, w_uq, w_ukv, w_out, w_rg, b_rg, w_re, b_re, w_gate, w_up, w_down):
    nb, seq, d = x_prompt.shape
    db, dseq, _ = x_sample.shape
    l = 0

    cvec = jnp.concatenate([c_ctx[None, :], c, jnp.zeros((8 - 1 - db, d), F32)], axis=0)
    mod = _ada(cvec, w_ada[l], b_ada[l]).reshape(8, 6, d)

    w_in_ctx = w_in_lat = w_in[l].T
    wuq = w_uq[l].reshape(Q_LORA, N_HEADS, D_NOPE + D_ROPE)
    wq_rope = wuq[:, :, D_NOPE:]
    wq = jnp.concatenate([wuq[:, :, :D_NOPE], _pad_lanes(wq_rope)], axis=-1)
    wq = wq.reshape(Q_LORA, N_HEADS * HEAD_W).astype(BF16)
    wqs = _pad_lanes(_rotate_half_cols(wq_rope)).reshape(Q_LORA, N_HEADS * LANE).astype(BF16)
    wukv4 = w_ukv[l].reshape(KV_LORA, N_HEADS, D_NOPE + D_V)
    wukv = jnp.concatenate([wukv4[:, :, :D_NOPE].reshape(KV_LORA, N_HEADS * D_NOPE),
                            wukv4[:, :, D_NOPE:].reshape(KV_LORA, N_HEADS * D_V)], axis=-1).astype(BF16)
    w_out_b = w_out[l]
    w_r = jnp.concatenate([w_re[l], w_rg[l], jnp.zeros((d, LANE - N_EXPERTS - N_GROUPS), F32)], axis=-1)
    wr_hi = w_r.astype(BF16)
    wr_lo = jnp.concatenate([wr_hi[:, :LANE // 2], (w_r - wr_hi.astype(F32)).astype(BF16)[:, :LANE // 2]], axis=-1)
    b_r = jnp.concatenate([b_re[l], b_rg[l], jnp.zeros((LANE - N_EXPERTS - N_GROUPS,), F32)])[None, :]
    cos, sin = _rope_tables(dseq)
    row = lambda v: v[l][None, :]

    tm_c = 512
    xp_flat = x_prompt.reshape(1, nb * seq, d)
    conv_w = (w_dw[l], row(b_dw), row(conv_ln_g), row(conv_ln_b))
    assert seq == CONV_TB, "a context step must hold whole sequences of CONV_TB tokens"
    conv_out, q, k, v, ckv, kr = _mix_in(xp_flat, mod, 0, row(g_pre1), w_in_ctx, row(q_norm_g), row(kv_norm_g),
                                         wq, wqs, wukv, cos[:tm_c], sin[:tm_c], conv_w, False, tm_c)
    per_seq = lambda a: a.reshape(nb, seq, a.shape[-1])
    att = _attn(per_seq(q), per_seq(k), per_seq(v), n_seq=4)
    flat = lambda a: a.reshape(1, nb * seq, a.shape[-1])
    post_c = _post(xp_flat, conv_out, flat(att), mod, 0, w_out_b, row(g_post1), row(g_pre2),
                   wr_hi, wr_lo, b_r)
    state_ckv = ckv.reshape(nb, 1, seq, KV_LORA)
    state_krope = kr.reshape(nb, 1, seq, D_ROPE)

    tm_s = 512
    kc, vc = _cache_kv(cache_ckv[:, l], cache_krope[:, l], wukv)
    conv_out, q, k, v, _, _ = _mix_in(x_sample, mod, 1, row(g_pre1), w_in_lat, row(q_norm_g), row(kv_norm_g),
                                      wq, wqs, wukv, cos, sin, conv_w, True, tm_s)
    att = _attn(q, k, v, kc, vc)
    post_s = _post(x_sample, conv_out, att, mod, 1, w_out_b, row(g_post1), row(g_pre2), wr_hi, wr_lo, b_r)

    tokens = lambda parts: tuple(a.reshape((-1,) + a.shape[2:]) for a in parts)
    steps_per_req = dseq // MOE_TD
    yp, ys = _moe(tokens(post_c), tokens(post_s), mod, lambda i: 0, lambda i: 1 + i // steps_per_req,
                  w_gate[l], w_up[l], w_down[l], row(g_post2))

    return (yp.reshape(nb, seq, d), ys.reshape(db, dseq, d), state_ckv, state_krope)
```

```python
import functools

import jax
import jax.numpy as jnp
import numpy as np
from jax import lax
from jax.experimental import pallas as pl
from jax.experimental.pallas import tpu as pltpu

D_MODEL = 1024
GRID_W = 64
C_CONV = 512
CONV_K = 31
N_HEADS = 4
D_NOPE = 128
D_ROPE = 64
D_V = 128
Q_LORA = 384
KV_LORA = 256
N_GROUPS = 4
E_PER_GROUP = 8
N_EXPERTS = 32
D_EXPERT = 256
ROPE_BASE = 10000.0
EPS = 1e-6
ATT_SCALE = (D_NOPE + D_ROPE) ** -0.5
LOG2E = 1.4426950408889634

LANE = 128
SUBLANE = 8
HEAD_W = 2 * LANE
CONV_HALO = 16
MOE_TD = 512
CHUNK = 2 * SUBLANE
MOE_NCH = (2 * MOE_TD + N_EXPERTS * (CHUNK - 1)) // CHUNK
MOE_L = MOE_NCH * CHUNK
MOE_TILE = 256
TILE_CH = MOE_TILE // CHUNK
GEMM_RING = 6
GEMM_OUT = 4
XS_W = D_MODEL + LANE
VMEM_LIMIT = 56 * 1024 * 1024

BF16 = jnp.bfloat16
F32 = jnp.float32


def _cparams(*sem):
    return pltpu.CompilerParams(dimension_semantics=sem, vmem_limit_bytes=VMEM_LIMIT)


def _rms(x, g):
    return x * lax.rsqrt(jnp.mean(x * x, axis=-1, keepdims=True) + EPS) * g


def _sigmoid(x):
    return 1.0 / (1.0 + jnp.exp(-x))


def _dot(a, b):
    return jnp.dot(a, b, preferred_element_type=F32)


def _dot_t(a, b_t):
    return lax.dot_general(a, b_t, (((1,), (1,)), ((), ())), preferred_element_type=F32)


def _chunk_copy(src, src_chunk, dst, dst_chunk, sem):
    rows = lambda c: pl.ds(c * CHUNK if isinstance(c, int) else pl.multiple_of(c * CHUNK, CHUNK), CHUNK)
    return pltpu.make_async_copy(src.at[rows(src_chunk), :], dst.at[rows(dst_chunk), :], sem)


def _ada_kernel(c_ref, w_ref, b_ref, o_ref):
    c = c_ref[...]
    s = (c * _sigmoid(c)).astype(BF16)
    o_ref[...] = _dot(s, w_ref[...].astype(BF16)) + b_ref[...]


def _ada(cvec, w_ada, b_ada):
    n = w_ada.shape[1]
    tn = 1536
    return pl.pallas_call(
        _ada_kernel,
        grid=(n // tn,),
        in_specs=[pl.BlockSpec((8, D_MODEL), lambda j: (0, 0)),
                  pl.BlockSpec((D_MODEL, tn), lambda j: (0, j)),
                  pl.BlockSpec((1, tn), lambda j: (0, j))],
        out_specs=pl.BlockSpec((8, tn), lambda j: (0, j)),
        out_shape=jax.ShapeDtypeStruct((8, n), F32),
        compiler_params=_cparams("arbitrary"),
        name="ada",
    )(cvec, w_ada, b_ada.reshape(1, n))


CONV_TB = 256
CONV_WIN = CONV_TB + 2 * CONV_HALO
CONV_TT = 32


def _conv_window(win_ref, sh_ref, w_ref, b_ref, g_ref, bb_ref, o_ref, row0):
    first = CONV_HALO - CONV_K // 2
    rows = CONV_WIN - SUBLANE
    win = win_ref[...]
    for s in range(SUBLANE):
        sh_ref[s, 0:rows, :] = win[s:s + rows, :]
    for c in range(CONV_TB // CONV_TT):
        acc = jnp.zeros((CONV_TT, C_CONV), F32)
        for k in range(CONV_K):
            off = first + k
            r0 = c * CONV_TT + off // SUBLANE * SUBLANE
            acc = acc + sh_ref[off % SUBLANE, r0:r0 + CONV_TT, :] * w_ref[k:k + 1, :]
        y = acc + b_ref[...]
        mu = jnp.mean(y, axis=-1, keepdims=True)
        yc = y - mu
        var = jnp.mean(yc * yc, axis=-1, keepdims=True)
        z = yc * lax.rsqrt(var + EPS) * g_ref[...] + bb_ref[...]
        o_ref[0, row0 + c * CONV_TT:row0 + (c + 1) * CONV_TT, :] = (z * _sigmoid(z)).astype(BF16)


def _mix_in_kernel(latent, x_ref, xp_ref, xn_ref, mod_ref, gpre_ref, win_ref, qg_ref, kvg_ref, wq_ref, wqs_ref,
                   wukv_ref, cos_ref, sin_ref, wdw_ref, bdw_ref, lng_ref, lnb_ref,
                   conv_ref, q_ref, k_ref, v_ref, ckv_ref, kr_ref, cwin, csh, wbf):
    rope = latent

    @pl.when((pl.program_id(0) == 0) & (pl.program_id(1) == 0))
    def _():
        wbf[...] = win_ref[...].astype(BF16)

    x = x_ref[0]
    tm = x.shape[0]
    sh1 = mod_ref[0, 0:1, :]
    sc1 = mod_ref[0, 1:2, :]
    modulate = lambda v: _rms(v, gpre_ref[...]) * (1.0 + sc1) + sh1
    glu = lambda uc: uc[:, :C_CONV] * _sigmoid(uc[:, C_CONV:2 * C_CONV])
    u = _dot_t(modulate(x).astype(BF16), wbf[...])
    hglu = glu(u)

    zeros = jnp.zeros((CONV_HALO, C_CONV), F32)
    n_win = tm // CONV_TB
    if latent:
        i = pl.program_id(1)
        xh = jnp.concatenate([xp_ref[0], xn_ref[0]], axis=0)
        hh = glu(_dot_t(modulate(xh).astype(BF16), wbf[0:2 * C_CONV, :]))
        before = jnp.where(i > 0, hh[0:CONV_HALO], zeros)
        after = jnp.where(i < pl.num_programs(1) - 1, hh[CONV_HALO:2 * CONV_HALO], zeros)
    for j in range(n_win):
        if latent:
            lo, hi, d0 = j * CONV_TB - CONV_HALO, j * CONV_TB + CONV_TB + CONV_HALO, 0
            if lo < 0:
                cwin[j, 0:CONV_HALO, :] = before
                lo, d0 = 0, CONV_HALO
            if hi > tm:
                cwin[j, CONV_WIN - CONV_HALO:CONV_WIN, :] = after
                hi = tm
            cwin[j, d0:d0 + hi - lo, :] = hglu[lo:hi, :]
        else:
            cwin[j, 0:CONV_HALO, :] = zeros
            cwin[j, CONV_HALO:CONV_HALO + CONV_TB, :] = hglu[j * CONV_TB:(j + 1) * CONV_TB, :]
            cwin[j, CONV_HALO + CONV_TB:CONV_WIN, :] = zeros
    for j in range(n_win):
        _conv_window(cwin.at[j], csh, wdw_ref, bdw_ref, lng_ref, lnb_ref, conv_ref, j * CONV_TB)

    o_q = 2 * C_CONV
    o_kv = o_q + Q_LORA
    o_kr = o_kv + KV_LORA
    qn = _rms(u[:, o_q:o_kv], qg_ref[...]).astype(BF16)
    qf = _dot(qn, wq_ref[...])
    ckv = _rms(u[:, o_kv:o_kr], kvg_ref[...])
    ckv_ref[0] = ckv
    kvd = _dot(ckv.astype(BF16), wukv_ref[...])
    kr64 = u[:, o_kr:o_kr + D_ROPE]
    kr_ref[0] = kr64
    kr = jnp.concatenate([kr64, jnp.zeros((tm, LANE - D_ROPE), F32)], axis=-1)
    if rope:
        cos = cos_ref[...]
        sin = sin_ref[...]
        qs = _dot(qn, wqs_ref[...])
        lane = lax.broadcasted_iota(jnp.int32, kr.shape, 1)
        partner = jnp.where(lane % (D_ROPE // 2) < D_ROPE // 4,
                            -pltpu.roll(kr, LANE - D_ROPE // 4, axis=1), pltpu.roll(kr, D_ROPE // 4, axis=1))
        kr = kr * cos + partner * sin
    q_parts = []
    k_parts = []
    for hd in range(N_HEADS):
        q_parts.append(qf[:, hd * HEAD_W:hd * HEAD_W + LANE])
        qr = qf[:, hd * HEAD_W + LANE:(hd + 1) * HEAD_W]
        if rope:
            qr = qr * cos + qs[:, hd * LANE:(hd + 1) * LANE] * sin
        q_parts.append(qr)
        k_parts.append(kvd[:, hd * D_NOPE:(hd + 1) * D_NOPE])
        k_parts.append(kr)
    q_ref[0] = jnp.concatenate(q_parts, axis=-1).astype(BF16)
    k_ref[0] = jnp.concatenate(k_parts, axis=-1).astype(BF16)
    v_ref[0] = kvd[:, N_HEADS * D_NOPE:].astype(BF16)


def _mix_in(x, mod, mod_row0, g_pre1, w_in_ext, q_norm_g, kv_norm_g, wq, wqs, wukv, cos, sin, conv_w, latent, tm):
    b, t, _ = x.shape
    ncol = w_in_ext.shape[0]
    const = lambda bi, i: (0, 0)
    tok = lambda bi, i: (bi, i, 0)
    table = (lambda bi, i: (i, 0)) if latent else const
    halo_blocks = tm // CONV_HALO
    prev_rows = lambda bi, i: (bi, jnp.maximum(i * halo_blocks - 1, 0), 0)
    next_rows = lambda bi, i: (bi, jnp.minimum((i + 1) * halo_blocks, t // CONV_HALO - 1), 0)
    outs = [(C_CONV, BF16), (N_HEADS * HEAD_W, BF16), (N_HEADS * HEAD_W, BF16), (N_HEADS * D_V, BF16),
            (KV_LORA, F32), (D_ROPE, F32)]
    return pl.pallas_call(
        functools.partial(_mix_in_kernel, latent),
        grid=(b, t // tm),
        in_specs=[pl.BlockSpec((1, tm, D_MODEL), tok),
                  pl.BlockSpec((1, CONV_HALO, D_MODEL), prev_rows),
                  pl.BlockSpec((1, CONV_HALO, D_MODEL), next_rows),
                  pl.BlockSpec((1, 6, D_MODEL), lambda bi, i: (mod_row0 + bi, 0, 0)),
                  pl.BlockSpec((1, D_MODEL), const),
                  pl.BlockSpec((ncol, D_MODEL), const),
                  pl.BlockSpec((1, Q_LORA), const),
                  pl.BlockSpec((1, KV_LORA), const),
                  pl.BlockSpec((Q_LORA, N_HEADS * HEAD_W), const),
                  pl.BlockSpec((Q_LORA, N_HEADS * LANE), const),
                  pl.BlockSpec((KV_LORA, N_HEADS * (D_NOPE + D_V)), const),
                  pl.BlockSpec((tm, LANE), table),
                  pl.BlockSpec((tm, LANE), table),
                  pl.BlockSpec((CONV_K, C_CONV), const),
                  pl.BlockSpec((1, C_CONV), const),
                  pl.BlockSpec((1, C_CONV), const),
                  pl.BlockSpec((1, C_CONV), const)],
        out_specs=[pl.BlockSpec((1, tm, w), tok) for w, _ in outs],
        out_shape=[jax.ShapeDtypeStruct((b, t, w), dt) for w, dt in outs],
        scratch_shapes=[pltpu.VMEM((tm // CONV_TB, CONV_WIN, C_CONV), F32),
                        pltpu.VMEM((SUBLANE, CONV_WIN, C_CONV), F32),
                        pltpu.VMEM((ncol, D_MODEL), BF16)],
        compiler_params=_cparams("arbitrary", "arbitrary"),
        name="mix_in_latent" if latent else "mix_in",
    )(x, x, x, mod, g_pre1, w_in_ext, q_norm_g, kv_norm_g, wq, wqs, wukv, cos, sin, *conv_w)


def _qk(q, k):
    return lax.dot_general(q, k, (((1,), (1,)), ((), ())), preferred_element_type=F32)


ATT_KCHUNK = 256


def _attn_kernel(cached, q_ref, k_ref, v_ref, *rest):
    if cached:
        kc_ref, vc_ref, o_ref, s_scr = rest
        sources = [(kc_ref, vc_ref), (k_ref, v_ref)]
    else:
        o_ref, s_scr = rest
        sources = [(k_ref, v_ref)]
    n_seq, tq, _ = q_ref.shape
    chunks = []
    for kr, vr in sources:
        for c0 in range(0, kr.shape[1], ATT_KCHUNK):
            c1 = min(c0 + ATT_KCHUNK, kr.shape[1])
            col = chunks[-1][4] + chunks[-1][3] - chunks[-1][2] if chunks else 0
            chunks.append((kr, vr, c0, c1, col))
    for b in range(n_seq):
        outs = []
        for hd in range(N_HEADS):
            ks = slice(hd * HEAD_W, (hd + 1) * HEAD_W)
            vs = slice(hd * D_V, (hd + 1) * D_V)
            qh = q_ref[b, :, ks]
            m = jnp.full((tq, 1), -jnp.inf, F32)
            for kr, vr, c0, c1, col in chunks:
                s = _qk(qh, kr[b, c0:c1, ks]) * (ATT_SCALE * LOG2E)
                s_scr[hd, :, col:col + c1 - c0] = s
                m = jnp.maximum(m, jnp.max(s, axis=-1, keepdims=True))
            l = jnp.zeros((tq, 1), F32)
            o = jnp.zeros((tq, D_V), F32)
            for kr, vr, c0, c1, col in chunks:
                p = jnp.exp2(s_scr[hd, :, col:col + c1 - c0] - m)
                l = l + jnp.sum(p, axis=-1, keepdims=True)
                o = o + _dot(p.astype(BF16), vr[b, c0:c1, vs])
            outs.append(o / l)
        o_ref[b] = jnp.concatenate(outs, axis=-1).astype(BF16)


def _attn(q, k, v, kc=None, vc=None, tq=256, n_seq=1):
    b, t, _ = q.shape
    s = k.shape[1]
    cached = kc is not None
    whole = lambda bi, i: (bi, 0, 0)
    in_specs = [pl.BlockSpec((n_seq, tq, N_HEADS * HEAD_W), lambda bi, i: (bi, i, 0)),
                pl.BlockSpec((n_seq, s, N_HEADS * HEAD_W), whole),
                pl.BlockSpec((n_seq, s, N_HEADS * D_V), whole)]
    args = [q, k, v]
    if cached:
        sc = kc.shape[1]
        in_specs += [pl.BlockSpec((n_seq, sc, N_HEADS * HEAD_W), whole),
                     pl.BlockSpec((n_seq, sc, N_HEADS * D_V), whole)]
        args += [kc, vc]
        s += sc
    return pl.pallas_call(
        functools.partial(_attn_kernel, cached),
        grid=(b // n_seq, t // tq),
        in_specs=in_specs,
        out_specs=pl.BlockSpec((n_seq, tq, N_HEADS * D_V), lambda bi, i: (bi, i, 0)),
        out_shape=jax.ShapeDtypeStruct((b, t, N_HEADS * D_V), BF16),
        scratch_shapes=[pltpu.VMEM((N_HEADS, tq, s), F32)],
        compiler_params=_cparams("arbitrary", "arbitrary"),
        name="attn_cached" if cached else "attn",
    )(*args)


def _cache_kv_kernel(ckv_ref, kr_ref, wukv_ref, k_ref, v_ref):
    kvd = _dot(ckv_ref[0].astype(BF16), wukv_ref[...])
    kr = kr_ref[0]
    kr = jnp.concatenate([kr, jnp.zeros_like(kr)], axis=-1)
    parts = []
    for hd in range(N_HEADS):
        parts.append(kvd[:, hd * D_NOPE:(hd + 1) * D_NOPE])
        parts.append(kr)
    k_ref[0] = jnp.concatenate(parts, axis=-1).astype(BF16)
    v_ref[0] = kvd[:, N_HEADS * D_NOPE:].astype(BF16)


def _cache_kv(ckv, krope, wukv):
    b, s, _ = ckv.shape
    tok = lambda bi: (bi, 0, 0)
    return pl.pallas_call(
        _cache_kv_kernel,
        grid=(b,),
        in_specs=[pl.BlockSpec((1, s, KV_LORA), tok),
                  pl.BlockSpec((1, s, D_ROPE), tok),
                  pl.BlockSpec((KV_LORA, N_HEADS * (D_NOPE + D_V)), lambda bi: (0, 0))],
        out_specs=[pl.BlockSpec((1, s, N_HEADS * HEAD_W), tok),
                   pl.BlockSpec((1, s, N_HEADS * D_V), tok)],
        out_shape=[jax.ShapeDtypeStruct((b, s, N_HEADS * HEAD_W), BF16),
                   jax.ShapeDtypeStruct((b, s, N_HEADS * D_V), BF16)],
        compiler_params=_cparams("arbitrary"),
        name="cache_kv",
    )(ckv, krope, wukv)


def _post_kernel(x_ref, conv_ref, att_ref, mod_ref, wo_ref, gpost_ref, gpre2_ref, wr_hi_ref, wr_cat_ref,
                 br_ref, x1_ref, h2_ref, ri_ref, rw_ref, cnt_ref, wo_b):
    @pl.when((pl.program_id(0) == 0) & (pl.program_id(1) == 0))
    def _():
        wo_b[...] = wo_ref[...].astype(BF16)

    out = _dot(conv_ref[0], wo_b[:C_CONV, :]) + _dot(att_ref[0], wo_b[C_CONV:, :])
    gt1 = mod_ref[0, 2:3, :]
    sh2 = mod_ref[0, 3:4, :]
    sc2 = mod_ref[0, 4:5, :]
    x1 = x_ref[0] + gt1 * _rms(out, gpost_ref[...])
    x1_ref[0] = x1
    h2 = _rms(x1, gpre2_ref[...]) * (1.0 + sc2) + sh2
    h_hi = h2.astype(BF16)
    h2_ref[0] = h_hi
    h_lo = (h2 - h_hi.astype(F32)).astype(BF16)
    hi_terms = _dot(h_hi, wr_cat_ref[...])
    logits = hi_terms + pltpu.roll(hi_terms, LANE // 2, axis=1) + _dot(h_lo, wr_hi_ref[...]) + br_ref[...]

    lane = lax.broadcasted_iota(jnp.int32, logits.shape, 1)
    neg = jnp.float32(-jnp.inf)
    big = jnp.int32(LANE)
    is_g = (lane >= N_EXPERTS) & (lane < N_EXPERTS + N_GROUPS)
    lg = jnp.where(is_g, logits, neg)
    gmax = jnp.max(lg, axis=-1, keepdims=True)
    gidx = jnp.min(jnp.where(lg == gmax, lane, big), axis=-1, keepdims=True) - N_EXPERTS
    g_top = 1.0 / jnp.sum(jnp.exp(lg - gmax), axis=-1, keepdims=True)

    in_grp = (lane >= gidx * E_PER_GROUP) & (lane < (gidx + 1) * E_PER_GROUP)
    le = jnp.where(in_grp, logits, neg)
    m1 = jnp.max(le, axis=-1, keepdims=True)
    i1 = jnp.min(jnp.where(le == m1, lane, big), axis=-1, keepdims=True)
    le2 = jnp.where(lane == i1, neg, le)
    m2 = jnp.max(le2, axis=-1, keepdims=True)
    i2 = jnp.min(jnp.where(le2 == m2, lane, big), axis=-1, keepdims=True)
    r = jnp.exp(m2 - m1)
    w1 = g_top / (1.0 + r)
    w2 = g_top * r / (1.0 + r)
    ri_ref[0] = jnp.where(lane == 0, i1, jnp.where(lane == 1, i2, 0))
    rw_ref[0] = jnp.where(lane == 0, w1, jnp.where(lane == 1, w2, 0.0))
    pairs = jnp.sum(jnp.where((lane == i1) | (lane == i2), 1.0, 0.0), axis=0, keepdims=True)
    cnt_ref[0, 0] = jnp.broadcast_to(pairs, (SUBLANE, LANE))


def _post(x, conv_out, att, mod, mod_row0, w_out, g_post1, g_pre2, wr_hi, wr_lo, br):
    b, t, _ = x.shape
    tm = MOE_TD
    const = lambda bi, i: (0, 0)
    tok = lambda bi, i: (bi, i, 0)
    return pl.pallas_call(
        _post_kernel,
        grid=(b, t // tm),
        in_specs=[pl.BlockSpec((1, tm, D_MODEL), tok),
                  pl.BlockSpec((1, tm, C_CONV), tok),
                  pl.BlockSpec((1, tm, N_HEADS * D_V), tok),
                  pl.BlockSpec((1, 6, D_MODEL), lambda bi, i: (mod_row0 + bi, 0, 0)),
                  pl.BlockSpec((D_MODEL, D_MODEL), const),
                  pl.BlockSpec((1, D_MODEL), const),
                  pl.BlockSpec((1, D_MODEL), const),
                  pl.BlockSpec((D_MODEL, LANE), const),
                  pl.BlockSpec((D_MODEL, LANE), const),
                  pl.BlockSpec((1, LANE), const)],
        out_specs=[pl.BlockSpec((1, tm, D_MODEL), tok),
                   pl.BlockSpec((1, tm, D_MODEL), tok),
                   pl.BlockSpec((1, tm, LANE), tok),
                   pl.BlockSpec((1, tm, LANE), tok),
                   pl.BlockSpec((1, 1, SUBLANE, LANE), lambda bi, i: (bi, i, 0, 0))],
        out_shape=[jax.ShapeDtypeStruct((b, t, D_MODEL), F32),
                   jax.ShapeDtypeStruct((b, t, D_MODEL), BF16),
                   jax.ShapeDtypeStruct((b, t, LANE), jnp.int32),
                   jax.ShapeDtypeStruct((b, t, LANE), F32),
                   jax.ShapeDtypeStruct((b, t // tm, SUBLANE, LANE), F32)],
        scratch_shapes=[pltpu.VMEM((D_MODEL, D_MODEL), BF16)],
        compiler_params=_cparams("arbitrary", "arbitrary"),
        name="post",
    )(x, conv_out, att, mod, w_out, g_post1, g_pre2, wr_hi, wr_lo, br)


def _one_hots(ri):
    lane = lax.broadcasted_iota(jnp.int32, ri.shape, 1)
    oh1 = lane == ri[:, 0:1]
    oh2 = lane == ri[:, 1:2]
    return oh1, oh2, jnp.where(oh1 | oh2, 1.0, 0.0)


def _strictly_lower(n):
    r = lax.broadcasted_iota(jnp.int32, (n, n), 0)
    c = lax.broadcasted_iota(jnp.int32, (n, n), 1)
    return jnp.where(c < r, 1.0, 0.0).astype(BF16)


def _local_positions(oh1, oh2, oh, earlier_ref):
    cnt = jnp.sum(oh, axis=0, keepdims=True)
    nch = jnp.floor((cnt + (CHUNK - 1)) * (1.0 / CHUNK))
    a = lax.broadcasted_iota(jnp.int32, (LANE, LANE), 0)
    b = lax.broadcasted_iota(jnp.int32, (LANE, LANE), 1)
    lower_experts = jnp.where(a < b, 1.0, 0.0).astype(BF16)
    run_start = _dot(jnp.broadcast_to(nch, (SUBLANE, LANE)).astype(BF16), lower_experts)[0:1, :] * CHUNK
    pos = _dot(earlier_ref[...], oh.astype(BF16)) + run_start
    lp1 = jnp.sum(jnp.where(oh1, pos, 0.0), axis=-1, keepdims=True)
    lp2 = jnp.sum(jnp.where(oh2, pos, 0.0), axis=-1, keepdims=True)
    return lp1, lp2


def _plan(cnt, n_tiles_max):
    k = (cnt + (CHUNK - 1)) // CHUNK
    run_end = jnp.cumsum(k, axis=1)
    run_start = run_end - k
    n_chunks = run_end[:, -1]
    total = jnp.sum(k, axis=0)
    padded = (total + (TILE_CH - 1)) // TILE_CH * TILE_CH
    seg_end = jnp.cumsum(padded)
    seg_start = seg_end - padded
    base = seg_start[None, :] + jnp.cumsum(k, axis=0) - k
    c = jnp.arange(MOE_NCH, dtype=jnp.int32)[None, :, None]
    in_run = (run_start[:, None, :] <= c) & (c < run_end[:, None, :])
    dst = jnp.sum(jnp.where(in_run, base[:, None, :] + c - run_start[:, None, :], 0), axis=2)
    n_tiles = seg_end[-1] // TILE_CH
    tail_start = seg_start + total
    tail_len = padded - total
    i32 = lambda t: t.astype(jnp.int32)
    return (i32(dst), i32(n_chunks), i32(tail_start), i32(tail_len), i32(seg_start // TILE_CH),
            i32(seg_end // TILE_CH), i32(n_tiles.reshape(1)))


def _dispatch_kernel(steps_a, dst_ref, nch_ref, ts_ref, tl_ref, nt_ref, h2a_ref, ria_ref, rwa_ref, h2b_ref, rib_ref,
                     rwb_ref, xs_hbm, lp_ref, xbuf, zx, sem, zsem, earlier):
    i = pl.program_id(0)
    n = pl.num_programs(0)
    slot = i % 2
    td = MOE_TD
    n_tiles_max = xs_hbm.shape[0] // MOE_TILE

    def zero_tile(m):
        rows = pl.ds(pl.multiple_of(m * MOE_TILE, MOE_TILE), MOE_TILE)
        return pltpu.make_async_copy(zx, xs_hbm.at[rows, :], zsem.at[1])

    @pl.when(i == 0)
    def _():
        zx[...] = jnp.zeros_like(zx)
        earlier[...] = _strictly_lower(MOE_TD)
        for e in range(N_EXPERTS):
            def fill(m, carry, e=e):
                _chunk_copy(zx, 0, xs_hbm, ts_ref[e] + m, zsem.at[0]).start()
                return carry
            lax.fori_loop(0, tl_ref[e], fill, 0)

        def fill_tile(m, carry):
            zero_tile(m).start()
            return carry
        lax.fori_loop(nt_ref[0], n_tiles_max, fill_tile, 0)

    from_a = i < steps_a
    ri = jnp.where(from_a, ria_ref[...], rib_ref[...])
    rw = jnp.where(from_a, rwa_ref[...], rwb_ref[...])
    h2 = jnp.where(from_a, h2a_ref[...], h2b_ref[...])
    oh1, oh2, oh = _one_hots(ri)
    lp1, lp2 = _local_positions(oh1, oh2, oh, earlier)
    lane = lax.broadcasted_iota(jnp.int32, ri.shape, 1)
    lp_ref[...] = jnp.where(lane == 0, lp1, jnp.where(lane == 1, lp2, 0.0)).astype(jnp.int32)

    cols = jnp.where(lane == 0, lp1, jnp.where(lane == 1, lp2, jnp.where(lane == 2, rw[:, 0:1],
                                                                         jnp.where(lane == 3, rw[:, 1:2], 0.0))))
    rows4 = cols.T
    row = lax.broadcasted_iota(jnp.int32, (MOE_L, td), 0).astype(F32)
    p1 = row == rows4[0:1, :]
    p2 = row == rows4[1:2, :]
    perm = jnp.where(p1 | p2, 1.0, 0.0).astype(BF16)
    xbuf[slot, :, 0:D_MODEL] = _dot(perm, h2).astype(BF16)
    w = jnp.sum(jnp.where(p1, rows4[2:3, :], 0.0) + jnp.where(p2, rows4[3:4, :], 0.0), axis=-1, keepdims=True)
    w0 = w.astype(BF16).astype(F32)
    w1 = (w - w0).astype(BF16).astype(F32)
    w2 = (w - w0 - w1).astype(BF16).astype(F32)
    wl = lax.broadcasted_iota(jnp.int32, (MOE_L, LANE), 1)
    terms = jnp.where(wl == 0, w0, jnp.where(wl == 1, w1, jnp.where(wl == 2, w2, 0.0)))
    xbuf[slot, :, D_MODEL:XS_W] = terms.astype(BF16)

    def wait_chunks(count, s):
        def body(c, carry):
            _chunk_copy(xbuf.at[s], 0, xs_hbm, 0, sem.at[s]).wait()
            return carry
        lax.fori_loop(0, count, body, 0)

    @pl.when(i > 0)
    def _():
        wait_chunks(nch_ref[i - 1], 1 - slot)

    def send(c, carry):
        _chunk_copy(xbuf.at[slot], c, xs_hbm, dst_ref[i, c], sem.at[slot]).start()
        return carry
    lax.fori_loop(0, nch_ref[i], send, 0)

    @pl.when(i == n - 1)
    def _():
        wait_chunks(nch_ref[i], slot)
        for e in range(N_EXPERTS):
            def drain(m, carry):
                _chunk_copy(zx, 0, xs_hbm, 0, zsem.at[0]).wait()
                return carry
            lax.fori_loop(0, tl_ref[e], drain, 0)

        def drain_tile(m, carry):
            zero_tile(m).wait()
            return carry
        lax.fori_loop(nt_ref[0], n_tiles_max, drain_tile, 0)


def _dispatch(dst, n_chunks, tail_start, tail_len, n_tiles, part_a, part_b, n_tiles_max):
    steps_a = part_a[0].shape[0] // MOE_TD
    steps_b = part_b[0].shape[0] // MOE_TD
    in_a = lambda i, *_: (jnp.minimum(i, steps_a - 1), 0)
    in_b = lambda i, *_: (jnp.maximum(i - steps_a, 0), 0)
    specs = lambda f: [pl.BlockSpec((MOE_TD, D_MODEL), f), pl.BlockSpec((MOE_TD, LANE), f),
                       pl.BlockSpec((MOE_TD, LANE), f)]
    rows = n_tiles_max * MOE_TILE
    return pl.pallas_call(
        functools.partial(_dispatch_kernel, steps_a),
        grid_spec=pltpu.PrefetchScalarGridSpec(
            num_scalar_prefetch=5,
            grid=(steps_a + steps_b,),
            in_specs=specs(in_a) + specs(in_b),
            out_specs=[pl.BlockSpec(memory_space=pl.ANY),
                       pl.BlockSpec((MOE_TD, LANE), lambda i, *_: (i, 0))],
            scratch_shapes=[pltpu.VMEM((2, MOE_L, XS_W), BF16), pltpu.VMEM((MOE_TILE, XS_W), BF16),
                            pltpu.SemaphoreType.DMA((2,)), pltpu.SemaphoreType.DMA((2,)),
                            pltpu.VMEM((MOE_TD, MOE_TD), BF16)]),
        out_shape=[jax.ShapeDtypeStruct((rows, XS_W), BF16),
                   jax.ShapeDtypeStruct(((steps_a + steps_b) * MOE_TD, LANE), jnp.int32)],
        compiler_params=_cparams("arbitrary"),
        name="moe_dispatch",
    )(dst, n_chunks, tail_start, tail_len, n_tiles, *part_a, *part_b)


def _moe_gemm_kernel(t0_ref, t1_ref, nt_ref, wg_ref, wu_ref, wd_ref, xs_hbm, y_hbm,
                     wg_b, wu_b, wd_b, xbuf, ybuf, zbuf, isem, osem, zsem, issued):
    e = pl.program_id(0)
    nt = nt_ref[0]
    n_tiles_max = y_hbm.shape[0] // MOE_TILE
    tile_rows = lambda t: pl.ds(pl.multiple_of(t * MOE_TILE, MOE_TILE), MOE_TILE)

    def load(t, s):
        return (pltpu.make_async_copy(xs_hbm.at[tile_rows(t), :], xbuf.at[s], isem.at[s]),)

    def store(t, s):
        return pltpu.make_async_copy(ybuf.at[s], y_hbm.at[tile_rows(t), :], osem.at[s])

    def zero_tile(t):
        return pltpu.make_async_copy(zbuf, y_hbm.at[tile_rows(t), :], zsem.at[0])

    @pl.when(e == 0)
    def _():
        issued[0] = 0
        zbuf[...] = jnp.zeros_like(zbuf)

        def fill(t, carry):
            zero_tile(t).start()
            return carry
        lax.fori_loop(nt, n_tiles_max, fill, 0)

    wg_b[...] = wg_ref[0].astype(BF16)
    wu_b[...] = wu_ref[0].astype(BF16)
    wd_b[...] = wd_ref[0].astype(BF16)

    def top_up(t):
        upto = jnp.minimum(t + GEMM_RING, nt)

        def start(u, carry):
            for cp in load(u, u % GEMM_RING):
                cp.start()
            return carry
        lax.fori_loop(issued[0], upto, start, 0)
        issued[0] = jnp.maximum(issued[0], upto)

    def process(t, n):
        top_up(t)
        for k in range(n):
            for cp in load(t + k, (t + k) % GEMM_RING):
                cp.wait()
        rows = [xbuf[(t + k) % GEMM_RING] for k in range(n)]
        xw = rows[0] if n == 1 else jnp.concatenate(rows, axis=0)
        x = xw[:, 0:D_MODEL]
        wt = xw[:, D_MODEL:XS_W].astype(F32)
        w = wt[:, 0:1] + wt[:, 1:2] + wt[:, 2:3]
        a = _dot(x, wg_b[...])
        u = _dot(x, wu_b[...])
        he = (a * _sigmoid(a)) * u
        y = (_dot(he.astype(BF16), wd_b[...]) * w).astype(BF16)
        for k in range(n):
            s = (t + k) % GEMM_OUT

            @pl.when(t + k >= GEMM_OUT)
            def _(k=k, s=s):
                store(t + k - GEMM_OUT, s).wait()
            ybuf[s] = y[k * MOE_TILE:(k + 1) * MOE_TILE, :]
            store(t + k, s).start()

    t0 = t0_ref[e]
    t1 = t1_ref[e]

    def pair(p, carry):
        process(t0 + 2 * p, 2)
        return carry
    lax.fori_loop(0, (t1 - t0) // 2, pair, 0)

    @pl.when((t1 - t0) % 2 == 1)
    def _():
        process(t1 - 1, 1)

    @pl.when(e == pl.num_programs(0) - 1)
    def _():
        for k in range(1, GEMM_OUT + 1):
            @pl.when(nt >= k)
            def _(k=k):
                store(nt - k, (nt - k) % GEMM_OUT).wait()

        def drain(t, carry):
            zero_tile(t).wait()
            return carry
        lax.fori_loop(nt, n_tiles_max, drain, 0)


def _moe_gemm(tile0, tile1, n_tiles, xs, w_gate, w_up, w_down):
    wmap = lambda e, *_: (e, 0, 0)
    return pl.pallas_call(
        _moe_gemm_kernel,
        grid_spec=pltpu.PrefetchScalarGridSpec(
            num_scalar_prefetch=3,
            grid=(N_EXPERTS,),
            in_specs=[pl.BlockSpec((1, D_MODEL, D_EXPERT), wmap),
                      pl.BlockSpec((1, D_MODEL, D_EXPERT), wmap),
                      pl.BlockSpec((1, D_EXPERT, D_MODEL), wmap),
                      pl.BlockSpec(memory_space=pl.ANY)],
            out_specs=pl.BlockSpec(memory_space=pl.ANY),
            scratch_shapes=[pltpu.VMEM((D_MODEL, D_EXPERT), BF16), pltpu.VMEM((D_MODEL, D_EXPERT), BF16),
                            pltpu.VMEM((D_EXPERT, D_MODEL), BF16),
                            pltpu.VMEM((GEMM_RING, MOE_TILE, XS_W), BF16),
                            pltpu.VMEM((GEMM_OUT, MOE_TILE, D_MODEL), BF16), pltpu.VMEM((MOE_TILE, D_MODEL), BF16),
                            pltpu.SemaphoreType.DMA((GEMM_RING,)), pltpu.SemaphoreType.DMA((GEMM_OUT,)),
                            pltpu.SemaphoreType.DMA((1,)), pltpu.SMEM((1,), jnp.int32)]),
        out_shape=jax.ShapeDtypeStruct((xs.shape[0], D_MODEL), BF16),
        compiler_params=_cparams("arbitrary"),
        name="moe_gemm",
    )(tile0, tile1, n_tiles, w_gate, w_up, w_down, xs)


def _moe_combine_kernel(step0, dst_ref, nch_ref, lp_ref, x1_ref, mod_ref, gpost_ref, y_hbm, o_ref, ybuf, sem):
    i = pl.program_id(0)
    n = pl.num_programs(0)
    slot = i % 2

    def fetch(step, s):
        def body(c, carry):
            _chunk_copy(y_hbm, dst_ref[step, c], ybuf.at[s], c, sem.at[s]).start()
            return carry
        lax.fori_loop(0, nch_ref[step], body, 0)

    @pl.when(i == 0)
    def _():
        ybuf[...] = jnp.zeros_like(ybuf)
        fetch(step0, 0)

    @pl.when(i + 1 < n)
    def _():
        fetch(step0 + i + 1, 1 - slot)

    def wait(c, carry):
        _chunk_copy(y_hbm, 0, ybuf.at[slot], 0, sem.at[slot]).wait()
        return carry
    lax.fori_loop(0, nch_ref[step0 + i], wait, 0)

    lp = lp_ref[...]
    col = lax.broadcasted_iota(jnp.int32, (MOE_TD, MOE_L), 1)
    unperm = jnp.where((col == lp[:, 0:1]) | (col == lp[:, 1:2]), 1.0, 0.0).astype(BF16)
    moe = _dot(unperm, ybuf[slot])
    gt2 = mod_ref[0, 5:6, :]
    o_ref[...] = x1_ref[...] + gt2 * _rms(moe, gpost_ref[...])


def _moe_combine(step0, dst, n_chunks, lp, x1, mod, mod_row, y_sorted, g_post2):
    n = x1.shape[0]
    tok = lambda i, *_: (i, 0)
    return pl.pallas_call(
        functools.partial(_moe_combine_kernel, step0),
        grid_spec=pltpu.PrefetchScalarGridSpec(
            num_scalar_prefetch=2,
            grid=(n // MOE_TD,),
            in_specs=[pl.BlockSpec((MOE_TD, LANE), lambda i, *_: (step0 + i, 0)),
                      pl.BlockSpec((MOE_TD, D_MODEL), tok),
                      pl.BlockSpec((1, 6, D_MODEL), lambda i, *_: (mod_row(i), 0, 0)),
                      pl.BlockSpec((1, D_MODEL), lambda i, *_: (0, 0)),
                      pl.BlockSpec(memory_space=pl.ANY)],
            out_specs=pl.BlockSpec((MOE_TD, D_MODEL), tok),
            scratch_shapes=[pltpu.VMEM((2, MOE_L, D_MODEL), BF16), pltpu.SemaphoreType.DMA((2,))]),
        out_shape=jax.ShapeDtypeStruct((n, D_MODEL), F32),
        compiler_params=_cparams("arbitrary"),
        name="moe_combine",
    )(dst, n_chunks, lp, x1, mod, g_post2, y_sorted)


def _moe(part_a, part_b, mod, mod_row_a, mod_row_b, w_gate, w_up, w_down, g_post2):
    cnt = jnp.concatenate([part_a[4], part_b[4]], axis=0)[:, 0, :N_EXPERTS].astype(jnp.int32)
    steps_a = part_a[4].shape[0]
    n_tiles_max = (cnt.shape[0] * MOE_NCH + N_EXPERTS * (TILE_CH - 1)) // TILE_CH + 1
    dst, n_chunks, tail_start, tail_len, tile0, tile1, n_tiles = _plan(cnt, n_tiles_max)
    xs, lp = _dispatch(dst, n_chunks, tail_start, tail_len, n_tiles, part_a[1:4], part_b[1:4], n_tiles_max)
    y_sorted = _moe_gemm(tile0, tile1, n_tiles, xs, w_gate, w_up, w_down)
    ya = _moe_combine(0, dst, n_chunks, lp, part_a[0], mod, mod_row_a, y_sorted, g_post2)
    yb = _moe_combine(steps_a, dst, n_chunks, lp, part_b[0], mod, mod_row_b, y_sorted, g_post2)
    return ya, yb


def _rotate_half_cols(w):
    n = w.shape[-1]
    w4 = w.reshape(w.shape[:-1] + (n // 32, 2, 16))
    return jnp.stack([-w4[..., 1, :], w4[..., 0, :]], axis=-2).reshape(w.shape)


def _pad_lanes(w):
    return jnp.concatenate([w, jnp.zeros(w.shape[:-1] + (LANE - w.shape[-1],), w.dtype)], axis=-1)


def _rope_tables(t):
    rows = t // GRID_W
    n_freq = D_ROPE // 4
    f32 = np.float32
    freqs = f32(ROPE_BASE) ** (-np.arange(n_freq, dtype=f32) / f32(n_freq))
    ang_r = np.arange(rows, dtype=f32)[:, None] * freqs
    ang_c = np.arange(GRID_W, dtype=f32)[:, None] * freqs

    def table(fn):
        r = np.repeat(fn(ang_r).astype(f32), GRID_W, axis=0)
        c = np.tile(fn(ang_c).astype(f32), (rows, 1))
        pad = np.zeros((t, LANE - D_ROPE), f32)
        return jnp.asarray(np.concatenate([r, r, c, c, pad], axis=-1))
    return table(np.cos), table(np.sin)


def kernel(x_prompt, x_sample, cache_ckv, cache_krope, c, c_ctx, w_ada, b_ada, g_pre1, g_post1, g_pre2, g_post2, w_in, w_dw, b_dw, conv_ln_g, conv_ln_b, q_norm_g, kv_norm_g, w_uq, w_ukv, w_out, w_rg, b_rg, w_re, b_re, w_gate, w_up, w_down):
    nb, seq, d = x_prompt.shape
    db, dseq, _ = x_sample.shape
    l = 0

    cvec = jnp.concatenate([c_ctx[None, :], c, jnp.zeros((8 - 1 - db, d), F32)], axis=0)
    mod = _ada(cvec, w_ada[l], b_ada[l]).reshape(8, 6, d)

    w_in_ctx = w_in_lat = w_in[l].T
    wuq = w_uq[l].reshape(Q_LORA, N_HEADS, D_NOPE + D_ROPE)
    wq_rope = wuq[:, :, D_NOPE:]
    wq = jnp.concatenate([wuq[:, :, :D_NOPE], _pad_lanes(wq_rope)], axis=-1)
    wq = wq.reshape(Q_LORA, N_HEADS * HEAD_W).astype(BF16)
    wqs = _pad_lanes(_rotate_half_cols(wq_rope)).reshape(Q_LORA, N_HEADS * LANE).astype(BF16)
    wukv4 = w_ukv[l].reshape(KV_LORA, N_HEADS, D_NOPE + D_V)
    wukv = jnp.concatenate([wukv4[:, :, :D_NOPE].reshape(KV_LORA, N_HEADS * D_NOPE),
                            wukv4[:, :, D_NOPE:].reshape(KV_LORA, N_HEADS * D_V)], axis=-1).astype(BF16)
    w_out_b = w_out[l]
    w_r = jnp.concatenate([w_re[l], w_rg[l], jnp.zeros((d, LANE - N_EXPERTS - N_GROUPS), F32)], axis=-1)
    wr_hi = w_r.astype(BF16)
    wr_lo = jnp.concatenate([wr_hi[:, :LANE // 2], (w_r - wr_hi.astype(F32)).astype(BF16)[:, :LANE // 2]], axis=-1)
    b_r = jnp.concatenate([b_re[l], b_rg[l], jnp.zeros((LANE - N_EXPERTS - N_GROUPS,), F32)])[None, :]
    cos, sin = _rope_tables(dseq)
    row = lambda v: v[l][None, :]

    tm_c = 512
    xp_flat = x_prompt.reshape(1, nb * seq, d)
    conv_w = (w_dw[l], row(b_dw), row(conv_ln_g), row(conv_ln_b))
    assert seq == CONV_TB, "a context step must hold whole sequences of CONV_TB tokens"
    conv_out, q, k, v, ckv, kr = _mix_in(xp_flat, mod, 0, row(g_pre1), w_in_ctx, row(q_norm_g), row(kv_norm_g),
                                         wq, wqs, wukv, cos[:tm_c], sin[:tm_c], conv_w, False, tm_c)
    per_seq = lambda a: a.reshape(nb, seq, a.shape[-1])
    att = _attn(per_seq(q), per_seq(k), per_seq(v), n_seq=4)
    flat = lambda a: a.reshape(1, nb * seq, a.shape[-1])
    post_c = _post(xp_flat, conv_out, flat(att), mod, 0, w_out_b, row(g_post1), row(g_pre2),
                   wr_hi, wr_lo, b_r)
    state_ckv = ckv.reshape(nb, 1, seq, KV_LORA)
    state_krope = kr.reshape(nb, 1, seq, D_ROPE)

    tm_s = 512
    kc, vc = _cache_kv(cache_ckv[:, l], cache_krope[:, l], wukv)
    conv_out, q, k, v, _, _ = _mix_in(x_sample, mod, 1, row(g_pre1), w_in_lat, row(q_norm_g), row(kv_norm_g),
                                      wq, wqs, wukv, cos, sin, conv_w, True, tm_s)
    att = _attn(q, k, v, kc, vc)
    post_s = _post(x_sample, conv_out, att, mod, 1, w_out_b, row(g_post1), row(g_pre2), wr_hi, wr_lo, b_r)

    tokens = lambda parts: tuple(a.reshape((-1,) + a.shape[2:]) for a in parts)
    steps_per_req = dseq // MOE_TD
    yp, ys = _moe(tokens(post_c), tokens(post_s), mod, lambda i: 0, lambda i: 1 + i // steps_per_req,
                  w_gate[l], w_up[l], w_down[l], row(g_post2))

    return (yp.reshape(nb, seq, d), ys.reshape(db, dseq, d), state_ckv, state_krope)
```

```python
import functools

import jax
import jax.numpy as jnp
import numpy as np
from jax import lax
from jax.experimental import pallas as pl
from jax.experimental.pallas import tpu as pltpu

D_MODEL = 1024
GRID_W = 64
C_CONV = 512
CONV_K = 31
N_HEADS = 4
D_NOPE = 128
D_ROPE = 64
D_V = 128
Q_LORA = 384
KV_LORA = 256
N_GROUPS = 4
E_PER_GROUP = 8
N_EXPERTS = 32
D_EXPERT = 256
ROPE_BASE = 10000.0
EPS = 1e-6
ATT_SCALE = (D_NOPE + D_ROPE) ** -0.5
LOG2E = 1.4426950408889634

LANE = 128
SUBLANE = 8
HEAD_W = 2 * LANE
CONV_HALO = 16
ADA_TN = 1536
ATT_TQ = 256
ATT_SEQS = 8
MOE_TD = 512
CHUNK = 2 * SUBLANE
MOE_NCH = (2 * MOE_TD + N_EXPERTS * (CHUNK - 1)) // CHUNK
MOE_L = MOE_NCH * CHUNK
MOE_TILE = 256
TILE_CH = MOE_TILE // CHUNK
GEMM_RING = 6
GEMM_OUT = 4
XS_W = D_MODEL + LANE
VMEM_LIMIT = 56 * 1024 * 1024

BF16 = jnp.bfloat16
F32 = jnp.float32


def _cparams(*sem):
    return pltpu.CompilerParams(dimension_semantics=sem, vmem_limit_bytes=VMEM_LIMIT)


def _rms(x, g):
    return x * lax.rsqrt(jnp.mean(x * x, axis=-1, keepdims=True) + EPS) * g


def _sigmoid(x):
    return 1.0 / (1.0 + jnp.exp(-x))


def _dot(a, b):
    return jnp.dot(a, b, preferred_element_type=F32)


def _dot_t(a, b_t):
    return lax.dot_general(a, b_t, (((1,), (1,)), ((), ())), preferred_element_type=F32)


def _chunk_copy(src, src_chunk, dst, dst_chunk, sem):
    rows = lambda c: pl.ds(c * CHUNK if isinstance(c, int) else pl.multiple_of(c * CHUNK, CHUNK), CHUNK)
    return pltpu.make_async_copy(src.at[rows(src_chunk), :], dst.at[rows(dst_chunk), :], sem)


def _ada_kernel(c_ref, w_ref, b_ref, o_ref):
    c = c_ref[...]
    s = (c * _sigmoid(c)).astype(BF16)
    o_ref[...] = _dot(s, w_ref[...].astype(BF16)) + b_ref[...]


def _ada(cvec, w_ada, b_ada):
    n = w_ada.shape[1]
    tn = ADA_TN
    return pl.pallas_call(
        _ada_kernel,
        grid=(n // tn,),
        in_specs=[pl.BlockSpec((8, D_MODEL), lambda j: (0, 0)),
                  pl.BlockSpec((D_MODEL, tn), lambda j: (0, j)),
                  pl.BlockSpec((1, tn), lambda j: (0, j))],
        out_specs=pl.BlockSpec((8, tn), lambda j: (0, j)),
        out_shape=jax.ShapeDtypeStruct((8, n), F32),
        compiler_params=_cparams("arbitrary"),
        name="ada",
    )(cvec, w_ada, b_ada.reshape(1, n))


CONV_TB = 256
CONV_WIN = CONV_TB + 2 * CONV_HALO
CONV_TT = 32


def _conv_window(win_ref, sh_ref, w_ref, b_ref, g_ref, bb_ref, o_ref, row0):
    first = CONV_HALO - CONV_K // 2
    rows = CONV_WIN - SUBLANE
    win = win_ref[...]
    for s in range(SUBLANE):
        sh_ref[s, 0:rows, :] = win[s:s + rows, :]
    for c in range(CONV_TB // CONV_TT):
        acc = jnp.zeros((CONV_TT, C_CONV), F32)
        for k in range(CONV_K):
            off = first + k
            r0 = c * CONV_TT + off // SUBLANE * SUBLANE
            acc = acc + sh_ref[off % SUBLANE, r0:r0 + CONV_TT, :] * w_ref[k:k + 1, :]
        y = acc + b_ref[...]
        mu = jnp.mean(y, axis=-1, keepdims=True)
        yc = y - mu
        var = jnp.mean(yc * yc, axis=-1, keepdims=True)
        z = yc * lax.rsqrt(var + EPS) * g_ref[...] + bb_ref[...]
        o_ref[0, row0 + c * CONV_TT:row0 + (c + 1) * CONV_TT, :] = (z * _sigmoid(z)).astype(BF16)


def _mix_in_kernel(latent, x_ref, xp_ref, xn_ref, mod_ref, gpre_ref, win_ref, qg_ref, kvg_ref, wq_ref, wqs_ref,
                   wukv_ref, cos_ref, sin_ref, wdw_ref, bdw_ref, lng_ref, lnb_ref,
                   conv_ref, q_ref, k_ref, v_ref, ckv_ref, kr_ref, cwin, csh, wbf):
    rope = latent

    @pl.when((pl.program_id(0) == 0) & (pl.program_id(1) == 0))
    def _():
        wbf[...] = win_ref[...].astype(BF16)

    x = x_ref[0]
    tm = x.shape[0]
    sh1 = mod_ref[0, 0:1, :]
    sc1 = mod_ref[0, 1:2, :]
    modulate = lambda v: _rms(v, gpre_ref[...]) * (1.0 + sc1) + sh1
    glu = lambda uc: uc[:, :C_CONV] * _sigmoid(uc[:, C_CONV:2 * C_CONV])
    u = _dot_t(modulate(x).astype(BF16), wbf[...])
    hglu = glu(u)

    zeros = jnp.zeros((CONV_HALO, C_CONV), F32)
    n_win = tm // CONV_TB
    if latent:
        i = pl.program_id(1)
        xh = jnp.concatenate([xp_ref[0], xn_ref[0]], axis=0)
        hh = glu(_dot_t(modulate(xh).astype(BF16), wbf[0:2 * C_CONV, :]))
        before = jnp.where(i > 0, hh[0:CONV_HALO], zeros)
        after = jnp.where(i < pl.num_programs(1) - 1, hh[CONV_HALO:2 * CONV_HALO], zeros)
    for j in range(n_win):
        if latent:
            lo, hi, d0 = j * CONV_TB - CONV_HALO, j * CONV_TB + CONV_TB + CONV_HALO, 0
            if lo < 0:
                cwin[j, 0:CONV_HALO, :] = before
                lo, d0 = 0, CONV_HALO
            if hi > tm:
                cwin[j, CONV_WIN - CONV_HALO:CONV_WIN, :] = after
                hi = tm
            cwin[j, d0:d0 + hi - lo, :] = hglu[lo:hi, :]
        else:
            cwin[j, 0:CONV_HALO, :] = zeros
            cwin[j, CONV_HALO:CONV_HALO + CONV_TB, :] = hglu[j * CONV_TB:(j + 1) * CONV_TB, :]
            cwin[j, CONV_HALO + CONV_TB:CONV_WIN, :] = zeros
    for j in range(n_win):
        _conv_window(cwin.at[j], csh, wdw_ref, bdw_ref, lng_ref, lnb_ref, conv_ref, j * CONV_TB)

    o_q = 2 * C_CONV
    o_kv = o_q + Q_LORA
    o_kr = o_kv + KV_LORA
    qn = _rms(u[:, o_q:o_kv], qg_ref[...]).astype(BF16)
    qf = _dot(qn, wq_ref[...])
    ckv = _rms(u[:, o_kv:o_kr], kvg_ref[...])
    ckv_ref[0] = ckv
    kvd = _dot(ckv.astype(BF16), wukv_ref[...])
    kr64 = u[:, o_kr:o_kr + D_ROPE]
    kr_ref[0] = kr64
    kr = jnp.concatenate([kr64, jnp.zeros((tm, LANE - D_ROPE), F32)], axis=-1)
    if rope:
        cos = cos_ref[...]
        sin = sin_ref[...]
        qs = _dot(qn, wqs_ref[...])
        lane = lax.broadcasted_iota(jnp.int32, kr.shape, 1)
        partner = jnp.where(lane % (D_ROPE // 2) < D_ROPE // 4,
                            -pltpu.roll(kr, LANE - D_ROPE // 4, axis=1), pltpu.roll(kr, D_ROPE // 4, axis=1))
        kr = kr * cos + partner * sin
    q_parts = []
    k_parts = []
    for hd in range(N_HEADS):
        q_parts.append(qf[:, hd * HEAD_W:hd * HEAD_W + LANE])
        qr = qf[:, hd * HEAD_W + LANE:(hd + 1) * HEAD_W]
        if rope:
            qr = qr * cos + qs[:, hd * LANE:(hd + 1) * LANE] * sin
        q_parts.append(qr)
        k_parts.append(kvd[:, hd * D_NOPE:(hd + 1) * D_NOPE])
        k_parts.append(kr)
    q_ref[0] = jnp.concatenate(q_parts, axis=-1).astype(BF16)
    k_ref[0] = jnp.concatenate(k_parts, axis=-1).astype(BF16)
    v_ref[0] = kvd[:, N_HEADS * D_NOPE:].astype(BF16)


def _mix_in(x, mod, mod_row0, g_pre1, w_in_ext, q_norm_g, kv_norm_g, wq, wqs, wukv, cos, sin, conv_w, latent, tm):
    b, t, _ = x.shape
    ncol = w_in_ext.shape[0]
    const = lambda bi, i: (0, 0)
    tok = lambda bi, i: (bi, i, 0)
    table = (lambda bi, i: (i, 0)) if latent else const
    halo_blocks = tm // CONV_HALO
    prev_rows = lambda bi, i: (bi, jnp.maximum(i * halo_blocks - 1, 0), 0)
    next_rows = lambda bi, i: (bi, jnp.minimum((i + 1) * halo_blocks, t // CONV_HALO - 1), 0)
    outs = [(C_CONV, BF16), (N_HEADS * HEAD_W, BF16), (N_HEADS * HEAD_W, BF16), (N_HEADS * D_V, BF16),
            (KV_LORA, F32), (D_ROPE, F32)]
    return pl.pallas_call(
        functools.partial(_mix_in_kernel, latent),
        grid=(b, t // tm),
        in_specs=[pl.BlockSpec((1, tm, D_MODEL), tok),
                  pl.BlockSpec((1, CONV_HALO, D_MODEL), prev_rows),
                  pl.BlockSpec((1, CONV_HALO, D_MODEL), next_rows),
                  pl.BlockSpec((1, 6, D_MODEL), lambda bi, i: (mod_row0 + bi, 0, 0)),
                  pl.BlockSpec((1, D_MODEL), const),
                  pl.BlockSpec((ncol, D_MODEL), const),
                  pl.BlockSpec((1, Q_LORA), const),
                  pl.BlockSpec((1, KV_LORA), const),
                  pl.BlockSpec((Q_LORA, N_HEADS * HEAD_W), const),
                  pl.BlockSpec((Q_LORA, N_HEADS * LANE), const),
                  pl.BlockSpec((KV_LORA, N_HEADS * (D_NOPE + D_V)), const),
                  pl.BlockSpec((tm, LANE), table),
                  pl.BlockSpec((tm, LANE), table),
                  pl.BlockSpec((CONV_K, C_CONV), const),
                  pl.BlockSpec((1, C_CONV), const),
                  pl.BlockSpec((1, C_CONV), const),
                  pl.BlockSpec((1, C_CONV), const)],
        out_specs=[pl.BlockSpec((1, tm, w), tok) for w, _ in outs],
        out_shape=[jax.ShapeDtypeStruct((b, t, w), dt) for w, dt in outs],
        scratch_shapes=[pltpu.VMEM((tm // CONV_TB, CONV_WIN, C_CONV), F32),
                        pltpu.VMEM((SUBLANE, CONV_WIN, C_CONV), F32),
                        pltpu.VMEM((ncol, D_MODEL), BF16)],
        compiler_params=_cparams("arbitrary", "arbitrary"),
        name="mix_in_latent" if latent else "mix_in",
    )(x, x, x, mod, g_pre1, w_in_ext, q_norm_g, kv_norm_g, wq, wqs, wukv, cos, sin, *conv_w)


ATT_KCHUNK = 256


def _attn_kernel(cached, q_ref, k_ref, v_ref, *rest):
    if cached:
        kc_ref, vc_ref, o_ref, s_scr = rest
        sources = [(kc_ref, vc_ref), (k_ref, v_ref)]
    else:
        o_ref, s_scr = rest
        sources = [(k_ref, v_ref)]
    n_seq, tq, _ = q_ref.shape
    chunks = []
    for kr, vr in sources:
        for c0 in range(0, kr.shape[1], ATT_KCHUNK):
            c1 = min(c0 + ATT_KCHUNK, kr.shape[1])
            col = chunks[-1][4] + chunks[-1][3] - chunks[-1][2] if chunks else 0
            chunks.append((kr, vr, c0, c1, col))
    for b in range(n_seq):
        outs = []
        for hd in range(N_HEADS):
            ks = slice(hd * HEAD_W, (hd + 1) * HEAD_W)
            vs = slice(hd * D_V, (hd + 1) * D_V)
            qh = q_ref[b, :, ks]
            m = jnp.full((tq, 1), -jnp.inf, F32)
            for kr, vr, c0, c1, col in chunks:
                s = _dot_t(qh, kr[b, c0:c1, ks]) * (ATT_SCALE * LOG2E)
                s_scr[hd, :, col:col + c1 - c0] = s
                m = jnp.maximum(m, jnp.max(s, axis=-1, keepdims=True))
            l = jnp.zeros((tq, 1), F32)
            o = jnp.zeros((tq, D_V), F32)
            for kr, vr, c0, c1, col in chunks:
                p = jnp.exp2(s_scr[hd, :, col:col + c1 - c0] - m)
                l = l + jnp.sum(p, axis=-1, keepdims=True)
                o = o + _dot(p.astype(BF16), vr[b, c0:c1, vs])
            outs.append(o / l)
        o_ref[b] = jnp.concatenate(outs, axis=-1).astype(BF16)


def _attn(q, k, v, kc=None, vc=None, tq=ATT_TQ, n_seq=1):
    b, t, _ = q.shape
    s = k.shape[1]
    cached = kc is not None
    whole = lambda bi, i: (bi, 0, 0)
    in_specs = [pl.BlockSpec((n_seq, tq, N_HEADS * HEAD_W), lambda bi, i: (bi, i, 0)),
                pl.BlockSpec((n_seq, s, N_HEADS * HEAD_W), whole),
                pl.BlockSpec((n_seq, s, N_HEADS * D_V), whole)]
    args = [q, k, v]
    if cached:
        sc = kc.shape[1]
        in_specs += [pl.BlockSpec((n_seq, sc, N_HEADS * HEAD_W), whole),
                     pl.BlockSpec((n_seq, sc, N_HEADS * D_V), whole)]
        args += [kc, vc]
        s += sc
    return pl.pallas_call(
        functools.partial(_attn_kernel, cached),
        grid=(b // n_seq, t // tq),
        in_specs=in_specs,
        out_specs=pl.BlockSpec((n_seq, tq, N_HEADS * D_V), lambda bi, i: (bi, i, 0)),
        out_shape=jax.ShapeDtypeStruct((b, t, N_HEADS * D_V), BF16),
        scratch_shapes=[pltpu.VMEM((N_HEADS, tq, s), F32)],
        compiler_params=_cparams("arbitrary", "arbitrary"),
        name="attn_cached" if cached else "attn",
    )(*args)


def _cache_kv_kernel(ckv_ref, kr_ref, wukv_ref, k_ref, v_ref):
    kvd = _dot(ckv_ref[0].astype(BF16), wukv_ref[...])
    kr = kr_ref[0]
    kr = jnp.concatenate([kr, jnp.zeros_like(kr)], axis=-1)
    parts = []
    for hd in range(N_HEADS):
        parts.append(kvd[:, hd * D_NOPE:(hd + 1) * D_NOPE])
        parts.append(kr)
    k_ref[0] = jnp.concatenate(parts, axis=-1).astype(BF16)
    v_ref[0] = kvd[:, N_HEADS * D_NOPE:].astype(BF16)


def _cache_kv(ckv, krope, wukv):
    b, s, _ = ckv.shape
    tok = lambda bi: (bi, 0, 0)
    return pl.pallas_call(
        _cache_kv_kernel,
        grid=(b,),
        in_specs=[pl.BlockSpec((1, s, KV_LORA), tok),
                  pl.BlockSpec((1, s, D_ROPE), tok),
                  pl.BlockSpec((KV_LORA, N_HEADS * (D_NOPE + D_V)), lambda bi: (0, 0))],
        out_specs=[pl.BlockSpec((1, s, N_HEADS * HEAD_W), tok),
                   pl.BlockSpec((1, s, N_HEADS * D_V), tok)],
        out_shape=[jax.ShapeDtypeStruct((b, s, N_HEADS * HEAD_W), BF16),
                   jax.ShapeDtypeStruct((b, s, N_HEADS * D_V), BF16)],
        compiler_params=_cparams("arbitrary"),
        name="cache_kv",
    )(ckv, krope, wukv)


def _post_kernel(x_ref, conv_ref, att_ref, mod_ref, wo_ref, gpost_ref, gpre2_ref, wr_hi_ref, wr_cat_ref,
                 br_ref, x1_ref, h2_ref, ri_ref, rw_ref, cnt_ref, wo_b):
    @pl.when((pl.program_id(0) == 0) & (pl.program_id(1) == 0))
    def _():
        wo_b[...] = wo_ref[...].astype(BF16)

    out = _dot(conv_ref[0], wo_b[:C_CONV, :]) + _dot(att_ref[0], wo_b[C_CONV:, :])
    gt1 = mod_ref[0, 2:3, :]
    sh2 = mod_ref[0, 3:4, :]
    sc2 = mod_ref[0, 4:5, :]
    x1 = x_ref[0] + gt1 * _rms(out, gpost_ref[...])
    x1_ref[0] = x1
    h2 = _rms(x1, gpre2_ref[...]) * (1.0 + sc2) + sh2
    h_hi = h2.astype(BF16)
    h2_ref[0] = h_hi
    h_lo = (h2 - h_hi.astype(F32)).astype(BF16)
    hi_terms = _dot(h_hi, wr_cat_ref[...])
    logits = hi_terms + pltpu.roll(hi_terms, LANE // 2, axis=1) + _dot(h_lo, wr_hi_ref[...]) + br_ref[...]

    lane = lax.broadcasted_iota(jnp.int32, logits.shape, 1)
    neg = jnp.float32(-jnp.inf)
    big = jnp.int32(LANE)
    is_g = (lane >= N_EXPERTS) & (lane < N_EXPERTS + N_GROUPS)
    lg = jnp.where(is_g, logits, neg)
    gmax = jnp.max(lg, axis=-1, keepdims=True)
    gidx = jnp.min(jnp.where(lg == gmax, lane, big), axis=-1, keepdims=True) - N_EXPERTS
    g_top = 1.0 / jnp.sum(jnp.exp(lg - gmax), axis=-1, keepdims=True)

    in_grp = (lane >= gidx * E_PER_GROUP) & (lane < (gidx + 1) * E_PER_GROUP)
    le = jnp.where(in_grp, logits, neg)
    m1 = jnp.max(le, axis=-1, keepdims=True)
    i1 = jnp.min(jnp.where(le == m1, lane, big), axis=-1, keepdims=True)
    le2 = jnp.where(lane == i1, neg, le)
    m2 = jnp.max(le2, axis=-1, keepdims=True)
    i2 = jnp.min(jnp.where(le2 == m2, lane, big), axis=-1, keepdims=True)
    r = jnp.exp(m2 - m1)
    w1 = g_top / (1.0 + r)
    w2 = g_top * r / (1.0 + r)
    ri_ref[0] = jnp.where(lane == 0, i1, jnp.where(lane == 1, i2, 0))
    rw_ref[0] = jnp.where(lane == 0, w1, jnp.where(lane == 1, w2, 0.0))
    pairs = jnp.sum(jnp.where((lane == i1) | (lane == i2), 1.0, 0.0), axis=0, keepdims=True)
    cnt_ref[0, 0] = jnp.broadcast_to(pairs, (SUBLANE, LANE))


def _post(x, conv_out, att, mod, mod_row0, w_out, g_post1, g_pre2, wr_hi, wr_cat, br):
    b, t, _ = x.shape
    tm = MOE_TD
    const = lambda bi, i: (0, 0)
    tok = lambda bi, i: (bi, i, 0)
    return pl.pallas_call(
        _post_kernel,
        grid=(b, t // tm),
        in_specs=[pl.BlockSpec((1, tm, D_MODEL), tok),
                  pl.BlockSpec((1, tm, C_CONV), tok),
                  pl.BlockSpec((1, tm, N_HEADS * D_V), tok),
                  pl.BlockSpec((1, 6, D_MODEL), lambda bi, i: (mod_row0 + bi, 0, 0)),
                  pl.BlockSpec((D_MODEL, D_MODEL), const),
                  pl.BlockSpec((1, D_MODEL), const),
                  pl.BlockSpec((1, D_MODEL), const),
                  pl.BlockSpec((D_MODEL, LANE), const),
                  pl.BlockSpec((D_MODEL, LANE), const),
                  pl.BlockSpec((1, LANE), const)],
        out_specs=[pl.BlockSpec((1, tm, D_MODEL), tok),
                   pl.BlockSpec((1, tm, D_MODEL), tok),
                   pl.BlockSpec((1, tm, LANE), tok),
                   pl.BlockSpec((1, tm, LANE), tok),
                   pl.BlockSpec((1, 1, SUBLANE, LANE), lambda bi, i: (bi, i, 0, 0))],
        out_shape=[jax.ShapeDtypeStruct((b, t, D_MODEL), F32),
                   jax.ShapeDtypeStruct((b, t, D_MODEL), BF16),
                   jax.ShapeDtypeStruct((b, t, LANE), jnp.int32),
                   jax.ShapeDtypeStruct((b, t, LANE), F32),
                   jax.ShapeDtypeStruct((b, t // tm, SUBLANE, LANE), F32)],
        scratch_shapes=[pltpu.VMEM((D_MODEL, D_MODEL), BF16)],
        compiler_params=_cparams("arbitrary", "arbitrary"),
        name="post",
    )(x, conv_out, att, mod, w_out, g_post1, g_pre2, wr_hi, wr_cat, br)


def _one_hots(ri):
    lane = lax.broadcasted_iota(jnp.int32, ri.shape, 1)
    oh1 = lane == ri[:, 0:1]
    oh2 = lane == ri[:, 1:2]
    return oh1, oh2, jnp.where(oh1 | oh2, 1.0, 0.0)


def _strictly_lower(n):
    r = lax.broadcasted_iota(jnp.int32, (n, n), 0)
    c = lax.broadcasted_iota(jnp.int32, (n, n), 1)
    return jnp.where(c < r, 1.0, 0.0).astype(BF16)


def _local_positions(oh1, oh2, oh, earlier_ref):
    cnt = jnp.sum(oh, axis=0, keepdims=True)
    nch = jnp.floor((cnt + (CHUNK - 1)) * (1.0 / CHUNK))
    a = lax.broadcasted_iota(jnp.int32, (LANE, LANE), 0)
    b = lax.broadcasted_iota(jnp.int32, (LANE, LANE), 1)
    lower_experts = jnp.where(a < b, 1.0, 0.0).astype(BF16)
    run_start = _dot(jnp.broadcast_to(nch, (SUBLANE, LANE)).astype(BF16), lower_experts)[0:1, :] * CHUNK
    pos = _dot(earlier_ref[...], oh.astype(BF16)) + run_start
    lp1 = jnp.sum(jnp.where(oh1, pos, 0.0), axis=-1, keepdims=True)
    lp2 = jnp.sum(jnp.where(oh2, pos, 0.0), axis=-1, keepdims=True)
    return lp1, lp2


def _plan(cnt, n_tiles_max):
    k = (cnt + (CHUNK - 1)) // CHUNK
    run_end = jnp.cumsum(k, axis=1)
    run_start = run_end - k
    n_chunks = run_end[:, -1]
    total = jnp.sum(k, axis=0)
    padded = (total + (TILE_CH - 1)) // TILE_CH * TILE_CH
    seg_end = jnp.cumsum(padded)
    seg_start = seg_end - padded
    base = seg_start[None, :] + jnp.cumsum(k, axis=0) - k
    c = jnp.arange(MOE_NCH, dtype=jnp.int32)[None, :, None]
    in_run = (run_start[:, None, :] <= c) & (c < run_end[:, None, :])
    dst = jnp.sum(jnp.where(in_run, base[:, None, :] + c - run_start[:, None, :], 0), axis=2)
    n_tiles = seg_end[-1] // TILE_CH
    tail_start = seg_start + total
    tail_len = padded - total
    i32 = lambda t: t.astype(jnp.int32)
    return (i32(dst), i32(n_chunks), i32(tail_start), i32(tail_len), i32(seg_start // TILE_CH),
            i32(seg_end // TILE_CH), i32(n_tiles.reshape(1)))


def _dispatch_kernel(steps_a, dst_ref, nch_ref, ts_ref, tl_ref, nt_ref, h2a_ref, ria_ref, rwa_ref, h2b_ref, rib_ref,
                     rwb_ref, xs_hbm, lp_ref, xbuf, zx, sem, zsem, earlier):
    i = pl.program_id(0)
    n = pl.num_programs(0)
    slot = i % 2
    td = MOE_TD
    n_tiles_max = xs_hbm.shape[0] // MOE_TILE

    def zero_tile(m):
        rows = pl.ds(pl.multiple_of(m * MOE_TILE, MOE_TILE), MOE_TILE)
        return pltpu.make_async_copy(zx, xs_hbm.at[rows, :], zsem.at[1])

    @pl.when(i == 0)
    def _():
        zx[...] = jnp.zeros_like(zx)
        earlier[...] = _strictly_lower(MOE_TD)
        for e in range(N_EXPERTS):
            def fill(m, carry, e=e):
                _chunk_copy(zx, 0, xs_hbm, ts_ref[e] + m, zsem.at[0]).start()
                return carry
            lax.fori_loop(0, tl_ref[e], fill, 0)

        def fill_tile(m, carry):
            zero_tile(m).start()
            return carry
        lax.fori_loop(nt_ref[0], n_tiles_max, fill_tile, 0)

    from_a = i < steps_a
    ri = jnp.where(from_a, ria_ref[...], rib_ref[...])
    rw = jnp.where(from_a, rwa_ref[...], rwb_ref[...])
    h2 = jnp.where(from_a, h2a_ref[...], h2b_ref[...])
    oh1, oh2, oh = _one_hots(ri)
    lp1, lp2 = _local_positions(oh1, oh2, oh, earlier)
    lane = lax.broadcasted_iota(jnp.int32, ri.shape, 1)
    lp_ref[...] = jnp.where(lane == 0, lp1, jnp.where(lane == 1, lp2, 0.0)).astype(jnp.int32)

    cols = jnp.where(lane == 0, lp1, jnp.where(lane == 1, lp2, jnp.where(lane == 2, rw[:, 0:1],
                                                                         jnp.where(lane == 3, rw[:, 1:2], 0.0))))
    rows4 = cols.T
    row = lax.broadcasted_iota(jnp.int32, (MOE_L, td), 0).astype(F32)
    p1 = row == rows4[0:1, :]
    p2 = row == rows4[1:2, :]
    perm = jnp.where(p1 | p2, 1.0, 0.0).astype(BF16)
    xbuf[slot, :, 0:D_MODEL] = _dot(perm, h2).astype(BF16)
    w = jnp.sum(jnp.where(p1, rows4[2:3, :], 0.0) + jnp.where(p2, rows4[3:4, :], 0.0), axis=-1, keepdims=True)
    w0 = w.astype(BF16).astype(F32)
    w1 = (w - w0).astype(BF16).astype(F32)
    w2 = (w - w0 - w1).astype(BF16).astype(F32)
    wl = lax.broadcasted_iota(jnp.int32, (MOE_L, LANE), 1)
    terms = jnp.where(wl == 0, w0, jnp.where(wl == 1, w1, jnp.where(wl == 2, w2, 0.0)))
    xbuf[slot, :, D_MODEL:XS_W] = terms.astype(BF16)

    def wait_chunks(count, s):
        def body(c, carry):
            _chunk_copy(xbuf.at[s], 0, xs_hbm, 0, sem.at[s]).wait()
            return carry
        lax.fori_loop(0, count, body, 0)

    @pl.when(i > 0)
    def _():
        wait_chunks(nch_ref[i - 1], 1 - slot)

    def send(c, carry):
        _chunk_copy(xbuf.at[slot], c, xs_hbm, dst_ref[i, c], sem.at[slot]).start()
        return carry
    lax.fori_loop(0, nch_ref[i], send, 0)

    @pl.when(i == n - 1)
    def _():
        wait_chunks(nch_ref[i], slot)
        for e in range(N_EXPERTS):
            def drain(m, carry):
                _chunk_copy(zx, 0, xs_hbm, 0, zsem.at[0]).wait()
                return carry
            lax.fori_loop(0, tl_ref[e], drain, 0)

        def drain_tile(m, carry):
            zero_tile(m).wait()
            return carry
        lax.fori_loop(nt_ref[0], n_tiles_max, drain_tile, 0)


def _dispatch(dst, n_chunks, tail_start, tail_len, n_tiles, part_a, part_b, n_tiles_max):
    steps_a = part_a[0].shape[0] // MOE_TD
    steps_b = part_b[0].shape[0] // MOE_TD
    in_a = lambda i, *_: (jnp.minimum(i, steps_a - 1), 0)
    in_b = lambda i, *_: (jnp.maximum(i - steps_a, 0), 0)
    specs = lambda f: [pl.BlockSpec((MOE_TD, D_MODEL), f), pl.BlockSpec((MOE_TD, LANE), f),
                       pl.BlockSpec((MOE_TD, LANE), f)]
    rows = n_tiles_max * MOE_TILE
    return pl.pallas_call(
        functools.partial(_dispatch_kernel, steps_a),
        grid_spec=pltpu.PrefetchScalarGridSpec(
            num_scalar_prefetch=5,
            grid=(steps_a + steps_b,),
            in_specs=specs(in_a) + specs(in_b),
            out_specs=[pl.BlockSpec(memory_space=pl.ANY),
                       pl.BlockSpec((MOE_TD, LANE), lambda i, *_: (i, 0))],
            scratch_shapes=[pltpu.VMEM((2, MOE_L, XS_W), BF16), pltpu.VMEM((MOE_TILE, XS_W), BF16),
                            pltpu.SemaphoreType.DMA((2,)), pltpu.SemaphoreType.DMA((2,)),
                            pltpu.VMEM((MOE_TD, MOE_TD), BF16)]),
        out_shape=[jax.ShapeDtypeStruct((rows, XS_W), BF16),
                   jax.ShapeDtypeStruct(((steps_a + steps_b) * MOE_TD, LANE), jnp.int32)],
        compiler_params=_cparams("arbitrary"),
        name="moe_dispatch",
    )(dst, n_chunks, tail_start, tail_len, n_tiles, *part_a, *part_b)


def _moe_gemm_kernel(t0_ref, t1_ref, nt_ref, wg_ref, wu_ref, wd_ref, xs_hbm, y_hbm,
                     wg_b, wu_b, wd_b, xbuf, ybuf, zbuf, isem, osem, zsem, issued):
    e = pl.program_id(0)
    nt = nt_ref[0]
    n_tiles_max = y_hbm.shape[0] // MOE_TILE
    tile_rows = lambda t: pl.ds(pl.multiple_of(t * MOE_TILE, MOE_TILE), MOE_TILE)

    def load(t, s):
        return (pltpu.make_async_copy(xs_hbm.at[tile_rows(t), :], xbuf.at[s], isem.at[s]),)

    def store(t, s):
        return pltpu.make_async_copy(ybuf.at[s], y_hbm.at[tile_rows(t), :], osem.at[s])

    def zero_tile(t):
        return pltpu.make_async_copy(zbuf, y_hbm.at[tile_rows(t), :], zsem.at[0])

    @pl.when(e == 0)
    def _():
        issued[0] = 0
        zbuf[...] = jnp.zeros_like(zbuf)

        def fill(t, carry):
            zero_tile(t).start()
            return carry
        lax.fori_loop(nt, n_tiles_max, fill, 0)

    wg_b[...] = wg_ref[0].astype(BF16)
    wu_b[...] = wu_ref[0].astype(BF16)
    wd_b[...] = wd_ref[0].astype(BF16)

    def top_up(t):
        upto = jnp.minimum(t + GEMM_RING, nt)

        def start(u, carry):
            for cp in load(u, u % GEMM_RING):
                cp.start()
            return carry
        lax.fori_loop(issued[0], upto, start, 0)
        issued[0] = jnp.maximum(issued[0], upto)

    def process(t, n):
        top_up(t)
        for k in range(n):
            for cp in load(t + k, (t + k) % GEMM_RING):
                cp.wait()
        rows = [xbuf[(t + k) % GEMM_RING] for k in range(n)]
        xw = rows[0] if n == 1 else jnp.concatenate(rows, axis=0)
        x = xw[:, 0:D_MODEL]
        wt = xw[:, D_MODEL:XS_W].astype(F32)
        w = wt[:, 0:1] + wt[:, 1:2] + wt[:, 2:3]
        a = _dot(x, wg_b[...])
        u = _dot(x, wu_b[...])
        he = (a * _sigmoid(a)) * u
        y = (_dot(he.astype(BF16), wd_b[...]) * w).astype(BF16)
        for k in range(n):
            s = (t + k) % GEMM_OUT

            @pl.when(t + k >= GEMM_OUT)
            def _(k=k, s=s):
                store(t + k - GEMM_OUT, s).wait()
            ybuf[s] = y[k * MOE_TILE:(k + 1) * MOE_TILE, :]
            store(t + k, s).start()

    t0 = t0_ref[e]
    t1 = t1_ref[e]

    def pair(p, carry):
        process(t0 + 2 * p, 2)
        return carry
    lax.fori_loop(0, (t1 - t0) // 2, pair, 0)

    @pl.when((t1 - t0) % 2 == 1)
    def _():
        process(t1 - 1, 1)

    @pl.when(e == pl.num_programs(0) - 1)
    def _():
        for k in range(1, GEMM_OUT + 1):
            @pl.when(nt >= k)
            def _(k=k):
                store(nt - k, (nt - k) % GEMM_OUT).wait()

        def drain(t, carry):
            zero_tile(t).wait()
            return carry
        lax.fori_loop(nt, n_tiles_max, drain, 0)


def _moe_gemm(tile0, tile1, n_tiles, xs, w_gate, w_up, w_down):
    wmap = lambda e, *_: (e, 0, 0)
    return pl.pallas_call(
        _moe_gemm_kernel,
        grid_spec=pltpu.PrefetchScalarGridSpec(
            num_scalar_prefetch=3,
            grid=(N_EXPERTS,),
            in_specs=[pl.BlockSpec((1, D_MODEL, D_EXPERT), wmap),
                      pl.BlockSpec((1, D_MODEL, D_EXPERT), wmap),
                      pl.BlockSpec((1, D_EXPERT, D_MODEL), wmap),
                      pl.BlockSpec(memory_space=pl.ANY)],
            out_specs=pl.BlockSpec(memory_space=pl.ANY),
            scratch_shapes=[pltpu.VMEM((D_MODEL, D_EXPERT), BF16), pltpu.VMEM((D_MODEL, D_EXPERT), BF16),
                            pltpu.VMEM((D_EXPERT, D_MODEL), BF16),
                            pltpu.VMEM((GEMM_RING, MOE_TILE, XS_W), BF16),
                            pltpu.VMEM((GEMM_OUT, MOE_TILE, D_MODEL), BF16), pltpu.VMEM((MOE_TILE, D_MODEL), BF16),
                            pltpu.SemaphoreType.DMA((GEMM_RING,)), pltpu.SemaphoreType.DMA((GEMM_OUT,)),
                            pltpu.SemaphoreType.DMA((1,)), pltpu.SMEM((1,), jnp.int32)]),
        out_shape=jax.ShapeDtypeStruct((xs.shape[0], D_MODEL), BF16),
        compiler_params=_cparams("arbitrary"),
        name="moe_gemm",
    )(tile0, tile1, n_tiles, w_gate, w_up, w_down, xs)


def _moe_combine_kernel(step0, dst_ref, nch_ref, lp_ref, x1_ref, mod_ref, gpost_ref, y_hbm, o_ref, ybuf, sem):
    i = pl.program_id(0)
    n = pl.num_programs(0)
    slot = i % 2

    def fetch(step, s):
        def body(c, carry):
            _chunk_copy(y_hbm, dst_ref[step, c], ybuf.at[s], c, sem.at[s]).start()
            return carry
        lax.fori_loop(0, nch_ref[step], body, 0)

    @pl.when(i == 0)
    def _():
        ybuf[...] = jnp.zeros_like(ybuf)
        fetch(step0, 0)

    @pl.when(i + 1 < n)
    def _():
        fetch(step0 + i + 1, 1 - slot)

    def wait(c, carry):
        _chunk_copy(y_hbm, 0, ybuf.at[slot], 0, sem.at[slot]).wait()
        return carry
    lax.fori_loop(0, nch_ref[step0 + i], wait, 0)

    lp = lp_ref[...]
    col = lax.broadcasted_iota(jnp.int32, (MOE_TD, MOE_L), 1)
    unperm = jnp.where((col == lp[:, 0:1]) | (col == lp[:, 1:2]), 1.0, 0.0).astype(BF16)
    moe = _dot(unperm, ybuf[slot])
    gt2 = mod_ref[0, 5:6, :]
    o_ref[...] = x1_ref[...] + gt2 * _rms(moe, gpost_ref[...])


def _moe_combine(step0, dst, n_chunks, lp, x1, mod, mod_row, y_sorted, g_post2):
    n = x1.shape[0]
    tok = lambda i, *_: (i, 0)
    return pl.pallas_call(
        functools.partial(_moe_combine_kernel, step0),
        grid_spec=pltpu.PrefetchScalarGridSpec(
            num_scalar_prefetch=2,
            grid=(n // MOE_TD,),
            in_specs=[pl.BlockSpec((MOE_TD, LANE), lambda i, *_: (step0 + i, 0)),
                      pl.BlockSpec((MOE_TD, D_MODEL), tok),
                      pl.BlockSpec((1, 6, D_MODEL), lambda i, *_: (mod_row(i), 0, 0)),
                      pl.BlockSpec((1, D_MODEL), lambda i, *_: (0, 0)),
                      pl.BlockSpec(memory_space=pl.ANY)],
            out_specs=pl.BlockSpec((MOE_TD, D_MODEL), tok),
            scratch_shapes=[pltpu.VMEM((2, MOE_L, D_MODEL), BF16), pltpu.SemaphoreType.DMA((2,))]),
        out_shape=jax.ShapeDtypeStruct((n, D_MODEL), F32),
        compiler_params=_cparams("arbitrary"),
        name="moe_combine",
    )(dst, n_chunks, lp, x1, mod, g_post2, y_sorted)


def _moe(part_a, part_b, mod, mod_row_a, mod_row_b, w_gate, w_up, w_down, g_post2):
    cnt = jnp.concatenate([part_a[4], part_b[4]], axis=0)[:, 0, :N_EXPERTS].astype(jnp.int32)
    steps_a = part_a[4].shape[0]
    n_tiles_max = (cnt.shape[0] * MOE_NCH + N_EXPERTS * (TILE_CH - 1)) // TILE_CH + 1
    dst, n_chunks, tail_start, tail_len, tile0, tile1, n_tiles = _plan(cnt, n_tiles_max)
    xs, lp = _dispatch(dst, n_chunks, tail_start, tail_len, n_tiles, part_a[1:4], part_b[1:4], n_tiles_max)
    y_sorted = _moe_gemm(tile0, tile1, n_tiles, xs, w_gate, w_up, w_down)
    ya = _moe_combine(0, dst, n_chunks, lp, part_a[0], mod, mod_row_a, y_sorted, g_post2)
    yb = _moe_combine(steps_a, dst, n_chunks, lp, part_b[0], mod, mod_row_b, y_sorted, g_post2)
    return ya, yb


def _rotate_half_cols(w):
    n = w.shape[-1]
    w4 = w.reshape(w.shape[:-1] + (n // 32, 2, 16))
    return jnp.stack([-w4[..., 1, :], w4[..., 0, :]], axis=-2).reshape(w.shape)


def _pad_lanes(w):
    return jnp.concatenate([w, jnp.zeros(w.shape[:-1] + (LANE - w.shape[-1],), w.dtype)], axis=-1)


def _rope_tables(t):
    rows = t // GRID_W
    n_freq = D_ROPE // 4
    f32 = np.float32
    freqs = f32(ROPE_BASE) ** (-np.arange(n_freq, dtype=f32) / f32(n_freq))
    ang_r = np.arange(rows, dtype=f32)[:, None] * freqs
    ang_c = np.arange(GRID_W, dtype=f32)[:, None] * freqs

    def table(fn):
        r = np.repeat(fn(ang_r).astype(f32), GRID_W, axis=0)
        c = np.tile(fn(ang_c).astype(f32), (rows, 1))
        pad = np.zeros((t, LANE - D_ROPE), f32)
        return jnp.asarray(np.concatenate([r, r, c, c, pad], axis=-1))
    return table(np.cos), table(np.sin)


def kernel(x_prompt, x_sample, cache_ckv, cache_krope, c, c_ctx, w_ada, b_ada, g_pre1, g_post1, g_pre2, g_post2, w_in, w_dw, b_dw, conv_ln_g, conv_ln_b, q_norm_g, kv_norm_g, w_uq, w_ukv, w_out, w_rg, b_rg, w_re, b_re, w_gate, w_up, w_down):
    nb, seq, d = x_prompt.shape
    db, dseq, _ = x_sample.shape
    l = 0

    cvec = jnp.concatenate([c_ctx[None, :], c, jnp.zeros((8 - 1 - db, d), F32)], axis=0)
    mod = _ada(cvec, w_ada[l], b_ada[l]).reshape(8, 6, d)

    w_in_t = w_in[l].T
    wuq = w_uq[l].reshape(Q_LORA, N_HEADS, D_NOPE + D_ROPE)
    wq_rope = wuq[:, :, D_NOPE:]
    wq = jnp.concatenate([wuq[:, :, :D_NOPE], _pad_lanes(wq_rope)], axis=-1)
    wq = wq.reshape(Q_LORA, N_HEADS * HEAD_W).astype(BF16)
    wqs = _pad_lanes(_rotate_half_cols(wq_rope)).reshape(Q_LORA, N_HEADS * LANE).astype(BF16)
    wukv4 = w_ukv[l].reshape(KV_LORA, N_HEADS, D_NOPE + D_V)
    wukv = jnp.concatenate([wukv4[:, :, :D_NOPE].reshape(KV_LORA, N_HEADS * D_NOPE),
                            wukv4[:, :, D_NOPE:].reshape(KV_LORA, N_HEADS * D_V)], axis=-1).astype(BF16)
    w_out_b = w_out[l]
    w_r = jnp.concatenate([w_re[l], w_rg[l], jnp.zeros((d, LANE - N_EXPERTS - N_GROUPS), F32)], axis=-1)
    wr_hi = w_r.astype(BF16)
    wr_lo = (w_r - wr_hi.astype(F32)).astype(BF16)
    wr_cat = jnp.concatenate([wr_hi[:, :LANE // 2], wr_lo[:, :LANE // 2]], axis=-1)
    b_r = jnp.concatenate([b_re[l], b_rg[l], jnp.zeros((LANE - N_EXPERTS - N_GROUPS,), F32)])[None, :]
    cos, sin = _rope_tables(dseq)
    row = lambda v: v[l][None, :]

    tm = MOE_TD
    xp_flat = x_prompt.reshape(1, nb * seq, d)
    conv_w = (w_dw[l], row(b_dw), row(conv_ln_g), row(conv_ln_b))
    assert seq == CONV_TB, "a context step must hold whole sequences of CONV_TB tokens"
    conv_out, q, k, v, ckv, kr = _mix_in(xp_flat, mod, 0, row(g_pre1), w_in_t, row(q_norm_g), row(kv_norm_g),
                                         wq, wqs, wukv, cos[:tm], sin[:tm], conv_w, False, tm)
    per_seq = lambda a: a.reshape(nb, seq, a.shape[-1])
    att = _attn(per_seq(q), per_seq(k), per_seq(v), n_seq=ATT_SEQS)
    flat = lambda a: a.reshape(1, nb * seq, a.shape[-1])
    post_c = _post(xp_flat, conv_out, flat(att), mod, 0, w_out_b, row(g_post1), row(g_pre2),
                   wr_hi, wr_cat, b_r)
    state_ckv = ckv.reshape(nb, 1, seq, KV_LORA)
    state_krope = kr.reshape(nb, 1, seq, D_ROPE)

    kc, vc = _cache_kv(cache_ckv[:, l], cache_krope[:, l], wukv)
    conv_out, q, k, v, _, _ = _mix_in(x_sample, mod, 1, row(g_pre1), w_in_t, row(q_norm_g), row(kv_norm_g),
                                      wq, wqs, wukv, cos, sin, conv_w, True, tm)
    att = _attn(q, k, v, kc, vc)
    post_s = _post(x_sample, conv_out, att, mod, 1, w_out_b, row(g_post1), row(g_pre2), wr_hi, wr_cat, b_r)

    tokens = lambda parts: tuple(a.reshape((-1,) + a.shape[2:]) for a in parts)
    steps_per_req = dseq // MOE_TD
    yp, ys = _moe(tokens(post_c), tokens(post_s), mod, lambda i: 0, lambda i: 1 + i // steps_per_req,
                  w_gate[l], w_up[l], w_down[l], row(g_post2))

    return (yp.reshape(nb, seq, d), ys.reshape(db, dseq, d), state_ckv, state_krope)
```

```python
import functools

import jax
import jax.numpy as jnp
import numpy as np
from jax import lax
from jax.experimental import pallas as pl
from jax.experimental.pallas import tpu as pltpu

D_MODEL = 1024
GRID_W = 64
C_CONV = 512
CONV_K = 31
N_HEADS = 4
D_NOPE = 128
D_ROPE = 64
D_V = 128
Q_LORA = 384
KV_LORA = 256
N_GROUPS = 4
E_PER_GROUP = 8
N_EXPERTS = 32
D_EXPERT = 256
ROPE_BASE = 10000.0
EPS = 1e-6
ATT_SCALE = (D_NOPE + D_ROPE) ** -0.5
LOG2E = 1.4426950408889634

LANE = 128
SUBLANE = 8
HEAD_W = 2 * LANE
CONV_HALO = 16
ADA_TN = 512
ATT_TQ = 256
ATT_SEQS = 8
MOE_TD = 512
CHUNK = 2 * SUBLANE
MOE_NCH = (2 * MOE_TD + N_EXPERTS * (CHUNK - 1)) // CHUNK
MOE_L = MOE_NCH * CHUNK
MOE_TILE = 128
TILE_CH = MOE_TILE // CHUNK
GEMM_BATCH = 4
GEMM_RING = 12
GEMM_OUT = 8
XS_W = D_MODEL + LANE
VMEM_LIMIT = 56 * 1024 * 1024

BF16 = jnp.bfloat16
F32 = jnp.float32


def _cparams(*sem):
    return pltpu.CompilerParams(dimension_semantics=sem, vmem_limit_bytes=VMEM_LIMIT)


def _rms(x, g):
    return x * lax.rsqrt(jnp.mean(x * x, axis=-1, keepdims=True) + EPS) * g


def _sigmoid(x):
    return 1.0 / (1.0 + jnp.exp(-x))


def _dot(a, b):
    return jnp.dot(a, b, preferred_element_type=F32)


def _dot_t(a, b_t):
    return lax.dot_general(a, b_t, (((1,), (1,)), ((), ())), preferred_element_type=F32)


def _chunk_copy(src, src_chunk, dst, dst_chunk, sem):
    rows = lambda c: pl.ds(c * CHUNK if isinstance(c, int) else pl.multiple_of(c * CHUNK, CHUNK), CHUNK)
    return pltpu.make_async_copy(src.at[rows(src_chunk), :], dst.at[rows(dst_chunk), :], sem)


def _ada_kernel(c_ref, w_ref, b_ref, o_ref):
    c = c_ref[...]
    s = (c * _sigmoid(c)).astype(BF16)
    o_ref[...] = _dot(s, w_ref[...].astype(BF16)) + b_ref[...]


def _ada(cvec, w_ada, b_ada):
    n = w_ada.shape[1]
    tn = ADA_TN
    return pl.pallas_call(
        _ada_kernel,
        grid=(n // tn,),
        in_specs=[pl.BlockSpec((8, D_MODEL), lambda j: (0, 0)),
                  pl.BlockSpec((D_MODEL, tn), lambda j: (0, j)),
                  pl.BlockSpec((1, tn), lambda j: (0, j))],
        out_specs=pl.BlockSpec((8, tn), lambda j: (0, j)),
        out_shape=jax.ShapeDtypeStruct((8, n), F32),
        compiler_params=_cparams("arbitrary"),
        name="ada",
    )(cvec, w_ada, b_ada.reshape(1, n))


CONV_TB = 256
CONV_WIN = CONV_TB + 2 * CONV_HALO
CONV_TT = 32


def _conv_window(win_ref, sh_ref, w_ref, b_ref, g_ref, bb_ref, o_ref, row0):
    first = CONV_HALO - CONV_K // 2
    rows = CONV_WIN - SUBLANE
    win = win_ref[...]
    for s in range(SUBLANE):
        sh_ref[s, 0:rows, :] = win[s:s + rows, :]
    for c in range(CONV_TB // CONV_TT):
        acc = jnp.zeros((CONV_TT, C_CONV), F32)
        for k in range(CONV_K):
            off = first + k
            r0 = c * CONV_TT + off // SUBLANE * SUBLANE
            acc = acc + sh_ref[off % SUBLANE, r0:r0 + CONV_TT, :] * w_ref[k:k + 1, :]
        y = acc + b_ref[...]
        mu = jnp.mean(y, axis=-1, keepdims=True)
        yc = y - mu
        var = jnp.mean(yc * yc, axis=-1, keepdims=True)
        z = yc * lax.rsqrt(var + EPS) * g_ref[...] + bb_ref[...]
        o_ref[0, row0 + c * CONV_TT:row0 + (c + 1) * CONV_TT, :] = (z * _sigmoid(z)).astype(BF16)


def _mix_in_kernel(latent, x_ref, xp_ref, xn_ref, mod_ref, gpre_ref, win_ref, qg_ref, kvg_ref, wq_ref, wqs_ref,
                   wukv_ref, cos_ref, sin_ref, wdw_ref, bdw_ref, lng_ref, lnb_ref,
                   conv_ref, q_ref, k_ref, v_ref, ckv_ref, kr_ref, cwin, csh, wbf):
    rope = latent

    @pl.when((pl.program_id(0) == 0) & (pl.program_id(1) == 0))
    def _():
        wbf[...] = win_ref[...].astype(BF16)

    x = x_ref[0]
    tm = x.shape[0]
    sh1 = mod_ref[0, 0:1, :]
    sc1 = mod_ref[0, 1:2, :]
    modulate = lambda v: _rms(v, gpre_ref[...]) * (1.0 + sc1) + sh1
    glu = lambda uc: uc[:, :C_CONV] * _sigmoid(uc[:, C_CONV:2 * C_CONV])
    u = _dot_t(modulate(x).astype(BF16), wbf[...])
    hglu = glu(u)

    zeros = jnp.zeros((CONV_HALO, C_CONV), F32)
    n_win = tm // CONV_TB
    if latent:
        i = pl.program_id(1)
        xh = jnp.concatenate([xp_ref[0], xn_ref[0]], axis=0)
        hh = glu(_dot_t(modulate(xh).astype(BF16), wbf[0:2 * C_CONV, :]))
        before = jnp.where(i > 0, hh[0:CONV_HALO], zeros)
        after = jnp.where(i < pl.num_programs(1) - 1, hh[CONV_HALO:2 * CONV_HALO], zeros)
    for j in range(n_win):
        if latent:
            lo, hi, d0 = j * CONV_TB - CONV_HALO, j * CONV_TB + CONV_TB + CONV_HALO, 0
            if lo < 0:
                cwin[j, 0:CONV_HALO, :] = before
                lo, d0 = 0, CONV_HALO
            if hi > tm:
                cwin[j, CONV_WIN - CONV_HALO:CONV_WIN, :] = after
                hi = tm
            cwin[j, d0:d0 + hi - lo, :] = hglu[lo:hi, :]
        else:
            cwin[j, 0:CONV_HALO, :] = zeros
            cwin[j, CONV_HALO:CONV_HALO + CONV_TB, :] = hglu[j * CONV_TB:(j + 1) * CONV_TB, :]
            cwin[j, CONV_HALO + CONV_TB:CONV_WIN, :] = zeros
    for j in range(n_win):
        _conv_window(cwin.at[j], csh, wdw_ref, bdw_ref, lng_ref, lnb_ref, conv_ref, j * CONV_TB)

    o_q = 2 * C_CONV
    o_kv = o_q + Q_LORA
    o_kr = o_kv + KV_LORA
    qn = _rms(u[:, o_q:o_kv], qg_ref[...]).astype(BF16)
    qf = _dot(qn, wq_ref[...])
    ckv = _rms(u[:, o_kv:o_kr], kvg_ref[...])
    ckv_ref[0] = ckv
    kvd = _dot(ckv.astype(BF16), wukv_ref[...])
    kr64 = u[:, o_kr:o_kr + D_ROPE]
    kr_ref[0] = kr64
    kr = jnp.concatenate([kr64, jnp.zeros((tm, LANE - D_ROPE), F32)], axis=-1)
    if rope:
        cos = cos_ref[...]
        sin = sin_ref[...]
        qs = _dot(qn, wqs_ref[...])
        lane = lax.broadcasted_iota(jnp.int32, kr.shape, 1)
        partner = jnp.where(lane % (D_ROPE // 2) < D_ROPE // 4,
                            -pltpu.roll(kr, LANE - D_ROPE // 4, axis=1), pltpu.roll(kr, D_ROPE // 4, axis=1))
        kr = kr * cos + partner * sin
    q_parts = []
    k_parts = []
    for hd in range(N_HEADS):
        q_parts.append(qf[:, hd * HEAD_W:hd * HEAD_W + LANE])
        qr = qf[:, hd * HEAD_W + LANE:(hd + 1) * HEAD_W]
        if rope:
            qr = qr * cos + qs[:, hd * LANE:(hd + 1) * LANE] * sin
        q_parts.append(qr)
        k_parts.append(kvd[:, hd * D_NOPE:(hd + 1) * D_NOPE])
        k_parts.append(kr)
    q_ref[0] = jnp.concatenate(q_parts, axis=-1).astype(BF16)
    k_ref[0] = jnp.concatenate(k_parts, axis=-1).astype(BF16)
    v_ref[0] = kvd[:, N_HEADS * D_NOPE:].astype(BF16)


def _mix_in(x, mod, mod_row0, g_pre1, w_in_ext, q_norm_g, kv_norm_g, wq, wqs, wukv, cos, sin, conv_w, latent, tm):
    b, t, _ = x.shape
    ncol = w_in_ext.shape[0]
    const = lambda bi, i: (0, 0)
    tok = lambda bi, i: (bi, i, 0)
    table = (lambda bi, i: (i, 0)) if latent else const
    halo_blocks = tm // CONV_HALO
    prev_rows = lambda bi, i: (bi, jnp.maximum(i * halo_blocks - 1, 0), 0)
    next_rows = lambda bi, i: (bi, jnp.minimum((i + 1) * halo_blocks, t // CONV_HALO - 1), 0)
    outs = [(C_CONV, BF16), (N_HEADS * HEAD_W, BF16), (N_HEADS * HEAD_W, BF16), (N_HEADS * D_V, BF16),
            (KV_LORA, F32), (D_ROPE, F32)]
    return pl.pallas_call(
        functools.partial(_mix_in_kernel, latent),
        grid=(b, t // tm),
        in_specs=[pl.BlockSpec((1, tm, D_MODEL), tok),
                  pl.BlockSpec((1, CONV_HALO, D_MODEL), prev_rows),
                  pl.BlockSpec((1, CONV_HALO, D_MODEL), next_rows),
                  pl.BlockSpec((1, 6, D_MODEL), lambda bi, i: (mod_row0 + bi, 0, 0)),
                  pl.BlockSpec((1, D_MODEL), const),
                  pl.BlockSpec((ncol, D_MODEL), const),
                  pl.BlockSpec((1, Q_LORA), const),
                  pl.BlockSpec((1, KV_LORA), const),
                  pl.BlockSpec((Q_LORA, N_HEADS * HEAD_W), const),
                  pl.BlockSpec((Q_LORA, N_HEADS * LANE), const),
                  pl.BlockSpec((KV_LORA, N_HEADS * (D_NOPE + D_V)), const),
                  pl.BlockSpec((tm, LANE), table),
                  pl.BlockSpec((tm, LANE), table),
                  pl.BlockSpec((CONV_K, C_CONV), const),
                  pl.BlockSpec((1, C_CONV), const),
                  pl.BlockSpec((1, C_CONV), const),
                  pl.BlockSpec((1, C_CONV), const)],
        out_specs=[pl.BlockSpec((1, tm, w), tok) for w, _ in outs],
        out_shape=[jax.ShapeDtypeStruct((b, t, w), dt) for w, dt in outs],
        scratch_shapes=[pltpu.VMEM((tm // CONV_TB, CONV_WIN, C_CONV), F32),
                        pltpu.VMEM((SUBLANE, CONV_WIN, C_CONV), F32),
                        pltpu.VMEM((ncol, D_MODEL), BF16)],
        compiler_params=_cparams("arbitrary", "arbitrary"),
        name="mix_in_latent" if latent else "mix_in",
    )(x, x, x, mod, g_pre1, w_in_ext, q_norm_g, kv_norm_g, wq, wqs, wukv, cos, sin, *conv_w)


ATT_KCHUNK = 256


def _attn_kernel(cached, q_ref, k_ref, v_ref, *rest):
    if cached:
        kc_ref, vc_ref, o_ref, s_scr = rest
        sources = [(kc_ref, vc_ref), (k_ref, v_ref)]
    else:
        o_ref, s_scr = rest
        sources = [(k_ref, v_ref)]
    n_seq, tq, _ = q_ref.shape
    chunks = []
    for kr, vr in sources:
        for c0 in range(0, kr.shape[1], ATT_KCHUNK):
            c1 = min(c0 + ATT_KCHUNK, kr.shape[1])
            col = chunks[-1][4] + chunks[-1][3] - chunks[-1][2] if chunks else 0
            chunks.append((kr, vr, c0, c1, col))
    for b in range(n_seq):
        outs = []
        for hd in range(N_HEADS):
            ks = slice(hd * HEAD_W, (hd + 1) * HEAD_W)
            vs = slice(hd * D_V, (hd + 1) * D_V)
            qh = q_ref[b, :, ks]
            m = jnp.full((tq, 1), -jnp.inf, F32)
            for kr, vr, c0, c1, col in chunks:
                s = _dot_t(qh, kr[b, c0:c1, ks]) * (ATT_SCALE * LOG2E)
                s_scr[hd, :, col:col + c1 - c0] = s
                m = jnp.maximum(m, jnp.max(s, axis=-1, keepdims=True))
            l = jnp.zeros((tq, 1), F32)
            o = jnp.zeros((tq, D_V), F32)
            for kr, vr, c0, c1, col in chunks:
                p = jnp.exp2(s_scr[hd, :, col:col + c1 - c0] - m)
                l = l + jnp.sum(p, axis=-1, keepdims=True)
                o = o + _dot(p.astype(BF16), vr[b, c0:c1, vs])
            outs.append(o / l)
        o_ref[b] = jnp.concatenate(outs, axis=-1).astype(BF16)


def _attn(q, k, v, kc=None, vc=None, tq=ATT_TQ, n_seq=1):
    b, t, _ = q.shape
    s = k.shape[1]
    cached = kc is not None
    whole = lambda bi, i: (bi, 0, 0)
    in_specs = [pl.BlockSpec((n_seq, tq, N_HEADS * HEAD_W), lambda bi, i: (bi, i, 0)),
                pl.BlockSpec((n_seq, s, N_HEADS * HEAD_W), whole),
                pl.BlockSpec((n_seq, s, N_HEADS * D_V), whole)]
    args = [q, k, v]
    if cached:
        sc = kc.shape[1]
        in_specs += [pl.BlockSpec((n_seq, sc, N_HEADS * HEAD_W), whole),
                     pl.BlockSpec((n_seq, sc, N_HEADS * D_V), whole)]
        args += [kc, vc]
        s += sc
    return pl.pallas_call(
        functools.partial(_attn_kernel, cached),
        grid=(b // n_seq, t // tq),
        in_specs=in_specs,
        out_specs=pl.BlockSpec((n_seq, tq, N_HEADS * D_V), lambda bi, i: (bi, i, 0)),
        out_shape=jax.ShapeDtypeStruct((b, t, N_HEADS * D_V), BF16),
        scratch_shapes=[pltpu.VMEM((N_HEADS, tq, s), F32)],
        compiler_params=_cparams("arbitrary", "arbitrary"),
        name="attn_cached" if cached else "attn",
    )(*args)


def _cache_kv_kernel(ckv_ref, kr_ref, wukv_ref, k_ref, v_ref):
    kvd = _dot(ckv_ref[0].astype(BF16), wukv_ref[...])
    kr = kr_ref[0]
    kr = jnp.concatenate([kr, jnp.zeros_like(kr)], axis=-1)
    parts = []
    for hd in range(N_HEADS):
        parts.append(kvd[:, hd * D_NOPE:(hd + 1) * D_NOPE])
        parts.append(kr)
    k_ref[0] = jnp.concatenate(parts, axis=-1).astype(BF16)
    v_ref[0] = kvd[:, N_HEADS * D_NOPE:].astype(BF16)


def _cache_kv(ckv, krope, wukv):
    b, s, _ = ckv.shape
    tok = lambda bi: (bi, 0, 0)
    return pl.pallas_call(
        _cache_kv_kernel,
        grid=(b,),
        in_specs=[pl.BlockSpec((1, s, KV_LORA), tok),
                  pl.BlockSpec((1, s, D_ROPE), tok),
                  pl.BlockSpec((KV_LORA, N_HEADS * (D_NOPE + D_V)), lambda bi: (0, 0))],
        out_specs=[pl.BlockSpec((1, s, N_HEADS * HEAD_W), tok),
                   pl.BlockSpec((1, s, N_HEADS * D_V), tok)],
        out_shape=[jax.ShapeDtypeStruct((b, s, N_HEADS * HEAD_W), BF16),
                   jax.ShapeDtypeStruct((b, s, N_HEADS * D_V), BF16)],
        compiler_params=_cparams("arbitrary"),
        name="cache_kv",
    )(ckv, krope, wukv)


def _post_kernel(x_ref, conv_ref, att_ref, mod_ref, wo_ref, gpost_ref, gpre2_ref, wr_hi_ref, wr_cat_ref,
                 br_ref, x1_ref, h2_ref, ri_ref, rw_ref, cnt_ref, wo_b):
    @pl.when((pl.program_id(0) == 0) & (pl.program_id(1) == 0))
    def _():
        wo_b[...] = wo_ref[...].astype(BF16)

    out = _dot(conv_ref[0], wo_b[:C_CONV, :]) + _dot(att_ref[0], wo_b[C_CONV:, :])
    gt1 = mod_ref[0, 2:3, :]
    sh2 = mod_ref[0, 3:4, :]
    sc2 = mod_ref[0, 4:5, :]
    x1 = x_ref[0] + gt1 * _rms(out, gpost_ref[...])
    x1_ref[0] = x1
    h2 = _rms(x1, gpre2_ref[...]) * (1.0 + sc2) + sh2
    h_hi = h2.astype(BF16)
    h2_ref[0] = h_hi
    h_lo = (h2 - h_hi.astype(F32)).astype(BF16)
    hi_terms = _dot(h_hi, wr_cat_ref[...])
    logits = hi_terms + pltpu.roll(hi_terms, LANE // 2, axis=1) + _dot(h_lo, wr_hi_ref[...]) + br_ref[...]

    lane = lax.broadcasted_iota(jnp.int32, logits.shape, 1)
    neg = jnp.float32(-jnp.inf)
    big = jnp.int32(LANE)
    is_g = (lane >= N_EXPERTS) & (lane < N_EXPERTS + N_GROUPS)
    lg = jnp.where(is_g, logits, neg)
    gmax = jnp.max(lg, axis=-1, keepdims=True)
    gidx = jnp.min(jnp.where(lg == gmax, lane, big), axis=-1, keepdims=True) - N_EXPERTS
    g_top = 1.0 / jnp.sum(jnp.exp(lg - gmax), axis=-1, keepdims=True)

    in_grp = (lane >= gidx * E_PER_GROUP) & (lane < (gidx + 1) * E_PER_GROUP)
    le = jnp.where(in_grp, logits, neg)
    m1 = jnp.max(le, axis=-1, keepdims=True)
    i1 = jnp.min(jnp.where(le == m1, lane, big), axis=-1, keepdims=True)
    le2 = jnp.where(lane == i1, neg, le)
    m2 = jnp.max(le2, axis=-1, keepdims=True)
    i2 = jnp.min(jnp.where(le2 == m2, lane, big), axis=-1, keepdims=True)
    r = jnp.exp(m2 - m1)
    w1 = g_top / (1.0 + r)
    w2 = g_top * r / (1.0 + r)
    ri_ref[0] = jnp.where(lane == 0, i1, jnp.where(lane == 1, i2, 0))
    rw_ref[0] = jnp.where(lane == 0, w1, jnp.where(lane == 1, w2, 0.0))
    pairs = jnp.sum(jnp.where((lane == i1) | (lane == i2), 1.0, 0.0), axis=0, keepdims=True)
    cnt_ref[0, 0] = jnp.broadcast_to(pairs, (SUBLANE, LANE))


def _post(x, conv_out, att, mod, mod_row0, w_out, g_post1, g_pre2, wr_hi, wr_cat, br):
    b, t, _ = x.shape
    tm = MOE_TD
    const = lambda bi, i: (0, 0)
    tok = lambda bi, i: (bi, i, 0)
    return pl.pallas_call(
        _post_kernel,
        grid=(b, t // tm),
        in_specs=[pl.BlockSpec((1, tm, D_MODEL), tok),
                  pl.BlockSpec((1, tm, C_CONV), tok),
                  pl.BlockSpec((1, tm, N_HEADS * D_V), tok),
                  pl.BlockSpec((1, 6, D_MODEL), lambda bi, i: (mod_row0 + bi, 0, 0)),
                  pl.BlockSpec((D_MODEL, D_MODEL), const),
                  pl.BlockSpec((1, D_MODEL), const),
                  pl.BlockSpec((1, D_MODEL), const),
                  pl.BlockSpec((D_MODEL, LANE), const),
                  pl.BlockSpec((D_MODEL, LANE), const),
                  pl.BlockSpec((1, LANE), const)],
        out_specs=[pl.BlockSpec((1, tm, D_MODEL), tok),
                   pl.BlockSpec((1, tm, D_MODEL), tok),
                   pl.BlockSpec((1, tm, LANE), tok),
                   pl.BlockSpec((1, tm, LANE), tok),
                   pl.BlockSpec((1, 1, SUBLANE, LANE), lambda bi, i: (bi, i, 0, 0))],
        out_shape=[jax.ShapeDtypeStruct((b, t, D_MODEL), F32),
                   jax.ShapeDtypeStruct((b, t, D_MODEL), BF16),
                   jax.ShapeDtypeStruct((b, t, LANE), jnp.int32),
                   jax.ShapeDtypeStruct((b, t, LANE), F32),
                   jax.ShapeDtypeStruct((b, t // tm, SUBLANE, LANE), F32)],
        scratch_shapes=[pltpu.VMEM((D_MODEL, D_MODEL), BF16)],
        compiler_params=_cparams("arbitrary", "arbitrary"),
        name="post",
    )(x, conv_out, att, mod, w_out, g_post1, g_pre2, wr_hi, wr_cat, br)


def _one_hots(ri):
    lane = lax.broadcasted_iota(jnp.int32, ri.shape, 1)
    oh1 = lane == ri[:, 0:1]
    oh2 = lane == ri[:, 1:2]
    return oh1, oh2, jnp.where(oh1 | oh2, 1.0, 0.0)


def _strictly_lower(n):
    r = lax.broadcasted_iota(jnp.int32, (n, n), 0)
    c = lax.broadcasted_iota(jnp.int32, (n, n), 1)
    return jnp.where(c < r, 1.0, 0.0).astype(BF16)


def _local_positions(oh1, oh2, oh, earlier_ref):
    cnt = jnp.sum(oh, axis=0, keepdims=True)
    nch = jnp.floor((cnt + (CHUNK - 1)) * (1.0 / CHUNK))
    a = lax.broadcasted_iota(jnp.int32, (LANE, LANE), 0)
    b = lax.broadcasted_iota(jnp.int32, (LANE, LANE), 1)
    lower_experts = jnp.where(a < b, 1.0, 0.0).astype(BF16)
    run_start = _dot(jnp.broadcast_to(nch, (SUBLANE, LANE)).astype(BF16), lower_experts)[0:1, :] * CHUNK
    pos = _dot(earlier_ref[...], oh.astype(BF16)) + run_start
    lp1 = jnp.sum(jnp.where(oh1, pos, 0.0), axis=-1, keepdims=True)
    lp2 = jnp.sum(jnp.where(oh2, pos, 0.0), axis=-1, keepdims=True)
    return lp1, lp2


def _plan(cnt, n_tiles_max):
    k = (cnt + (CHUNK - 1)) // CHUNK
    run_end = jnp.cumsum(k, axis=1)
    run_start = run_end - k
    n_chunks = run_end[:, -1]
    total = jnp.sum(k, axis=0)
    padded = (total + (TILE_CH - 1)) // TILE_CH * TILE_CH
    seg_end = jnp.cumsum(padded)
    seg_start = seg_end - padded
    base = seg_start[None, :] + jnp.cumsum(k, axis=0) - k
    c = jnp.arange(MOE_NCH, dtype=jnp.int32)[None, :, None]
    in_run = (run_start[:, None, :] <= c) & (c < run_end[:, None, :])
    dst = jnp.sum(jnp.where(in_run, base[:, None, :] + c - run_start[:, None, :], 0), axis=2)
    n_tiles = seg_end[-1] // TILE_CH
    tail_start = seg_start + total
    tail_len = padded - total
    i32 = lambda t: t.astype(jnp.int32)
    return (i32(dst), i32(n_chunks), i32(tail_start), i32(tail_len), i32(seg_start // TILE_CH),
            i32(seg_end // TILE_CH), i32(n_tiles.reshape(1)))


def _dispatch_kernel(steps_a, dst_ref, nch_ref, ts_ref, tl_ref, nt_ref, h2a_ref, ria_ref, rwa_ref, h2b_ref, rib_ref,
                     rwb_ref, xs_hbm, lp_ref, xbuf, zx, sem, zsem, earlier):
    i = pl.program_id(0)
    n = pl.num_programs(0)
    slot = i % 2
    td = MOE_TD
    n_tiles_max = xs_hbm.shape[0] // MOE_TILE

    def zero_tile(m):
        rows = pl.ds(pl.multiple_of(m * MOE_TILE, MOE_TILE), MOE_TILE)
        return pltpu.make_async_copy(zx, xs_hbm.at[rows, :], zsem.at[1])

    @pl.when(i == 0)
    def _():
        zx[...] = jnp.zeros_like(zx)
        earlier[...] = _strictly_lower(MOE_TD)
        for e in range(N_EXPERTS):
            def fill(m, carry, e=e):
                _chunk_copy(zx, 0, xs_hbm, ts_ref[e] + m, zsem.at[0]).start()
                return carry
            lax.fori_loop(0, tl_ref[e], fill, 0)

        def fill_tile(m, carry):
            zero_tile(m).start()
            return carry
        lax.fori_loop(nt_ref[0], n_tiles_max, fill_tile, 0)

    from_a = i < steps_a
    ri = jnp.where(from_a, ria_ref[...], rib_ref[...])
    rw = jnp.where(from_a, rwa_ref[...], rwb_ref[...])
    h2 = jnp.where(from_a, h2a_ref[...], h2b_ref[...])
    oh1, oh2, oh = _one_hots(ri)
    lp1, lp2 = _local_positions(oh1, oh2, oh, earlier)
    lane = lax.broadcasted_iota(jnp.int32, ri.shape, 1)
    lp_ref[...] = jnp.where(lane == 0, lp1, jnp.where(lane == 1, lp2, 0.0)).astype(jnp.int32)

    cols = jnp.where(lane == 0, lp1, jnp.where(lane == 1, lp2, jnp.where(lane == 2, rw[:, 0:1],
                                                                         jnp.where(lane == 3, rw[:, 1:2], 0.0))))
    rows4 = cols.T
    row = lax.broadcasted_iota(jnp.int32, (MOE_L, td), 0).astype(F32)
    p1 = row == rows4[0:1, :]
    p2 = row == rows4[1:2, :]
    perm = jnp.where(p1 | p2, 1.0, 0.0).astype(BF16)
    xbuf[slot, :, 0:D_MODEL] = _dot(perm, h2).astype(BF16)
    w = jnp.sum(jnp.where(p1, rows4[2:3, :], 0.0) + jnp.where(p2, rows4[3:4, :], 0.0), axis=-1, keepdims=True)
    w0 = w.astype(BF16).astype(F32)
    w1 = (w - w0).astype(BF16).astype(F32)
    w2 = (w - w0 - w1).astype(BF16).astype(F32)
    wl = lax.broadcasted_iota(jnp.int32, (MOE_L, LANE), 1)
    terms = jnp.where(wl == 0, w0, jnp.where(wl == 1, w1, jnp.where(wl == 2, w2, 0.0)))
    xbuf[slot, :, D_MODEL:XS_W] = terms.astype(BF16)

    def wait_chunks(count, s):
        def body(c, carry):
            _chunk_copy(xbuf.at[s], 0, xs_hbm, 0, sem.at[s]).wait()
            return carry
        lax.fori_loop(0, count, body, 0)

    @pl.when(i > 0)
    def _():
        wait_chunks(nch_ref[i - 1], 1 - slot)

    def send(c, carry):
        _chunk_copy(xbuf.at[slot], c, xs_hbm, dst_ref[i, c], sem.at[slot]).start()
        return carry
    lax.fori_loop(0, nch_ref[i], send, 0)

    @pl.when(i == n - 1)
    def _():
        wait_chunks(nch_ref[i], slot)
        for e in range(N_EXPERTS):
            def drain(m, carry):
                _chunk_copy(zx, 0, xs_hbm, 0, zsem.at[0]).wait()
                return carry
            lax.fori_loop(0, tl_ref[e], drain, 0)

        def drain_tile(m, carry):
            zero_tile(m).wait()
            return carry
        lax.fori_loop(nt_ref[0], n_tiles_max, drain_tile, 0)


def _dispatch(dst, n_chunks, tail_start, tail_len, n_tiles, part_a, part_b, n_tiles_max):
    steps_a = part_a[0].shape[0] // MOE_TD
    steps_b = part_b[0].shape[0] // MOE_TD
    in_a = lambda i, *_: (jnp.minimum(i, steps_a - 1), 0)
    in_b = lambda i, *_: (jnp.maximum(i - steps_a, 0), 0)
    specs = lambda f: [pl.BlockSpec((MOE_TD, D_MODEL), f), pl.BlockSpec((MOE_TD, LANE), f),
                       pl.BlockSpec((MOE_TD, LANE), f)]
    rows = n_tiles_max * MOE_TILE
    return pl.pallas_call(
        functools.partial(_dispatch_kernel, steps_a),
        grid_spec=pltpu.PrefetchScalarGridSpec(
            num_scalar_prefetch=5,
            grid=(steps_a + steps_b,),
            in_specs=specs(in_a) + specs(in_b),
            out_specs=[pl.BlockSpec(memory_space=pl.ANY),
                       pl.BlockSpec((MOE_TD, LANE), lambda i, *_: (i, 0))],
            scratch_shapes=[pltpu.VMEM((2, MOE_L, XS_W), BF16), pltpu.VMEM((MOE_TILE, XS_W), BF16),
                            pltpu.SemaphoreType.DMA((2,)), pltpu.SemaphoreType.DMA((2,)),
                            pltpu.VMEM((MOE_TD, MOE_TD), BF16)]),
        out_shape=[jax.ShapeDtypeStruct((rows, XS_W), BF16),
                   jax.ShapeDtypeStruct(((steps_a + steps_b) * MOE_TD, LANE), jnp.int32)],
        compiler_params=_cparams("arbitrary"),
        name="moe_dispatch",
    )(dst, n_chunks, tail_start, tail_len, n_tiles, *part_a, *part_b)


def _moe_gemm_kernel(t0_ref, t1_ref, nt_ref, wg_ref, wu_ref, wd_ref, xs_hbm, y_hbm,
                     wg_b, wu_b, wd_b, xbuf, ybuf, zbuf, isem, osem, zsem, issued):
    e = pl.program_id(0)
    nt = nt_ref[0]
    n_tiles_max = y_hbm.shape[0] // MOE_TILE
    tile_rows = lambda t: pl.ds(pl.multiple_of(t * MOE_TILE, MOE_TILE), MOE_TILE)

    def load(t, s):
        return (pltpu.make_async_copy(xs_hbm.at[tile_rows(t), :], xbuf.at[s], isem.at[s]),)

    def store(t, s):
        return pltpu.make_async_copy(ybuf.at[s], y_hbm.at[tile_rows(t), :], osem.at[s])

    def zero_tile(t):
        return pltpu.make_async_copy(zbuf, y_hbm.at[tile_rows(t), :], zsem.at[0])

    @pl.when(e == 0)
    def _():
        issued[0] = 0
        zbuf[...] = jnp.zeros_like(zbuf)

        def fill(t, carry):
            zero_tile(t).start()
            return carry
        lax.fori_loop(nt, n_tiles_max, fill, 0)

    wg_b[...] = wg_ref[0].astype(BF16)
    wu_b[...] = wu_ref[0].astype(BF16)
    wd_b[...] = wd_ref[0].astype(BF16)

    def top_up(t):
        upto = jnp.minimum(t + GEMM_RING, nt)

        def start(u, carry):
            for cp in load(u, u % GEMM_RING):
                cp.start()
            return carry
        lax.fori_loop(issued[0], upto, start, 0)
        issued[0] = jnp.maximum(issued[0], upto)

    def process(t, n):
        top_up(t)
        for k in range(n):
            for cp in load(t + k, (t + k) % GEMM_RING):
                cp.wait()
        rows = [xbuf[(t + k) % GEMM_RING] for k in range(n)]
        xw = rows[0] if n == 1 else jnp.concatenate(rows, axis=0)
        x = xw[:, 0:D_MODEL]
        wt = xw[:, D_MODEL:XS_W].astype(F32)
        w = wt[:, 0:1] + wt[:, 1:2] + wt[:, 2:3]
        a = _dot(x, wg_b[...])
        u = _dot(x, wu_b[...])
        he = (a * _sigmoid(a)) * u
        y = (_dot(he.astype(BF16), wd_b[...]) * w).astype(BF16)
        for k in range(n):
            s = (t + k) % GEMM_OUT

            @pl.when(t + k >= GEMM_OUT)
            def _(k=k, s=s):
                store(t + k - GEMM_OUT, s).wait()
            ybuf[s] = y[k * MOE_TILE:(k + 1) * MOE_TILE, :]
            store(t + k, s).start()

    t0 = t0_ref[e]
    t1 = t1_ref[e]

    n_full = (t1 - t0) // GEMM_BATCH
    rest = t0 + GEMM_BATCH * n_full

    def full(p, carry):
        process(t0 + GEMM_BATCH * p, GEMM_BATCH)
        return carry
    lax.fori_loop(0, n_full, full, 0)

    n = GEMM_BATCH // 2
    while n >= 1:
        @pl.when((t1 - rest) & n != 0)
        def _(n=n):
            process(rest + ((t1 - rest) & ~(2 * n - 1)), n)
        n //= 2

    @pl.when(e == pl.num_programs(0) - 1)
    def _():
        for k in range(1, GEMM_OUT + 1):
            @pl.when(nt >= k)
            def _(k=k):
                store(nt - k, (nt - k) % GEMM_OUT).wait()

        def drain(t, carry):
            zero_tile(t).wait()
            return carry
        lax.fori_loop(nt, n_tiles_max, drain, 0)


def _moe_gemm(tile0, tile1, n_tiles, xs, w_gate, w_up, w_down):
    wmap = lambda e, *_: (e, 0, 0)
    return pl.pallas_call(
        _moe_gemm_kernel,
        grid_spec=pltpu.PrefetchScalarGridSpec(
            num_scalar_prefetch=3,
            grid=(N_EXPERTS,),
            in_specs=[pl.BlockSpec((1, D_MODEL, D_EXPERT), wmap),
                      pl.BlockSpec((1, D_MODEL, D_EXPERT), wmap),
                      pl.BlockSpec((1, D_EXPERT, D_MODEL), wmap),
                      pl.BlockSpec(memory_space=pl.ANY)],
            out_specs=pl.BlockSpec(memory_space=pl.ANY),
            scratch_shapes=[pltpu.VMEM((D_MODEL, D_EXPERT), BF16), pltpu.VMEM((D_MODEL, D_EXPERT), BF16),
                            pltpu.VMEM((D_EXPERT, D_MODEL), BF16),
                            pltpu.VMEM((GEMM_RING, MOE_TILE, XS_W), BF16),
                            pltpu.VMEM((GEMM_OUT, MOE_TILE, D_MODEL), BF16), pltpu.VMEM((MOE_TILE, D_MODEL), BF16),
                            pltpu.SemaphoreType.DMA((GEMM_RING,)), pltpu.SemaphoreType.DMA((GEMM_OUT,)),
                            pltpu.SemaphoreType.DMA((1,)), pltpu.SMEM((1,), jnp.int32)]),
        out_shape=jax.ShapeDtypeStruct((xs.shape[0], D_MODEL), BF16),
        compiler_params=_cparams("arbitrary"),
        name="moe_gemm",
    )(tile0, tile1, n_tiles, w_gate, w_up, w_down, xs)


def _moe_combine_kernel(step0, dst_ref, nch_ref, lp_ref, x1_ref, mod_ref, gpost_ref, y_hbm, o_ref, ybuf, sem):
    i = pl.program_id(0)
    n = pl.num_programs(0)
    slot = i % 2

    def fetch(step, s):
        def body(c, carry):
            _chunk_copy(y_hbm, dst_ref[step, c], ybuf.at[s], c, sem.at[s]).start()
            return carry
        lax.fori_loop(0, nch_ref[step], body, 0)

    @pl.when(i == 0)
    def _():
        ybuf[...] = jnp.zeros_like(ybuf)
        fetch(step0, 0)

    @pl.when(i + 1 < n)
    def _():
        fetch(step0 + i + 1, 1 - slot)

    def wait(c, carry):
        _chunk_copy(y_hbm, 0, ybuf.at[slot], 0, sem.at[slot]).wait()
        return carry
    lax.fori_loop(0, nch_ref[step0 + i], wait, 0)

    lp = lp_ref[...]
    col = lax.broadcasted_iota(jnp.int32, (MOE_TD, MOE_L), 1)
    unperm = jnp.where((col == lp[:, 0:1]) | (col == lp[:, 1:2]), 1.0, 0.0).astype(BF16)
    moe = _dot(unperm, ybuf[slot])
    gt2 = mod_ref[0, 5:6, :]
    o_ref[...] = x1_ref[...] + gt2 * _rms(moe, gpost_ref[...])


def _moe_combine(step0, dst, n_chunks, lp, x1, mod, mod_row, y_sorted, g_post2):
    n = x1.shape[0]
    tok = lambda i, *_: (i, 0)
    return pl.pallas_call(
        functools.partial(_moe_combine_kernel, step0),
        grid_spec=pltpu.PrefetchScalarGridSpec(
            num_scalar_prefetch=2,
            grid=(n // MOE_TD,),
            in_specs=[pl.BlockSpec((MOE_TD, LANE), lambda i, *_: (step0 + i, 0)),
                      pl.BlockSpec((MOE_TD, D_MODEL), tok),
                      pl.BlockSpec((1, 6, D_MODEL), lambda i, *_: (mod_row(i), 0, 0)),
                      pl.BlockSpec((1, D_MODEL), lambda i, *_: (0, 0)),
                      pl.BlockSpec(memory_space=pl.ANY)],
            out_specs=pl.BlockSpec((MOE_TD, D_MODEL), tok),
            scratch_shapes=[pltpu.VMEM((2, MOE_L, D_MODEL), BF16), pltpu.SemaphoreType.DMA((2,))]),
        out_shape=jax.ShapeDtypeStruct((n, D_MODEL), F32),
        compiler_params=_cparams("arbitrary"),
        name="moe_combine",
    )(dst, n_chunks, lp, x1, mod, g_post2, y_sorted)


def _moe(part_a, part_b, mod, mod_row_a, mod_row_b, w_gate, w_up, w_down, g_post2):
    cnt = jnp.concatenate([part_a[4], part_b[4]], axis=0)[:, 0, :N_EXPERTS].astype(jnp.int32)
    steps_a = part_a[4].shape[0]
    n_tiles_max = (cnt.shape[0] * MOE_NCH + N_EXPERTS * (TILE_CH - 1)) // TILE_CH + 1
    dst, n_chunks, tail_start, tail_len, tile0, tile1, n_tiles = _plan(cnt, n_tiles_max)
    xs, lp = _dispatch(dst, n_chunks, tail_start, tail_len, n_tiles, part_a[1:4], part_b[1:4], n_tiles_max)
    y_sorted = _moe_gemm(tile0, tile1, n_tiles, xs, w_gate, w_up, w_down)
    ya = _moe_combine(0, dst, n_chunks, lp, part_a[0], mod, mod_row_a, y_sorted, g_post2)
    yb = _moe_combine(steps_a, dst, n_chunks, lp, part_b[0], mod, mod_row_b, y_sorted, g_post2)
    return ya, yb


def _rotate_half_cols(w):
    n = w.shape[-1]
    w4 = w.reshape(w.shape[:-1] + (n // 32, 2, 16))
    return jnp.stack([-w4[..., 1, :], w4[..., 0, :]], axis=-2).reshape(w.shape)


def _pad_lanes(w):
    return jnp.concatenate([w, jnp.zeros(w.shape[:-1] + (LANE - w.shape[-1],), w.dtype)], axis=-1)


def _rope_tables(t):
    rows = t // GRID_W
    n_freq = D_ROPE // 4
    f32 = np.float32
    freqs = f32(ROPE_BASE) ** (-np.arange(n_freq, dtype=f32) / f32(n_freq))
    ang_r = np.arange(rows, dtype=f32)[:, None] * freqs
    ang_c = np.arange(GRID_W, dtype=f32)[:, None] * freqs

    def table(fn):
        r = np.repeat(fn(ang_r).astype(f32), GRID_W, axis=0)
        c = np.tile(fn(ang_c).astype(f32), (rows, 1))
        pad = np.zeros((t, LANE - D_ROPE), f32)
        return jnp.asarray(np.concatenate([r, r, c, c, pad], axis=-1))
    return table(np.cos), table(np.sin)


def kernel(x_prompt, x_sample, cache_ckv, cache_krope, c, c_ctx, w_ada, b_ada, g_pre1, g_post1, g_pre2, g_post2, w_in, w_dw, b_dw, conv_ln_g, conv_ln_b, q_norm_g, kv_norm_g, w_uq, w_ukv, w_out, w_rg, b_rg, w_re, b_re, w_gate, w_up, w_down):
    nb, seq, d = x_prompt.shape
    db, dseq, _ = x_sample.shape
    l = 0

    cvec = jnp.concatenate([c_ctx[None, :], c, jnp.zeros((8 - 1 - db, d), F32)], axis=0)
    mod = _ada(cvec, w_ada[l], b_ada[l]).reshape(8, 6, d)

    w_in_t = w_in[l].T
    wuq = w_uq[l].reshape(Q_LORA, N_HEADS, D_NOPE + D_ROPE)
    wq_rope = wuq[:, :, D_NOPE:]
    wq = jnp.concatenate([wuq[:, :, :D_NOPE], _pad_lanes(wq_rope)], axis=-1)
    wq = wq.reshape(Q_LORA, N_HEADS * HEAD_W).astype(BF16)
    wqs = _pad_lanes(_rotate_half_cols(wq_rope)).reshape(Q_LORA, N_HEADS * LANE).astype(BF16)
    wukv4 = w_ukv[l].reshape(KV_LORA, N_HEADS, D_NOPE + D_V)
    wukv = jnp.concatenate([wukv4[:, :, :D_NOPE].reshape(KV_LORA, N_HEADS * D_NOPE),
                            wukv4[:, :, D_NOPE:].reshape(KV_LORA, N_HEADS * D_V)], axis=-1).astype(BF16)
    w_out_b = w_out[l]
    w_r = jnp.concatenate([w_re[l], w_rg[l], jnp.zeros((d, LANE - N_EXPERTS - N_GROUPS), F32)], axis=-1)
    wr_hi = w_r.astype(BF16)
    wr_lo = (w_r - wr_hi.astype(F32)).astype(BF16)
    wr_cat = jnp.concatenate([wr_hi[:, :LANE // 2], wr_lo[:, :LANE // 2]], axis=-1)
    b_r = jnp.concatenate([b_re[l], b_rg[l], jnp.zeros((LANE - N_EXPERTS - N_GROUPS,), F32)])[None, :]
    cos, sin = _rope_tables(dseq)
    row = lambda v: v[l][None, :]

    tm = MOE_TD
    xp_flat = x_prompt.reshape(1, nb * seq, d)
    conv_w = (w_dw[l], row(b_dw), row(conv_ln_g), row(conv_ln_b))
    assert seq == CONV_TB, "a context step must hold whole sequences of CONV_TB tokens"
    conv_out, q, k, v, ckv, kr = _mix_in(xp_flat, mod, 0, row(g_pre1), w_in_t, row(q_norm_g), row(kv_norm_g),
                                         wq, wqs, wukv, cos[:tm], sin[:tm], conv_w, False, tm)
    per_seq = lambda a: a.reshape(nb, seq, a.shape[-1])
    att = _attn(per_seq(q), per_seq(k), per_seq(v), n_seq=ATT_SEQS)
    flat = lambda a: a.reshape(1, nb * seq, a.shape[-1])
    post_c = _post(xp_flat, conv_out, flat(att), mod, 0, w_out_b, row(g_post1), row(g_pre2),
                   wr_hi, wr_cat, b_r)
    state_ckv = ckv.reshape(nb, 1, seq, KV_LORA)
    state_krope = kr.reshape(nb, 1, seq, D_ROPE)

    kc, vc = _cache_kv(cache_ckv[:, l], cache_krope[:, l], wukv)
    conv_out, q, k, v, _, _ = _mix_in(x_sample, mod, 1, row(g_pre1), w_in_t, row(q_norm_g), row(kv_norm_g),
                                      wq, wqs, wukv, cos, sin, conv_w, True, tm)
    att = _attn(q, k, v, kc, vc)
    post_s = _post(x_sample, conv_out, att, mod, 1, w_out_b, row(g_post1), row(g_pre2), wr_hi, wr_cat, b_r)

    tokens = lambda parts: tuple(a.reshape((-1,) + a.shape[2:]) for a in parts)
    steps_per_req = dseq // MOE_TD
    yp, ys = _moe(tokens(post_c), tokens(post_s), mod, lambda i: 0, lambda i: 1 + i // steps_per_req,
                  w_gate[l], w_up[l], w_down[l], row(g_post2))

    return (yp.reshape(nb, seq, d), ys.reshape(db, dseq, d), state_ckv, state_krope)
```

```python
import functools

import jax
import jax.numpy as jnp
import numpy as np
from jax import lax
from jax.experimental import pallas as pl
from jax.experimental.pallas import tpu as pltpu

D_MODEL = 1024
GRID_W = 64
C_CONV = 512
CONV_K = 31
N_HEADS = 4
D_NOPE = 128
D_ROPE = 64
D_V = 128
Q_LORA = 384
KV_LORA = 256
N_GROUPS = 4
E_PER_GROUP = 8
N_EXPERTS = 32
D_EXPERT = 256
ROPE_BASE = 10000.0
EPS = 1e-6
ATT_SCALE = (D_NOPE + D_ROPE) ** -0.5
LOG2E = 1.4426950408889634

LANE = 128
SUBLANE = 8
HEAD_W = 2 * LANE
CONV_HALO = 16
ADA_TN = 1536
ATT_TQ = 256
ATT_SEQS = 8
MOE_TD = 512
CHUNK = 2 * SUBLANE
MOE_NCH = (2 * MOE_TD + N_EXPERTS * (CHUNK - 1)) // CHUNK
MOE_L = MOE_NCH * CHUNK
MOE_TILE = 256
TILE_CH = MOE_TILE // CHUNK
GEMM_RING = 6
GEMM_OUT = 4
XS_W = D_MODEL + LANE
VMEM_LIMIT = 56 * 1024 * 1024

BF16 = jnp.bfloat16
F32 = jnp.float32


def _cparams(*sem):
    return pltpu.CompilerParams(dimension_semantics=sem, vmem_limit_bytes=VMEM_LIMIT)


def _rms(x, g):
    return x * lax.rsqrt(jnp.mean(x * x, axis=-1, keepdims=True) + EPS) * g


def _sigmoid(x):
    return 1.0 / (1.0 + jnp.exp(-x))


def _dot(a, b):
    return jnp.dot(a, b, preferred_element_type=F32)


def _dot_t(a, b_t):
    return lax.dot_general(a, b_t, (((1,), (1,)), ((), ())), preferred_element_type=F32)


def _chunk_copy(src, src_chunk, dst, dst_chunk, sem):
    rows = lambda c: pl.ds(c * CHUNK if isinstance(c, int) else pl.multiple_of(c * CHUNK, CHUNK), CHUNK)
    return pltpu.make_async_copy(src.at[rows(src_chunk), :], dst.at[rows(dst_chunk), :], sem)


def _ada_kernel(c_ref, w_ref, b_ref, o_ref):
    c = c_ref[...]
    s = (c * _sigmoid(c)).astype(BF16)
    o_ref[...] = _dot(s, w_ref[...].astype(BF16)) + b_ref[...]


def _ada(cvec, w_ada, b_ada):
    n = w_ada.shape[1]
    tn = ADA_TN
    return pl.pallas_call(
        _ada_kernel,
        grid=(n // tn,),
        in_specs=[pl.BlockSpec((8, D_MODEL), lambda j: (0, 0)),
                  pl.BlockSpec((D_MODEL, tn), lambda j: (0, j)),
                  pl.BlockSpec((1, tn), lambda j: (0, j))],
        out_specs=pl.BlockSpec((8, tn), lambda j: (0, j)),
        out_shape=jax.ShapeDtypeStruct((8, n), F32),
        compiler_params=_cparams("arbitrary"),
        name="ada",
    )(cvec, w_ada, b_ada.reshape(1, n))


CONV_TB = 256
CONV_WIN = CONV_TB + 2 * CONV_HALO
CONV_TT = 32


def _put_rows(win_ref, row, x):
    for s in range(C_CONV // LANE):
        win_ref[s, pl.ds(2 * row, x.shape[0], stride=2), :] = x[:, s * LANE:(s + 1) * LANE]


def _conv_window(win_ref, w_ref, b_ref, g_ref, bb_ref, o_ref, row0):
    first = CONV_HALO - CONV_K // 2
    for c in range(CONV_TB // CONV_TT):
        slabs = []
        for s in range(C_CONV // LANE):
            lanes = slice(s * LANE, (s + 1) * LANE)
            acc = jnp.zeros((CONV_TT, LANE), F32)
            for k in range(CONV_K):
                rows = pl.ds(2 * (c * CONV_TT + first + k), CONV_TT, stride=2)
                acc = acc + win_ref[s, rows, :] * w_ref[k:k + 1, lanes]
            slabs.append(acc)
        y = jnp.concatenate(slabs, axis=-1) + b_ref[...]
        mu = jnp.mean(y, axis=-1, keepdims=True)
        yc = y - mu
        var = jnp.mean(yc * yc, axis=-1, keepdims=True)
        z = yc * lax.rsqrt(var + EPS) * g_ref[...] + bb_ref[...]
        o_ref[0, row0 + c * CONV_TT:row0 + (c + 1) * CONV_TT, :] = (z * _sigmoid(z)).astype(BF16)


def _mix_in_kernel(latent, x_ref, xp_ref, xn_ref, mod_ref, gpre_ref, win_ref, qg_ref, kvg_ref, wq_ref, wqs_ref,
                   wukv_ref, cos_ref, sin_ref, wdw_ref, bdw_ref, lng_ref, lnb_ref,
                   conv_ref, q_ref, k_ref, v_ref, ckv_ref, kr_ref, cwin, wbf):
    rope = latent

    @pl.when((pl.program_id(0) == 0) & (pl.program_id(1) == 0))
    def _():
        wbf[...] = win_ref[...].astype(BF16)

    x = x_ref[0]
    tm = x.shape[0]
    sh1 = mod_ref[0, 0:1, :]
    sc1 = mod_ref[0, 1:2, :]
    modulate = lambda v: _rms(v, gpre_ref[...]) * (1.0 + sc1) + sh1
    glu = lambda uc: uc[:, :C_CONV] * _sigmoid(uc[:, C_CONV:2 * C_CONV])
    u = _dot_t(modulate(x).astype(BF16), wbf[...])
    hglu = glu(u)

    zeros = jnp.zeros((CONV_HALO, C_CONV), F32)
    n_win = tm // CONV_TB
    if latent:
        i = pl.program_id(1)
        xh = jnp.concatenate([xp_ref[0], xn_ref[0]], axis=0)
        hh = glu(_dot_t(modulate(xh).astype(BF16), wbf[0:2 * C_CONV, :]))
        before = jnp.where(i > 0, hh[0:CONV_HALO], zeros)
        after = jnp.where(i < pl.num_programs(1) - 1, hh[CONV_HALO:2 * CONV_HALO], zeros)
    for j in range(n_win):
        if latent:
            lo, hi, d0 = j * CONV_TB - CONV_HALO, j * CONV_TB + CONV_TB + CONV_HALO, 0
            if lo < 0:
                _put_rows(cwin.at[j], 0, before)
                lo, d0 = 0, CONV_HALO
            if hi > tm:
                _put_rows(cwin.at[j], CONV_WIN - CONV_HALO, after)
                hi = tm
            _put_rows(cwin.at[j], d0, hglu[lo:hi, :])
        else:
            _put_rows(cwin.at[j], 0, zeros)
            _put_rows(cwin.at[j], CONV_HALO, hglu[j * CONV_TB:(j + 1) * CONV_TB, :])
            _put_rows(cwin.at[j], CONV_HALO + CONV_TB, zeros)
    for j in range(n_win):
        _conv_window(cwin.at[j], wdw_ref, bdw_ref, lng_ref, lnb_ref, conv_ref, j * CONV_TB)

    o_q = 2 * C_CONV
    o_kv = o_q + Q_LORA
    o_kr = o_kv + KV_LORA
    qn = _rms(u[:, o_q:o_kv], qg_ref[...]).astype(BF16)
    qf = _dot(qn, wq_ref[...])
    ckv = _rms(u[:, o_kv:o_kr], kvg_ref[...])
    ckv_ref[0] = ckv
    kvd = _dot(ckv.astype(BF16), wukv_ref[...])
    kr64 = u[:, o_kr:o_kr + D_ROPE]
    kr_ref[0] = kr64
    kr = jnp.concatenate([kr64, jnp.zeros((tm, LANE - D_ROPE), F32)], axis=-1)
    if rope:
        cos = cos_ref[...]
        sin = sin_ref[...]
        qs = _dot(qn, wqs_ref[...])
        lane = lax.broadcasted_iota(jnp.int32, kr.shape, 1)
        partner = jnp.where(lane % (D_ROPE // 2) < D_ROPE // 4,
                            -pltpu.roll(kr, LANE - D_ROPE // 4, axis=1), pltpu.roll(kr, D_ROPE // 4, axis=1))
        kr = kr * cos + partner * sin
    q_parts = []
    k_parts = []
    for hd in range(N_HEADS):
        q_parts.append(qf[:, hd * HEAD_W:hd * HEAD_W + LANE])
        qr = qf[:, hd * HEAD_W + LANE:(hd + 1) * HEAD_W]
        if rope:
            qr = qr * cos + qs[:, hd * LANE:(hd + 1) * LANE] * sin
        q_parts.append(qr)
        k_parts.append(kvd[:, hd * D_NOPE:(hd + 1) * D_NOPE])
        k_parts.append(kr)
    q_ref[0] = jnp.concatenate(q_parts, axis=-1).astype(BF16)
    k_ref[0] = jnp.concatenate(k_parts, axis=-1).astype(BF16)
    v_ref[0] = kvd[:, N_HEADS * D_NOPE:].astype(BF16)


def _mix_in(x, mod, mod_row0, g_pre1, w_in_ext, q_norm_g, kv_norm_g, wq, wqs, wukv, cos, sin, conv_w, latent, tm):
    b, t, _ = x.shape
    ncol = w_in_ext.shape[0]
    const = lambda bi, i: (0, 0)
    tok = lambda bi, i: (bi, i, 0)
    table = (lambda bi, i: (i, 0)) if latent else const
    halo_blocks = tm // CONV_HALO
    prev_rows = lambda bi, i: (bi, jnp.maximum(i * halo_blocks - 1, 0), 0)
    next_rows = lambda bi, i: (bi, jnp.minimum((i + 1) * halo_blocks, t // CONV_HALO - 1), 0)
    outs = [(C_CONV, BF16), (N_HEADS * HEAD_W, BF16), (N_HEADS * HEAD_W, BF16), (N_HEADS * D_V, BF16),
            (KV_LORA, F32), (D_ROPE, F32)]
    return pl.pallas_call(
        functools.partial(_mix_in_kernel, latent),
        grid=(b, t // tm),
        in_specs=[pl.BlockSpec((1, tm, D_MODEL), tok),
                  pl.BlockSpec((1, CONV_HALO, D_MODEL), prev_rows),
                  pl.BlockSpec((1, CONV_HALO, D_MODEL), next_rows),
                  pl.BlockSpec((1, 6, D_MODEL), lambda bi, i: (mod_row0 + bi, 0, 0)),
                  pl.BlockSpec((1, D_MODEL), const),
                  pl.BlockSpec((ncol, D_MODEL), const),
                  pl.BlockSpec((1, Q_LORA), const),
                  pl.BlockSpec((1, KV_LORA), const),
                  pl.BlockSpec((Q_LORA, N_HEADS * HEAD_W), const),
                  pl.BlockSpec((Q_LORA, N_HEADS * LANE), const),
                  pl.BlockSpec((KV_LORA, N_HEADS * (D_NOPE + D_V)), const),
                  pl.BlockSpec((tm, LANE), table),
                  pl.BlockSpec((tm, LANE), table),
                  pl.BlockSpec((CONV_K, C_CONV), const),
                  pl.BlockSpec((1, C_CONV), const),
                  pl.BlockSpec((1, C_CONV), const),
                  pl.BlockSpec((1, C_CONV), const)],
        out_specs=[pl.BlockSpec((1, tm, w), tok) for w, _ in outs],
        out_shape=[jax.ShapeDtypeStruct((b, t, w), dt) for w, dt in outs],
        scratch_shapes=[pltpu.VMEM((tm // CONV_TB, C_CONV // LANE, 2 * CONV_WIN, LANE), F32),
                        pltpu.VMEM((ncol, D_MODEL), BF16)],
        compiler_params=_cparams("arbitrary", "arbitrary"),
        name="mix_in_latent" if latent else "mix_in",
    )(x, x, x, mod, g_pre1, w_in_ext, q_norm_g, kv_norm_g, wq, wqs, wukv, cos, sin, *conv_w)


ATT_KCHUNK = 256


def _attn_kernel(cached, q_ref, k_ref, v_ref, *rest):
    if cached:
        kc_ref, vc_ref, o_ref, s_scr = rest
        sources = [(kc_ref, vc_ref), (k_ref, v_ref)]
    else:
        o_ref, s_scr = rest
        sources = [(k_ref, v_ref)]
    n_seq, tq, _ = q_ref.shape
    chunks = []
    for kr, vr in sources:
        for c0 in range(0, kr.shape[1], ATT_KCHUNK):
            c1 = min(c0 + ATT_KCHUNK, kr.shape[1])
            col = chunks[-1][4] + chunks[-1][3] - chunks[-1][2] if chunks else 0
            chunks.append((kr, vr, c0, c1, col))
    for b in range(n_seq):
        outs = []
        for hd in range(N_HEADS):
            ks = slice(hd * HEAD_W, (hd + 1) * HEAD_W)
            vs = slice(hd * D_V, (hd + 1) * D_V)
            qh = q_ref[b, :, ks]
            m = jnp.full((tq, 1), -jnp.inf, F32)
            for kr, vr, c0, c1, col in chunks:
                s = _dot_t(qh, kr[b, c0:c1, ks]) * (ATT_SCALE * LOG2E)
                s_scr[hd, :, col:col + c1 - c0] = s
                m = jnp.maximum(m, jnp.max(s, axis=-1, keepdims=True))
            l = jnp.zeros((tq, 1), F32)
            o = jnp.zeros((tq, D_V), F32)
            for kr, vr, c0, c1, col in chunks:
                p = jnp.exp2(s_scr[hd, :, col:col + c1 - c0] - m)
                l = l + jnp.sum(p, axis=-1, keepdims=True)
                o = o + _dot(p.astype(BF16), vr[b, c0:c1, vs])
            outs.append(o / l)
        o_ref[b] = jnp.concatenate(outs, axis=-1).astype(BF16)


def _attn(q, k, v, kc=None, vc=None, tq=ATT_TQ, n_seq=1):
    b, t, _ = q.shape
    s = k.shape[1]
    cached = kc is not None
    whole = lambda bi, i: (bi, 0, 0)
    in_specs = [pl.BlockSpec((n_seq, tq, N_HEADS * HEAD_W), lambda bi, i: (bi, i, 0)),
                pl.BlockSpec((n_seq, s, N_HEADS * HEAD_W), whole),
                pl.BlockSpec((n_seq, s, N_HEADS * D_V), whole)]
    args = [q, k, v]
    if cached:
        sc = kc.shape[1]
        in_specs += [pl.BlockSpec((n_seq, sc, N_HEADS * HEAD_W), whole),
                     pl.BlockSpec((n_seq, sc, N_HEADS * D_V), whole)]
        args += [kc, vc]
        s += sc
    return pl.pallas_call(
        functools.partial(_attn_kernel, cached),
        grid=(b // n_seq, t // tq),
        in_specs=in_specs,
        out_specs=pl.BlockSpec((n_seq, tq, N_HEADS * D_V), lambda bi, i: (bi, i, 0)),
        out_shape=jax.ShapeDtypeStruct((b, t, N_HEADS * D_V), BF16),
        scratch_shapes=[pltpu.VMEM((N_HEADS, tq, s), F32)],
        compiler_params=_cparams("arbitrary", "arbitrary"),
        name="attn_cached" if cached else "attn",
    )(*args)


def _cache_kv_kernel(ckv_ref, kr_ref, wukv_ref, k_ref, v_ref):
    kvd = _dot(ckv_ref[0].astype(BF16), wukv_ref[...])
    kr = kr_ref[0]
    kr = jnp.concatenate([kr, jnp.zeros_like(kr)], axis=-1)
    parts = []
    for hd in range(N_HEADS):
        parts.append(kvd[:, hd * D_NOPE:(hd + 1) * D_NOPE])
        parts.append(kr)
    k_ref[0] = jnp.concatenate(parts, axis=-1).astype(BF16)
    v_ref[0] = kvd[:, N_HEADS * D_NOPE:].astype(BF16)


def _cache_kv(ckv, krope, wukv):
    b, s, _ = ckv.shape
    tok = lambda bi: (bi, 0, 0)
    return pl.pallas_call(
        _cache_kv_kernel,
        grid=(b,),
        in_specs=[pl.BlockSpec((1, s, KV_LORA), tok),
                  pl.BlockSpec((1, s, D_ROPE), tok),
                  pl.BlockSpec((KV_LORA, N_HEADS * (D_NOPE + D_V)), lambda bi: (0, 0))],
        out_specs=[pl.BlockSpec((1, s, N_HEADS * HEAD_W), tok),
                   pl.BlockSpec((1, s, N_HEADS * D_V), tok)],
        out_shape=[jax.ShapeDtypeStruct((b, s, N_HEADS * HEAD_W), BF16),
                   jax.ShapeDtypeStruct((b, s, N_HEADS * D_V), BF16)],
        compiler_params=_cparams("arbitrary"),
        name="cache_kv",
    )(ckv, krope, wukv)


def _post_kernel(x_ref, conv_ref, att_ref, mod_ref, wo_ref, gpost_ref, gpre2_ref, wr_hi_ref, wr_cat_ref,
                 br_ref, x1_ref, h2_ref, ri_ref, rw_ref, cnt_ref, wo_b):
    @pl.when((pl.program_id(0) == 0) & (pl.program_id(1) == 0))
    def _():
        wo_b[...] = wo_ref[...].astype(BF16)

    out = _dot(conv_ref[0], wo_b[:C_CONV, :]) + _dot(att_ref[0], wo_b[C_CONV:, :])
    gt1 = mod_ref[0, 2:3, :]
    sh2 = mod_ref[0, 3:4, :]
    sc2 = mod_ref[0, 4:5, :]
    x1 = x_ref[0] + gt1 * _rms(out, gpost_ref[...])
    x1_ref[0] = x1
    h2 = _rms(x1, gpre2_ref[...]) * (1.0 + sc2) + sh2
    h_hi = h2.astype(BF16)
    h2_ref[0] = h_hi
    h_lo = (h2 - h_hi.astype(F32)).astype(BF16)
    hi_terms = _dot(h_hi, wr_cat_ref[...])
    logits = hi_terms + pltpu.roll(hi_terms, LANE // 2, axis=1) + _dot(h_lo, wr_hi_ref[...]) + br_ref[...]

    lane = lax.broadcasted_iota(jnp.int32, logits.shape, 1)
    neg = jnp.float32(-jnp.inf)
    big = jnp.int32(LANE)
    is_g = (lane >= N_EXPERTS) & (lane < N_EXPERTS + N_GROUPS)
    lg = jnp.where(is_g, logits, neg)
    gmax = jnp.max(lg, axis=-1, keepdims=True)
    gidx = jnp.min(jnp.where(lg == gmax, lane, big), axis=-1, keepdims=True) - N_EXPERTS
    g_top = 1.0 / jnp.sum(jnp.exp(lg - gmax), axis=-1, keepdims=True)

    in_grp = (lane >= gidx * E_PER_GROUP) & (lane < (gidx + 1) * E_PER_GROUP)
    le = jnp.where(in_grp, logits, neg)
    m1 = jnp.max(le, axis=-1, keepdims=True)
    i1 = jnp.min(jnp.where(le == m1, lane, big), axis=-1, keepdims=True)
    le2 = jnp.where(lane == i1, neg, le)
    m2 = jnp.max(le2, axis=-1, keepdims=True)
    i2 = jnp.min(jnp.where(le2 == m2, lane, big), axis=-1, keepdims=True)
    r = jnp.exp(m2 - m1)
    w1 = g_top / (1.0 + r)
    w2 = g_top * r / (1.0 + r)
    ri_ref[0] = jnp.where(lane == 0, i1, jnp.where(lane == 1, i2, 0))
    rw_ref[0] = jnp.where(lane == 0, w1, jnp.where(lane == 1, w2, 0.0))
    pairs = jnp.sum(jnp.where((lane == i1) | (lane == i2), 1.0, 0.0), axis=0, keepdims=True)
    cnt_ref[0, 0] = jnp.broadcast_to(pairs, (SUBLANE, LANE))


def _post(x, conv_out, att, mod, mod_row0, w_out, g_post1, g_pre2, wr_hi, wr_cat, br):
    b, t, _ = x.shape
    tm = MOE_TD
    const = lambda bi, i: (0, 0)
    tok = lambda bi, i: (bi, i, 0)
    return pl.pallas_call(
        _post_kernel,
        grid=(b, t // tm),
        in_specs=[pl.BlockSpec((1, tm, D_MODEL), tok),
                  pl.BlockSpec((1, tm, C_CONV), tok),
                  pl.BlockSpec((1, tm, N_HEADS * D_V), tok),
                  pl.BlockSpec((1, 6, D_MODEL), lambda bi, i: (mod_row0 + bi, 0, 0)),
                  pl.BlockSpec((D_MODEL, D_MODEL), const),
                  pl.BlockSpec((1, D_MODEL), const),
                  pl.BlockSpec((1, D_MODEL), const),
                  pl.BlockSpec((D_MODEL, LANE), const),
                  pl.BlockSpec((D_MODEL, LANE), const),
                  pl.BlockSpec((1, LANE), const)],
        out_specs=[pl.BlockSpec((1, tm, D_MODEL), tok),
                   pl.BlockSpec((1, tm, D_MODEL), tok),
                   pl.BlockSpec((1, tm, LANE), tok),
                   pl.BlockSpec((1, tm, LANE), tok),
                   pl.BlockSpec((1, 1, SUBLANE, LANE), lambda bi, i: (bi, i, 0, 0))],
        out_shape=[jax.ShapeDtypeStruct((b, t, D_MODEL), F32),
                   jax.ShapeDtypeStruct((b, t, D_MODEL), BF16),
                   jax.ShapeDtypeStruct((b, t, LANE), jnp.int32),
                   jax.ShapeDtypeStruct((b, t, LANE), F32),
                   jax.ShapeDtypeStruct((b, t // tm, SUBLANE, LANE), F32)],
        scratch_shapes=[pltpu.VMEM((D_MODEL, D_MODEL), BF16)],
        compiler_params=_cparams("arbitrary", "arbitrary"),
        name="post",
    )(x, conv_out, att, mod, w_out, g_post1, g_pre2, wr_hi, wr_cat, br)


def _one_hots(ri):
    lane = lax.broadcasted_iota(jnp.int32, ri.shape, 1)
    oh1 = lane == ri[:, 0:1]
    oh2 = lane == ri[:, 1:2]
    return oh1, oh2, jnp.where(oh1 | oh2, 1.0, 0.0)


def _strictly_lower(n):
    r = lax.broadcasted_iota(jnp.int32, (n, n), 0)
    c = lax.broadcasted_iota(jnp.int32, (n, n), 1)
    return jnp.where(c < r, 1.0, 0.0).astype(BF16)


def _local_positions(oh1, oh2, oh, earlier_ref):
    cnt = jnp.sum(oh, axis=0, keepdims=True)
    nch = jnp.floor((cnt + (CHUNK - 1)) * (1.0 / CHUNK))
    a = lax.broadcasted_iota(jnp.int32, (LANE, LANE), 0)
    b = lax.broadcasted_iota(jnp.int32, (LANE, LANE), 1)
    lower_experts = jnp.where(a < b, 1.0, 0.0).astype(BF16)
    run_start = _dot(jnp.broadcast_to(nch, (SUBLANE, LANE)).astype(BF16), lower_experts)[0:1, :] * CHUNK
    pos = _dot(earlier_ref[...], oh.astype(BF16)) + run_start
    lp1 = jnp.sum(jnp.where(oh1, pos, 0.0), axis=-1, keepdims=True)
    lp2 = jnp.sum(jnp.where(oh2, pos, 0.0), axis=-1, keepdims=True)
    return lp1, lp2


def _plan(cnt, n_tiles_max):
    k = (cnt + (CHUNK - 1)) // CHUNK
    run_end = jnp.cumsum(k, axis=1)
    run_start = run_end - k
    n_chunks = run_end[:, -1]
    total = jnp.sum(k, axis=0)
    padded = (total + (TILE_CH - 1)) // TILE_CH * TILE_CH
    seg_end = jnp.cumsum(padded)
    seg_start = seg_end - padded
    base = seg_start[None, :] + jnp.cumsum(k, axis=0) - k
    c = jnp.arange(MOE_NCH, dtype=jnp.int32)[None, :, None]
    in_run = (run_start[:, None, :] <= c) & (c < run_end[:, None, :])
    dst = jnp.sum(jnp.where(in_run, base[:, None, :] + c - run_start[:, None, :], 0), axis=2)
    n_tiles = seg_end[-1] // TILE_CH
    tail_start = seg_start + total
    tail_len = padded - total
    i32 = lambda t: t.astype(jnp.int32)
    return (i32(dst), i32(n_chunks), i32(tail_start), i32(tail_len), i32(seg_start // TILE_CH),
            i32(seg_end // TILE_CH), i32(n_tiles.reshape(1)))


def _dispatch_kernel(steps_a, dst_ref, nch_ref, ts_ref, tl_ref, nt_ref, h2a_ref, ria_ref, rwa_ref, h2b_ref, rib_ref,
                     rwb_ref, xs_hbm, lp_ref, xbuf, zx, sem, zsem, earlier):
    i = pl.program_id(0)
    n = pl.num_programs(0)
    slot = i % 2
    td = MOE_TD
    n_tiles_max = xs_hbm.shape[0] // MOE_TILE

    def zero_tile(m):
        rows = pl.ds(pl.multiple_of(m * MOE_TILE, MOE_TILE), MOE_TILE)
        return pltpu.make_async_copy(zx, xs_hbm.at[rows, :], zsem.at[1])

    @pl.when(i == 0)
    def _():
        zx[...] = jnp.zeros_like(zx)
        earlier[...] = _strictly_lower(MOE_TD)
        for e in range(N_EXPERTS):
            def fill(m, carry, e=e):
                _chunk_copy(zx, 0, xs_hbm, ts_ref[e] + m, zsem.at[0]).start()
                return carry
            lax.fori_loop(0, tl_ref[e], fill, 0)

        def fill_tile(m, carry):
            zero_tile(m).start()
            return carry
        lax.fori_loop(nt_ref[0], n_tiles_max, fill_tile, 0)

    from_a = i < steps_a
    ri = jnp.where(from_a, ria_ref[...], rib_ref[...])
    rw = jnp.where(from_a, rwa_ref[...], rwb_ref[...])
    h2 = jnp.where(from_a, h2a_ref[...], h2b_ref[...])
    oh1, oh2, oh = _one_hots(ri)
    lp1, lp2 = _local_positions(oh1, oh2, oh, earlier)
    lane = lax.broadcasted_iota(jnp.int32, ri.shape, 1)
    lp_ref[...] = jnp.where(lane == 0, lp1, jnp.where(lane == 1, lp2, 0.0)).astype(jnp.int32)

    cols = jnp.where(lane == 0, lp1, jnp.where(lane == 1, lp2, jnp.where(lane == 2, rw[:, 0:1],
                                                                         jnp.where(lane == 3, rw[:, 1:2], 0.0))))
    rows4 = cols.T
    row = lax.broadcasted_iota(jnp.int32, (MOE_L, td), 0).astype(F32)
    p1 = row == rows4[0:1, :]
    p2 = row == rows4[1:2, :]
    perm = jnp.where(p1 | p2, 1.0, 0.0).astype(BF16)
    xbuf[slot, :, 0:D_MODEL] = _dot(perm, h2).astype(BF16)
    w = jnp.sum(jnp.where(p1, rows4[2:3, :], 0.0) + jnp.where(p2, rows4[3:4, :], 0.0), axis=-1, keepdims=True)
    w0 = w.astype(BF16).astype(F32)
    w1 = (w - w0).astype(BF16).astype(F32)
    w2 = (w - w0 - w1).astype(BF16).astype(F32)
    wl = lax.broadcasted_iota(jnp.int32, (MOE_L, LANE), 1)
    terms = jnp.where(wl == 0, w0, jnp.where(wl == 1, w1, jnp.where(wl == 2, w2, 0.0)))
    xbuf[slot, :, D_MODEL:XS_W] = terms.astype(BF16)

    def wait_chunks(count, s):
        def body(c, carry):
            _chunk_copy(xbuf.at[s], 0, xs_hbm, 0, sem.at[s]).wait()
            return carry
        lax.fori_loop(0, count, body, 0)

    @pl.when(i > 0)
    def _():
        wait_chunks(nch_ref[i - 1], 1 - slot)

    def send(c, carry):
        _chunk_copy(xbuf.at[slot], c, xs_hbm, dst_ref[i, c], sem.at[slot]).start()
        return carry
    lax.fori_loop(0, nch_ref[i], send, 0)

    @pl.when(i == n - 1)
    def _():
        wait_chunks(nch_ref[i], slot)
        for e in range(N_EXPERTS):
            def drain(m, carry):
                _chunk_copy(zx, 0, xs_hbm, 0, zsem.at[0]).wait()
                return carry
            lax.fori_loop(0, tl_ref[e], drain, 0)

        def drain_tile(m, carry):
            zero_tile(m).wait()
            return carry
        lax.fori_loop(nt_ref[0], n_tiles_max, drain_tile, 0)


def _dispatch(dst, n_chunks, tail_start, tail_len, n_tiles, part_a, part_b, n_tiles_max):
    steps_a = part_a[0].shape[0] // MOE_TD
    steps_b = part_b[0].shape[0] // MOE_TD
    in_a = lambda i, *_: (jnp.minimum(i, steps_a - 1), 0)
    in_b = lambda i, *_: (jnp.maximum(i - steps_a, 0), 0)
    specs = lambda f: [pl.BlockSpec((MOE_TD, D_MODEL), f), pl.BlockSpec((MOE_TD, LANE), f),
                       pl.BlockSpec((MOE_TD, LANE), f)]
    rows = n_tiles_max * MOE_TILE
    return pl.pallas_call(
        functools.partial(_dispatch_kernel, steps_a),
        grid_spec=pltpu.PrefetchScalarGridSpec(
            num_scalar_prefetch=5,
            grid=(steps_a + steps_b,),
            in_specs=specs(in_a) + specs(in_b),
            out_specs=[pl.BlockSpec(memory_space=pl.ANY),
                       pl.BlockSpec((MOE_TD, LANE), lambda i, *_: (i, 0))],
            scratch_shapes=[pltpu.VMEM((2, MOE_L, XS_W), BF16), pltpu.VMEM((MOE_TILE, XS_W), BF16),
                            pltpu.SemaphoreType.DMA((2,)), pltpu.SemaphoreType.DMA((2,)),
                            pltpu.VMEM((MOE_TD, MOE_TD), BF16)]),
        out_shape=[jax.ShapeDtypeStruct((rows, XS_W), BF16),
                   jax.ShapeDtypeStruct(((steps_a + steps_b) * MOE_TD, LANE), jnp.int32)],
        compiler_params=_cparams("arbitrary"),
        name="moe_dispatch",
    )(dst, n_chunks, tail_start, tail_len, n_tiles, *part_a, *part_b)


def _moe_gemm_kernel(t0_ref, t1_ref, nt_ref, wg_ref, wu_ref, wd_ref, xs_hbm, y_hbm,
                     wg_b, wu_b, wd_b, xbuf, ybuf, zbuf, isem, osem, zsem, issued):
    e = pl.program_id(0)
    nt = nt_ref[0]
    n_tiles_max = y_hbm.shape[0] // MOE_TILE
    tile_rows = lambda t: pl.ds(pl.multiple_of(t * MOE_TILE, MOE_TILE), MOE_TILE)

    def load(t, s):
        return (pltpu.make_async_copy(xs_hbm.at[tile_rows(t), :], xbuf.at[s], isem.at[s]),)

    def store(t, s):
        return pltpu.make_async_copy(ybuf.at[s], y_hbm.at[tile_rows(t), :], osem.at[s])

    def zero_tile(t):
        return pltpu.make_async_copy(zbuf, y_hbm.at[tile_rows(t), :], zsem.at[0])

    @pl.when(e == 0)
    def _():
        issued[0] = 0
        zbuf[...] = jnp.zeros_like(zbuf)

        def fill(t, carry):
            zero_tile(t).start()
            return carry
        lax.fori_loop(nt, n_tiles_max, fill, 0)

    wg_b[...] = wg_ref[0].astype(BF16)
    wu_b[...] = wu_ref[0].astype(BF16)
    wd_b[...] = wd_ref[0].astype(BF16)

    def top_up(t):
        upto = jnp.minimum(t + GEMM_RING, nt)

        def start(u, carry):
            for cp in load(u, u % GEMM_RING):
                cp.start()
            return carry
        lax.fori_loop(issued[0], upto, start, 0)
        issued[0] = jnp.maximum(issued[0], upto)

    def process(t, n):
        top_up(t)
        for k in range(n):
            for cp in load(t + k, (t + k) % GEMM_RING):
                cp.wait()
        rows = [xbuf[(t + k) % GEMM_RING] for k in range(n)]
        xw = rows[0] if n == 1 else jnp.concatenate(rows, axis=0)
        x = xw[:, 0:D_MODEL]
        wt = xw[:, D_MODEL:XS_W].astype(F32)
        w = wt[:, 0:1] + wt[:, 1:2] + wt[:, 2:3]
        a = _dot(x, wg_b[...])
        u = _dot(x, wu_b[...])
        he = (a * _sigmoid(a)) * u
        y = (_dot(he.astype(BF16), wd_b[...]) * w).astype(BF16)
        for k in range(n):
            s = (t + k) % GEMM_OUT

            @pl.when(t + k >= GEMM_OUT)
            def _(k=k, s=s):
                store(t + k - GEMM_OUT, s).wait()
            ybuf[s] = y[k * MOE_TILE:(k + 1) * MOE_TILE, :]
            store(t + k, s).start()

    t0 = t0_ref[e]
    t1 = t1_ref[e]

    def pair(p, carry):
        process(t0 + 2 * p, 2)
        return carry
    lax.fori_loop(0, (t1 - t0) // 2, pair, 0)

    @pl.when((t1 - t0) % 2 == 1)
    def _():
        process(t1 - 1, 1)

    @pl.when(e == pl.num_programs(0) - 1)
    def _():
        for k in range(1, GEMM_OUT + 1):
            @pl.when(nt >= k)
            def _(k=k):
                store(nt - k, (nt - k) % GEMM_OUT).wait()

        def drain(t, carry):
            zero_tile(t).wait()
            return carry
        lax.fori_loop(nt, n_tiles_max, drain, 0)


def _moe_gemm(tile0, tile1, n_tiles, xs, w_gate, w_up, w_down):
    wmap = lambda e, *_: (e, 0, 0)
    return pl.pallas_call(
        _moe_gemm_kernel,
        grid_spec=pltpu.PrefetchScalarGridSpec(
            num_scalar_prefetch=3,
            grid=(N_EXPERTS,),
            in_specs=[pl.BlockSpec((1, D_MODEL, D_EXPERT), wmap),
                      pl.BlockSpec((1, D_MODEL, D_EXPERT), wmap),
                      pl.BlockSpec((1, D_EXPERT, D_MODEL), wmap),
                      pl.BlockSpec(memory_space=pl.ANY)],
            out_specs=pl.BlockSpec(memory_space=pl.ANY),
            scratch_shapes=[pltpu.VMEM((D_MODEL, D_EXPERT), BF16), pltpu.VMEM((D_MODEL, D_EXPERT), BF16),
                            pltpu.VMEM((D_EXPERT, D_MODEL), BF16),
                            pltpu.VMEM((GEMM_RING, MOE_TILE, XS_W), BF16),
                            pltpu.VMEM((GEMM_OUT, MOE_TILE, D_MODEL), BF16), pltpu.VMEM((MOE_TILE, D_MODEL), BF16),
                            pltpu.SemaphoreType.DMA((GEMM_RING,)), pltpu.SemaphoreType.DMA((GEMM_OUT,)),
                            pltpu.SemaphoreType.DMA((1,)), pltpu.SMEM((1,), jnp.int32)]),
        out_shape=jax.ShapeDtypeStruct((xs.shape[0], D_MODEL), BF16),
        compiler_params=_cparams("arbitrary"),
        name="moe_gemm",
    )(tile0, tile1, n_tiles, w_gate, w_up, w_down, xs)


def _moe_combine_kernel(step0, dst_ref, nch_ref, lp_ref, x1_ref, mod_ref, gpost_ref, y_hbm, o_ref, ybuf, sem):
    i = pl.program_id(0)
    n = pl.num_programs(0)
    slot = i % 2

    def fetch(step, s):
        def body(c, carry):
            _chunk_copy(y_hbm, dst_ref[step, c], ybuf.at[s], c, sem.at[s]).start()
            return carry
        lax.fori_loop(0, nch_ref[step], body, 0)

    @pl.when(i == 0)
    def _():
        ybuf[...] = jnp.zeros_like(ybuf)
        fetch(step0, 0)

    @pl.when(i + 1 < n)
    def _():
        fetch(step0 + i + 1, 1 - slot)

    def wait(c, carry):
        _chunk_copy(y_hbm, 0, ybuf.at[slot], 0, sem.at[slot]).wait()
        return carry
    lax.fori_loop(0, nch_ref[step0 + i], wait, 0)

    lp = lp_ref[...]
    col = lax.broadcasted_iota(jnp.int32, (MOE_TD, MOE_L), 1)
    unperm = jnp.where((col == lp[:, 0:1]) | (col == lp[:, 1:2]), 1.0, 0.0).astype(BF16)
    moe = _dot(unperm, ybuf[slot])
    gt2 = mod_ref[0, 5:6, :]
    o_ref[...] = x1_ref[...] + gt2 * _rms(moe, gpost_ref[...])


def _moe_combine(step0, dst, n_chunks, lp, x1, mod, mod_row, y_sorted, g_post2):
    n = x1.shape[0]
    tok = lambda i, *_: (i, 0)
    return pl.pallas_call(
        functools.partial(_moe_combine_kernel, step0),
        grid_spec=pltpu.PrefetchScalarGridSpec(
            num_scalar_prefetch=2,
            grid=(n // MOE_TD,),
            in_specs=[pl.BlockSpec((MOE_TD, LANE), lambda i, *_: (step0 + i, 0)),
                      pl.BlockSpec((MOE_TD, D_MODEL), tok),
                      pl.BlockSpec((1, 6, D_MODEL), lambda i, *_: (mod_row(i), 0, 0)),
                      pl.BlockSpec((1, D_MODEL), lambda i, *_: (0, 0)),
                      pl.BlockSpec(memory_space=pl.ANY)],
            out_specs=pl.BlockSpec((MOE_TD, D_MODEL), tok),
            scratch_shapes=[pltpu.VMEM((2, MOE_L, D_MODEL), BF16), pltpu.SemaphoreType.DMA((2,))]),
        out_shape=jax.ShapeDtypeStruct((n, D_MODEL), F32),
        compiler_params=_cparams("arbitrary"),
        name="moe_combine",
    )(dst, n_chunks, lp, x1, mod, g_post2, y_sorted)


def _moe(part_a, part_b, mod, mod_row_a, mod_row_b, w_gate, w_up, w_down, g_post2):
    cnt = jnp.concatenate([part_a[4], part_b[4]], axis=0)[:, 0, :N_EXPERTS].astype(jnp.int32)
    steps_a = part_a[4].shape[0]
    n_tiles_max = (cnt.shape[0] * MOE_NCH + N_EXPERTS * (TILE_CH - 1)) // TILE_CH + 1
    dst, n_chunks, tail_start, tail_len, tile0, tile1, n_tiles = _plan(cnt, n_tiles_max)
    xs, lp = _dispatch(dst, n_chunks, tail_start, tail_len, n_tiles, part_a[1:4], part_b[1:4], n_tiles_max)
    y_sorted = _moe_gemm(tile0, tile1, n_tiles, xs, w_gate, w_up, w_down)
    ya = _moe_combine(0, dst, n_chunks, lp, part_a[0], mod, mod_row_a, y_sorted, g_post2)
    yb = _moe_combine(steps_a, dst, n_chunks, lp, part_b[0], mod, mod_row_b, y_sorted, g_post2)
    return ya, yb


def _rotate_half_cols(w):
    n = w.shape[-1]
    w4 = w.reshape(w.shape[:-1] + (n // 32, 2, 16))
    return jnp.stack([-w4[..., 1, :], w4[..., 0, :]], axis=-2).reshape(w.shape)


def _pad_lanes(w):
    return jnp.concatenate([w, jnp.zeros(w.shape[:-1] + (LANE - w.shape[-1],), w.dtype)], axis=-1)


def _rope_tables(t):
    rows = t // GRID_W
    n_freq = D_ROPE // 4
    f32 = np.float32
    freqs = f32(ROPE_BASE) ** (-np.arange(n_freq, dtype=f32) / f32(n_freq))
    ang_r = np.arange(rows, dtype=f32)[:, None] * freqs
    ang_c = np.arange(GRID_W, dtype=f32)[:, None] * freqs

    def table(fn):
        r = np.repeat(fn(ang_r).astype(f32), GRID_W, axis=0)
        c = np.tile(fn(ang_c).astype(f32), (rows, 1))
        pad = np.zeros((t, LANE - D_ROPE), f32)
        return jnp.asarray(np.concatenate([r, r, c, c, pad], axis=-1))
    return table(np.cos), table(np.sin)


def kernel(x_prompt, x_sample, cache_ckv, cache_krope, c, c_ctx, w_ada, b_ada, g_pre1, g_post1, g_pre2, g_post2, w_in, w_dw, b_dw, conv_ln_g, conv_ln_b, q_norm_g, kv_norm_g, w_uq, w_ukv, w_out, w_rg, b_rg, w_re, b_re, w_gate, w_up, w_down):
    nb, seq, d = x_prompt.shape
    db, dseq, _ = x_sample.shape
    l = 0

    cvec = jnp.concatenate([c_ctx[None, :], c, jnp.zeros((8 - 1 - db, d), F32)], axis=0)
    mod = _ada(cvec, w_ada[l], b_ada[l]).reshape(8, 6, d)

    w_in_t = w_in[l].T
    wuq = w_uq[l].reshape(Q_LORA, N_HEADS, D_NOPE + D_ROPE)
    wq_rope = wuq[:, :, D_NOPE:]
    wq = jnp.concatenate([wuq[:, :, :D_NOPE], _pad_lanes(wq_rope)], axis=-1)
    wq = wq.reshape(Q_LORA, N_HEADS * HEAD_W).astype(BF16)
    wqs = _pad_lanes(_rotate_half_cols(wq_rope)).reshape(Q_LORA, N_HEADS * LANE).astype(BF16)
    wukv4 = w_ukv[l].reshape(KV_LORA, N_HEADS, D_NOPE + D_V)
    wukv = jnp.concatenate([wukv4[:, :, :D_NOPE].reshape(KV_LORA, N_HEADS * D_NOPE),
                            wukv4[:, :, D_NOPE:].reshape(KV_LORA, N_HEADS * D_V)], axis=-1).astype(BF16)
    w_out_b = w_out[l]
    w_r = jnp.concatenate([w_re[l], w_rg[l], jnp.zeros((d, LANE - N_EXPERTS - N_GROUPS), F32)], axis=-1)
    wr_hi = w_r.astype(BF16)
    wr_lo = (w_r - wr_hi.astype(F32)).astype(BF16)
    wr_cat = jnp.concatenate([wr_hi[:, :LANE // 2], wr_lo[:, :LANE // 2]], axis=-1)
    b_r = jnp.concatenate([b_re[l], b_rg[l], jnp.zeros((LANE - N_EXPERTS - N_GROUPS,), F32)])[None, :]
    cos, sin = _rope_tables(dseq)
    row = lambda v: v[l][None, :]

    tm = MOE_TD
    xp_flat = x_prompt.reshape(1, nb * seq, d)
    conv_w = (w_dw[l], row(b_dw), row(conv_ln_g), row(conv_ln_b))
    assert seq == CONV_TB, "a context step must hold whole sequences of CONV_TB tokens"
    conv_out, q, k, v, ckv, kr = _mix_in(xp_flat, mod, 0, row(g_pre1), w_in_t, row(q_norm_g), row(kv_norm_g),
                                         wq, wqs, wukv, cos[:tm], sin[:tm], conv_w, False, tm)
    per_seq = lambda a: a.reshape(nb, seq, a.shape[-1])
    att = _attn(per_seq(q), per_seq(k), per_seq(v), n_seq=ATT_SEQS)
    flat = lambda a: a.reshape(1, nb * seq, a.shape[-1])
    post_c = _post(xp_flat, conv_out, flat(att), mod, 0, w_out_b, row(g_post1), row(g_pre2),
                   wr_hi, wr_cat, b_r)
    state_ckv = ckv.reshape(nb, 1, seq, KV_LORA)
    state_krope = kr.reshape(nb, 1, seq, D_ROPE)

    kc, vc = _cache_kv(cache_ckv[:, l], cache_krope[:, l], wukv)
    conv_out, q, k, v, _, _ = _mix_in(x_sample, mod, 1, row(g_pre1), w_in_t, row(q_norm_g), row(kv_norm_g),
                                      wq, wqs, wukv, cos, sin, conv_w, True, tm)
    att = _attn(q, k, v, kc, vc)
    post_s = _post(x_sample, conv_out, att, mod, 1, w_out_b, row(g_post1), row(g_pre2), wr_hi, wr_cat, b_r)

    tokens = lambda parts: tuple(a.reshape((-1,) + a.shape[2:]) for a in parts)
    steps_per_req = dseq // MOE_TD
    yp, ys = _moe(tokens(post_c), tokens(post_s), mod, lambda i: 0, lambda i: 1 + i // steps_per_req,
                  w_gate[l], w_up[l], w_down[l], row(g_post2))

    return (yp.reshape(nb, seq, d), ys.reshape(db, dseq, d), state_ckv, state_krope)
```

```python
import functools

import jax
import jax.numpy as jnp
import numpy as np
from jax import lax
from jax.experimental import pallas as pl
from jax.experimental.pallas import tpu as pltpu

D_MODEL = 1024
GRID_W = 64
C_CONV = 512
CONV_K = 31
N_HEADS = 4
D_NOPE = 128
D_ROPE = 64
D_V = 128
Q_LORA = 384
KV_LORA = 256
N_GROUPS = 4
E_PER_GROUP = 8
N_EXPERTS = 32
D_EXPERT = 256
ROPE_BASE = 10000.0
EPS = 1e-6
ATT_SCALE = (D_NOPE + D_ROPE) ** -0.5
LOG2E = 1.4426950408889634

LANE = 128
SUBLANE = 8
HEAD_W = 2 * LANE
CONV_HALO = 16
ADA_TN = 1536
ATT_TQ = 256
ATT_SEQS = 8
POST_SUB = 256
MOE_TD = 512
CHUNK = 2 * SUBLANE
MOE_NCH = (2 * MOE_TD + N_EXPERTS * (CHUNK - 1)) // CHUNK
MOE_L = MOE_NCH * CHUNK
MOE_TILE = 256
TILE_CH = MOE_TILE // CHUNK
GEMM_RING = 6
GEMM_OUT = 4
XS_W = D_MODEL + LANE
VMEM_LIMIT = 56 * 1024 * 1024

BF16 = jnp.bfloat16
F32 = jnp.float32


def _cparams(*sem):
    return pltpu.CompilerParams(dimension_semantics=sem, vmem_limit_bytes=VMEM_LIMIT)


def _rms(x, g):
    return x * lax.rsqrt(jnp.mean(x * x, axis=-1, keepdims=True) + EPS) * g


def _sigmoid(x):
    return 1.0 / (1.0 + jnp.exp(-x))


def _dot(a, b):
    return jnp.dot(a, b, preferred_element_type=F32)


def _dot_t(a, b_t):
    return lax.dot_general(a, b_t, (((1,), (1,)), ((), ())), preferred_element_type=F32)


def _chunk_copy(src, src_chunk, dst, dst_chunk, sem):
    rows = lambda c: pl.ds(c * CHUNK if isinstance(c, int) else pl.multiple_of(c * CHUNK, CHUNK), CHUNK)
    return pltpu.make_async_copy(src.at[rows(src_chunk), :], dst.at[rows(dst_chunk), :], sem)


def _ada_kernel(c_ref, w_ref, b_ref, o_ref):
    c = c_ref[...]
    s = (c * _sigmoid(c)).astype(BF16)
    o_ref[...] = _dot(s, w_ref[...].astype(BF16)) + b_ref[...]


def _ada(cvec, w_ada, b_ada):
    n = w_ada.shape[1]
    tn = ADA_TN
    return pl.pallas_call(
        _ada_kernel,
        grid=(n // tn,),
        in_specs=[pl.BlockSpec((8, D_MODEL), lambda j: (0, 0)),
                  pl.BlockSpec((D_MODEL, tn), lambda j: (0, j)),
                  pl.BlockSpec((1, tn), lambda j: (0, j))],
        out_specs=pl.BlockSpec((8, tn), lambda j: (0, j)),
        out_shape=jax.ShapeDtypeStruct((8, n), F32),
        compiler_params=_cparams("arbitrary"),
        name="ada",
    )(cvec, w_ada, b_ada.reshape(1, n))


CONV_TB = 256
CONV_WIN = CONV_TB + 2 * CONV_HALO
CONV_TT = 32


def _put_rows(win_ref, row, x):
    for s in range(C_CONV // LANE):
        win_ref[s, pl.ds(2 * row, x.shape[0], stride=2), :] = x[:, s * LANE:(s + 1) * LANE]


def _conv_window(win_ref, w_ref, b_ref, g_ref, bb_ref, o_ref, row0):
    first = CONV_HALO - CONV_K // 2
    for c in range(CONV_TB // CONV_TT):
        slabs = []
        for s in range(C_CONV // LANE):
            lanes = slice(s * LANE, (s + 1) * LANE)
            acc = jnp.zeros((CONV_TT, LANE), F32)
            for k in range(CONV_K):
                rows = pl.ds(2 * (c * CONV_TT + first + k), CONV_TT, stride=2)
                acc = acc + win_ref[s, rows, :] * w_ref[k:k + 1, lanes]
            slabs.append(acc)
        y = jnp.concatenate(slabs, axis=-1) + b_ref[...]
        mu = jnp.mean(y, axis=-1, keepdims=True)
        yc = y - mu
        var = jnp.mean(yc * yc, axis=-1, keepdims=True)
        z = yc * lax.rsqrt(var + EPS) * g_ref[...] + bb_ref[...]
        o_ref[0, row0 + c * CONV_TT:row0 + (c + 1) * CONV_TT, :] = (z * _sigmoid(z)).astype(BF16)


def _mix_in_kernel(latent, x_ref, xp_ref, xn_ref, mod_ref, gpre_ref, win_ref, qg_ref, kvg_ref, wq_ref, wqs_ref,
                   wukv_ref, cos_ref, sin_ref, wdw_ref, bdw_ref, lng_ref, lnb_ref,
                   conv_ref, q_ref, k_ref, v_ref, ckv_ref, kr_ref, cwin, wbf):
    rope = latent

    @pl.when((pl.program_id(0) == 0) & (pl.program_id(1) == 0))
    def _():
        wbf[...] = win_ref[...].astype(BF16)

    x = x_ref[0]
    tm = x.shape[0]
    sh1 = mod_ref[0, 0:1, :]
    sc1 = mod_ref[0, 1:2, :]
    modulate = lambda v: _rms(v, gpre_ref[...]) * (1.0 + sc1) + sh1
    glu = lambda uc: uc[:, :C_CONV] * _sigmoid(uc[:, C_CONV:2 * C_CONV])
    u = _dot_t(modulate(x).astype(BF16), wbf[...])
    hglu = glu(u)

    zeros = jnp.zeros((CONV_HALO, C_CONV), F32)
    n_win = tm // CONV_TB
    if latent:
        i = pl.program_id(1)
        xh = jnp.concatenate([xp_ref[0], xn_ref[0]], axis=0)
        hh = glu(_dot_t(modulate(xh).astype(BF16), wbf[0:2 * C_CONV, :]))
        before = jnp.where(i > 0, hh[0:CONV_HALO], zeros)
        after = jnp.where(i < pl.num_programs(1) - 1, hh[CONV_HALO:2 * CONV_HALO], zeros)
    for j in range(n_win):
        if latent:
            lo, hi, d0 = j * CONV_TB - CONV_HALO, j * CONV_TB + CONV_TB + CONV_HALO, 0
            if lo < 0:
                _put_rows(cwin.at[j], 0, before)
                lo, d0 = 0, CONV_HALO
            if hi > tm:
                _put_rows(cwin.at[j], CONV_WIN - CONV_HALO, after)
                hi = tm
            _put_rows(cwin.at[j], d0, hglu[lo:hi, :])
        else:
            _put_rows(cwin.at[j], 0, zeros)
            _put_rows(cwin.at[j], CONV_HALO, hglu[j * CONV_TB:(j + 1) * CONV_TB, :])
            _put_rows(cwin.at[j], CONV_HALO + CONV_TB, zeros)
    for j in range(n_win):
        _conv_window(cwin.at[j], wdw_ref, bdw_ref, lng_ref, lnb_ref, conv_ref, j * CONV_TB)

    o_q = 2 * C_CONV
    o_kv = o_q + Q_LORA
    o_kr = o_kv + KV_LORA
    qn = _rms(u[:, o_q:o_kv], qg_ref[...]).astype(BF16)
    qf = _dot(qn, wq_ref[...])
    ckv = _rms(u[:, o_kv:o_kr], kvg_ref[...])
    ckv_ref[0] = ckv
    kvd = _dot(ckv.astype(BF16), wukv_ref[...])
    kr64 = u[:, o_kr:o_kr + D_ROPE]
    kr_ref[0] = kr64
    kr = jnp.concatenate([kr64, jnp.zeros((tm, LANE - D_ROPE), F32)], axis=-1)
    if rope:
        cos = cos_ref[...]
        sin = sin_ref[...]
        qs = _dot(qn, wqs_ref[...])
        lane = lax.broadcasted_iota(jnp.int32, kr.shape, 1)
        partner = jnp.where(lane % (D_ROPE // 2) < D_ROPE // 4,
                            -pltpu.roll(kr, LANE - D_ROPE // 4, axis=1), pltpu.roll(kr, D_ROPE // 4, axis=1))
        kr = kr * cos + partner * sin
    q_parts = []
    k_parts = []
    for hd in range(N_HEADS):
        q_parts.append(qf[:, hd * HEAD_W:hd * HEAD_W + LANE])
        qr = qf[:, hd * HEAD_W + LANE:(hd + 1) * HEAD_W]
        if rope:
            qr = qr * cos + qs[:, hd * LANE:(hd + 1) * LANE] * sin
        q_parts.append(qr)
        k_parts.append(kvd[:, hd * D_NOPE:(hd + 1) * D_NOPE])
        k_parts.append(kr)
    q_ref[0] = jnp.concatenate(q_parts, axis=-1).astype(BF16)
    k_ref[0] = jnp.concatenate(k_parts, axis=-1).astype(BF16)
    v_ref[0] = kvd[:, N_HEADS * D_NOPE:].astype(BF16)


def _mix_in(x, mod, mod_row0, g_pre1, w_in_ext, q_norm_g, kv_norm_g, wq, wqs, wukv, cos, sin, conv_w, latent, tm):
    b, t, _ = x.shape
    ncol = w_in_ext.shape[0]
    const = lambda bi, i: (0, 0)
    tok = lambda bi, i: (bi, i, 0)
    table = (lambda bi, i: (i, 0)) if latent else const
    halo_blocks = tm // CONV_HALO
    prev_rows = lambda bi, i: (bi, jnp.maximum(i * halo_blocks - 1, 0), 0)
    next_rows = lambda bi, i: (bi, jnp.minimum((i + 1) * halo_blocks, t // CONV_HALO - 1), 0)
    outs = [(C_CONV, BF16), (N_HEADS * HEAD_W, BF16), (N_HEADS * HEAD_W, BF16), (N_HEADS * D_V, BF16),
            (KV_LORA, F32), (D_ROPE, F32)]
    return pl.pallas_call(
        functools.partial(_mix_in_kernel, latent),
        grid=(b, t // tm),
        in_specs=[pl.BlockSpec((1, tm, D_MODEL), tok),
                  pl.BlockSpec((1, CONV_HALO, D_MODEL), prev_rows),
                  pl.BlockSpec((1, CONV_HALO, D_MODEL), next_rows),
                  pl.BlockSpec((1, 6, D_MODEL), lambda bi, i: (mod_row0 + bi, 0, 0)),
                  pl.BlockSpec((1, D_MODEL), const),
                  pl.BlockSpec((ncol, D_MODEL), const),
                  pl.BlockSpec((1, Q_LORA), const),
                  pl.BlockSpec((1, KV_LORA), const),
                  pl.BlockSpec((Q_LORA, N_HEADS * HEAD_W), const),
                  pl.BlockSpec((Q_LORA, N_HEADS * LANE), const),
                  pl.BlockSpec((KV_LORA, N_HEADS * (D_NOPE + D_V)), const),
                  pl.BlockSpec((tm, LANE), table),
                  pl.BlockSpec((tm, LANE), table),
                  pl.BlockSpec((CONV_K, C_CONV), const),
                  pl.BlockSpec((1, C_CONV), const),
                  pl.BlockSpec((1, C_CONV), const),
                  pl.BlockSpec((1, C_CONV), const)],
        out_specs=[pl.BlockSpec((1, tm, w), tok) for w, _ in outs],
        out_shape=[jax.ShapeDtypeStruct((b, t, w), dt) for w, dt in outs],
        scratch_shapes=[pltpu.VMEM((tm // CONV_TB, C_CONV // LANE, 2 * CONV_WIN, LANE), F32),
                        pltpu.VMEM((ncol, D_MODEL), BF16)],
        compiler_params=_cparams("arbitrary", "arbitrary"),
        name="mix_in_latent" if latent else "mix_in",
    )(x, x, x, mod, g_pre1, w_in_ext, q_norm_g, kv_norm_g, wq, wqs, wukv, cos, sin, *conv_w)


ATT_KCHUNK = 256


def _attn_kernel(cached, q_ref, k_ref, v_ref, *rest):
    if cached:
        kc_ref, vc_ref, o_ref, s_scr = rest
        sources = [(kc_ref, vc_ref), (k_ref, v_ref)]
    else:
        o_ref, s_scr = rest
        sources = [(k_ref, v_ref)]
    n_seq, tq, _ = q_ref.shape
    chunks = []
    for kr, vr in sources:
        for c0 in range(0, kr.shape[1], ATT_KCHUNK):
            c1 = min(c0 + ATT_KCHUNK, kr.shape[1])
            col = chunks[-1][4] + chunks[-1][3] - chunks[-1][2] if chunks else 0
            chunks.append((kr, vr, c0, c1, col))
    for b in range(n_seq):
        outs = []
        for hd in range(N_HEADS):
            ks = slice(hd * HEAD_W, (hd + 1) * HEAD_W)
            vs = slice(hd * D_V, (hd + 1) * D_V)
            qh = q_ref[b, :, ks]
            m = jnp.full((tq, 1), -jnp.inf, F32)
            for kr, vr, c0, c1, col in chunks:
                s = _dot_t(qh, kr[b, c0:c1, ks]) * (ATT_SCALE * LOG2E)
                s_scr[hd, :, col:col + c1 - c0] = s
                m = jnp.maximum(m, jnp.max(s, axis=-1, keepdims=True))
            l = jnp.zeros((tq, 1), F32)
            o = jnp.zeros((tq, D_V), F32)
            for kr, vr, c0, c1, col in chunks:
                p = jnp.exp2(s_scr[hd, :, col:col + c1 - c0] - m)
                l = l + jnp.sum(p, axis=-1, keepdims=True)
                o = o + _dot(p.astype(BF16), vr[b, c0:c1, vs])
            outs.append(o / l)
        o_ref[b] = jnp.concatenate(outs, axis=-1).astype(BF16)


def _attn(q, k, v, kc=None, vc=None, tq=ATT_TQ, n_seq=1):
    b, t, _ = q.shape
    s = k.shape[1]
    cached = kc is not None
    whole = lambda bi, i: (bi, 0, 0)
    in_specs = [pl.BlockSpec((n_seq, tq, N_HEADS * HEAD_W), lambda bi, i: (bi, i, 0)),
                pl.BlockSpec((n_seq, s, N_HEADS * HEAD_W), whole),
                pl.BlockSpec((n_seq, s, N_HEADS * D_V), whole)]
    args = [q, k, v]
    if cached:
        sc = kc.shape[1]
        in_specs += [pl.BlockSpec((n_seq, sc, N_HEADS * HEAD_W), whole),
                     pl.BlockSpec((n_seq, sc, N_HEADS * D_V), whole)]
        args += [kc, vc]
        s += sc
    return pl.pallas_call(
        functools.partial(_attn_kernel, cached),
        grid=(b // n_seq, t // tq),
        in_specs=in_specs,
        out_specs=pl.BlockSpec((n_seq, tq, N_HEADS * D_V), lambda bi, i: (bi, i, 0)),
        out_shape=jax.ShapeDtypeStruct((b, t, N_HEADS * D_V), BF16),
        scratch_shapes=[pltpu.VMEM((N_HEADS, tq, s), F32)],
        compiler_params=_cparams("arbitrary", "arbitrary"),
        name="attn_cached" if cached else "attn",
    )(*args)


def _cache_kv_kernel(ckv_ref, kr_ref, wukv_ref, k_ref, v_ref):
    kvd = _dot(ckv_ref[0].astype(BF16), wukv_ref[...])
    kr = kr_ref[0]
    kr = jnp.concatenate([kr, jnp.zeros_like(kr)], axis=-1)
    parts = []
    for hd in range(N_HEADS):
        parts.append(kvd[:, hd * D_NOPE:(hd + 1) * D_NOPE])
        parts.append(kr)
    k_ref[0] = jnp.concatenate(parts, axis=-1).astype(BF16)
    v_ref[0] = kvd[:, N_HEADS * D_NOPE:].astype(BF16)


def _cache_kv(ckv, krope, wukv):
    b, s, _ = ckv.shape
    tok = lambda bi: (bi, 0, 0)
    return pl.pallas_call(
        _cache_kv_kernel,
        grid=(b,),
        in_specs=[pl.BlockSpec((1, s, KV_LORA), tok),
                  pl.BlockSpec((1, s, D_ROPE), tok),
                  pl.BlockSpec((KV_LORA, N_HEADS * (D_NOPE + D_V)), lambda bi: (0, 0))],
        out_specs=[pl.BlockSpec((1, s, N_HEADS * HEAD_W), tok),
                   pl.BlockSpec((1, s, N_HEADS * D_V), tok)],
        out_shape=[jax.ShapeDtypeStruct((b, s, N_HEADS * HEAD_W), BF16),
                   jax.ShapeDtypeStruct((b, s, N_HEADS * D_V), BF16)],
        compiler_params=_cparams("arbitrary"),
        name="cache_kv",
    )(ckv, krope, wukv)


def _post_kernel(x_ref, conv_ref, att_ref, mod_ref, wo_ref, gpost_ref, gpre2_ref, wr_hi_ref, wr_cat_ref,
                 br_ref, x1_ref, h2_ref, ri_ref, rw_ref, cnt_ref, wo_b):
    @pl.when((pl.program_id(0) == 0) & (pl.program_id(1) == 0))
    def _():
        wo_b[...] = wo_ref[...].astype(BF16)

    gt1 = mod_ref[0, 2:3, :]
    sh2 = mod_ref[0, 3:4, :]
    sc2 = mod_ref[0, 4:5, :]
    tm = x_ref.shape[1]
    halves = [slice(r0, r0 + POST_SUB) for r0 in range(0, tm, POST_SUB)]
    outs = [_dot(conv_ref[0, rows, :], wo_b[:C_CONV, :]) + _dot(att_ref[0, rows, :], wo_b[C_CONV:, :])
            for rows in halves]
    his, los = [], []
    for rows, out in zip(halves, outs):
        x1 = x_ref[0, rows, :] + gt1 * _rms(out, gpost_ref[...])
        x1_ref[0, rows, :] = x1
        h2 = _rms(x1, gpre2_ref[...]) * (1.0 + sc2) + sh2
        h_hi = h2.astype(BF16)
        h2_ref[0, rows, :] = h_hi
        his.append(h_hi)
        los.append((h2 - h_hi.astype(F32)).astype(BF16))
    logit_blocks = []
    for h_hi, h_lo in zip(his, los):
        hi_terms = _dot(h_hi, wr_cat_ref[...])
        logit_blocks.append(hi_terms + pltpu.roll(hi_terms, LANE // 2, axis=1) + _dot(h_lo, wr_hi_ref[...])
                            + br_ref[...])
    pairs = jnp.zeros((1, LANE), F32)
    for rows, logits in zip(halves, logit_blocks):
        lane = lax.broadcasted_iota(jnp.int32, logits.shape, 1)
        neg = jnp.float32(-jnp.inf)
        big = jnp.int32(LANE)
        is_g = (lane >= N_EXPERTS) & (lane < N_EXPERTS + N_GROUPS)
        lg = jnp.where(is_g, logits, neg)
        gmax = jnp.max(lg, axis=-1, keepdims=True)
        gidx = jnp.min(jnp.where(lg == gmax, lane, big), axis=-1, keepdims=True) - N_EXPERTS
        g_top = 1.0 / jnp.sum(jnp.exp(lg - gmax), axis=-1, keepdims=True)

        in_grp = (lane >= gidx * E_PER_GROUP) & (lane < (gidx + 1) * E_PER_GROUP)
        le = jnp.where(in_grp, logits, neg)
        m1 = jnp.max(le, axis=-1, keepdims=True)
        i1 = jnp.min(jnp.where(le == m1, lane, big), axis=-1, keepdims=True)
        le2 = jnp.where(lane == i1, neg, le)
        m2 = jnp.max(le2, axis=-1, keepdims=True)
        i2 = jnp.min(jnp.where(le2 == m2, lane, big), axis=-1, keepdims=True)
        r = jnp.exp(m2 - m1)
        w1 = g_top / (1.0 + r)
        w2 = g_top * r / (1.0 + r)
        ri_ref[0, rows, :] = jnp.where(lane == 0, i1, jnp.where(lane == 1, i2, 0))
        rw_ref[0, rows, :] = jnp.where(lane == 0, w1, jnp.where(lane == 1, w2, 0.0))
        pairs = pairs + jnp.sum(jnp.where((lane == i1) | (lane == i2), 1.0, 0.0), axis=0, keepdims=True)
    cnt_ref[0, 0] = jnp.broadcast_to(pairs, (SUBLANE, LANE))


def _post(x, conv_out, att, mod, mod_row0, w_out, g_post1, g_pre2, wr_hi, wr_cat, br):
    b, t, _ = x.shape
    tm = MOE_TD
    const = lambda bi, i: (0, 0)
    tok = lambda bi, i: (bi, i, 0)
    return pl.pallas_call(
        _post_kernel,
        grid=(b, t // tm),
        in_specs=[pl.BlockSpec((1, tm, D_MODEL), tok),
                  pl.BlockSpec((1, tm, C_CONV), tok),
                  pl.BlockSpec((1, tm, N_HEADS * D_V), tok),
                  pl.BlockSpec((1, 6, D_MODEL), lambda bi, i: (mod_row0 + bi, 0, 0)),
                  pl.BlockSpec((D_MODEL, D_MODEL), const),
                  pl.BlockSpec((1, D_MODEL), const),
                  pl.BlockSpec((1, D_MODEL), const),
                  pl.BlockSpec((D_MODEL, LANE), const),
                  pl.BlockSpec((D_MODEL, LANE), const),
                  pl.BlockSpec((1, LANE), const)],
        out_specs=[pl.BlockSpec((1, tm, D_MODEL), tok),
                   pl.BlockSpec((1, tm, D_MODEL), tok),
                   pl.BlockSpec((1, tm, LANE), tok),
                   pl.BlockSpec((1, tm, LANE), tok),
                   pl.BlockSpec((1, 1, SUBLANE, LANE), lambda bi, i: (bi, i, 0, 0))],
        out_shape=[jax.ShapeDtypeStruct((b, t, D_MODEL), F32),
                   jax.ShapeDtypeStruct((b, t, D_MODEL), BF16),
                   jax.ShapeDtypeStruct((b, t, LANE), jnp.int32),
                   jax.ShapeDtypeStruct((b, t, LANE), F32),
                   jax.ShapeDtypeStruct((b, t // tm, SUBLANE, LANE), F32)],
        scratch_shapes=[pltpu.VMEM((D_MODEL, D_MODEL), BF16)],
        compiler_params=_cparams("arbitrary", "arbitrary"),
        name="post",
    )(x, conv_out, att, mod, w_out, g_post1, g_pre2, wr_hi, wr_cat, br)


def _one_hots(ri):
    lane = lax.broadcasted_iota(jnp.int32, ri.shape, 1)
    oh1 = lane == ri[:, 0:1]
    oh2 = lane == ri[:, 1:2]
    return oh1, oh2, jnp.where(oh1 | oh2, 1.0, 0.0)


def _strictly_lower(n):
    r = lax.broadcasted_iota(jnp.int32, (n, n), 0)
    c = lax.broadcasted_iota(jnp.int32, (n, n), 1)
    return jnp.where(c < r, 1.0, 0.0).astype(BF16)


def _local_positions(oh1, oh2, oh, earlier_ref):
    cnt = jnp.sum(oh, axis=0, keepdims=True)
    nch = jnp.floor((cnt + (CHUNK - 1)) * (1.0 / CHUNK))
    a = lax.broadcasted_iota(jnp.int32, (LANE, LANE), 0)
    b = lax.broadcasted_iota(jnp.int32, (LANE, LANE), 1)
    lower_experts = jnp.where(a < b, 1.0, 0.0).astype(BF16)
    run_start = _dot(jnp.broadcast_to(nch, (SUBLANE, LANE)).astype(BF16), lower_experts)[0:1, :] * CHUNK
    pos = _dot(earlier_ref[...], oh.astype(BF16)) + run_start
    lp1 = jnp.sum(jnp.where(oh1, pos, 0.0), axis=-1, keepdims=True)
    lp2 = jnp.sum(jnp.where(oh2, pos, 0.0), axis=-1, keepdims=True)
    return lp1, lp2


def _plan(cnt, n_tiles_max):
    k = (cnt + (CHUNK - 1)) // CHUNK
    run_end = jnp.cumsum(k, axis=1)
    run_start = run_end - k
    n_chunks = run_end[:, -1]
    total = jnp.sum(k, axis=0)
    padded = (total + (TILE_CH - 1)) // TILE_CH * TILE_CH
    seg_end = jnp.cumsum(padded)
    seg_start = seg_end - padded
    base = seg_start[None, :] + jnp.cumsum(k, axis=0) - k
    c = jnp.arange(MOE_NCH, dtype=jnp.int32)[None, :, None]
    in_run = (run_start[:, None, :] <= c) & (c < run_end[:, None, :])
    dst = jnp.sum(jnp.where(in_run, base[:, None, :] + c - run_start[:, None, :], 0), axis=2)
    n_tiles = seg_end[-1] // TILE_CH
    tail_start = seg_start + total
    tail_len = padded - total
    i32 = lambda t: t.astype(jnp.int32)
    return (i32(dst), i32(n_chunks), i32(tail_start), i32(tail_len), i32(seg_start // TILE_CH),
            i32(seg_end // TILE_CH), i32(n_tiles.reshape(1)))


def _dispatch_kernel(steps_a, dst_ref, nch_ref, ts_ref, tl_ref, nt_ref, h2a_ref, ria_ref, rwa_ref, h2b_ref, rib_ref,
                     rwb_ref, xs_hbm, lp_ref, xbuf, zx, sem, zsem, earlier):
    i = pl.program_id(0)
    n = pl.num_programs(0)
    slot = i % 2
    td = MOE_TD
    n_tiles_max = xs_hbm.shape[0] // MOE_TILE

    def zero_tile(m):
        rows = pl.ds(pl.multiple_of(m * MOE_TILE, MOE_TILE), MOE_TILE)
        return pltpu.make_async_copy(zx, xs_hbm.at[rows, :], zsem.at[1])

    @pl.when(i == 0)
    def _():
        zx[...] = jnp.zeros_like(zx)
        earlier[...] = _strictly_lower(MOE_TD)
        for e in range(N_EXPERTS):
            def fill(m, carry, e=e):
                _chunk_copy(zx, 0, xs_hbm, ts_ref[e] + m, zsem.at[0]).start()
                return carry
            lax.fori_loop(0, tl_ref[e], fill, 0)

        def fill_tile(m, carry):
            zero_tile(m).start()
            return carry
        lax.fori_loop(nt_ref[0], n_tiles_max, fill_tile, 0)

    from_a = i < steps_a
    ri = jnp.where(from_a, ria_ref[...], rib_ref[...])
    rw = jnp.where(from_a, rwa_ref[...], rwb_ref[...])
    h2 = jnp.where(from_a, h2a_ref[...], h2b_ref[...])
    oh1, oh2, oh = _one_hots(ri)
    lp1, lp2 = _local_positions(oh1, oh2, oh, earlier)
    lane = lax.broadcasted_iota(jnp.int32, ri.shape, 1)
    lp_ref[...] = jnp.where(lane == 0, lp1, jnp.where(lane == 1, lp2, 0.0)).astype(jnp.int32)

    cols = jnp.where(lane == 0, lp1, jnp.where(lane == 1, lp2, jnp.where(lane == 2, rw[:, 0:1],
                                                                         jnp.where(lane == 3, rw[:, 1:2], 0.0))))
    rows4 = cols.T
    row = lax.broadcasted_iota(jnp.int32, (MOE_L, td), 0).astype(F32)
    p1 = row == rows4[0:1, :]
    p2 = row == rows4[1:2, :]
    perm = jnp.where(p1 | p2, 1.0, 0.0).astype(BF16)
    xbuf[slot, :, 0:D_MODEL] = _dot(perm, h2).astype(BF16)
    w = jnp.sum(jnp.where(p1, rows4[2:3, :], 0.0) + jnp.where(p2, rows4[3:4, :], 0.0), axis=-1, keepdims=True)
    w0 = w.astype(BF16).astype(F32)
    w1 = (w - w0).astype(BF16).astype(F32)
    w2 = (w - w0 - w1).astype(BF16).astype(F32)
    wl = lax.broadcasted_iota(jnp.int32, (MOE_L, LANE), 1)
    terms = jnp.where(wl == 0, w0, jnp.where(wl == 1, w1, jnp.where(wl == 2, w2, 0.0)))
    xbuf[slot, :, D_MODEL:XS_W] = terms.astype(BF16)

    def wait_chunks(count, s):
        def body(c, carry):
            _chunk_copy(xbuf.at[s], 0, xs_hbm, 0, sem.at[s]).wait()
            return carry
        lax.fori_loop(0, count, body, 0)

    @pl.when(i > 0)
    def _():
        wait_chunks(nch_ref[i - 1], 1 - slot)

    def send(c, carry):
        _chunk_copy(xbuf.at[slot], c, xs_hbm, dst_ref[i, c], sem.at[slot]).start()
        return carry
    lax.fori_loop(0, nch_ref[i], send, 0)

    @pl.when(i == n - 1)
    def _():
        wait_chunks(nch_ref[i], slot)
        for e in range(N_EXPERTS):
            def drain(m, carry):
                _chunk_copy(zx, 0, xs_hbm, 0, zsem.at[0]).wait()
                return carry
            lax.fori_loop(0, tl_ref[e], drain, 0)

        def drain_tile(m, carry):
            zero_tile(m).wait()
            return carry
        lax.fori_loop(nt_ref[0], n_tiles_max, drain_tile, 0)


def _dispatch(dst, n_chunks, tail_start, tail_len, n_tiles, part_a, part_b, n_tiles_max):
    steps_a = part_a[0].shape[0] // MOE_TD
    steps_b = part_b[0].shape[0] // MOE_TD
    in_a = lambda i, *_: (jnp.minimum(i, steps_a - 1), 0)
    in_b = lambda i, *_: (jnp.maximum(i - steps_a, 0), 0)
    specs = lambda f: [pl.BlockSpec((MOE_TD, D_MODEL), f), pl.BlockSpec((MOE_TD, LANE), f),
                       pl.BlockSpec((MOE_TD, LANE), f)]
    rows = n_tiles_max * MOE_TILE
    return pl.pallas_call(
        functools.partial(_dispatch_kernel, steps_a),
        grid_spec=pltpu.PrefetchScalarGridSpec(
            num_scalar_prefetch=5,
            grid=(steps_a + steps_b,),
            in_specs=specs(in_a) + specs(in_b),
            out_specs=[pl.BlockSpec(memory_space=pl.ANY),
                       pl.BlockSpec((MOE_TD, LANE), lambda i, *_: (i, 0))],
            scratch_shapes=[pltpu.VMEM((2, MOE_L, XS_W), BF16), pltpu.VMEM((MOE_TILE, XS_W), BF16),
                            pltpu.SemaphoreType.DMA((2,)), pltpu.SemaphoreType.DMA((2,)),
                            pltpu.VMEM((MOE_TD, MOE_TD), BF16)]),
        out_shape=[jax.ShapeDtypeStruct((rows, XS_W), BF16),
                   jax.ShapeDtypeStruct(((steps_a + steps_b) * MOE_TD, LANE), jnp.int32)],
        compiler_params=_cparams("arbitrary"),
        name="moe_dispatch",
    )(dst, n_chunks, tail_start, tail_len, n_tiles, *part_a, *part_b)


def _moe_gemm_kernel(t0_ref, t1_ref, nt_ref, wg_ref, wu_ref, wd_ref, xs_hbm, y_hbm,
                     wg_b, wu_b, wd_b, xbuf, ybuf, zbuf, isem, osem, zsem, issued):
    e = pl.program_id(0)
    nt = nt_ref[0]
    n_tiles_max = y_hbm.shape[0] // MOE_TILE
    tile_rows = lambda t: pl.ds(pl.multiple_of(t * MOE_TILE, MOE_TILE), MOE_TILE)

    def load(t, s):
        return (pltpu.make_async_copy(xs_hbm.at[tile_rows(t), :], xbuf.at[s], isem.at[s]),)

    def store(t, s):
        return pltpu.make_async_copy(ybuf.at[s], y_hbm.at[tile_rows(t), :], osem.at[s])

    def zero_tile(t):
        return pltpu.make_async_copy(zbuf, y_hbm.at[tile_rows(t), :], zsem.at[0])

    @pl.when(e == 0)
    def _():
        issued[0] = 0
        zbuf[...] = jnp.zeros_like(zbuf)

        def fill(t, carry):
            zero_tile(t).start()
            return carry
        lax.fori_loop(nt, n_tiles_max, fill, 0)

    wg_b[...] = wg_ref[0].astype(BF16)
    wu_b[...] = wu_ref[0].astype(BF16)
    wd_b[...] = wd_ref[0].astype(BF16)

    def top_up(t):
        upto = jnp.minimum(t + GEMM_RING, nt)

        def start(u, carry):
            for cp in load(u, u % GEMM_RING):
                cp.start()
            return carry
        lax.fori_loop(issued[0], upto, start, 0)
        issued[0] = jnp.maximum(issued[0], upto)

    def process(t, n):
        top_up(t)
        for k in range(n):
            for cp in load(t + k, (t + k) % GEMM_RING):
                cp.wait()
        rows = [xbuf[(t + k) % GEMM_RING] for k in range(n)]
        xw = rows[0] if n == 1 else jnp.concatenate(rows, axis=0)
        x = xw[:, 0:D_MODEL]
        wt = xw[:, D_MODEL:XS_W].astype(F32)
        w = wt[:, 0:1] + wt[:, 1:2] + wt[:, 2:3]
        a = _dot(x, wg_b[...])
        u = _dot(x, wu_b[...])
        he = (a * _sigmoid(a)) * u
        y = (_dot(he.astype(BF16), wd_b[...]) * w).astype(BF16)
        for k in range(n):
            s = (t + k) % GEMM_OUT

            @pl.when(t + k >= GEMM_OUT)
            def _(k=k, s=s):
                store(t + k - GEMM_OUT, s).wait()
            ybuf[s] = y[k * MOE_TILE:(k + 1) * MOE_TILE, :]
            store(t + k, s).start()

    t0 = t0_ref[e]
    t1 = t1_ref[e]

    def pair(p, carry):
        process(t0 + 2 * p, 2)
        return carry
    lax.fori_loop(0, (t1 - t0) // 2, pair, 0)

    @pl.when((t1 - t0) % 2 == 1)
    def _():
        process(t1 - 1, 1)

    @pl.when(e == pl.num_programs(0) - 1)
    def _():
        for k in range(1, GEMM_OUT + 1):
            @pl.when(nt >= k)
            def _(k=k):
                store(nt - k, (nt - k) % GEMM_OUT).wait()

        def drain(t, carry):
            zero_tile(t).wait()
            return carry
        lax.fori_loop(nt, n_tiles_max, drain, 0)


def _moe_gemm(tile0, tile1, n_tiles, xs, w_gate, w_up, w_down):
    wmap = lambda e, *_: (e, 0, 0)
    return pl.pallas_call(
        _moe_gemm_kernel,
        grid_spec=pltpu.PrefetchScalarGridSpec(
            num_scalar_prefetch=3,
            grid=(N_EXPERTS,),
            in_specs=[pl.BlockSpec((1, D_MODEL, D_EXPERT), wmap),
                      pl.BlockSpec((1, D_MODEL, D_EXPERT), wmap),
                      pl.BlockSpec((1, D_EXPERT, D_MODEL), wmap),
                      pl.BlockSpec(memory_space=pl.ANY)],
            out_specs=pl.BlockSpec(memory_space=pl.ANY),
            scratch_shapes=[pltpu.VMEM((D_MODEL, D_EXPERT), BF16), pltpu.VMEM((D_MODEL, D_EXPERT), BF16),
                            pltpu.VMEM((D_EXPERT, D_MODEL), BF16),
                            pltpu.VMEM((GEMM_RING, MOE_TILE, XS_W), BF16),
                            pltpu.VMEM((GEMM_OUT, MOE_TILE, D_MODEL), BF16), pltpu.VMEM((MOE_TILE, D_MODEL), BF16),
                            pltpu.SemaphoreType.DMA((GEMM_RING,)), pltpu.SemaphoreType.DMA((GEMM_OUT,)),
                            pltpu.SemaphoreType.DMA((1,)), pltpu.SMEM((1,), jnp.int32)]),
        out_shape=jax.ShapeDtypeStruct((xs.shape[0], D_MODEL), BF16),
        compiler_params=_cparams("arbitrary"),
        name="moe_gemm",
    )(tile0, tile1, n_tiles, w_gate, w_up, w_down, xs)


def _moe_combine_kernel(step0, dst_ref, nch_ref, lp_ref, x1_ref, mod_ref, gpost_ref, y_hbm, o_ref, ybuf, sem):
    i = pl.program_id(0)
    n = pl.num_programs(0)
    slot = i % 2

    def fetch(step, s):
        def body(c, carry):
            _chunk_copy(y_hbm, dst_ref[step, c], ybuf.at[s], c, sem.at[s]).start()
            return carry
        lax.fori_loop(0, nch_ref[step], body, 0)

    @pl.when(i == 0)
    def _():
        ybuf[...] = jnp.zeros_like(ybuf)
        fetch(step0, 0)

    @pl.when(i + 1 < n)
    def _():
        fetch(step0 + i + 1, 1 - slot)

    def wait(c, carry):
        _chunk_copy(y_hbm, 0, ybuf.at[slot], 0, sem.at[slot]).wait()
        return carry
    lax.fori_loop(0, nch_ref[step0 + i], wait, 0)

    lp = lp_ref[...]
    col = lax.broadcasted_iota(jnp.int32, (MOE_TD, MOE_L), 1)
    unperm = jnp.where((col == lp[:, 0:1]) | (col == lp[:, 1:2]), 1.0, 0.0).astype(BF16)
    moe = _dot(unperm, ybuf[slot])
    gt2 = mod_ref[0, 5:6, :]
    o_ref[...] = x1_ref[...] + gt2 * _rms(moe, gpost_ref[...])


def _moe_combine(step0, dst, n_chunks, lp, x1, mod, mod_row, y_sorted, g_post2):
    n = x1.shape[0]
    tok = lambda i, *_: (i, 0)
    return pl.pallas_call(
        functools.partial(_moe_combine_kernel, step0),
        grid_spec=pltpu.PrefetchScalarGridSpec(
            num_scalar_prefetch=2,
            grid=(n // MOE_TD,),
            in_specs=[pl.BlockSpec((MOE_TD, LANE), lambda i, *_: (step0 + i, 0)),
                      pl.BlockSpec((MOE_TD, D_MODEL), tok),
                      pl.BlockSpec((1, 6, D_MODEL), lambda i, *_: (mod_row(i), 0, 0)),
                      pl.BlockSpec((1, D_MODEL), lambda i, *_: (0, 0)),
                      pl.BlockSpec(memory_space=pl.ANY)],
            out_specs=pl.BlockSpec((MOE_TD, D_MODEL), tok),
            scratch_shapes=[pltpu.VMEM((2, MOE_L, D_MODEL), BF16), pltpu.SemaphoreType.DMA((2,))]),
        out_shape=jax.ShapeDtypeStruct((n, D_MODEL), F32),
        compiler_params=_cparams("arbitrary"),
        name="moe_combine",
    )(dst, n_chunks, lp, x1, mod, g_post2, y_sorted)


def _moe(part_a, part_b, mod, mod_row_a, mod_row_b, w_gate, w_up, w_down, g_post2):
    cnt = jnp.concatenate([part_a[4], part_b[4]], axis=0)[:, 0, :N_EXPERTS].astype(jnp.int32)
    steps_a = part_a[4].shape[0]
    n_tiles_max = (cnt.shape[0] * MOE_NCH + N_EXPERTS * (TILE_CH - 1)) // TILE_CH + 1
    dst, n_chunks, tail_start, tail_len, tile0, tile1, n_tiles = _plan(cnt, n_tiles_max)
    xs, lp = _dispatch(dst, n_chunks, tail_start, tail_len, n_tiles, part_a[1:4], part_b[1:4], n_tiles_max)
    y_sorted = _moe_gemm(tile0, tile1, n_tiles, xs, w_gate, w_up, w_down)
    ya = _moe_combine(0, dst, n_chunks, lp, part_a[0], mod, mod_row_a, y_sorted, g_post2)
    yb = _moe_combine(steps_a, dst, n_chunks, lp, part_b[0], mod, mod_row_b, y_sorted, g_post2)
    return ya, yb


def _rotate_half_cols(w):
    n = w.shape[-1]
    w4 = w.reshape(w.shape[:-1] + (n // 32, 2, 16))
    return jnp.stack([-w4[..., 1, :], w4[..., 0, :]], axis=-2).reshape(w.shape)


def _pad_lanes(w):
    return jnp.concatenate([w, jnp.zeros(w.shape[:-1] + (LANE - w.shape[-1],), w.dtype)], axis=-1)


def _rope_tables(t):
    rows = t // GRID_W
    n_freq = D_ROPE // 4
    f32 = np.float32
    freqs = f32(ROPE_BASE) ** (-np.arange(n_freq, dtype=f32) / f32(n_freq))
    ang_r = np.arange(rows, dtype=f32)[:, None] * freqs
    ang_c = np.arange(GRID_W, dtype=f32)[:, None] * freqs

    def table(fn):
        r = np.repeat(fn(ang_r).astype(f32), GRID_W, axis=0)
        c = np.tile(fn(ang_c).astype(f32), (rows, 1))
        pad = np.zeros((t, LANE - D_ROPE), f32)
        return jnp.asarray(np.concatenate([r, r, c, c, pad], axis=-1))
    return table(np.cos), table(np.sin)


def kernel(x_prompt, x_sample, cache_ckv, cache_krope, c, c_ctx, w_ada, b_ada, g_pre1, g_post1, g_pre2, g_post2, w_in, w_dw, b_dw, conv_ln_g, conv_ln_b, q_norm_g, kv_norm_g, w_uq, w_ukv, w_out, w_rg, b_rg, w_re, b_re, w_gate, w_up, w_down):
    nb, seq, d = x_prompt.shape
    db, dseq, _ = x_sample.shape
    l = 0

    cvec = jnp.concatenate([c_ctx[None, :], c, jnp.zeros((8 - 1 - db, d), F32)], axis=0)
    mod = _ada(cvec, w_ada[l], b_ada[l]).reshape(8, 6, d)

    w_in_t = w_in[l].T
    wuq = w_uq[l].reshape(Q_LORA, N_HEADS, D_NOPE + D_ROPE)
    wq_rope = wuq[:, :, D_NOPE:]
    wq = jnp.concatenate([wuq[:, :, :D_NOPE], _pad_lanes(wq_rope)], axis=-1)
    wq = wq.reshape(Q_LORA, N_HEADS * HEAD_W).astype(BF16)
    wqs = _pad_lanes(_rotate_half_cols(wq_rope)).reshape(Q_LORA, N_HEADS * LANE).astype(BF16)
    wukv4 = w_ukv[l].reshape(KV_LORA, N_HEADS, D_NOPE + D_V)
    wukv = jnp.concatenate([wukv4[:, :, :D_NOPE].reshape(KV_LORA, N_HEADS * D_NOPE),
                            wukv4[:, :, D_NOPE:].reshape(KV_LORA, N_HEADS * D_V)], axis=-1).astype(BF16)
    w_out_b = w_out[l]
    w_r = jnp.concatenate([w_re[l], w_rg[l], jnp.zeros((d, LANE - N_EXPERTS - N_GROUPS), F32)], axis=-1)
    wr_hi = w_r.astype(BF16)
    wr_lo = (w_r - wr_hi.astype(F32)).astype(BF16)
    wr_cat = jnp.concatenate([wr_hi[:, :LANE // 2], wr_lo[:, :LANE // 2]], axis=-1)
    b_r = jnp.concatenate([b_re[l], b_rg[l], jnp.zeros((LANE - N_EXPERTS - N_GROUPS,), F32)])[None, :]
    cos, sin = _rope_tables(dseq)
    row = lambda v: v[l][None, :]

    tm = MOE_TD
    xp_flat = x_prompt.reshape(1, nb * seq, d)
    conv_w = (w_dw[l], row(b_dw), row(conv_ln_g), row(conv_ln_b))
    assert seq == CONV_TB, "a context step must hold whole sequences of CONV_TB tokens"
    conv_out, q, k, v, ckv, kr = _mix_in(xp_flat, mod, 0, row(g_pre1), w_in_t, row(q_norm_g), row(kv_norm_g),
                                         wq, wqs, wukv, cos[:tm], sin[:tm], conv_w, False, tm)
    per_seq = lambda a: a.reshape(nb, seq, a.shape[-1])
    att = _attn(per_seq(q), per_seq(k), per_seq(v), n_seq=ATT_SEQS)
    flat = lambda a: a.reshape(1, nb * seq, a.shape[-1])
    post_c = _post(xp_flat, conv_out, flat(att), mod, 0, w_out_b, row(g_post1), row(g_pre2),
                   wr_hi, wr_cat, b_r)
    state_ckv = ckv.reshape(nb, 1, seq, KV_LORA)
    state_krope = kr.reshape(nb, 1, seq, D_ROPE)

    kc, vc = _cache_kv(cache_ckv[:, l], cache_krope[:, l], wukv)
    conv_out, q, k, v, _, _ = _mix_in(x_sample, mod, 1, row(g_pre1), w_in_t, row(q_norm_g), row(kv_norm_g),
                                      wq, wqs, wukv, cos, sin, conv_w, True, tm)
    att = _attn(q, k, v, kc, vc)
    post_s = _post(x_sample, conv_out, att, mod, 1, w_out_b, row(g_post1), row(g_pre2), wr_hi, wr_cat, b_r)

    tokens = lambda parts: tuple(a.reshape((-1,) + a.shape[2:]) for a in parts)
    steps_per_req = dseq // MOE_TD
    yp, ys = _moe(tokens(post_c), tokens(post_s), mod, lambda i: 0, lambda i: 1 + i // steps_per_req,
                  w_gate[l], w_up[l], w_down[l], row(g_post2))

    return (yp.reshape(nb, seq, d), ys.reshape(db, dseq, d), state_ckv, state_krope)
```
